```python
import jax, jax.numpy as jnp
from jax import lax
import numpy as np

D_MODEL = 2048
BATCH = 2
SEQ = 4096
DEPTH = 1

GRID_W = 64
CTX_LEN = 256
HEAD_DIM = 128
NA_HEADS = 8
RET_HEADS = 8
NA_WIDTH = NA_HEADS * HEAD_DIM
RET_WIDTH = RET_HEADS * HEAD_DIM
MIX_WIDTH = NA_WIDTH + RET_WIDTH
WIN_ROWS = 8
WIN_COLS = 16
RET_CHUNK = 128
ROPE_BASE = 10000.0
N_EXPERTS = 16
EXPERT_FF = 1024
CAPACITY_FACTOR = 2
N_MOD = 6
EPS = 1e-6
NEG_INF = -1e30
KV_COLS = 2 * NA_WIDTH + 2 * RET_WIDTH
IN_COLS = KV_COLS + NA_WIDTH + 2 * RET_WIDTH
LAT_SPLITS = (NA_WIDTH, 2 * NA_WIDTH, 2 * NA_WIDTH + RET_WIDTH, KV_COLS, KV_COLS + NA_WIDTH, KV_COLS + NA_WIDTH + RET_WIDTH)
CTX_KV_SPLITS = (NA_WIDTH, 2 * NA_WIDTH, 2 * NA_WIDTH + RET_WIDTH)
CTX_Q_SPLITS = (NA_WIDTH, NA_WIDTH + RET_WIDTH)

kernel_name = 'hybrid_natten_retention_ecmoe_dit_block'


def rms_norm(x, w):
    xf = x.astype(jnp.float32)
    y = xf * lax.rsqrt(jnp.mean(xf * xf, axis=-1, keepdims=True) + EPS)
    return (y * w.astype(jnp.float32)).astype(x.dtype)


def modulate(h, shift, scale):
    return h * (1.0 + scale) + shift


def to_heads(x):
    b, l, w = x.shape
    return x.reshape(b, l, w // HEAD_DIM, HEAD_DIM).transpose(0, 2, 1, 3)


def from_heads(x):
    b, h, l, d = x.shape
    return x.transpose(0, 2, 1, 3).reshape(b, l, h * d)


def _rotate_pairs(x, ang):
    cos, sin = jnp.cos(ang), jnp.sin(ang)
    x1, x2 = jnp.split(x, 2, axis=-1)
    return jnp.concatenate([x1 * cos - x2 * sin, x2 * cos + x1 * sin], axis=-1)


def axial_rope(x, row_pos, col_pos):
    axis_dim = HEAD_DIM // 2
    inv_freq = ROPE_BASE ** (-jnp.arange(0, axis_dim, 2, dtype=jnp.float32) / axis_dim)
    ang_r = row_pos.astype(jnp.float32)[:, None] * inv_freq
    ang_c = col_pos.astype(jnp.float32)[:, None] * inv_freq
    xf = x.astype(jnp.float32)
    out = jnp.concatenate([_rotate_pairs(xf[..., :axis_dim], ang_r), _rotate_pairs(xf[..., axis_dim:], ang_c)], axis=-1)
    return out.astype(x.dtype)


def neighbourhood_attention(q, k, v, k_ctx, v_ctx, rpb, rows):
    b, h, l, dh = q.shape
    kh = min(WIN_ROWS, rows)
    kw = WIN_COLS
    scale = dh ** -0.5
    qg = q.reshape(b, h, rows, GRID_W, dh)
    kg = k.reshape(b, h, rows, GRID_W, dh)
    vg = v.reshape(b, h, rows, GRID_W, dh)
    r = jnp.arange(rows)
    row_start = jnp.clip(r - kh // 2, 0, rows - kh)
    key_rows = row_start[:, None] + jnp.arange(kh)[None, :]
    k_blk = kg[:, :, key_rows]
    v_blk = vg[:, :, key_rows]
    cq = jnp.arange(GRID_W)
    col_start = jnp.clip(cq - kw // 2, 0, GRID_W - kw)
    col_ok = (cq[None, :] >= col_start[:, None]) & (cq[None, :] < col_start[:, None] + kw)
    row_off = key_rows - r[:, None] + (WIN_ROWS - 1)
    col_off = jnp.clip(cq[None, :] - cq[:, None], -(kw - 1), kw - 1) + (WIN_COLS - 1)
    bias = rpb[:, row_off[:, None, :, None], col_off[None, :, None, :]]
    s_win = jnp.einsum('bhrcd,bhrikd->bhrcik', qg, k_blk).astype(jnp.float32) * scale + bias[None].astype(jnp.float32)
    s_win = jnp.where(col_ok[:, None, :], s_win, NEG_INF).reshape(b, h, rows, GRID_W, kh * GRID_W)
    s_ctx = jnp.einsum('bhrcd,bhnd->bhrcn', qg, k_ctx).astype(jnp.float32) * scale
    p = jax.nn.softmax(jnp.concatenate([s_win, s_ctx], axis=-1), axis=-1)
    p_win = p[..., :kh * GRID_W].reshape(b, h, rows, GRID_W, kh, GRID_W).astype(v.dtype)
    p_ctx = p[..., kh * GRID_W:].astype(v.dtype)
    out = jnp.einsum('bhrcik,bhrikd->bhrcd', p_win, v_blk) + jnp.einsum('bhrcn,bhnd->bhrcd', p_ctx, v_ctx)
    return out.reshape(b, h, l, dh)


def context_self_attention(q, k, v):
    s = jnp.einsum('bhnd,bhmd->bhnm', q, k).astype(jnp.float32) * HEAD_DIM ** -0.5
    p = jax.nn.softmax(s, axis=-1).astype(v.dtype)
    return jnp.einsum('bhnm,bhmd->bhnd', p, v)


def retention_chunkwise(q, k, v, log_gamma, state0):
    b, h, l, d = q.shape
    n = l // RET_CHUNK
    pos = jnp.arange(RET_CHUNK, dtype=jnp.float32)
    diff = pos[:, None] - pos[None, :]
    intra = jnp.where(diff >= 0, jnp.exp(log_gamma[:, None, None] * jnp.maximum(diff, 0.0)), 0.0)
    q_decay = jnp.exp(log_gamma[:, None] * (pos + 1.0))
    k_decay = jnp.exp(log_gamma[:, None] * (RET_CHUNK - 1.0 - pos))
    chunk_decay = jnp.exp(log_gamma * RET_CHUNK)[None, :, None, None]
    qc = q.astype(jnp.float32).reshape(b, h, n, RET_CHUNK, d)
    kc = k.astype(jnp.float32).reshape(b, h, n, RET_CHUNK, d)
    vc = v.astype(jnp.float32).reshape(b, h, n, RET_CHUNK, d)
    scores = jnp.einsum('bhnid,bhnjd->bhnij', qc, kc) * intra[None, :, None]
    inner = jnp.einsum('bhnij,bhnje->bhnie', scores, vc)
    chunk_kv = jnp.einsum('bhnjd,bhnje->nbhde', kc * k_decay[None, :, None, :, None], vc)

    def step(state, kv_n):
        return state * chunk_decay + kv_n, state

    _, prev_states = lax.scan(step, state0.astype(jnp.float32), chunk_kv)
    cross = jnp.einsum('bhnid,nbhde->bhnie', qc * q_decay[None, :, None, :, None], prev_states)
    return (inner + cross).reshape(b, h, l, d)


def context_state(k, v, log_gamma, reverse):
    l = k.shape[2]
    m = jnp.arange(l, dtype=jnp.float32)
    dist = m if reverse else (l - 1.0 - m)
    w = jnp.exp(log_gamma[:, None] * dist)
    return jnp.einsum('bhmd,bhme->bhde', k.astype(jnp.float32) * w[None, :, :, None], v.astype(jnp.float32))


def bidirectional_retention(q, k, v, lg_f, lg_b, s_f, s_b):
    fwd = retention_chunkwise(q, k, v, lg_f, s_f)
    bwd = jnp.flip(retention_chunkwise(jnp.flip(q, 2), jnp.flip(k, 2), jnp.flip(v, 2), lg_b, s_b), 2)
    return fwd + bwd


def retention_output(o, g, gn_w):
    mu = jnp.mean(o, axis=-1, keepdims=True)
    var = jnp.mean(jnp.square(o - mu), axis=-1, keepdims=True)
    o = from_heads((o - mu) * lax.rsqrt(var + EPS)) * gn_w.astype(jnp.float32)
    return o.astype(g.dtype) * jax.nn.silu(g)


def expert_choice_ffn(h, w_router, w_gate, w_up, w_down):
    b, n, d = h.shape
    cap = CAPACITY_FACTOR * n // N_EXPERTS
    logits = jnp.einsum('bnd,de->ben', h, w_router).astype(jnp.float32)
    affinity = jax.nn.softmax(logits, axis=1)
    gates, idx = lax.top_k(affinity, cap)
    xe = jax.vmap(lambda hb, ib: hb[ib])(h, idx)
    hid = jax.nn.silu(jnp.einsum('becd,edf->becf', xe, w_gate)) * jnp.einsum('becd,edf->becf', xe, w_up)
    ye = jnp.einsum('becf,efd->becd', hid, w_down) * gates[..., None].astype(h.dtype)

    def combine(idx_b, y_b):
        return jnp.zeros((n, d), y_b.dtype).at[idx_b.reshape(-1)].add(y_b.reshape(-1, d))

    return jax.vmap(combine)(idx, ye)


def setup_inputs(seed: int = 0) -> dict:
    key = jax.random.key(seed)
    ks = jax.random.split(key, 24)
    f32 = jnp.float32

    def nrm(k, shape, scale):
        return jax.random.normal(k, shape, f32) * scale

    gamma0 = 1.0 - 2.0 ** (-5.0 - jnp.arange(RET_HEADS, dtype=f32))
    decay_logit = jnp.log(gamma0) - jnp.log1p(-gamma0)
    return {
        'x': nrm(ks[0], (BATCH, SEQ, D_MODEL), 1.0),
        'c': nrm(ks[1], (BATCH, D_MODEL), 1.0),
        'ctx': nrm(ks[2], (BATCH, CTX_LEN, D_MODEL), 1.0),
        'c_ctx': nrm(ks[3], (D_MODEL,), 1.0),
        'w_mod': nrm(ks[4], (DEPTH, D_MODEL, N_MOD * D_MODEL), 0.5 * D_MODEL ** -0.5),
        'b_mod': nrm(ks[5], (DEPTH, N_MOD * D_MODEL), 0.02),
        'norm_mix_w': 1.0 + nrm(ks[6], (DEPTH, D_MODEL), 0.02),
        'norm_ffn_w': 1.0 + nrm(ks[7], (DEPTH, D_MODEL), 0.02),
        'w_in': nrm(ks[8], (DEPTH, D_MODEL, IN_COLS), D_MODEL ** -0.5),
        'na_rpb': nrm(ks[9], (DEPTH, NA_HEADS, 2 * WIN_ROWS - 1, 2 * WIN_COLS - 1), 0.1),
        'ret_decay_fwd': decay_logit[None] + nrm(ks[10], (DEPTH, RET_HEADS), 0.1),
        'ret_decay_bwd': decay_logit[None] + nrm(ks[11], (DEPTH, RET_HEADS), 0.1),
        'ret_gn_w': 1.0 + nrm(ks[12], (DEPTH, RET_WIDTH), 0.02),
        'w_out': nrm(ks[13], (DEPTH, MIX_WIDTH, D_MODEL), MIX_WIDTH ** -0.5),
        'w_router': nrm(ks[14], (DEPTH, D_MODEL, N_EXPERTS), D_MODEL ** -0.5),
        'w_gate': nrm(ks[15], (DEPTH, N_EXPERTS, D_MODEL, EXPERT_FF), D_MODEL ** -0.5),
        'w_up': nrm(ks[16], (DEPTH, N_EXPERTS, D_MODEL, EXPERT_FF), D_MODEL ** -0.5),
        'w_down': nrm(ks[17], (DEPTH, N_EXPERTS, EXPERT_FF, D_MODEL), EXPERT_FF ** -0.5),
        'final_norm_w': 1.0 + nrm(ks[18], (D_MODEL,), 0.02),
    }


def reference(x, c, ctx, c_ctx, w_mod, b_mod, norm_mix_w, norm_ffn_w, w_in, na_rpb, ret_decay_fwd, ret_decay_bwd, ret_gn_w, w_out, w_router, w_gate, w_up, w_down, final_norm_w):
    seq = x.shape[1]
    rows = seq // GRID_W
    t = jnp.arange(seq)
    row_pos, col_pos = t // GRID_W, t % GRID_W
    h_lat, h_ctx = x, ctx
    for layer in range(DEPTH):
        ctx_out = layer < DEPTH - 1
        mod_lat = jnp.split((jax.nn.silu(c) @ w_mod[layer] + b_mod[layer])[:, None, :], N_MOD, axis=-1)
        mod_ctx = jnp.split(jax.nn.silu(c_ctx) @ w_mod[layer] + b_mod[layer], N_MOD, axis=-1)
        lg_f = jax.nn.log_sigmoid(ret_decay_fwd[layer].astype(jnp.float32))
        lg_b = jax.nn.log_sigmoid(ret_decay_bwd[layer].astype(jnp.float32))

        u_lat = modulate(rms_norm(h_lat, norm_mix_w[layer]), mod_lat[0], mod_lat[1])
        u_ctx = modulate(rms_norm(h_ctx, norm_mix_w[layer]), mod_ctx[0], mod_ctx[1])
        k_na, v_na, k_ret, v_ret, q_na, q_ret, g_ret = jnp.split(u_lat @ w_in[layer], LAT_SPLITS, axis=-1)
        ck_na, cv_na, ck_ret, cv_ret = [to_heads(a) for a in jnp.split(u_ctx @ w_in[layer][:, :KV_COLS], CTX_KV_SPLITS, axis=-1)]
        ck_ret = ck_ret * HEAD_DIM ** -0.5
        s_f = context_state(ck_ret, cv_ret, lg_f, reverse=False)
        s_b = context_state(ck_ret, cv_ret, lg_b, reverse=True)

        na = neighbourhood_attention(to_heads(q_na), to_heads(k_na), to_heads(v_na), ck_na, cv_na, na_rpb[layer], rows)
        q_r = axial_rope(to_heads(q_ret), row_pos, col_pos)
        k_r = axial_rope(to_heads(k_ret), row_pos, col_pos) * HEAD_DIM ** -0.5
        ret = bidirectional_retention(q_r, k_r, to_heads(v_ret), lg_f, lg_b, s_f, s_b)
        mix_lat = jnp.concatenate([from_heads(na), retention_output(ret, g_ret, ret_gn_w[layer])], axis=-1) @ w_out[layer]
        if ctx_out:
            cq_na, cq_ret, cg_ret = jnp.split(u_ctx @ w_in[layer][:, KV_COLS:], CTX_Q_SPLITS, axis=-1)
            zero_state = jnp.zeros_like(s_f)
            na_c = context_self_attention(to_heads(cq_na), ck_na, cv_na)
            ret_c = bidirectional_retention(to_heads(cq_ret), ck_ret, cv_ret, lg_f, lg_b, zero_state, zero_state)
            mix_ctx = jnp.concatenate([from_heads(na_c), retention_output(ret_c, cg_ret, ret_gn_w[layer])], axis=-1) @ w_out[layer]
            h_ctx = h_ctx + mod_ctx[2] * mix_ctx
        h_lat = h_lat + mod_lat[2] * mix_lat

        u2_lat = modulate(rms_norm(h_lat, norm_ffn_w[layer]), mod_lat[3], mod_lat[4])
        h_lat = h_lat + mod_lat[5] * expert_choice_ffn(u2_lat, w_router[layer], w_gate[layer], w_up[layer], w_down[layer])
        if ctx_out:
            u2_ctx = modulate(rms_norm(h_ctx, norm_ffn_w[layer]), mod_ctx[3], mod_ctx[4])
            h_ctx = h_ctx + mod_ctx[5] * expert_choice_ffn(u2_ctx, w_router[layer], w_gate[layer], w_up[layer], w_down[layer])
    return rms_norm(h_lat, final_norm_w)
```

```python
import functools

import jax
import jax.numpy as jnp
from jax import lax
from jax.experimental import pallas as pl
from jax.experimental.pallas import tpu as pltpu

GRID_W = 64
HEAD_DIM = 128
NA_HEADS = 8
RET_HEADS = 8
NA_WIDTH = NA_HEADS * HEAD_DIM
RET_WIDTH = RET_HEADS * HEAD_DIM
WIN_ROWS = 8
WIN_COLS = 16
RET_CHUNK = 128
ROPE_BASE = 10000.0
N_EXPERTS = 16
CAPACITY_FACTOR = 2
N_MOD = 6
EPS = 1e-6
NEG_INF = -1e30
KV_COLS = 2 * NA_WIDTH + 2 * RET_WIDTH

F32 = jnp.float32
BF16 = jnp.bfloat16
MIB = 1024 * 1024
VMEM_LIMIT_V7X = 56 * MIB


def _cparams(n_axes):
    return pltpu.CompilerParams(
        dimension_semantics=("arbitrary",) * n_axes, vmem_limit_bytes=VMEM_LIMIT_V7X)


def _silu(x):
    return x * jax.nn.sigmoid(x)


def _dot(a, b):
    return jnp.dot(a, b, preferred_element_type=F32)


def _dot_nt(a, b):
    return lax.dot_general(a, b, (((1,), (1,)), ((), ())), preferred_element_type=F32)


def _dot_tn(a, b):
    return lax.dot_general(a, b, (((0,), (0,)), ((), ())), preferred_element_type=F32)


def _mod_body(c_ref, w_ref, b_ref, o_ref):
    a = _silu(c_ref[...]).astype(BF16)
    o_ref[...] = _dot(a, w_ref[...].astype(BF16)) + b_ref[...]


def _modulation(cc, w_mod, b_mod, tn=1024):
    rows, d = cc.shape
    n = w_mod.shape[1]
    return pl.pallas_call(
        _mod_body,
        grid=(n // tn,),
        in_specs=[
            pl.BlockSpec((rows, d), lambda j: (0, 0)),
            pl.BlockSpec((d, tn), lambda j: (0, j)),
            pl.BlockSpec((1, tn), lambda j: (0, j)),
        ],
        out_specs=pl.BlockSpec((rows, tn), lambda j: (0, j)),
        out_shape=jax.ShapeDtypeStruct((rows, n), F32),
        compiler_params=_cparams(1),
        name="modulation",
    )(cc, w_mod, b_mod.reshape(1, n))


def _rms_mod(x, nw, shift, scale):
    ms = jnp.mean(x * x, axis=-1, keepdims=True)
    y = x * lax.rsqrt(ms + EPS) * nw
    return y * (1.0 + scale) + shift


def _inproj_body(x_ref, nw_ref, sh_ref, sc_ref, w_ref, o_ref, u_ref):
    @pl.when(pl.program_id(1) == 0)
    def _():
        u_ref[...] = _rms_mod(x_ref[...], nw_ref[...], sh_ref[0], sc_ref[0]).astype(BF16)

    o_ref[...] = _dot(u_ref[...], w_ref[...].astype(BF16)).astype(o_ref.dtype)


def _in_projection(x2d, norm_w, mod3, mod_row_fn, w_in, n_cols, tm, tn):
    rows, d = x2d.shape
    return pl.pallas_call(
        _inproj_body,
        grid=(rows // tm, n_cols // tn),
        in_specs=[
            pl.BlockSpec((tm, d), lambda i, j: (i, 0)),
            pl.BlockSpec((1, d), lambda i, j: (0, 0)),
            pl.BlockSpec((1, 1, d), lambda i, j: (mod_row_fn(i), 0, 0)),
            pl.BlockSpec((1, 1, d), lambda i, j: (mod_row_fn(i), 0, 1)),
            pl.BlockSpec((d, tn), lambda i, j: (0, j)),
        ],
        out_specs=pl.BlockSpec((tm, tn), lambda i, j: (i, j)),
        out_shape=jax.ShapeDtypeStruct((rows, n_cols), BF16),
        scratch_shapes=[pltpu.VMEM((tm, d), BF16)],
        compiler_params=_cparams(2),
        name="in_projection",
    )(x2d, norm_w.reshape(1, d), mod3, mod3, w_in)


NA_QROWS = 8
NA_KROWS = 2 * NA_QROWS


def _na_row_offset(tile_kind, i, w, rows):
    half = WIN_ROWS // 2
    if tile_kind == 0:
        r, key = i, w
    elif tile_kind == 1:
        r, key = NA_QROWS + i, NA_QROWS - half + w
    else:
        r, key = rows - NA_QROWS + i, rows - NA_KROWS + w
    start = min(max(r - half, 0), rows - WIN_ROWS)
    if not (start <= key < start + WIN_ROWS):
        return None
    return key - r + (WIN_ROWS - 1)


def _na_build_bias(rpb_ref, bias_ref, h, rows):
    w = GRID_W
    cq = lax.broadcasted_iota(jnp.int32, (w, 2 * w), 0)
    ck = lax.broadcasted_iota(jnp.int32, (w, 2 * w), 1) % w
    col_start = jnp.clip(cq - WIN_COLS // 2, 0, w - WIN_COLS)
    col_ok = (ck >= col_start) & (ck < col_start + WIN_COLS)
    col_off = jnp.clip(ck - cq, -(WIN_COLS - 1), WIN_COLS - 1) + (WIN_COLS - 1)
    neg = jnp.full((w, 2 * w), NEG_INF, F32)
    n_ro, n_co = 2 * WIN_ROWS - 1, 2 * WIN_COLS - 1
    tabs = []
    for ro in range(n_ro):
        t = jnp.zeros((w, 2 * w), F32)
        for j in range(n_co):
            t = jnp.where(col_off == j, rpb_ref[h, ro * n_co + j], t)
        tabs.append(jnp.where(col_ok, t, neg))
    left = lax.broadcasted_iota(jnp.int32, (w, 2 * w), 1) < w
    for kind in range(3):
        for i in range(NA_QROWS):
            for wp in range(NA_KROWS // 2):
                ra = _na_row_offset(kind, i, 2 * wp, rows)
                rb = _na_row_offset(kind, i, 2 * wp + 1, rows)
                ta = neg if ra is None else tabs[ra]
                tb = neg if rb is None else tabs[rb]
                blk = ta if ra == rb else jnp.where(left, ta, tb)
                bias_ref[kind, i * w:(i + 1) * w, wp * 2 * w:(wp + 1) * 2 * w] = blk


def _na_body(rpb_ref, q_ref, k_ref, v_ref, ck_ref, cv_ref, o_ref, bias_ref, *, rows):
    h = pl.program_id(0)
    w = GRID_W
    tq, tk = NA_QROWS * w, NA_KROWS * w
    n_tiles = rows // NA_QROWS
    scale = HEAD_DIM ** -0.5

    @pl.when(pl.program_id(1) == 0)
    def _():
        _na_build_bias(rpb_ref, bias_ref, h, rows)

    ck = ck_ref[...]
    cv = cv_ref[...]

    def tile(t, carry):
        kind = jnp.where(t == 0, 0, jnp.where(t == n_tiles - 1, 2, 1))
        krow0 = jnp.clip(t * NA_QROWS - WIN_ROWS // 2, 0, rows - NA_KROWS)
        q0 = pl.multiple_of(t * tq, tq)
        k0 = pl.multiple_of(krow0 * w, 4 * w)
        q = q_ref[pl.ds(q0, tq), :]
        kw = k_ref[pl.ds(k0, tk), :]
        vw = v_ref[pl.ds(k0, tk), :]
        s = _dot_nt(q, kw) * scale + bias_ref[kind]
        sc = _dot_nt(q, ck) * scale
        m = jnp.maximum(jnp.max(s, axis=-1, keepdims=True), jnp.max(sc, axis=-1, keepdims=True))
        p = jnp.exp(s - m)
        pc = jnp.exp(sc - m)
        l = jnp.sum(p, axis=-1, keepdims=True) + jnp.sum(pc, axis=-1, keepdims=True)
        o = _dot(p.astype(BF16), vw) + _dot(pc.astype(BF16), cv)
        o_ref[pl.ds(q0, tq), :] = (o / l).astype(o_ref.dtype)
        return carry

    lax.fori_loop(0, n_tiles, tile, 0)


def _neighbourhood_attention(proj, cproj, rpb, batch, seq, ctx_len):
    rows = seq // GRID_W
    hb = NA_WIDTH // HEAD_DIM
    n_rpb = (2 * WIN_ROWS - 1) * (2 * WIN_COLS - 1)
    tq, tk = NA_QROWS * GRID_W, NA_KROWS * GRID_W
    grid_spec = pltpu.PrefetchScalarGridSpec(
        num_scalar_prefetch=0,
        grid=(NA_HEADS, batch),
        in_specs=[
            pl.BlockSpec(memory_space=pltpu.SMEM),
            pl.BlockSpec((seq, HEAD_DIM), lambda h, b: (b, 4 * hb + h)),
            pl.BlockSpec((seq, HEAD_DIM), lambda h, b: (b, h)),
            pl.BlockSpec((seq, HEAD_DIM), lambda h, b: (b, hb + h)),
            pl.BlockSpec((ctx_len, HEAD_DIM), lambda h, b: (b, h)),
            pl.BlockSpec((ctx_len, HEAD_DIM), lambda h, b: (b, hb + h)),
        ],
        out_specs=pl.BlockSpec((seq, HEAD_DIM), lambda h, b: (b, h)),
        scratch_shapes=[pltpu.VMEM((3, tq, tk), F32)],
    )
    return pl.pallas_call(
        functools.partial(_na_body, rows=rows),
        grid_spec=grid_spec,
        out_shape=jax.ShapeDtypeStruct((batch * seq, NA_WIDTH), BF16),
        compiler_params=_cparams(2),
        name="neighbourhood_attention",
    )(rpb.reshape(NA_HEADS, n_rpb), proj, proj, proj, cproj, cproj)


def _log_sigmoid(x):
    return -(jnp.maximum(-x, 0.0) + jnp.log1p(jnp.exp(-jnp.abs(x))))


def _rope(x, cos, sin_signed):
    lane = lax.broadcasted_iota(jnp.int32, x.shape, 1)
    quarter = HEAD_DIM // 4
    partner = jnp.where(lane % (2 * quarter) < quarter,
                        pltpu.roll(x, HEAD_DIM - quarter, 1), pltpu.roll(x, quarter, 1))
    return x * cos + partner * sin_signed


def _ret_body(df_ref, db_ref, gn_ref, cos_ref, sin_ref, q_ref, k_ref, v_ref, g_ref, ck_ref, cv_ref,
              o_ref, qr_ref, kr_ref, sf_ref, sb_ref):
    c, d = RET_CHUNK, HEAD_DIM
    seq = q_ref.shape[0]
    n_chunks = seq // c
    ctx_len = ck_ref.shape[0]
    scale = HEAD_DIM ** -0.5
    lgf = jnp.broadcast_to(_log_sigmoid(df_ref[0]), (c, d))
    lgb = jnp.broadcast_to(_log_sigmoid(db_ref[0]), (c, d))
    pos = lax.broadcasted_iota(jnp.int32, (c, d), 0).astype(F32)
    col = lax.broadcasted_iota(jnp.int32, (c, d), 1).astype(F32)
    kdf = jnp.exp(lgf * (c - 1.0 - pos))
    kdb = jnp.exp(lgb * pos)
    qdf = jnp.exp(lgf * (pos + 1.0))
    qdb = jnp.exp(lgb * (c - pos))
    cdf = jnp.exp(lgf * float(c))
    cdb = jnp.exp(lgb * float(c))
    diff = pos - col
    dmat = (jnp.where(diff >= 0, jnp.exp(lgf * jnp.maximum(diff, 0.0)), 0.0)
            + jnp.where(diff <= 0, jnp.exp(lgb * jnp.maximum(-diff, 0.0)), 0.0))

    cpos = lax.broadcasted_iota(jnp.int32, (ctx_len, d), 0).astype(F32)
    ckf = ck_ref[...].astype(F32) * scale
    cv = cv_ref[...]
    wf = jnp.exp(jnp.broadcast_to(lgf[:1], (ctx_len, d)) * (ctx_len - 1.0 - cpos))
    wb = jnp.exp(jnp.broadcast_to(lgb[:1], (ctx_len, d)) * cpos)
    s_f = _dot_tn((ckf * wf).astype(BF16), cv)
    s_b = _dot_tn((ckf * wb).astype(BF16), cv)

    rope_rows = 4 * c

    def rope_blk(i, carry):
        r0 = pl.multiple_of(i * rope_rows, rope_rows)
        cs = cos_ref[pl.ds(r0, rope_rows), :]
        sn = sin_ref[pl.ds(r0, rope_rows), :]
        qr_ref[pl.ds(r0, rope_rows), :] = _rope(q_ref[pl.ds(r0, rope_rows), :].astype(F32), cs, sn)
        kr_ref[pl.ds(r0, rope_rows), :] = _rope(k_ref[pl.ds(r0, rope_rows), :].astype(F32), cs, sn) * scale
        return carry

    lax.fori_loop(0, seq // rope_rows, rope_blk, 0)

    def scan_blk(n, carry):
        s, t = carry
        r0 = pl.multiple_of(n * c, c)
        kvf = _dot_tn((kr_ref[pl.ds(r0, c), :] * kdf).astype(BF16), v_ref[pl.ds(r0, c), :])
        sf_ref[n] = s.astype(BF16)
        m = n_chunks - 1 - n
        m0 = pl.multiple_of(m * c, c)
        kvb = _dot_tn((kr_ref[pl.ds(m0, c), :] * kdb).astype(BF16), v_ref[pl.ds(m0, c), :])
        sb_ref[m] = t.astype(BF16)
        return s * cdf + kvf, t * cdb + kvb

    lax.fori_loop(0, n_chunks, scan_blk, (s_f, s_b))

    gn = gn_ref[...]

    def out_blk(n, carry):
        r0 = pl.multiple_of(n * c, c)
        qc = qr_ref[pl.ds(r0, c), :]
        kc = kr_ref[pl.ds(r0, c), :]
        a = _dot_nt(qc.astype(BF16), kc.astype(BF16))
        o = (_dot((a * dmat).astype(BF16), v_ref[pl.ds(r0, c), :])
             + _dot((qc * qdf).astype(BF16), sf_ref[n])
             + _dot((qc * qdb).astype(BF16), sb_ref[n]))
        mu = jnp.mean(o, axis=-1, keepdims=True)
        var = jnp.mean(jnp.square(o - mu), axis=-1, keepdims=True)
        y = (o - mu) * lax.rsqrt(var + EPS) * gn
        o_ref[pl.ds(r0, c), :] = (y * _silu(g_ref[pl.ds(r0, c), :].astype(F32))).astype(o_ref.dtype)
        return carry

    lax.fori_loop(0, n_chunks, out_blk, 0)


def _rope_tables(seq):
    axis_dim = HEAD_DIM // 2
    inv_freq = ROPE_BASE ** (-jnp.arange(0, axis_dim, 2, dtype=F32) / axis_dim)
    t = jnp.arange(seq)
    ang_r = (t // GRID_W).astype(F32)[:, None] * inv_freq
    ang_c = (t % GRID_W).astype(F32)[:, None] * inv_freq
    cos = jnp.concatenate([jnp.cos(ang_r), jnp.cos(ang_r), jnp.cos(ang_c), jnp.cos(ang_c)], axis=-1)
    sin = jnp.concatenate([-jnp.sin(ang_r), jnp.sin(ang_r), -jnp.sin(ang_c), jnp.sin(ang_c)], axis=-1)
    return cos, sin


def _retention(proj, cproj, decay_f, decay_b, gn_w, batch, seq, ctx_len):
    hb = RET_WIDTH // HEAD_DIM
    cos, sin = _rope_tables(seq)
    dec_f = jnp.broadcast_to(decay_f.astype(F32)[:, None, None], (RET_HEADS, 1, HEAD_DIM))
    dec_b = jnp.broadcast_to(decay_b.astype(F32)[:, None, None], (RET_HEADS, 1, HEAD_DIM))
    n_chunks = seq // RET_CHUNK
    blk = lambda rows, fn: pl.BlockSpec((rows, HEAD_DIM), fn)
    return pl.pallas_call(
        _ret_body,
        grid=(RET_HEADS, batch),
        in_specs=[
            pl.BlockSpec((1, 1, HEAD_DIM), lambda h, b: (h, 0, 0)),
            pl.BlockSpec((1, 1, HEAD_DIM), lambda h, b: (h, 0, 0)),
            blk(1, lambda h, b: (0, h)),
            blk(seq, lambda h, b: (0, 0)),
            blk(seq, lambda h, b: (0, 0)),
            blk(seq, lambda h, b: (b, 5 * hb + h)),
            blk(seq, lambda h, b: (b, 2 * hb + h)),
            blk(seq, lambda h, b: (b, 3 * hb + h)),
            blk(seq, lambda h, b: (b, 6 * hb + h)),
            blk(ctx_len, lambda h, b: (b, 2 * hb + h)),
            blk(ctx_len, lambda h, b: (b, 3 * hb + h)),
        ],
        out_specs=blk(seq, lambda h, b: (b, h)),
        out_shape=jax.ShapeDtypeStruct((batch * seq, RET_WIDTH), BF16),
        scratch_shapes=[
            pltpu.VMEM((seq, HEAD_DIM), F32),
            pltpu.VMEM((seq, HEAD_DIM), F32),
            pltpu.VMEM((n_chunks, HEAD_DIM, HEAD_DIM), BF16),
            pltpu.VMEM((n_chunks, HEAD_DIM, HEAD_DIM), BF16),
        ],
        compiler_params=_cparams(2),
        name="retention",
    )(dec_f, dec_b, gn_w.reshape(1, RET_WIDTH), cos, sin, proj, proj, proj, proj, cproj, cproj)


def _split_bf16(x):
    hi = x.astype(BF16)
    lo = (x - hi.astype(F32)).astype(BF16)
    return hi, lo


def _outproj_body(na_ref, ret_ref, w_ref, x_ref, gate_ref, nw_ref, sh_ref, sc_ref, wr_ref,
                  h1_ref, u2_ref, lg_ref, h1s_ref, *, n_j):
    j = pl.program_id(1)
    half = na_ref.shape[1]
    w = w_ref[...].astype(BF16)
    mix = _dot(na_ref[...], w[:half]) + _dot(ret_ref[...], w[half:])
    h1 = x_ref[...] + gate_ref[0] * mix
    h1_ref[...] = h1
    h1s_ref[j] = h1

    @pl.when(j == n_j - 1)
    def _():
        full = jnp.concatenate([h1s_ref[jj] for jj in range(n_j)], axis=-1)
        u2 = _rms_mod(full, nw_ref[...], sh_ref[0], sc_ref[0])
        u2_ref[...] = u2
        uh, ul = _split_bf16(u2)
        wh, wl = _split_bf16(wr_ref[...])
        lg_ref[...] = _dot_nt(wh, uh) + (_dot_nt(wl, uh) + _dot_nt(wh, ul))


def _out_projection(na, ret, w_out, x2d, mod3, norm_w, w_router_t, seq, tm=512, tn=512):
    rows, d = x2d.shape
    half = na.shape[1]
    n_j = d // tn
    per_b = seq // tm
    cpd = d // tn
    return pl.pallas_call(
        functools.partial(_outproj_body, n_j=n_j),
        grid=(rows // tm, n_j),
        in_specs=[
            pl.BlockSpec((tm, half), lambda i, j: (i, 0)),
            pl.BlockSpec((tm, half), lambda i, j: (i, 0)),
            pl.BlockSpec((2 * half, tn), lambda i, j: (0, j)),
            pl.BlockSpec((tm, tn), lambda i, j: (i, j)),
            pl.BlockSpec((1, 1, tn), lambda i, j: (i // per_b, 0, 2 * cpd + j)),
            pl.BlockSpec((1, d), lambda i, j: (0, 0)),
            pl.BlockSpec((1, 1, d), lambda i, j: (i // per_b, 0, 3)),
            pl.BlockSpec((1, 1, d), lambda i, j: (i // per_b, 0, 4)),
            pl.BlockSpec((N_EXPERTS, d), lambda i, j: (0, 0)),
        ],
        out_specs=[
            pl.BlockSpec((tm, tn), lambda i, j: (i, j)),
            pl.BlockSpec((tm, d), lambda i, j: (i, 0)),
            pl.BlockSpec((N_EXPERTS, tm), lambda i, j: (0, i)),
        ],
        out_shape=[
            jax.ShapeDtypeStruct((rows, d), F32),
            jax.ShapeDtypeStruct((rows, d), F32),
            jax.ShapeDtypeStruct((N_EXPERTS, rows), F32),
        ],
        scratch_shapes=[pltpu.VMEM((n_j, tm, tn), F32)],
        compiler_params=_cparams(2),
        name="out_projection",
    )(na, ret, w_out, x2d, mod3, norm_w.reshape(1, d), mod3, mod3, w_router_t)


LANES = 128


def _prefix_incl_lanes(x, tri):
    r, l = x.shape
    nb = l // LANES
    xs = jnp.concatenate([x[:, t * LANES:(t + 1) * LANES] for t in range(nb)], axis=0).astype(BF16)
    p = _dot(xs, tri)
    outs, run = [], jnp.zeros((r, 1), F32)
    for t in range(nb):
        blk = p[t * r:(t + 1) * r] + run
        outs.append(blk)
        run = blk[:, LANES - 1:LANES]
    return jnp.concatenate(outs, axis=1)


def _route_body(lg_ref, gidx_ref, spos_ref, gate_ref, rstart_ref, cnt_ref, *, cap):
    b = pl.program_id(0)
    n_e, seq = lg_ref.shape
    kf = float(cap)
    lg = lg_ref[...]
    ex = jnp.exp(lg - jnp.max(lg, axis=0, keepdims=True))
    aff = ex / jnp.sum(ex, axis=0, keepdims=True)

    def cond(c):
        return (c[0] < 4096) & (c[5] > 0.5)

    def step(c):
        it, lo, hi, thr, done, _ = c
        mid = 0.5 * (lo + hi)
        above = jnp.sum((aff > mid).astype(F32), axis=1, keepdims=True)
        hit = above == kf
        stuck = (mid <= lo) | (mid >= hi)
        active = done < 0.5
        thr = jnp.where(active & hit, mid, jnp.where(active & stuck, hi, thr))
        go = active & ~(hit | stuck)
        ge = above >= kf
        lo = jnp.where(go & ge, mid, lo)
        hi = jnp.where(go & ~ge, mid, hi)
        done = jnp.where(active & (hit | stuck), 1.0, done)
        return it + 1, lo, hi, thr, done, jnp.sum(1.0 - done)

    col = lambda v: jnp.full((n_e, 1), v, F32)
    init = (jnp.int32(0), col(-1.0), col(2.0), col(0.0), col(0.0), jnp.float32(n_e))
    thr = lax.while_loop(cond, step, init)[3]

    ii = lax.broadcasted_iota(jnp.int32, (LANES, LANES), 0)
    jj = lax.broadcasted_iota(jnp.int32, (LANES, LANES), 1)
    tri = (ii <= jj).astype(BF16)
    gt = aff > thr
    eq = (aff == thr).astype(F32)
    need = kf - jnp.sum(gt.astype(F32), axis=1, keepdims=True)
    eq_before = _prefix_incl_lanes(eq, tri) - eq
    mask = jnp.where(gt | ((eq > 0.5) & (eq_before < need)), 1.0, 0.0)

    slot = _prefix_incl_lanes(mask, tri) - mask
    cnt = jnp.sum(mask, axis=0, keepdims=True)
    row0 = _prefix_incl_lanes(jnp.broadcast_to(cnt, (8, seq)), tri)[:1] - cnt
    ei = lax.broadcasted_iota(jnp.int32, (n_e, n_e), 0)
    ej = lax.broadcasted_iota(jnp.int32, (n_e, n_e), 1)
    rank = _dot((ej < ei).astype(BF16), mask.astype(BF16))
    base = (b * (n_e * cap)).astype(F32)
    pos = row0 + rank + base
    rstart_ref[0] = (row0 + base).astype(jnp.int32)
    cnt_ref[0] = cnt.astype(jnp.int32)

    tok = lax.broadcasted_iota(jnp.int32, (1, seq), 1).astype(F32)
    tok_hi = jnp.floor(tok * (1.0 / 64))
    tok_lo = tok - 64.0 * tok_hi
    sub = lax.broadcasted_iota(jnp.int32, (8, seq), 0)
    jrow = lax.broadcasted_iota(jnp.int32, (cap, seq), 0).astype(F32)
    for e in range(n_e):
        sel = ((slot[e:e + 1] == jrow) & (mask[e:e + 1] > 0.5)).astype(BF16)
        pe = pos[e:e + 1]
        p_hi = jnp.floor(pe * (1.0 / 128))
        p_lo = pe - 128.0 * p_hi
        a = aff[e:e + 1]
        a_hi = a.astype(BF16).astype(F32)
        a_mid = (a - a_hi).astype(BF16).astype(F32)
        a_lo = a - a_hi - a_mid
        rows = (tok_hi, tok_lo, p_hi, p_lo, a_hi, a_mid, a_lo)
        stack = jnp.zeros((8, seq), F32)
        for r, v in enumerate(rows):
            stack = jnp.where(sub == r, v, stack)
        got = _dot_nt(stack.astype(BF16), sel)
        gidx_ref[0, e:e + 1, :] = (got[0:1] * 64.0 + got[1:2]).astype(jnp.int32) + b * seq
        spos_ref[0, e:e + 1, :] = (got[2:3] * 128.0 + got[3:4]).astype(jnp.int32)
        gate_ref[0, e:e + 1, :] = got[4:5] + got[5:6] + got[6:7]


def _routing(logits_t, batch, seq):
    n_e = logits_t.shape[0]
    cap = CAPACITY_FACTOR * seq // n_e
    bec = pl.BlockSpec((1, n_e, cap), lambda b: (b, 0, 0))
    b1l = pl.BlockSpec((1, 1, seq), lambda b: (b, 0, 0))
    return pl.pallas_call(
        functools.partial(_route_body, cap=cap),
        grid=(batch,),
        in_specs=[pl.BlockSpec((n_e, seq), lambda b: (0, b))],
        out_specs=[bec, bec, bec, b1l, b1l],
        out_shape=[
            jax.ShapeDtypeStruct((batch, n_e, cap), jnp.int32),
            jax.ShapeDtypeStruct((batch, n_e, cap), jnp.int32),
            jax.ShapeDtypeStruct((batch, n_e, cap), F32),
            jax.ShapeDtypeStruct((batch, 1, seq), jnp.int32),
            jax.ShapeDtypeStruct((batch, 1, seq), jnp.int32),
        ],
        compiler_params=_cparams(1),
        name="routing",
    )(logits_t)


def _moe_body(gidx_ref, gnext_ref, spos_ref, gate_ref, u2_hbm, wg_ref, wu_ref, wd_ref, y_hbm,
              stage_ref, xe_ref, acc_ref, gsem, ssem, *, n_f):
    e = pl.program_id(0)
    f = pl.program_id(1)
    n_e = pl.num_programs(0)
    m = stage_ref.shape[0]

    def gather_copy(idx_ref, j):
        return pltpu.make_async_copy(u2_hbm.at[pl.ds(idx_ref[0, j], 1)], stage_ref.at[pl.ds(j, 1)], gsem)

    def scatter_copy(j):
        return pltpu.make_async_copy(acc_ref.at[pl.ds(j, 1)], y_hbm.at[pl.ds(spos_ref[0, j], 1)], ssem)

    def start_gather(idx_ref):
        def body(j, c):
            gather_copy(idx_ref, j).start()
            return c
        lax.fori_loop(0, m, body, 0, unroll=8)

    def wait_all(copy_fn):
        def body(j, c):
            copy_fn(j).wait()
            return c
        lax.fori_loop(0, m, body, 0, unroll=8)

    @pl.when(f == 0)
    def _():
        @pl.when(e == 0)
        def _():
            start_gather(gidx_ref)

        wait_all(lambda j: gather_copy(gidx_ref, j))
        xe_ref[...] = stage_ref[...].astype(BF16)

        @pl.when(e + 1 < n_e)
        def _():
            start_gather(gnext_ref)

        @pl.when(e > 0)
        def _():
            wait_all(scatter_copy)

        acc_ref[...] = jnp.zeros_like(acc_ref)

    x = xe_ref[...]
    hid = _silu(_dot(x, wg_ref[...].astype(BF16))) * _dot(x, wu_ref[...].astype(BF16))
    acc_ref[...] += _dot(hid.astype(BF16), wd_ref[...].astype(BF16))

    @pl.when(f == n_f - 1)
    def _():
        acc_ref[...] = acc_ref[...] * gate_ref[...]

        def body(j, c):
            scatter_copy(j).start()
            return c
        lax.fori_loop(0, m, body, 0, unroll=8)

        @pl.when(e == n_e - 1)
        def _():
            wait_all(scatter_copy)


def _moe_ffn(gidx, spos, gates, u2, w_gate, w_up, w_down, tf=256):
    n_e, _, m = gidx.shape
    d = u2.shape[1]
    ff = w_gate.shape[2]
    n_f = ff // tf
    smem = lambda fn: pl.BlockSpec((None, 1, m), fn, memory_space=pltpu.SMEM)
    return pl.pallas_call(
        functools.partial(_moe_body, n_f=n_f),
        grid=(n_e, n_f),
        in_specs=[
            smem(lambda e, f: (e, 0, 0)),
            smem(lambda e, f: (jnp.minimum(e + 1, n_e - 1), 0, 0)),
            smem(lambda e, f: (e, 0, 0)),
            pl.BlockSpec((None, m, 1), lambda e, f: (e, 0, 0)),
            pl.BlockSpec(memory_space=pl.ANY),
            pl.BlockSpec((None, d, tf), lambda e, f: (e, 0, f)),
            pl.BlockSpec((None, d, tf), lambda e, f: (e, 0, f)),
            pl.BlockSpec((None, tf, d), lambda e, f: (e, f, 0)),
        ],
        out_specs=pl.BlockSpec(memory_space=pl.ANY),
        out_shape=jax.ShapeDtypeStruct((n_e * m, d), F32),
        scratch_shapes=[
            pltpu.VMEM((m, d), F32),
            pltpu.VMEM((m, d), BF16),
            pltpu.VMEM((m, d), F32),
            pltpu.SemaphoreType.DMA,
            pltpu.SemaphoreType.DMA,
        ],
        compiler_params=_cparams(2),
        name="moe_ffn",
    )(gidx, gidx, spos, gates, u2, w_gate, w_up, w_down)


COMBINE_TOKENS = 256
COMBINE_ROWS = 256


def _combine_body(r_ref, h1_ref, y_hbm, rs_ref, cnt_ref, gate_ref, nw_ref, o_ref, buf_ref, acc_ref, sem):
    i = pl.program_id(0)
    t, ch = COMBINE_TOKENS, COMBINE_ROWS
    lo = r_ref[i * t] // ch
    hi = jnp.maximum((r_ref[(i + 1) * t] + ch - 1) // ch, lo)

    def copy(c, slot):
        return pltpu.make_async_copy(y_hbm.at[pl.ds(pl.multiple_of(c * ch, ch), ch)], buf_ref.at[slot], sem.at[slot])

    @pl.when(hi > lo)
    def _():
        copy(lo, 0).start()

    acc_ref[...] = jnp.zeros_like(acc_ref)
    first = rs_ref[0]
    last = first + cnt_ref[0]

    def chunk(c, carry):
        slot = (c - lo) % 2

        @pl.when(c + 1 < hi)
        def _():
            copy(c + 1, 1 - slot).start()

        copy(c, slot).wait()
        row = c * ch + lax.broadcasted_iota(jnp.int32, (ch, t), 0)
        sel = ((row >= first) & (row < last)).astype(BF16)
        acc_ref[...] += _dot_tn(sel, buf_ref[slot].astype(BF16))
        return carry

    lax.fori_loop(lo, hi, chunk, 0)
    h2 = h1_ref[...] + gate_ref[0] * acc_ref[...]
    ms = jnp.mean(h2 * h2, axis=-1, keepdims=True)
    o_ref[...] = h2 * lax.rsqrt(ms + EPS) * nw_ref[...]


def _combine(row_prefix, h1, y, rstart, cnt, mod3, final_w, seq):
    rows, d = h1.shape
    t = COMBINE_TOKENS
    per_b = seq // t
    grid_spec = pltpu.PrefetchScalarGridSpec(
        num_scalar_prefetch=1,
        grid=(rows // t,),
        in_specs=[
            pl.BlockSpec((t, d), lambda i, r: (i, 0)),
            pl.BlockSpec(memory_space=pl.ANY),
            pl.BlockSpec((1, 1, t), lambda i, r: (i, 0, 0)),
            pl.BlockSpec((1, 1, t), lambda i, r: (i, 0, 0)),
            pl.BlockSpec((1, 1, d), lambda i, r: (i // per_b, 0, 5)),
            pl.BlockSpec((1, d), lambda i, r: (0, 0)),
        ],
        out_specs=pl.BlockSpec((t, d), lambda i, r: (i, 0)),
        scratch_shapes=[
            pltpu.VMEM((2, COMBINE_ROWS, d), F32),
            pltpu.VMEM((t, d), F32),
            pltpu.SemaphoreType.DMA((2,)),
        ],
    )
    return pl.pallas_call(
        _combine_body,
        grid_spec=grid_spec,
        out_shape=jax.ShapeDtypeStruct((rows, d), F32),
        compiler_params=_cparams(1),
        name="combine",
    )(row_prefix, h1, y, rstart, cnt, mod3, final_w.reshape(1, d))


def kernel(x, c, ctx, c_ctx, w_mod, b_mod, norm_mix_w, norm_ffn_w, w_in, na_rpb, ret_decay_fwd,
           ret_decay_bwd, ret_gn_w, w_out, w_router, w_gate, w_up, w_down, final_norm_w):
    batch, seq, d = x.shape
    ctx_len = ctx.shape[1]
    assert w_mod.shape[0] == 1, "one trunk layer"
    assert seq % (NA_QROWS * GRID_W) == 0 and seq // GRID_W >= 3 * NA_QROWS
    n_e = w_router.shape[2]
    cap = CAPACITY_FACTOR * seq // n_e

    mod_rows = 8
    cc = jnp.concatenate([c, c_ctx[None], jnp.zeros((mod_rows - batch - 1, d), c.dtype)], axis=0)
    mod3 = _modulation(cc, w_mod[0], b_mod[0]).reshape(mod_rows, 1, N_MOD * d)

    x2d = x.reshape(batch * seq, d)
    tm = 1024
    proj = _in_projection(x2d, norm_mix_w[0], mod3, lambda i: i // (seq // tm), w_in[0],
                          w_in.shape[2], tm, 1024)
    cproj = _in_projection(ctx.reshape(batch * ctx_len, d), norm_mix_w[0], mod3, lambda i: batch,
                           w_in[0], KV_COLS, batch * ctx_len, 1024)

    na = _neighbourhood_attention(proj, cproj, na_rpb[0], batch, seq, ctx_len)
    ret = _retention(proj, cproj, ret_decay_fwd[0], ret_decay_bwd[0], ret_gn_w[0], batch, seq, ctx_len)

    h1, u2, logits_t = _out_projection(na, ret, w_out[0], x2d, mod3, norm_ffn_w[0], w_router[0].T, seq)

    gidx, spos, gates, rstart, cnt = _routing(logits_t, batch, seq)
    per_expert = lambda a: a.transpose(1, 0, 2).reshape(n_e, batch * cap)
    y = _moe_ffn(per_expert(gidx)[:, None, :], per_expert(spos)[:, None, :], per_expert(gates)[:, :, None],
                 u2, w_gate[0], w_up[0], w_down[0])

    row_prefix = jnp.concatenate([rstart.reshape(-1), jnp.full((1,), batch * n_e * cap, jnp.int32)])
    t = COMBINE_TOKENS
    out = _combine(row_prefix, h1, y, rstart.reshape(batch * seq // t, 1, t), cnt.reshape(batch * seq // t, 1, t),
                   mod3, final_norm_w, seq)
    return out.reshape(batch, seq, d)
```

```python
import functools

import jax
import jax.numpy as jnp
from jax import lax
from jax.experimental import pallas as pl
from jax.experimental.pallas import tpu as pltpu

GRID_W = 64
HEAD_DIM = 128
NA_HEADS = 8
RET_HEADS = 8
NA_WIDTH = NA_HEADS * HEAD_DIM
RET_WIDTH = RET_HEADS * HEAD_DIM
WIN_ROWS = 8
WIN_COLS = 16
RET_BLOCK = 256
ROPE_BASE = 10000.0
N_EXPERTS = 16
CAPACITY_FACTOR = 2
N_MOD = 6
EPS = 1e-6
NEG_INF = -1e30
KV_COLS = 2 * NA_WIDTH + 2 * RET_WIDTH

F32 = jnp.float32
BF16 = jnp.bfloat16
MIB = 1024 * 1024
VMEM_LIMIT_V7X = 56 * MIB


def _cparams(n_axes):
    return pltpu.CompilerParams(
        dimension_semantics=("arbitrary",) * n_axes, vmem_limit_bytes=VMEM_LIMIT_V7X)


def _silu(x):
    return x * jax.nn.sigmoid(x)


def _dot(a, b):
    return jnp.dot(a, b, preferred_element_type=F32)


def _dot_nt(a, b):
    return lax.dot_general(a, b, (((1,), (1,)), ((), ())), preferred_element_type=F32)


def _dot_tn(a, b):
    return lax.dot_general(a, b, (((0,), (0,)), ((), ())), preferred_element_type=F32)


def _mod_body(c_ref, w_ref, b_ref, o_ref):
    a = _silu(c_ref[...]).astype(BF16)
    o_ref[...] = _dot(a, w_ref[...].astype(BF16)) + b_ref[...]


def _modulation(cc, w_mod, b_mod, tn=1024):
    rows, d = cc.shape
    n = w_mod.shape[1]
    return pl.pallas_call(
        _mod_body,
        grid=(n // tn,),
        in_specs=[
            pl.BlockSpec((rows, d), lambda j: (0, 0)),
            pl.BlockSpec((d, tn), lambda j: (0, j)),
            pl.BlockSpec((1, tn), lambda j: (0, j)),
        ],
        out_specs=pl.BlockSpec((rows, tn), lambda j: (0, j)),
        out_shape=jax.ShapeDtypeStruct((rows, n), F32),
        compiler_params=_cparams(1),
        name="modulation",
    )(cc, w_mod, b_mod.reshape(1, n))


def _rms_mod(x, nw, shift, scale):
    ms = jnp.mean(x * x, axis=-1, keepdims=True)
    y = x * lax.rsqrt(ms + EPS) * nw
    return y * (1.0 + scale) + shift


def _inproj_body(x_ref, nw_ref, sh_ref, sc_ref, w_ref, o_ref, u_ref):
    @pl.when(pl.program_id(1) == 0)
    def _():
        u_ref[...] = _rms_mod(x_ref[...], nw_ref[...], sh_ref[0], sc_ref[0]).astype(BF16)

    o_ref[...] = _dot(u_ref[...], w_ref[...].astype(BF16)).astype(o_ref.dtype)


def _in_projection(x2d, norm_w, mod3, mod_row_fn, w_in, n_cols, tm, tn):
    rows, d = x2d.shape
    return pl.pallas_call(
        _inproj_body,
        grid=(rows // tm, n_cols // tn),
        in_specs=[
            pl.BlockSpec((tm, d), lambda i, j: (i, 0)),
            pl.BlockSpec((1, d), lambda i, j: (0, 0)),
            pl.BlockSpec((1, 1, d), lambda i, j: (mod_row_fn(i), 0, 0)),
            pl.BlockSpec((1, 1, d), lambda i, j: (mod_row_fn(i), 0, 1)),
            pl.BlockSpec((d, tn), lambda i, j: (0, j)),
        ],
        out_specs=pl.BlockSpec((tm, tn), lambda i, j: (i, j)),
        out_shape=jax.ShapeDtypeStruct((rows, n_cols), BF16),
        scratch_shapes=[pltpu.VMEM((tm, d), BF16)],
        compiler_params=_cparams(2),
        name="in_projection",
    )(x2d, norm_w.reshape(1, d), mod3, mod3, w_in)


NA_QROWS = 8
NA_KROWS = 2 * NA_QROWS


def _na_row_offset(tile_kind, i, w, rows):
    half = WIN_ROWS // 2
    if tile_kind == 0:
        r, key = i, w
    elif tile_kind == 1:
        r, key = NA_QROWS + i, NA_QROWS - half + w
    else:
        r, key = rows - NA_QROWS + i, rows - NA_KROWS + w
    start = min(max(r - half, 0), rows - WIN_ROWS)
    if not (start <= key < start + WIN_ROWS):
        return None
    return key - r + (WIN_ROWS - 1)


def _na_build_bias(rpb_ref, bias_ref, h, rows):
    w = GRID_W
    cq = lax.broadcasted_iota(jnp.int32, (w, 2 * w), 0)
    ck = lax.broadcasted_iota(jnp.int32, (w, 2 * w), 1) % w
    col_start = jnp.clip(cq - WIN_COLS // 2, 0, w - WIN_COLS)
    col_ok = (ck >= col_start) & (ck < col_start + WIN_COLS)
    col_off = jnp.clip(ck - cq, -(WIN_COLS - 1), WIN_COLS - 1) + (WIN_COLS - 1)
    neg = jnp.full((w, 2 * w), NEG_INF, F32)
    n_ro, n_co = 2 * WIN_ROWS - 1, 2 * WIN_COLS - 1
    tabs = []
    for ro in range(n_ro):
        t = jnp.zeros((w, 2 * w), F32)
        for j in range(n_co):
            t = jnp.where(col_off == j, rpb_ref[h, ro * n_co + j], t)
        tabs.append(jnp.where(col_ok, t, neg))
    left = lax.broadcasted_iota(jnp.int32, (w, 2 * w), 1) < w
    for kind in range(3):
        for i in range(NA_QROWS):
            for wp in range(NA_KROWS // 2):
                ra = _na_row_offset(kind, i, 2 * wp, rows)
                rb = _na_row_offset(kind, i, 2 * wp + 1, rows)
                ta = neg if ra is None else tabs[ra]
                tb = neg if rb is None else tabs[rb]
                blk = ta if ra == rb else jnp.where(left, ta, tb)
                bias_ref[kind, i * w:(i + 1) * w, wp * 2 * w:(wp + 1) * 2 * w] = blk


def _na_body(rpb_ref, q_ref, k_ref, v_ref, ck_ref, cv_ref, o_ref, bias_ref, *, rows):
    h = pl.program_id(0)
    w = GRID_W
    tq, tk = NA_QROWS * w, NA_KROWS * w
    n_tiles = rows // NA_QROWS
    scale = HEAD_DIM ** -0.5

    @pl.when(pl.program_id(1) == 0)
    def _():
        _na_build_bias(rpb_ref, bias_ref, h, rows)

    ck = ck_ref[...]
    cv = cv_ref[...]

    def tile(t, carry):
        kind = jnp.where(t == 0, 0, jnp.where(t == n_tiles - 1, 2, 1))
        krow0 = jnp.clip(t * NA_QROWS - WIN_ROWS // 2, 0, rows - NA_KROWS)
        q0 = pl.multiple_of(t * tq, tq)
        k0 = pl.multiple_of(krow0 * w, 4 * w)
        q = q_ref[pl.ds(q0, tq), :]
        kw = k_ref[pl.ds(k0, tk), :]
        vw = v_ref[pl.ds(k0, tk), :]
        s = _dot_nt(q, kw) * scale + bias_ref[kind]
        sc = _dot_nt(q, ck) * scale
        m = jnp.maximum(jnp.max(s, axis=-1, keepdims=True), jnp.max(sc, axis=-1, keepdims=True))
        p = jnp.exp(s - m)
        pc = jnp.exp(sc - m)
        l = jnp.sum(p, axis=-1, keepdims=True) + jnp.sum(pc, axis=-1, keepdims=True)
        o = _dot(p.astype(BF16), vw) + _dot(pc.astype(BF16), cv)
        o_ref[pl.ds(q0, tq), :] = (o / l).astype(o_ref.dtype)
        return carry

    lax.fori_loop(0, n_tiles, tile, 0)


def _neighbourhood_attention(proj, cproj, rpb, batch, seq, ctx_len):
    rows = seq // GRID_W
    hb = NA_WIDTH // HEAD_DIM
    n_rpb = (2 * WIN_ROWS - 1) * (2 * WIN_COLS - 1)
    tq, tk = NA_QROWS * GRID_W, NA_KROWS * GRID_W
    grid_spec = pltpu.PrefetchScalarGridSpec(
        num_scalar_prefetch=0,
        grid=(NA_HEADS, batch),
        in_specs=[
            pl.BlockSpec(memory_space=pltpu.SMEM),
            pl.BlockSpec((seq, HEAD_DIM), lambda h, b: (b, 4 * hb + h)),
            pl.BlockSpec((seq, HEAD_DIM), lambda h, b: (b, h)),
            pl.BlockSpec((seq, HEAD_DIM), lambda h, b: (b, hb + h)),
            pl.BlockSpec((ctx_len, HEAD_DIM), lambda h, b: (b, h)),
            pl.BlockSpec((ctx_len, HEAD_DIM), lambda h, b: (b, hb + h)),
        ],
        out_specs=pl.BlockSpec((seq, HEAD_DIM), lambda h, b: (b, h)),
        scratch_shapes=[pltpu.VMEM((3, tq, tk), F32)],
    )
    return pl.pallas_call(
        functools.partial(_na_body, rows=rows),
        grid_spec=grid_spec,
        out_shape=jax.ShapeDtypeStruct((batch * seq, NA_WIDTH), BF16),
        compiler_params=_cparams(2),
        name="neighbourhood_attention",
    )(rpb.reshape(NA_HEADS, n_rpb), proj, proj, proj, cproj, cproj)


def _log_sigmoid(x):
    return -(jnp.maximum(-x, 0.0) + jnp.log1p(jnp.exp(-jnp.abs(x))))


def _rope(x, cos, sin_signed):
    lane = lax.broadcasted_iota(jnp.int32, x.shape, 1)
    quarter = HEAD_DIM // 4
    partner = jnp.where(lane % (2 * quarter) < quarter,
                        pltpu.roll(x, HEAD_DIM - quarter, 1), pltpu.roll(x, quarter, 1))
    return x * cos + partner * sin_signed


def _ret_body(df_ref, db_ref, gn_ref, cos_ref, sin_ref, q_ref, k_ref, v_ref, g_ref, ck_ref, cv_ref,
              o_ref, qr_ref, kr_ref, sf_ref, sb_ref):
    c, d = RET_BLOCK, HEAD_DIM
    seq = q_ref.shape[0]
    n_chunks = seq // c
    ctx_len = ck_ref.shape[0]
    scale = HEAD_DIM ** -0.5
    lgf_row = _log_sigmoid(df_ref[0])
    lgb_row = _log_sigmoid(db_ref[0])
    lgf = jnp.broadcast_to(lgf_row, (c, d))
    lgb = jnp.broadcast_to(lgb_row, (c, d))
    pos = lax.broadcasted_iota(jnp.int32, (c, d), 0).astype(F32)
    kdf = jnp.exp(lgf * (c - 1.0 - pos))
    kdb = jnp.exp(lgb * pos)
    qdf = jnp.exp(lgf * (pos + 1.0))
    qdb = jnp.exp(lgb * (c - pos))
    cdf = jnp.exp(lgf_row * float(c))
    cdb = jnp.exp(lgb_row * float(c))
    diff = (lax.broadcasted_iota(jnp.int32, (c, c), 0) - lax.broadcasted_iota(jnp.int32, (c, c), 1)).astype(F32)
    lgf_cc = jnp.broadcast_to(lgf_row[:, :1], (c, c))
    lgb_cc = jnp.broadcast_to(lgb_row[:, :1], (c, c))
    dmat = (jnp.where(diff >= 0, jnp.exp(lgf_cc * jnp.maximum(diff, 0.0)), 0.0)
            + jnp.where(diff <= 0, jnp.exp(lgb_cc * jnp.maximum(-diff, 0.0)), 0.0))

    cpos = lax.broadcasted_iota(jnp.int32, (ctx_len, d), 0).astype(F32)
    ckf = ck_ref[...].astype(F32) * scale
    cv = cv_ref[...]
    wf = jnp.exp(jnp.broadcast_to(lgf_row, (ctx_len, d)) * (ctx_len - 1.0 - cpos))
    wb = jnp.exp(jnp.broadcast_to(lgb_row, (ctx_len, d)) * cpos)
    s_f = _dot_tn((ckf * wf).astype(BF16), cv)
    s_b = _dot_tn((ckf * wb).astype(BF16), cv)

    rope_rows = 512

    def rope_blk(i, carry):
        r0 = pl.multiple_of(i * rope_rows, rope_rows)
        cs = cos_ref[pl.ds(r0, rope_rows), :]
        sn = sin_ref[pl.ds(r0, rope_rows), :]
        qr_ref[pl.ds(r0, rope_rows), :] = _rope(q_ref[pl.ds(r0, rope_rows), :].astype(F32), cs, sn)
        kr_ref[pl.ds(r0, rope_rows), :] = _rope(k_ref[pl.ds(r0, rope_rows), :].astype(F32), cs, sn) * scale
        return carry

    lax.fori_loop(0, seq // rope_rows, rope_blk, 0)

    def scan_blk(n, carry):
        s, t = carry
        r0 = pl.multiple_of(n * c, c)
        kvf = _dot_tn((kr_ref[pl.ds(r0, c), :] * kdf).astype(BF16), v_ref[pl.ds(r0, c), :])
        sf_ref[n] = s.astype(BF16)
        m = n_chunks - 1 - n
        m0 = pl.multiple_of(m * c, c)
        kvb = _dot_tn((kr_ref[pl.ds(m0, c), :] * kdb).astype(BF16), v_ref[pl.ds(m0, c), :])
        sb_ref[m] = t.astype(BF16)
        return s * cdf + kvf, t * cdb + kvb

    lax.fori_loop(0, n_chunks, scan_blk, (s_f, s_b), unroll=2)

    gn = gn_ref[...]

    def out_blk(n, carry):
        r0 = pl.multiple_of(n * c, c)
        qc = qr_ref[pl.ds(r0, c), :]
        kc = kr_ref[pl.ds(r0, c), :]
        a = _dot_nt(qc.astype(BF16), kc.astype(BF16))
        o = (_dot((a * dmat).astype(BF16), v_ref[pl.ds(r0, c), :])
             + _dot((qc * qdf).astype(BF16), sf_ref[n])
             + _dot((qc * qdb).astype(BF16), sb_ref[n]))
        mu = jnp.mean(o, axis=-1, keepdims=True)
        var = jnp.mean(jnp.square(o - mu), axis=-1, keepdims=True)
        y = (o - mu) * lax.rsqrt(var + EPS) * gn
        o_ref[pl.ds(r0, c), :] = (y * _silu(g_ref[pl.ds(r0, c), :].astype(F32))).astype(o_ref.dtype)
        return carry

    lax.fori_loop(0, n_chunks, out_blk, 0, unroll=2)


def _rope_tables(seq):
    axis_dim = HEAD_DIM // 2
    inv_freq = ROPE_BASE ** (-jnp.arange(0, axis_dim, 2, dtype=F32) / axis_dim)
    rows = seq // GRID_W
    ang_r = jnp.arange(rows, dtype=F32)[:, None] * inv_freq
    ang_c = jnp.arange(GRID_W, dtype=F32)[:, None] * inv_freq
    by_row = lambda a: jnp.repeat(a, GRID_W, axis=0)
    by_col = lambda a: jnp.tile(a, (rows, 1))
    cr, sr, cc, sc = by_row(jnp.cos(ang_r)), by_row(jnp.sin(ang_r)), by_col(jnp.cos(ang_c)), by_col(jnp.sin(ang_c))
    return jnp.concatenate([cr, cr, cc, cc], axis=-1), jnp.concatenate([-sr, sr, -sc, sc], axis=-1)


def _retention(proj, cproj, decay_f, decay_b, gn_w, batch, seq, ctx_len):
    hb = RET_WIDTH // HEAD_DIM
    cos, sin = _rope_tables(seq)
    dec_f = jnp.broadcast_to(decay_f.astype(F32)[:, None, None], (RET_HEADS, 1, HEAD_DIM))
    dec_b = jnp.broadcast_to(decay_b.astype(F32)[:, None, None], (RET_HEADS, 1, HEAD_DIM))
    n_chunks = seq // RET_BLOCK
    blk = lambda rows, fn: pl.BlockSpec((rows, HEAD_DIM), fn)
    return pl.pallas_call(
        _ret_body,
        grid=(RET_HEADS, batch),
        in_specs=[
            pl.BlockSpec((1, 1, HEAD_DIM), lambda h, b: (h, 0, 0)),
            pl.BlockSpec((1, 1, HEAD_DIM), lambda h, b: (h, 0, 0)),
            blk(1, lambda h, b: (0, h)),
            blk(seq, lambda h, b: (0, 0)),
            blk(seq, lambda h, b: (0, 0)),
            blk(seq, lambda h, b: (b, 5 * hb + h)),
            blk(seq, lambda h, b: (b, 2 * hb + h)),
            blk(seq, lambda h, b: (b, 3 * hb + h)),
            blk(seq, lambda h, b: (b, 6 * hb + h)),
            blk(ctx_len, lambda h, b: (b, 2 * hb + h)),
            blk(ctx_len, lambda h, b: (b, 3 * hb + h)),
        ],
        out_specs=blk(seq, lambda h, b: (b, h)),
        out_shape=jax.ShapeDtypeStruct((batch * seq, RET_WIDTH), BF16),
        scratch_shapes=[
            pltpu.VMEM((seq, HEAD_DIM), F32),
            pltpu.VMEM((seq, HEAD_DIM), F32),
            pltpu.VMEM((n_chunks, HEAD_DIM, HEAD_DIM), BF16),
            pltpu.VMEM((n_chunks, HEAD_DIM, HEAD_DIM), BF16),
        ],
        compiler_params=_cparams(2),
        name="retention",
    )(dec_f, dec_b, gn_w.reshape(1, RET_WIDTH), cos, sin, proj, proj, proj, proj, cproj, cproj)


def _split_bf16(x):
    hi = x.astype(BF16)
    lo = (x - hi.astype(F32)).astype(BF16)
    return hi, lo


def _cast_body(x_ref, o_ref):
    o_ref[...] = x_ref[...].astype(o_ref.dtype)


def _cast_bf16(w, tm=512):
    rows, cols = w.shape
    return pl.pallas_call(
        _cast_body,
        grid=(rows // tm,),
        in_specs=[pl.BlockSpec((tm, cols), lambda i: (i, 0))],
        out_specs=pl.BlockSpec((tm, cols), lambda i: (i, 0)),
        out_shape=jax.ShapeDtypeStruct((rows, cols), BF16),
        compiler_params=_cparams(1),
        name="cast_bf16",
    )(w)


def _outproj_body(na_ref, ret_ref, w_ref, x_ref, gate_ref, nw_ref, sh_ref, sc_ref, wr_ref,
                  h1_ref, u2_ref, lg_ref):
    half = na_ref.shape[1]
    n_e = wr_ref.shape[0]
    mix = _dot(na_ref[...], w_ref[:half, :]) + _dot(ret_ref[...], w_ref[half:, :])
    h1 = x_ref[...] + gate_ref[0] * mix
    h1_ref[...] = h1
    u2 = _rms_mod(h1, nw_ref[...], sh_ref[0], sc_ref[0])
    u2_ref[...] = u2
    uh, ul = _split_bf16(u2)
    wh, wl = _split_bf16(wr_ref[...])
    both = _dot_nt(jnp.concatenate([wh, wl], axis=0), uh)
    lg_ref[...] = both[:n_e] + (both[n_e:] + _dot_nt(wh, ul))


def _out_projection(na, ret, w_out_bf16, x2d, mod3, norm_w, w_router_t, seq, tm=512):
    rows, d = x2d.shape
    half = na.shape[1]
    per_b = seq // tm
    return pl.pallas_call(
        _outproj_body,
        grid=(rows // tm,),
        in_specs=[
            pl.BlockSpec((tm, half), lambda i: (i, 0)),
            pl.BlockSpec((tm, half), lambda i: (i, 0)),
            pl.BlockSpec((2 * half, d), lambda i: (0, 0)),
            pl.BlockSpec((tm, d), lambda i: (i, 0)),
            pl.BlockSpec((1, 1, d), lambda i: (i // per_b, 0, 2)),
            pl.BlockSpec((1, d), lambda i: (0, 0)),
            pl.BlockSpec((1, 1, d), lambda i: (i // per_b, 0, 3)),
            pl.BlockSpec((1, 1, d), lambda i: (i // per_b, 0, 4)),
            pl.BlockSpec((N_EXPERTS, d), lambda i: (0, 0)),
        ],
        out_specs=[
            pl.BlockSpec((tm, d), lambda i: (i, 0)),
            pl.BlockSpec((tm, d), lambda i: (i, 0)),
            pl.BlockSpec((N_EXPERTS, tm), lambda i: (0, i)),
        ],
        out_shape=[
            jax.ShapeDtypeStruct((rows, d), F32),
            jax.ShapeDtypeStruct((rows, d), F32),
            jax.ShapeDtypeStruct((N_EXPERTS, rows), F32),
        ],
        compiler_params=_cparams(1),
        name="out_projection",
    )(na, ret, w_out_bf16, x2d, mod3, norm_w.reshape(1, d), mod3, mod3, w_router_t)


LANES = 128


def _prefix_incl_lanes(x, tri):
    r, l = x.shape
    nb = l // LANES
    xs = jnp.concatenate([x[:, t * LANES:(t + 1) * LANES] for t in range(nb)], axis=0).astype(BF16)
    p = _dot(xs, tri)
    outs, run = [], jnp.zeros((r, 1), F32)
    for t in range(nb):
        blk = p[t * r:(t + 1) * r] + run
        outs.append(blk)
        run = blk[:, LANES - 1:LANES]
    return jnp.concatenate(outs, axis=1)


def _route_body(lg_ref, gidx_ref, spos_ref, gate_ref, rstart_ref, cnt_ref, *, cap):
    b = pl.program_id(0)
    n_e, seq = lg_ref.shape
    kf = float(cap)
    lg = lg_ref[...]
    ex = jnp.exp(lg - jnp.max(lg, axis=0, keepdims=True))
    aff = ex / jnp.sum(ex, axis=0, keepdims=True)

    def cond(c):
        return (c[0] < 4096) & (c[5] > 0.5)

    def step(c):
        it, lo, hi, thr, done, _ = c
        mid = 0.5 * (lo + hi)
        above = jnp.sum((aff > mid).astype(F32), axis=1, keepdims=True)
        hit = above == kf
        stuck = (mid <= lo) | (mid >= hi)
        active = done < 0.5
        thr = jnp.where(active & hit, mid, jnp.where(active & stuck, hi, thr))
        go = active & ~(hit | stuck)
        ge = above >= kf
        lo = jnp.where(go & ge, mid, lo)
        hi = jnp.where(go & ~ge, mid, hi)
        done = jnp.where(active & (hit | stuck), 1.0, done)
        return it + 1, lo, hi, thr, done, jnp.sum(1.0 - done)

    col = lambda v: jnp.full((n_e, 1), v, F32)
    init = (jnp.int32(0), col(-1.0), col(2.0), col(0.0), col(0.0), jnp.float32(n_e))
    thr = lax.while_loop(cond, step, init)[3]

    ii = lax.broadcasted_iota(jnp.int32, (LANES, LANES), 0)
    jj = lax.broadcasted_iota(jnp.int32, (LANES, LANES), 1)
    tri = (ii <= jj).astype(BF16)
    gt = aff > thr
    eq = (aff == thr).astype(F32)
    need = kf - jnp.sum(gt.astype(F32), axis=1, keepdims=True)
    eq_before = _prefix_incl_lanes(eq, tri) - eq
    mask = jnp.where(gt | ((eq > 0.5) & (eq_before < need)), 1.0, 0.0)

    slot = _prefix_incl_lanes(mask, tri) - mask
    cnt = jnp.sum(mask, axis=0, keepdims=True)
    row0 = _prefix_incl_lanes(jnp.broadcast_to(cnt, (8, seq)), tri)[:1] - cnt
    ei = lax.broadcasted_iota(jnp.int32, (n_e, n_e), 0)
    ej = lax.broadcasted_iota(jnp.int32, (n_e, n_e), 1)
    rank = _dot((ej < ei).astype(BF16), mask.astype(BF16))
    base = (b * (n_e * cap)).astype(F32)
    pos = row0 + rank + base
    rstart_ref[0] = (row0 + base).astype(jnp.int32)
    cnt_ref[0] = cnt.astype(jnp.int32)

    tok = lax.broadcasted_iota(jnp.int32, (1, seq), 1).astype(F32)
    tok_hi = jnp.floor(tok * (1.0 / 64))
    tok_lo = tok - 64.0 * tok_hi
    sub = lax.broadcasted_iota(jnp.int32, (8, seq), 0)
    jrow = lax.broadcasted_iota(jnp.int32, (cap, seq), 0).astype(F32)
    for e in range(n_e):
        sel = ((slot[e:e + 1] == jrow) & (mask[e:e + 1] > 0.5)).astype(BF16)
        pe = pos[e:e + 1]
        p_hi = jnp.floor(pe * (1.0 / 128))
        p_lo = pe - 128.0 * p_hi
        a = aff[e:e + 1]
        a_hi = a.astype(BF16).astype(F32)
        a_mid = (a - a_hi).astype(BF16).astype(F32)
        a_lo = a - a_hi - a_mid
        rows = (tok_hi, tok_lo, p_hi, p_lo, a_hi, a_mid, a_lo)
        stack = jnp.zeros((8, seq), F32)
        for r, v in enumerate(rows):
            stack = jnp.where(sub == r, v, stack)
        got = _dot_nt(stack.astype(BF16), sel)
        gidx_ref[0, e:e + 1, :] = (got[0:1] * 64.0 + got[1:2]).astype(jnp.int32) + b * seq
        spos_ref[0, e:e + 1, :] = (got[2:3] * 128.0 + got[3:4]).astype(jnp.int32)
        gate_ref[0, e:e + 1, :] = got[4:5] + got[5:6] + got[6:7]


def _routing(logits_t, batch, seq):
    n_e = logits_t.shape[0]
    cap = CAPACITY_FACTOR * seq // n_e
    bec = pl.BlockSpec((1, n_e, cap), lambda b: (b, 0, 0))
    b1l = pl.BlockSpec((1, 1, seq), lambda b: (b, 0, 0))
    return pl.pallas_call(
        functools.partial(_route_body, cap=cap),
        grid=(batch,),
        in_specs=[pl.BlockSpec((n_e, seq), lambda b: (0, b))],
        out_specs=[bec, bec, bec, b1l, b1l],
        out_shape=[
            jax.ShapeDtypeStruct((batch, n_e, cap), jnp.int32),
            jax.ShapeDtypeStruct((batch, n_e, cap), jnp.int32),
            jax.ShapeDtypeStruct((batch, n_e, cap), F32),
            jax.ShapeDtypeStruct((batch, 1, seq), jnp.int32),
            jax.ShapeDtypeStruct((batch, 1, seq), jnp.int32),
        ],
        compiler_params=_cparams(1),
        name="routing",
    )(logits_t)


def _moe_body(gidx_ref, gnext_ref, spos_ref, gate_ref, u2_hbm, wg_ref, wu_ref, wd_ref, y_hbm,
              stage_ref, xe_ref, acc_ref, yout_ref, gsem, ssem, *, n_f):
    e = pl.program_id(0)
    f = pl.program_id(1)
    n_e = pl.num_programs(0)
    m = stage_ref.shape[0]
    q = m // n_f
    q0 = pl.multiple_of(f * q, q)

    def gather_copy(idx_ref, j):
        return pltpu.make_async_copy(u2_hbm.at[pl.ds(idx_ref[0, j], 1)], stage_ref.at[pl.ds(j, 1)], gsem)

    def scatter_copy(pos_ref, j):
        return pltpu.make_async_copy(yout_ref.at[pl.ds(j, 1)], y_hbm.at[pl.ds(pos_ref[0, j], 1)], ssem)

    def for_rows(fn):
        def body(j, c):
            fn(j)
            return c
        lax.fori_loop(0, m, body, 0, unroll=8)

    wait_gather = lambda: for_rows(lambda j: gather_copy(gidx_ref, j).wait())
    wait_scatter = lambda: for_rows(lambda j: scatter_copy(spos_ref, j).wait())

    @pl.when(f == 0)
    def _():
        @pl.when(e == 0)
        def _():
            for_rows(lambda j: gather_copy(gidx_ref, j).start())

        wait_gather()
        xe_ref[...] = stage_ref[...].astype(BF16)
        acc_ref[...] = jnp.zeros_like(acc_ref)

    for k in range(q):
        gather_copy(gnext_ref, q0 + k).start()
    x = xe_ref[...]
    hid = _silu(_dot(x, wg_ref[...].astype(BF16))) * _dot(x, wu_ref[...].astype(BF16))
    acc_ref[...] += _dot(hid.astype(BF16), wd_ref[...].astype(BF16))

    @pl.when(f == n_f - 1)
    def _():
        @pl.when(e > 0)
        def _():
            wait_scatter()

        g = jnp.transpose(jnp.broadcast_to(gate_ref[...], (LANES, m)))
        for k in range(acc_ref.shape[1] // LANES):
            cols = slice(k * LANES, (k + 1) * LANES)
            yout_ref[:, cols] = acc_ref[:, cols] * g
        for_rows(lambda j: scatter_copy(spos_ref, j).start())

        @pl.when(e == n_e - 1)
        def _():
            wait_scatter()
            wait_gather()


def _moe_ffn(gidx, spos, gates, u2, w_gate, w_up, w_down, tf=256):
    n_e, _, m = gidx.shape
    d = u2.shape[1]
    ff = w_gate.shape[2]
    n_f = ff // tf
    smem = lambda fn: pl.BlockSpec((None, 1, m), fn, memory_space=pltpu.SMEM)
    return pl.pallas_call(
        functools.partial(_moe_body, n_f=n_f),
        grid=(n_e, n_f),
        in_specs=[
            smem(lambda e, f: (e, 0, 0)),
            smem(lambda e, f: (jnp.minimum(e + 1, n_e - 1), 0, 0)),
            smem(lambda e, f: (e, 0, 0)),
            pl.BlockSpec((None, 1, m), lambda e, f: (e, 0, 0)),
            pl.BlockSpec(memory_space=pl.ANY),
            pl.BlockSpec((None, d, tf), lambda e, f: (e, 0, f)),
            pl.BlockSpec((None, d, tf), lambda e, f: (e, 0, f)),
            pl.BlockSpec((None, tf, d), lambda e, f: (e, f, 0)),
        ],
        out_specs=pl.BlockSpec(memory_space=pl.ANY),
        out_shape=jax.ShapeDtypeStruct((n_e * m, d), F32),
        scratch_shapes=[
            pltpu.VMEM((m, d), F32),
            pltpu.VMEM((m, d), BF16),
            pltpu.VMEM((m, d), F32),
            pltpu.VMEM((m, d), F32),
            pltpu.SemaphoreType.DMA,
            pltpu.SemaphoreType.DMA,
        ],
        compiler_params=_cparams(2),
        name="moe_ffn",
    )(gidx, gidx, spos, gates, u2, w_gate, w_up, w_down)


COMBINE_TOKENS = 256
COMBINE_ROWS = 256


def _combine_body(r_ref, h1_ref, y_hbm, rs_ref, cnt_ref, gate_ref, nw_ref, o_ref, buf_ref, acc_ref, sem):
    i = pl.program_id(0)
    t, ch = COMBINE_TOKENS, COMBINE_ROWS
    lo = r_ref[i * t] // ch
    hi = jnp.maximum((r_ref[(i + 1) * t] + ch - 1) // ch, lo)

    def copy(c, slot):
        return pltpu.make_async_copy(y_hbm.at[pl.ds(pl.multiple_of(c * ch, ch), ch)], buf_ref.at[slot], sem.at[slot])

    @pl.when(hi > lo)
    def _():
        copy(lo, 0).start()

    acc_ref[...] = jnp.zeros_like(acc_ref)
    first = rs_ref[0]
    last = first + cnt_ref[0]

    def chunk(c, carry):
        slot = (c - lo) % 2

        @pl.when(c + 1 < hi)
        def _():
            copy(c + 1, 1 - slot).start()

        copy(c, slot).wait()
        row = c * ch + lax.broadcasted_iota(jnp.int32, (ch, t), 0)
        sel = ((row >= first) & (row < last)).astype(BF16)
        acc_ref[...] += _dot_tn(sel, buf_ref[slot].astype(BF16))
        return carry

    lax.fori_loop(lo, hi, chunk, 0)
    h2 = h1_ref[...] + gate_ref[0] * acc_ref[...]
    ms = jnp.mean(h2 * h2, axis=-1, keepdims=True)
    o_ref[...] = h2 * lax.rsqrt(ms + EPS) * nw_ref[...]


def _combine(row_prefix, h1, y, rstart, cnt, mod3, final_w, seq):
    rows, d = h1.shape
    t = COMBINE_TOKENS
    per_b = seq // t
    grid_spec = pltpu.PrefetchScalarGridSpec(
        num_scalar_prefetch=1,
        grid=(rows // t,),
        in_specs=[
            pl.BlockSpec((t, d), lambda i, r: (i, 0)),
            pl.BlockSpec(memory_space=pl.ANY),
            pl.BlockSpec((1, 1, t), lambda i, r: (i, 0, 0)),
            pl.BlockSpec((1, 1, t), lambda i, r: (i, 0, 0)),
            pl.BlockSpec((1, 1, d), lambda i, r: (i // per_b, 0, 5)),
            pl.BlockSpec((1, d), lambda i, r: (0, 0)),
        ],
        out_specs=pl.BlockSpec((t, d), lambda i, r: (i, 0)),
        scratch_shapes=[
            pltpu.VMEM((2, COMBINE_ROWS, d), F32),
            pltpu.VMEM((t, d), F32),
            pltpu.SemaphoreType.DMA((2,)),
        ],
    )
    return pl.pallas_call(
        _combine_body,
        grid_spec=grid_spec,
        out_shape=jax.ShapeDtypeStruct((rows, d), F32),
        compiler_params=_cparams(1),
        name="combine",
    )(row_prefix, h1, y, rstart, cnt, mod3, final_w.reshape(1, d))


def kernel(x, c, ctx, c_ctx, w_mod, b_mod, norm_mix_w, norm_ffn_w, w_in, na_rpb, ret_decay_fwd,
           ret_decay_bwd, ret_gn_w, w_out, w_router, w_gate, w_up, w_down, final_norm_w):
    batch, seq, d = x.shape
    ctx_len = ctx.shape[1]
    assert w_mod.shape[0] == 1, "one trunk layer"
    assert seq % (NA_QROWS * GRID_W) == 0 and seq // GRID_W >= 3 * NA_QROWS
    n_e = w_router.shape[2]
    cap = CAPACITY_FACTOR * seq // n_e

    mod_rows = 8
    cc = jnp.concatenate([c, c_ctx[None], jnp.zeros((mod_rows - batch - 1, d), c.dtype)], axis=0)
    mod3 = _modulation(cc, w_mod[0], b_mod[0]).reshape(mod_rows, 1, N_MOD * d)

    x2d = x.reshape(batch * seq, d)
    tm = 1024
    proj = _in_projection(x2d, norm_mix_w[0], mod3, lambda i: i // (seq // tm), w_in[0],
                          w_in.shape[2], tm, 1024)
    cproj = _in_projection(ctx.reshape(batch * ctx_len, d), norm_mix_w[0], mod3, lambda i: batch,
                           w_in[0], KV_COLS, batch * ctx_len, 1024)

    na = _neighbourhood_attention(proj, cproj, na_rpb[0], batch, seq, ctx_len)
    ret = _retention(proj, cproj, ret_decay_fwd[0], ret_decay_bwd[0], ret_gn_w[0], batch, seq, ctx_len)

    h1, u2, logits_t = _out_projection(na, ret, _cast_bf16(w_out[0]), x2d, mod3, norm_ffn_w[0],
                                       w_router[0].T, seq)

    gidx, spos, gates, rstart, cnt = _routing(logits_t, batch, seq)
    per_expert = lambda a: a.transpose(1, 0, 2).reshape(n_e, 1, batch * cap)
    y = _moe_ffn(per_expert(gidx), per_expert(spos), per_expert(gates), u2, w_gate[0], w_up[0], w_down[0])

    row_prefix = jnp.concatenate([rstart.reshape(-1), jnp.full((1,), batch * n_e * cap, jnp.int32)])
    t = COMBINE_TOKENS
    out = _combine(row_prefix, h1, y, rstart.reshape(batch * seq // t, 1, t), cnt.reshape(batch * seq // t, 1, t),
                   mod3, final_norm_w, seq)
    return out.reshape(batch, seq, d)
```

```python
import functools

import jax
import jax.numpy as jnp
from jax import lax
from jax.experimental import pallas as pl
from jax.experimental.pallas import tpu as pltpu

GRID_W = 64
HEAD_DIM = 128
NA_HEADS = 8
RET_HEADS = 8
NA_WIDTH = NA_HEADS * HEAD_DIM
RET_WIDTH = RET_HEADS * HEAD_DIM
WIN_ROWS = 8
WIN_COLS = 16
RET_BLOCK = 256
ROPE_BASE = 10000.0
N_EXPERTS = 16
CAPACITY_FACTOR = 2
N_MOD = 6
EPS = 1e-6
NEG_INF = -1e30
KV_COLS = 2 * NA_WIDTH + 2 * RET_WIDTH

F32 = jnp.float32
BF16 = jnp.bfloat16
MIB = 1024 * 1024
VMEM_LIMIT_V7X = 56 * MIB


def _cparams(n_axes):
    return pltpu.CompilerParams(
        dimension_semantics=("arbitrary",) * n_axes, vmem_limit_bytes=VMEM_LIMIT_V7X)


def _silu(x):
    return x * jax.nn.sigmoid(x)


def _dot(a, b):
    return jnp.dot(a, b, preferred_element_type=F32)


def _dot_nt(a, b):
    return lax.dot_general(a, b, (((1,), (1,)), ((), ())), preferred_element_type=F32)


def _dot_tn(a, b):
    return lax.dot_general(a, b, (((0,), (0,)), ((), ())), preferred_element_type=F32)


def _mod_body(c_ref, w_ref, b_ref, o_ref):
    a = _silu(c_ref[...]).astype(BF16)
    o_ref[...] = _dot(a, w_ref[...].astype(BF16)) + b_ref[...]


def _modulation(cc, w_mod, b_mod, tn=1024):
    rows, d = cc.shape
    n = w_mod.shape[1]
    return pl.pallas_call(
        _mod_body,
        grid=(n // tn,),
        in_specs=[
            pl.BlockSpec((rows, d), lambda j: (0, 0)),
            pl.BlockSpec((d, tn), lambda j: (0, j)),
            pl.BlockSpec((1, tn), lambda j: (0, j)),
        ],
        out_specs=pl.BlockSpec((rows, tn), lambda j: (0, j)),
        out_shape=jax.ShapeDtypeStruct((rows, n), F32),
        compiler_params=_cparams(1),
        name="modulation",
    )(cc, w_mod, b_mod.reshape(1, n))


def _rms_mod(x, nw, shift, scale):
    ms = jnp.mean(x * x, axis=-1, keepdims=True)
    y = x * lax.rsqrt(ms + EPS) * nw
    return y * (1.0 + scale) + shift


def _inproj_body(x_ref, nw_ref, sh_ref, sc_ref, w_ref, o_ref, u_ref):
    @pl.when(pl.program_id(1) == 0)
    def _():
        u_ref[...] = _rms_mod(x_ref[...], nw_ref[...], sh_ref[0], sc_ref[0]).astype(BF16)

    o_ref[...] = _dot(u_ref[...], w_ref[...].astype(BF16)).astype(o_ref.dtype)


def _in_projection(x2d, norm_w, mod3, mod_row_fn, w_in, n_cols, tm, tn):
    rows, d = x2d.shape
    return pl.pallas_call(
        _inproj_body,
        grid=(rows // tm, n_cols // tn),
        in_specs=[
            pl.BlockSpec((tm, d), lambda i, j: (i, 0)),
            pl.BlockSpec((1, d), lambda i, j: (0, 0)),
            pl.BlockSpec((1, 1, d), lambda i, j: (mod_row_fn(i), 0, 0)),
            pl.BlockSpec((1, 1, d), lambda i, j: (mod_row_fn(i), 0, 1)),
            pl.BlockSpec((d, tn), lambda i, j: (0, j)),
        ],
        out_specs=pl.BlockSpec((tm, tn), lambda i, j: (i, j)),
        out_shape=jax.ShapeDtypeStruct((rows, n_cols), BF16),
        scratch_shapes=[pltpu.VMEM((tm, d), BF16)],
        compiler_params=_cparams(2),
        name="in_projection",
    )(x2d, norm_w.reshape(1, d), mod3, mod3, w_in)


NA_QROWS = 8
NA_KROWS = 2 * NA_QROWS


def _na_row_offset(tile_kind, i, w, rows):
    half = WIN_ROWS // 2
    if tile_kind == 0:
        r, key = i, w
    elif tile_kind == 1:
        r, key = NA_QROWS + i, NA_QROWS - half + w
    else:
        r, key = rows - NA_QROWS + i, rows - NA_KROWS + w
    start = min(max(r - half, 0), rows - WIN_ROWS)
    if not (start <= key < start + WIN_ROWS):
        return None
    return key - r + (WIN_ROWS - 1)


def _na_build_bias(rpb_ref, bias_ref, h, rows):
    w = GRID_W
    cq = lax.broadcasted_iota(jnp.int32, (w, 2 * w), 0)
    ck = lax.broadcasted_iota(jnp.int32, (w, 2 * w), 1) % w
    col_start = jnp.clip(cq - WIN_COLS // 2, 0, w - WIN_COLS)
    col_ok = (ck >= col_start) & (ck < col_start + WIN_COLS)
    col_off = jnp.clip(ck - cq, -(WIN_COLS - 1), WIN_COLS - 1) + (WIN_COLS - 1)
    neg = jnp.full((w, 2 * w), NEG_INF, F32)
    n_ro, n_co = 2 * WIN_ROWS - 1, 2 * WIN_COLS - 1
    tabs = []
    for ro in range(n_ro):
        t = jnp.zeros((w, 2 * w), F32)
        for j in range(n_co):
            t = jnp.where(col_off == j, rpb_ref[h, ro * n_co + j], t)
        tabs.append(jnp.where(col_ok, t, neg))
    left = lax.broadcasted_iota(jnp.int32, (w, 2 * w), 1) < w
    for kind in range(3):
        for i in range(NA_QROWS):
            for wp in range(NA_KROWS // 2):
                ra = _na_row_offset(kind, i, 2 * wp, rows)
                rb = _na_row_offset(kind, i, 2 * wp + 1, rows)
                ta = neg if ra is None else tabs[ra]
                tb = neg if rb is None else tabs[rb]
                blk = ta if ra == rb else jnp.where(left, ta, tb)
                bias_ref[kind, i * w:(i + 1) * w, wp * 2 * w:(wp + 1) * 2 * w] = blk


def _na_body(rpb_ref, q_ref, k_ref, v_ref, ck_ref, cv_ref, o_ref, bias_ref, *, rows):
    h = pl.program_id(0)
    w = GRID_W
    tq, tk = NA_QROWS * w, NA_KROWS * w
    n_tiles = rows // NA_QROWS
    scale = HEAD_DIM ** -0.5

    @pl.when(pl.program_id(1) == 0)
    def _():
        _na_build_bias(rpb_ref, bias_ref, h, rows)

    ck = ck_ref[...]
    cv = cv_ref[...]

    def tile(t, carry):
        kind = jnp.where(t == 0, 0, jnp.where(t == n_tiles - 1, 2, 1))
        krow0 = jnp.clip(t * NA_QROWS - WIN_ROWS // 2, 0, rows - NA_KROWS)
        q0 = pl.multiple_of(t * tq, tq)
        k0 = pl.multiple_of(krow0 * w, 4 * w)
        q = q_ref[pl.ds(q0, tq), :]
        kw = k_ref[pl.ds(k0, tk), :]
        vw = v_ref[pl.ds(k0, tk), :]
        s = _dot_nt(q, kw) * scale + bias_ref[kind]
        sc = _dot_nt(q, ck) * scale
        m = jnp.maximum(jnp.max(s, axis=-1, keepdims=True), jnp.max(sc, axis=-1, keepdims=True))
        p = jnp.exp(s - m)
        pc = jnp.exp(sc - m)
        l = jnp.sum(p, axis=-1, keepdims=True) + jnp.sum(pc, axis=-1, keepdims=True)
        o = _dot(p.astype(BF16), vw) + _dot(pc.astype(BF16), cv)
        o_ref[pl.ds(q0, tq), :] = (o / l).astype(o_ref.dtype)
        return carry

    lax.fori_loop(0, n_tiles, tile, 0, unroll=2)


def _neighbourhood_attention(proj, cproj, rpb, batch, seq, ctx_len):
    rows = seq // GRID_W
    hb = NA_WIDTH // HEAD_DIM
    n_rpb = (2 * WIN_ROWS - 1) * (2 * WIN_COLS - 1)
    tq, tk = NA_QROWS * GRID_W, NA_KROWS * GRID_W
    grid_spec = pltpu.PrefetchScalarGridSpec(
        num_scalar_prefetch=0,
        grid=(NA_HEADS, batch),
        in_specs=[
            pl.BlockSpec(memory_space=pltpu.SMEM),
            pl.BlockSpec((seq, HEAD_DIM), lambda h, b: (b, 4 * hb + h)),
            pl.BlockSpec((seq, HEAD_DIM), lambda h, b: (b, h)),
            pl.BlockSpec((seq, HEAD_DIM), lambda h, b: (b, hb + h)),
            pl.BlockSpec((ctx_len, HEAD_DIM), lambda h, b: (b, h)),
            pl.BlockSpec((ctx_len, HEAD_DIM), lambda h, b: (b, hb + h)),
        ],
        out_specs=pl.BlockSpec((seq, HEAD_DIM), lambda h, b: (b, h)),
        scratch_shapes=[pltpu.VMEM((3, tq, tk), F32)],
    )
    return pl.pallas_call(
        functools.partial(_na_body, rows=rows),
        grid_spec=grid_spec,
        out_shape=jax.ShapeDtypeStruct((batch * seq, NA_WIDTH), BF16),
        compiler_params=_cparams(2),
        name="neighbourhood_attention",
    )(rpb.reshape(NA_HEADS, n_rpb), proj, proj, proj, cproj, cproj)


def _log_sigmoid(x):
    return -(jnp.maximum(-x, 0.0) + jnp.log1p(jnp.exp(-jnp.abs(x))))


def _rope(x, cos, sin_signed):
    lane = lax.broadcasted_iota(jnp.int32, x.shape, 1)
    quarter = HEAD_DIM // 4
    partner = jnp.where(lane % (2 * quarter) < quarter,
                        pltpu.roll(x, HEAD_DIM - quarter, 1), pltpu.roll(x, quarter, 1))
    return x * cos + partner * sin_signed


def _ret_body(df_ref, db_ref, gn_ref, cos_ref, sin_ref, q_ref, k_ref, v_ref, g_ref, ck_ref, cv_ref,
              o_ref, qr_ref, kr_ref, sf_ref, sb_ref):
    c, d = RET_BLOCK, HEAD_DIM
    seq = q_ref.shape[0]
    n_chunks = seq // c
    ctx_len = ck_ref.shape[0]
    scale = HEAD_DIM ** -0.5
    lgf_row = _log_sigmoid(df_ref[0])
    lgb_row = _log_sigmoid(db_ref[0])
    lgf = jnp.broadcast_to(lgf_row, (c, d))
    lgb = jnp.broadcast_to(lgb_row, (c, d))
    pos = lax.broadcasted_iota(jnp.int32, (c, d), 0).astype(F32)
    kdf = jnp.exp(lgf * (c - 1.0 - pos))
    kdb = jnp.exp(lgb * pos)
    qdf = jnp.exp(lgf * (pos + 1.0))
    qdb = jnp.exp(lgb * (c - pos))
    cdf = jnp.exp(lgf_row * float(c))
    cdb = jnp.exp(lgb_row * float(c))
    diff = (lax.broadcasted_iota(jnp.int32, (c, c), 0) - lax.broadcasted_iota(jnp.int32, (c, c), 1)).astype(F32)
    lgf_cc = jnp.broadcast_to(lgf_row[:, :1], (c, c))
    lgb_cc = jnp.broadcast_to(lgb_row[:, :1], (c, c))
    dmat = (jnp.where(diff >= 0, jnp.exp(lgf_cc * jnp.maximum(diff, 0.0)), 0.0)
            + jnp.where(diff <= 0, jnp.exp(lgb_cc * jnp.maximum(-diff, 0.0)), 0.0))

    cpos = lax.broadcasted_iota(jnp.int32, (ctx_len, d), 0).astype(F32)
    ckf = ck_ref[...].astype(F32) * scale
    cv = cv_ref[...]
    wf = jnp.exp(jnp.broadcast_to(lgf_row, (ctx_len, d)) * (ctx_len - 1.0 - cpos))
    wb = jnp.exp(jnp.broadcast_to(lgb_row, (ctx_len, d)) * cpos)
    s_f = _dot_tn((ckf * wf).astype(BF16), cv)
    s_b = _dot_tn((ckf * wb).astype(BF16), cv)

    rope_rows = 512

    def rope_blk(i, carry):
        r0 = pl.multiple_of(i * rope_rows, rope_rows)
        cs = cos_ref[pl.ds(r0, rope_rows), :]
        sn = sin_ref[pl.ds(r0, rope_rows), :]
        qr_ref[pl.ds(r0, rope_rows), :] = _rope(q_ref[pl.ds(r0, rope_rows), :].astype(F32), cs, sn)
        kr_ref[pl.ds(r0, rope_rows), :] = _rope(k_ref[pl.ds(r0, rope_rows), :].astype(F32), cs, sn) * scale
        return carry

    lax.fori_loop(0, seq // rope_rows, rope_blk, 0)

    def scan_blk(n, carry):
        s, t = carry
        r0 = pl.multiple_of(n * c, c)
        kvf = _dot_tn((kr_ref[pl.ds(r0, c), :] * kdf).astype(BF16), v_ref[pl.ds(r0, c), :])
        sf_ref[n] = s.astype(BF16)
        m = n_chunks - 1 - n
        m0 = pl.multiple_of(m * c, c)
        kvb = _dot_tn((kr_ref[pl.ds(m0, c), :] * kdb).astype(BF16), v_ref[pl.ds(m0, c), :])
        sb_ref[m] = t.astype(BF16)
        return s * cdf + kvf, t * cdb + kvb

    lax.fori_loop(0, n_chunks, scan_blk, (s_f, s_b), unroll=2)

    gn = gn_ref[...]

    def out_blk(n, carry):
        r0 = pl.multiple_of(n * c, c)
        qc = qr_ref[pl.ds(r0, c), :]
        kc = kr_ref[pl.ds(r0, c), :]
        a = _dot_nt(qc.astype(BF16), kc.astype(BF16))
        o = (_dot((a * dmat).astype(BF16), v_ref[pl.ds(r0, c), :])
             + _dot((qc * qdf).astype(BF16), sf_ref[n])
             + _dot((qc * qdb).astype(BF16), sb_ref[n]))
        mu = jnp.mean(o, axis=-1, keepdims=True)
        var = jnp.mean(jnp.square(o - mu), axis=-1, keepdims=True)
        y = (o - mu) * lax.rsqrt(var + EPS) * gn
        o_ref[pl.ds(r0, c), :] = (y * _silu(g_ref[pl.ds(r0, c), :].astype(F32))).astype(o_ref.dtype)
        return carry

    lax.fori_loop(0, n_chunks, out_blk, 0, unroll=2)


def _rope_tables(seq):
    axis_dim = HEAD_DIM // 2
    inv_freq = ROPE_BASE ** (-jnp.arange(0, axis_dim, 2, dtype=F32) / axis_dim)
    rows = seq // GRID_W
    ang_r = jnp.arange(rows, dtype=F32)[:, None] * inv_freq
    ang_c = jnp.arange(GRID_W, dtype=F32)[:, None] * inv_freq
    by_row = lambda a: jnp.repeat(a, GRID_W, axis=0)
    by_col = lambda a: jnp.tile(a, (rows, 1))
    cr, sr, cc, sc = by_row(jnp.cos(ang_r)), by_row(jnp.sin(ang_r)), by_col(jnp.cos(ang_c)), by_col(jnp.sin(ang_c))
    return jnp.concatenate([cr, cr, cc, cc], axis=-1), jnp.concatenate([-sr, sr, -sc, sc], axis=-1)


def _retention(proj, cproj, decay_f, decay_b, gn_w, batch, seq, ctx_len):
    hb = RET_WIDTH // HEAD_DIM
    cos, sin = _rope_tables(seq)
    dec_f = jnp.broadcast_to(decay_f.astype(F32)[:, None, None], (RET_HEADS, 1, HEAD_DIM))
    dec_b = jnp.broadcast_to(decay_b.astype(F32)[:, None, None], (RET_HEADS, 1, HEAD_DIM))
    n_chunks = seq // RET_BLOCK
    blk = lambda rows, fn: pl.BlockSpec((rows, HEAD_DIM), fn)
    return pl.pallas_call(
        _ret_body,
        grid=(RET_HEADS, batch),
        in_specs=[
            pl.BlockSpec((1, 1, HEAD_DIM), lambda h, b: (h, 0, 0)),
            pl.BlockSpec((1, 1, HEAD_DIM), lambda h, b: (h, 0, 0)),
            blk(1, lambda h, b: (0, h)),
            blk(seq, lambda h, b: (0, 0)),
            blk(seq, lambda h, b: (0, 0)),
            blk(seq, lambda h, b: (b, 5 * hb + h)),
            blk(seq, lambda h, b: (b, 2 * hb + h)),
            blk(seq, lambda h, b: (b, 3 * hb + h)),
            blk(seq, lambda h, b: (b, 6 * hb + h)),
            blk(ctx_len, lambda h, b: (b, 2 * hb + h)),
            blk(ctx_len, lambda h, b: (b, 3 * hb + h)),
        ],
        out_specs=blk(seq, lambda h, b: (b, h)),
        out_shape=jax.ShapeDtypeStruct((batch * seq, RET_WIDTH), BF16),
        scratch_shapes=[
            pltpu.VMEM((seq, HEAD_DIM), F32),
            pltpu.VMEM((seq, HEAD_DIM), F32),
            pltpu.VMEM((n_chunks, HEAD_DIM, HEAD_DIM), BF16),
            pltpu.VMEM((n_chunks, HEAD_DIM, HEAD_DIM), BF16),
        ],
        compiler_params=_cparams(2),
        name="retention",
    )(dec_f, dec_b, gn_w.reshape(1, RET_WIDTH), cos, sin, proj, proj, proj, proj, cproj, cproj)


def _split_bf16(x):
    hi = x.astype(BF16)
    lo = (x - hi.astype(F32)).astype(BF16)
    return hi, lo


def _cast_body(x_ref, o_ref):
    o_ref[...] = x_ref[...].astype(o_ref.dtype)


def _cast_bf16(w, tm=512):
    rows, cols = w.shape
    return pl.pallas_call(
        _cast_body,
        grid=(rows // tm,),
        in_specs=[pl.BlockSpec((tm, cols), lambda i: (i, 0))],
        out_specs=pl.BlockSpec((tm, cols), lambda i: (i, 0)),
        out_shape=jax.ShapeDtypeStruct((rows, cols), BF16),
        compiler_params=_cparams(1),
        name="cast_bf16",
    )(w)


def _outproj_body(na_ref, ret_ref, w_ref, x_ref, gate_ref, nw_ref, sh_ref, sc_ref, wr_ref,
                  h1_ref, u2_ref, lg_ref):
    half = na_ref.shape[1]
    n_e = wr_ref.shape[0]
    mix = _dot(na_ref[...], w_ref[:half, :]) + _dot(ret_ref[...], w_ref[half:, :])
    h1 = x_ref[...] + gate_ref[0] * mix
    h1_ref[...] = h1
    u2 = _rms_mod(h1, nw_ref[...], sh_ref[0], sc_ref[0])
    u2_ref[...] = u2
    uh, ul = _split_bf16(u2)
    wh, wl = _split_bf16(wr_ref[...])
    both = _dot_nt(jnp.concatenate([wh, wl], axis=0), uh)
    lg_ref[...] = both[:n_e] + (both[n_e:] + _dot_nt(wh, ul))


def _out_projection(na, ret, w_out_bf16, x2d, mod3, norm_w, w_router_t, seq, tm=512):
    rows, d = x2d.shape
    half = na.shape[1]
    per_b = seq // tm
    return pl.pallas_call(
        _outproj_body,
        grid=(rows // tm,),
        in_specs=[
            pl.BlockSpec((tm, half), lambda i: (i, 0)),
            pl.BlockSpec((tm, half), lambda i: (i, 0)),
            pl.BlockSpec((2 * half, d), lambda i: (0, 0)),
            pl.BlockSpec((tm, d), lambda i: (i, 0)),
            pl.BlockSpec((1, 1, d), lambda i: (i // per_b, 0, 2)),
            pl.BlockSpec((1, d), lambda i: (0, 0)),
            pl.BlockSpec((1, 1, d), lambda i: (i // per_b, 0, 3)),
            pl.BlockSpec((1, 1, d), lambda i: (i // per_b, 0, 4)),
            pl.BlockSpec((N_EXPERTS, d), lambda i: (0, 0)),
        ],
        out_specs=[
            pl.BlockSpec((tm, d), lambda i: (i, 0)),
            pl.BlockSpec((tm, d), lambda i: (i, 0)),
            pl.BlockSpec((N_EXPERTS, tm), lambda i: (0, i)),
        ],
        out_shape=[
            jax.ShapeDtypeStruct((rows, d), F32),
            jax.ShapeDtypeStruct((rows, d), F32),
            jax.ShapeDtypeStruct((N_EXPERTS, rows), F32),
        ],
        compiler_params=_cparams(1),
        name="out_projection",
    )(na, ret, w_out_bf16, x2d, mod3, norm_w.reshape(1, d), mod3, mod3, w_router_t)


LANES = 128
SUBLANES = 8


def _prefix_incl_lanes(x, tri):
    r, l = x.shape
    nb = l // LANES
    xs = jnp.concatenate([x[:, t * LANES:(t + 1) * LANES] for t in range(nb)], axis=0).astype(BF16)
    p = _dot(xs, tri)
    outs, run = [], jnp.zeros((r, 1), F32)
    for t in range(nb):
        blk = p[t * r:(t + 1) * r] + run
        outs.append(blk)
        run = blk[:, LANES - 1:LANES]
    return jnp.concatenate(outs, axis=1)


def _route_body(lg_ref, gidx_ref, spos_ref, gate_ref, rstart_ref, cnt_ref, *, cap):
    b = pl.program_id(0)
    n_e, seq = lg_ref.shape
    kf = float(cap)
    lg = lg_ref[...]
    ex = jnp.exp(lg - jnp.max(lg, axis=0, keepdims=True))
    aff = ex / jnp.sum(ex, axis=0, keepdims=True)

    def cond(c):
        return (c[0] < 4096) & (c[5] > 0.5)

    def step(c):
        it, lo, hi, thr, done, _ = c
        mid = 0.5 * (lo + hi)
        above = jnp.sum((aff > mid).astype(F32), axis=1, keepdims=True)
        hit = above == kf
        stuck = (mid <= lo) | (mid >= hi)
        active = done < 0.5
        thr = jnp.where(active & hit, mid, jnp.where(active & stuck, hi, thr))
        go = active & ~(hit | stuck)
        ge = above >= kf
        lo = jnp.where(go & ge, mid, lo)
        hi = jnp.where(go & ~ge, mid, hi)
        done = jnp.where(active & (hit | stuck), 1.0, done)
        return it + 1, lo, hi, thr, done, jnp.sum(1.0 - done)

    col = lambda v: jnp.full((n_e, 1), v, F32)
    init = (jnp.int32(0), col(-1.0), col(2.0), col(0.0), col(0.0), jnp.float32(n_e))
    thr = lax.while_loop(cond, step, init)[3]

    ii = lax.broadcasted_iota(jnp.int32, (LANES, LANES), 0)
    jj = lax.broadcasted_iota(jnp.int32, (LANES, LANES), 1)
    tri = (ii <= jj).astype(BF16)
    gt = aff > thr
    eq = (aff == thr).astype(F32)
    need = kf - jnp.sum(gt.astype(F32), axis=1, keepdims=True)
    eq_before = _prefix_incl_lanes(eq, tri) - eq
    mask = jnp.where(gt | ((eq > 0.5) & (eq_before < need)), 1.0, 0.0)

    slot = _prefix_incl_lanes(mask, tri) - mask
    cnt = jnp.sum(mask, axis=0, keepdims=True)
    row0 = _prefix_incl_lanes(jnp.broadcast_to(cnt, (8, seq)), tri)[:1] - cnt
    ei = lax.broadcasted_iota(jnp.int32, (n_e, n_e), 0)
    ej = lax.broadcasted_iota(jnp.int32, (n_e, n_e), 1)
    rank = _dot((ej < ei).astype(BF16), mask.astype(BF16))
    base = (b * (n_e * cap)).astype(F32)
    pos = row0 + rank + base
    rstart_ref[0] = (row0 + base).astype(jnp.int32)
    cnt_ref[0] = cnt.astype(jnp.int32)

    tok = lax.broadcasted_iota(jnp.int32, (1, seq), 1).astype(F32)
    tok_hi = jnp.floor(tok * (1.0 / 64))
    tok_lo = tok - 64.0 * tok_hi
    n_hi = ROUTE_SLOT_HI
    n_lo = cap // n_hi
    hi_iota = lax.broadcasted_iota(jnp.int32, (n_hi, seq), 0).astype(F32)
    lo_iota = lax.broadcasted_iota(jnp.int32, (n_lo, seq), 0).astype(F32)
    for e in range(n_e):
        se = slot[e:e + 1]
        s_hi = jnp.floor(se * (1.0 / n_lo))
        s_lo = se - n_lo * s_hi
        in_hi = jnp.where((s_hi == hi_iota) & (mask[e:e + 1] > 0.5), 1.0, 0.0)
        in_lo = (s_lo == lo_iota).astype(BF16)
        pe = pos[e:e + 1]
        p_hi = jnp.floor(pe * (1.0 / 128))
        p_lo = pe - 128.0 * p_hi
        a = aff[e:e + 1]
        a_hi = a.astype(BF16).astype(F32)
        a_mid = (a - a_hi).astype(BF16).astype(F32)
        a_lo = a - a_hi - a_mid
        vals = (tok_hi, tok_lo, p_hi, p_lo, a_hi, a_mid, a_lo)
        lhs = jnp.concatenate([in_hi * v for v in vals], axis=0).astype(BF16)
        got = _dot_nt(lhs, in_lo)
        part = lambda r: got[r * n_hi:(r + 1) * n_hi]
        gidx_ref[0, e] = (part(0) * 64.0 + part(1)).astype(jnp.int32) + b * seq
        spos_ref[0, e] = (part(2) * 128.0 + part(3)).astype(jnp.int32)
        gate_ref[0, e] = part(4) + part(5) + part(6)


ROUTE_SLOT_HI = 16


def _routing(logits_t, batch, seq):
    n_e = logits_t.shape[0]
    cap = CAPACITY_FACTOR * seq // n_e
    n_hi, n_lo = ROUTE_SLOT_HI, cap // ROUTE_SLOT_HI
    bec = pl.BlockSpec((1, n_e, n_hi, n_lo), lambda b: (b, 0, 0, 0))
    b1l = pl.BlockSpec((1, 1, seq), lambda b: (b, 0, 0))
    gidx, spos, gates, rstart, cnt = pl.pallas_call(
        functools.partial(_route_body, cap=cap),
        grid=(batch,),
        in_specs=[pl.BlockSpec((n_e, seq), lambda b: (0, b))],
        out_specs=[bec, bec, bec, b1l, b1l],
        out_shape=[
            jax.ShapeDtypeStruct((batch, n_e, n_hi, n_lo), jnp.int32),
            jax.ShapeDtypeStruct((batch, n_e, n_hi, n_lo), jnp.int32),
            jax.ShapeDtypeStruct((batch, n_e, n_hi, n_lo), F32),
            jax.ShapeDtypeStruct((batch, 1, seq), jnp.int32),
            jax.ShapeDtypeStruct((batch, 1, seq), jnp.int32),
        ],
        compiler_params=_cparams(1),
        name="routing",
    )(logits_t)
    flat = lambda a: a.reshape(batch, n_e, cap)
    return flat(gidx), flat(spos), flat(gates), rstart, cnt


def _moe_body(gidx_ref, gnext_ref, spos_ref, gate_ref, u2_hbm, wg_ref, wu_ref, wd_ref, y_hbm,
              stage_ref, xe_ref, acc_ref, yout_ref, gsem, ssem, *, n_f):
    e = pl.program_id(0)
    f = pl.program_id(1)
    n_e = pl.num_programs(0)
    m, d = acc_ref.shape
    sub = SUBLANES
    q = m // n_f

    def hbm_row(ref, r):
        return ref.at[lax.shift_right_logical(r, 3), pl.ds(r & (sub - 1), 1)]

    def gather_copy(idx_ref, i, k):
        return pltpu.make_async_copy(hbm_row(u2_hbm, idx_ref[0, i * sub + k]), stage_ref.at[i, pl.ds(k, 1)], gsem)

    def scatter_copy(pos_ref, i, k):
        return pltpu.make_async_copy(yout_ref.at[i, pl.ds(k, 1)], hbm_row(y_hbm, pos_ref[0, i * sub + k]), ssem)

    def for_rows(fn, tiles_per_trip=4):
        def body(t, c):
            for kk in range(tiles_per_trip * sub):
                fn(t * tiles_per_trip + kk // sub, kk % sub)
            return c
        lax.fori_loop(0, m // (sub * tiles_per_trip), body, 0)

    wait_gather = lambda: for_rows(lambda i, k: gather_copy(gidx_ref, i, k).wait())
    wait_scatter = lambda: for_rows(lambda i, k: scatter_copy(spos_ref, i, k).wait())

    @pl.when(f == 0)
    def _():
        @pl.when(e == 0)
        def _():
            for_rows(lambda i, k: gather_copy(gidx_ref, i, k).start())

        wait_gather()
        xe_ref[...] = stage_ref[...].reshape(m, d).astype(BF16)
        acc_ref[...] = jnp.zeros_like(acc_ref)

    i0 = f * (q // sub)
    for kk in range(q):
        gather_copy(gnext_ref, i0 + kk // sub, kk % sub).start()
    x = xe_ref[...]
    hid = _silu(_dot(x, wg_ref[...].astype(BF16))) * _dot(x, wu_ref[...].astype(BF16))
    acc_ref[...] += _dot(hid.astype(BF16), wd_ref[...].astype(BF16))

    @pl.when(f == n_f - 1)
    def _():
        @pl.when(e > 0)
        def _():
            wait_scatter()

        g = jnp.transpose(jnp.broadcast_to(gate_ref[...], (LANES, m)))
        for k in range(d // LANES):
            cols = slice(k * LANES, (k + 1) * LANES)
            yout_ref[:, :, cols] = (acc_ref[:, cols] * g).reshape(m // sub, sub, LANES)
        for_rows(lambda i, k: scatter_copy(spos_ref, i, k).start())

        @pl.when(e == n_e - 1)
        def _():
            wait_scatter()
            wait_gather()


def _moe_ffn(gidx, spos, gates, u2, w_gate, w_up, w_down, tf=256):
    n_e, _, m = gidx.shape
    rows, d = u2.shape
    ff = w_gate.shape[2]
    n_f = ff // tf
    sub = SUBLANES
    smem = lambda fn: pl.BlockSpec((None, 1, m), fn, memory_space=pltpu.SMEM)
    y = pl.pallas_call(
        functools.partial(_moe_body, n_f=n_f),
        grid=(n_e, n_f),
        in_specs=[
            smem(lambda e, f: (e, 0, 0)),
            smem(lambda e, f: (jnp.minimum(e + 1, n_e - 1), 0, 0)),
            smem(lambda e, f: (e, 0, 0)),
            pl.BlockSpec((None, 1, m), lambda e, f: (e, 0, 0)),
            pl.BlockSpec(memory_space=pl.ANY),
            pl.BlockSpec((None, d, tf), lambda e, f: (e, 0, f)),
            pl.BlockSpec((None, d, tf), lambda e, f: (e, 0, f)),
            pl.BlockSpec((None, tf, d), lambda e, f: (e, f, 0)),
        ],
        out_specs=pl.BlockSpec(memory_space=pl.ANY),
        out_shape=jax.ShapeDtypeStruct((n_e * m // sub, sub, d), F32),
        scratch_shapes=[
            pltpu.VMEM((m // sub, sub, d), F32),
            pltpu.VMEM((m, d), BF16),
            pltpu.VMEM((m, d), F32),
            pltpu.VMEM((m // sub, sub, d), F32),
            pltpu.SemaphoreType.DMA,
            pltpu.SemaphoreType.DMA,
        ],
        compiler_params=_cparams(2),
        name="moe_ffn",
    )(gidx, gidx, spos, gates, u2.reshape(rows // sub, sub, d), w_gate, w_up, w_down)
    return y.reshape(n_e * m, d)


COMBINE_TOKENS = 256
COMBINE_ROWS = 256


def _combine_body(tile_ref, chunk_ref, flag_ref, h1_ref, y_ref, rs_ref, cnt_ref, gate_ref, nw_ref, o_ref, acc_ref):
    s = pl.program_id(0)
    t, ch = COMBINE_TOKENS, COMBINE_ROWS
    flags = flag_ref[s]
    is_first, is_last, is_valid = (flags & 1) == 1, (flags & 2) == 2, (flags & 4) == 4

    @pl.when(is_first)
    def _():
        acc_ref[...] = jnp.zeros_like(acc_ref)

    @pl.when(is_valid)
    def _():
        first = rs_ref[0]
        last = first + cnt_ref[0]
        row = chunk_ref[s] * ch + lax.broadcasted_iota(jnp.int32, (ch, t), 0)
        sel = ((row >= first) & (row < last)).astype(BF16)
        acc_ref[...] += _dot_tn(sel, y_ref[...].astype(BF16))

    @pl.when(is_last)
    def _():
        h2 = h1_ref[...] + gate_ref[0] * acc_ref[...]
        ms = jnp.mean(h2 * h2, axis=-1, keepdims=True)
        o_ref[...] = h2 * lax.rsqrt(ms + EPS) * nw_ref[...]


def _combine_schedule(row_prefix, n_tiles, n_chunks):
    t, ch = COMBINE_TOKENS, COMBINE_ROWS
    n_steps = n_tiles + n_chunks
    bounds = row_prefix[::t]
    lo = jnp.minimum(bounds[:-1] // ch, n_chunks - 1)
    hi = jnp.clip((bounds[1:] + ch - 1) // ch, lo + 1, n_chunks)
    n = hi - lo
    end = jnp.cumsum(n)
    start = end - n
    step = jnp.arange(n_steps, dtype=jnp.int32)
    tile = jnp.minimum(jnp.sum(step[:, None] >= end[None, :], axis=1), n_tiles - 1).astype(jnp.int32)
    valid = step < end[-1]
    k = jnp.minimum(step - start[tile], n[tile] - 1)
    chunk = (lo[tile] + k).astype(jnp.int32)
    flags = ((valid & (step == start[tile])).astype(jnp.int32)
             + 2 * (valid & (step == end[tile] - 1)).astype(jnp.int32) + 4 * valid.astype(jnp.int32))
    return tile, chunk, flags


def _combine(row_prefix, h1, y, rstart, cnt, mod3, final_w, seq):
    rows, d = h1.shape
    t, ch = COMBINE_TOKENS, COMBINE_ROWS
    per_b = seq // t
    n_tiles, n_chunks = rows // t, y.shape[0] // ch
    tile, chunk, flags = _combine_schedule(row_prefix, n_tiles, n_chunks)
    grid_spec = pltpu.PrefetchScalarGridSpec(
        num_scalar_prefetch=3,
        grid=(n_tiles + n_chunks,),
        in_specs=[
            pl.BlockSpec((t, d), lambda s, ti, ci, fl: (ti[s], 0)),
            pl.BlockSpec((ch, d), lambda s, ti, ci, fl: (ci[s], 0)),
            pl.BlockSpec((1, 1, t), lambda s, ti, ci, fl: (ti[s], 0, 0)),
            pl.BlockSpec((1, 1, t), lambda s, ti, ci, fl: (ti[s], 0, 0)),
            pl.BlockSpec((1, 1, d), lambda s, ti, ci, fl: (ti[s] // per_b, 0, 5)),
            pl.BlockSpec((1, d), lambda s, ti, ci, fl: (0, 0)),
        ],
        out_specs=pl.BlockSpec((t, d), lambda s, ti, ci, fl: (ti[s], 0)),
        scratch_shapes=[pltpu.VMEM((t, d), F32)],
    )
    return pl.pallas_call(
        _combine_body,
        grid_spec=grid_spec,
        out_shape=jax.ShapeDtypeStruct((rows, d), F32),
        compiler_params=_cparams(1),
        name="combine",
    )(tile, chunk, flags, h1, y, rstart, cnt, mod3, final_w.reshape(1, d))


def kernel(x, c, ctx, c_ctx, w_mod, b_mod, norm_mix_w, norm_ffn_w, w_in, na_rpb, ret_decay_fwd,
           ret_decay_bwd, ret_gn_w, w_out, w_router, w_gate, w_up, w_down, final_norm_w):
    batch, seq, d = x.shape
    ctx_len = ctx.shape[1]
    assert w_mod.shape[0] == 1, "one trunk layer"
    assert seq % (NA_QROWS * GRID_W) == 0 and seq // GRID_W >= 3 * NA_QROWS
    n_e = w_router.shape[2]
    cap = CAPACITY_FACTOR * seq // n_e

    mod_rows = 8
    cc = jnp.concatenate([c, c_ctx[None], jnp.zeros((mod_rows - batch - 1, d), c.dtype)], axis=0)
    mod3 = _modulation(cc, w_mod[0], b_mod[0]).reshape(mod_rows, 1, N_MOD * d)

    x2d = x.reshape(batch * seq, d)
    tm = 1024
    w_in_bf16 = _cast_bf16(w_in[0], tm=256)
    proj = _in_projection(x2d, norm_mix_w[0], mod3, lambda i: i // (seq // tm), w_in_bf16,
                          w_in.shape[2], tm, w_in.shape[2] // 4)
    cproj = _in_projection(ctx.reshape(batch * ctx_len, d), norm_mix_w[0], mod3, lambda i: batch,
                           w_in_bf16, KV_COLS, batch * ctx_len, 1024)

    na = _neighbourhood_attention(proj, cproj, na_rpb[0], batch, seq, ctx_len)
    ret = _retention(proj, cproj, ret_decay_fwd[0], ret_decay_bwd[0], ret_gn_w[0], batch, seq, ctx_len)

    h1, u2, logits_t = _out_projection(na, ret, _cast_bf16(w_out[0]), x2d, mod3, norm_ffn_w[0],
                                       w_router[0].T, seq)

    gidx, spos, gates, rstart, cnt = _routing(logits_t, batch, seq)
    per_expert = lambda a: a.transpose(1, 0, 2).reshape(n_e, 1, batch * cap)
    y = _moe_ffn(per_expert(gidx), per_expert(spos), per_expert(gates), u2, w_gate[0], w_up[0], w_down[0])

    row_prefix = jnp.concatenate([rstart.reshape(-1), jnp.full((1,), batch * n_e * cap, jnp.int32)])
    t = COMBINE_TOKENS
    out = _combine(row_prefix, h1, y, rstart.reshape(batch * seq // t, 1, t), cnt.reshape(batch * seq // t, 1, t),
                   mod3, final_norm_w, seq)
    return out.reshape(batch, seq, d)
```

```python
import functools

import jax
import jax.numpy as jnp
from jax import lax
from jax.experimental import pallas as pl
from jax.experimental.pallas import tpu as pltpu

GRID_W = 64
HEAD_DIM = 128
NA_HEADS = 8
RET_HEADS = 8
NA_WIDTH = NA_HEADS * HEAD_DIM
RET_WIDTH = RET_HEADS * HEAD_DIM
WIN_ROWS = 8
WIN_COLS = 16
RET_BLOCK = 256
ROPE_BASE = 10000.0
N_EXPERTS = 16
CAPACITY_FACTOR = 2
N_MOD = 6
EPS = 1e-6
NEG_INF = -1e30
KV_COLS = 2 * NA_WIDTH + 2 * RET_WIDTH

F32 = jnp.float32
BF16 = jnp.bfloat16
MIB = 1024 * 1024
VMEM_LIMIT_V7X = 56 * MIB


def _cparams(n_axes):
    return pltpu.CompilerParams(
        dimension_semantics=("arbitrary",) * n_axes, vmem_limit_bytes=VMEM_LIMIT_V7X)


def _silu(x):
    return x * jax.nn.sigmoid(x)


def _dot(a, b):
    return jnp.dot(a, b, preferred_element_type=F32)


def _dot_nt(a, b):
    return lax.dot_general(a, b, (((1,), (1,)), ((), ())), preferred_element_type=F32)


def _dot_tn(a, b):
    return lax.dot_general(a, b, (((0,), (0,)), ((), ())), preferred_element_type=F32)


def _mod_body(c_ref, w_ref, b_ref, o_ref):
    a = _silu(c_ref[...]).astype(BF16)
    o_ref[...] = _dot(a, w_ref[...].astype(BF16)) + b_ref[...]


def _modulation(cc, w_mod, b_mod, tn=1024):
    rows, d = cc.shape
    n = w_mod.shape[1]
    return pl.pallas_call(
        _mod_body,
        grid=(n // tn,),
        in_specs=[
            pl.BlockSpec((rows, d), lambda j: (0, 0)),
            pl.BlockSpec((d, tn), lambda j: (0, j)),
            pl.BlockSpec((1, tn), lambda j: (0, j)),
        ],
        out_specs=pl.BlockSpec((rows, tn), lambda j: (0, j)),
        out_shape=jax.ShapeDtypeStruct((rows, n), F32),
        compiler_params=_cparams(1),
        name="modulation",
    )(cc, w_mod, b_mod.reshape(1, n))


def _rms_mod(x, nw, shift, scale):
    ms = jnp.mean(x * x, axis=-1, keepdims=True)
    y = x * lax.rsqrt(ms + EPS) * nw
    return y * (1.0 + scale) + shift


def _inproj_body(x_ref, nw_ref, sh_ref, sc_ref, w_ref, o_ref, u_ref):
    @pl.when(pl.program_id(1) == 0)
    def _():
        u_ref[...] = _rms_mod(x_ref[...], nw_ref[...], sh_ref[0], sc_ref[0]).astype(BF16)

    o_ref[...] = _dot(u_ref[...], w_ref[...].astype(BF16)).astype(o_ref.dtype)


def _in_projection(x2d, norm_w, mod3, mod_row_fn, w_in, n_cols, tm, tn):
    rows, d = x2d.shape
    return pl.pallas_call(
        _inproj_body,
        grid=(rows // tm, n_cols // tn),
        in_specs=[
            pl.BlockSpec((tm, d), lambda i, j: (i, 0)),
            pl.BlockSpec((1, d), lambda i, j: (0, 0)),
            pl.BlockSpec((1, 1, d), lambda i, j: (mod_row_fn(i), 0, 0)),
            pl.BlockSpec((1, 1, d), lambda i, j: (mod_row_fn(i), 0, 1)),
            pl.BlockSpec((d, tn), lambda i, j: (0, j)),
        ],
        out_specs=pl.BlockSpec((tm, tn), lambda i, j: (i, j)),
        out_shape=jax.ShapeDtypeStruct((rows, n_cols), BF16),
        scratch_shapes=[pltpu.VMEM((tm, d), BF16)],
        compiler_params=_cparams(2),
        name="in_projection",
    )(x2d, norm_w.reshape(1, d), mod3, mod3, w_in)


NA_QROWS = 8
NA_KROWS = 2 * NA_QROWS


def _na_row_offset(tile_kind, i, w, rows):
    half = WIN_ROWS // 2
    if tile_kind == 0:
        r, key = i, w
    elif tile_kind == 1:
        r, key = NA_QROWS + i, NA_QROWS - half + w
    else:
        r, key = rows - NA_QROWS + i, rows - NA_KROWS + w
    start = min(max(r - half, 0), rows - WIN_ROWS)
    if not (start <= key < start + WIN_ROWS):
        return None
    return key - r + (WIN_ROWS - 1)


def _na_build_bias(rpb_ref, bias_ref, h, rows):
    w = GRID_W
    cq = lax.broadcasted_iota(jnp.int32, (w, 2 * w), 0)
    ck = lax.broadcasted_iota(jnp.int32, (w, 2 * w), 1) % w
    col_start = jnp.clip(cq - WIN_COLS // 2, 0, w - WIN_COLS)
    col_ok = (ck >= col_start) & (ck < col_start + WIN_COLS)
    col_off = jnp.clip(ck - cq, -(WIN_COLS - 1), WIN_COLS - 1) + (WIN_COLS - 1)
    neg = jnp.full((w, 2 * w), NEG_INF, F32)
    n_ro, n_co = 2 * WIN_ROWS - 1, 2 * WIN_COLS - 1
    tabs = []
    for ro in range(n_ro):
        t = jnp.zeros((w, 2 * w), F32)
        for j in range(n_co):
            t = jnp.where(col_off == j, rpb_ref[h, ro * n_co + j], t)
        tabs.append(jnp.where(col_ok, t, neg))
    left = lax.broadcasted_iota(jnp.int32, (w, 2 * w), 1) < w
    for kind in range(3):
        for i in range(NA_QROWS):
            for wp in range(NA_KROWS // 2):
                ra = _na_row_offset(kind, i, 2 * wp, rows)
                rb = _na_row_offset(kind, i, 2 * wp + 1, rows)
                ta = neg if ra is None else tabs[ra]
                tb = neg if rb is None else tabs[rb]
                blk = ta if ra == rb else jnp.where(left, ta, tb)
                bias_ref[kind, i * w:(i + 1) * w, wp * 2 * w:(wp + 1) * 2 * w] = blk


def _na_body(rpb_ref, q_ref, k_ref, v_ref, ck_ref, cv_ref, o_ref, bias_ref, sa_ref, sb_ref, *, rows):
    h = pl.program_id(0)
    w = GRID_W
    tq, tk = NA_QROWS * w, NA_KROWS * w
    n_tiles = rows // NA_QROWS
    scale = HEAD_DIM ** -0.5

    @pl.when(pl.program_id(1) == 0)
    def _():
        _na_build_bias(rpb_ref, bias_ref, h, rows)

    def offsets(t):
        krow0 = jnp.clip(t * NA_QROWS - WIN_ROWS // 2, 0, rows - NA_KROWS)
        return pl.multiple_of(t * tq, tq), pl.multiple_of(krow0 * w, 4 * w)

    def scores(t, s_ref):
        kind = jnp.where(t == 0, 0, jnp.where(t == n_tiles - 1, 2, 1))
        q0, k0 = offsets(t)
        q = q_ref[pl.ds(q0, tq), :]
        s_ref[:, :tk] = _dot_nt(q, k_ref[pl.ds(k0, tk), :]) * scale + bias_ref[kind]
        s_ref[:, tk:] = _dot_nt(q, ck_ref[...]) * scale

    def attend(t, s_ref):
        q0, k0 = offsets(t)
        s = s_ref[...]
        p = jnp.exp(s - jnp.max(s, axis=-1, keepdims=True))
        l = jnp.sum(p, axis=-1, keepdims=True)
        pb = p.astype(BF16)
        o = _dot(pb[:, :tk], v_ref[pl.ds(k0, tk), :]) + _dot(pb[:, tk:], cv_ref[...])
        o_ref[pl.ds(q0, tq), :] = (o / l).astype(o_ref.dtype)

    scores(jnp.int32(0), sa_ref)

    def pair(i, carry):
        t = 2 * i
        scores(t + 1, sb_ref)
        attend(t, sa_ref)
        scores(t + 2, sa_ref)
        attend(t + 1, sb_ref)
        return carry

    lax.fori_loop(0, n_tiles // 2 - 1, pair, 0)
    scores(jnp.int32(n_tiles - 1), sb_ref)
    attend(jnp.int32(n_tiles - 2), sa_ref)
    attend(jnp.int32(n_tiles - 1), sb_ref)


def _neighbourhood_attention(proj, cproj, rpb, batch, seq, ctx_len):
    rows = seq // GRID_W
    hb = NA_WIDTH // HEAD_DIM
    n_rpb = (2 * WIN_ROWS - 1) * (2 * WIN_COLS - 1)
    tq, tk = NA_QROWS * GRID_W, NA_KROWS * GRID_W
    grid_spec = pltpu.PrefetchScalarGridSpec(
        num_scalar_prefetch=0,
        grid=(NA_HEADS, batch),
        in_specs=[
            pl.BlockSpec(memory_space=pltpu.SMEM),
            pl.BlockSpec((seq, HEAD_DIM), lambda h, b: (b, 4 * hb + h)),
            pl.BlockSpec((seq, HEAD_DIM), lambda h, b: (b, h)),
            pl.BlockSpec((seq, HEAD_DIM), lambda h, b: (b, hb + h)),
            pl.BlockSpec((ctx_len, HEAD_DIM), lambda h, b: (b, h)),
            pl.BlockSpec((ctx_len, HEAD_DIM), lambda h, b: (b, hb + h)),
        ],
        out_specs=pl.BlockSpec((seq, HEAD_DIM), lambda h, b: (b, h)),
        scratch_shapes=[
            pltpu.VMEM((3, tq, tk), F32),
            pltpu.VMEM((tq, tk + ctx_len), F32),
            pltpu.VMEM((tq, tk + ctx_len), F32),
        ],
    )
    return pl.pallas_call(
        functools.partial(_na_body, rows=rows),
        grid_spec=grid_spec,
        out_shape=jax.ShapeDtypeStruct((batch * seq, NA_WIDTH), BF16),
        compiler_params=_cparams(2),
        name="neighbourhood_attention",
    )(rpb.reshape(NA_HEADS, n_rpb), proj, proj, proj, cproj, cproj)


def _log_sigmoid(x):
    return -(jnp.maximum(-x, 0.0) + jnp.log1p(jnp.exp(-jnp.abs(x))))


def _rope(x, cos, sin_signed):
    lane = lax.broadcasted_iota(jnp.int32, x.shape, 1)
    quarter = HEAD_DIM // 4
    partner = jnp.where(lane % (2 * quarter) < quarter,
                        pltpu.roll(x, HEAD_DIM - quarter, 1), pltpu.roll(x, quarter, 1))
    return x * cos + partner * sin_signed


def _ret_body(df_ref, db_ref, gn_ref, cos_ref, sin_ref, q_ref, k_ref, v_ref, g_ref, ck_ref, cv_ref,
              o_ref, qr_ref, kr_ref, sf_ref, sb_ref):
    c, d = RET_BLOCK, HEAD_DIM
    seq = q_ref.shape[0]
    n_chunks = seq // c
    ctx_len = ck_ref.shape[0]
    scale = HEAD_DIM ** -0.5
    lgf_row = _log_sigmoid(df_ref[0])
    lgb_row = _log_sigmoid(db_ref[0])
    lgf = jnp.broadcast_to(lgf_row, (c, d))
    lgb = jnp.broadcast_to(lgb_row, (c, d))
    pos = lax.broadcasted_iota(jnp.int32, (c, d), 0).astype(F32)
    kdf = jnp.exp(lgf * (c - 1.0 - pos))
    kdb = jnp.exp(lgb * pos)
    qdf = jnp.exp(lgf * (pos + 1.0))
    qdb = jnp.exp(lgb * (c - pos))
    cdf = jnp.exp(lgf_row * float(c))
    cdb = jnp.exp(lgb_row * float(c))
    diff = (lax.broadcasted_iota(jnp.int32, (c, c), 0) - lax.broadcasted_iota(jnp.int32, (c, c), 1)).astype(F32)
    lgf_cc = jnp.broadcast_to(lgf_row[:, :1], (c, c))
    lgb_cc = jnp.broadcast_to(lgb_row[:, :1], (c, c))
    dmat = (jnp.where(diff >= 0, jnp.exp(lgf_cc * jnp.maximum(diff, 0.0)), 0.0)
            + jnp.where(diff <= 0, jnp.exp(lgb_cc * jnp.maximum(-diff, 0.0)), 0.0))

    cpos = lax.broadcasted_iota(jnp.int32, (ctx_len, d), 0).astype(F32)
    ckf = ck_ref[...].astype(F32) * scale
    cv = cv_ref[...]
    wf = jnp.exp(jnp.broadcast_to(lgf_row, (ctx_len, d)) * (ctx_len - 1.0 - cpos))
    wb = jnp.exp(jnp.broadcast_to(lgb_row, (ctx_len, d)) * cpos)
    s_f = _dot_tn((ckf * wf).astype(BF16), cv)
    s_b = _dot_tn((ckf * wb).astype(BF16), cv)

    rope_rows = 512

    def rope_blk(i, carry):
        r0 = pl.multiple_of(i * rope_rows, rope_rows)
        cs = cos_ref[pl.ds(r0, rope_rows), :]
        sn = sin_ref[pl.ds(r0, rope_rows), :]
        qr_ref[pl.ds(r0, rope_rows), :] = _rope(q_ref[pl.ds(r0, rope_rows), :].astype(F32), cs, sn)
        kr_ref[pl.ds(r0, rope_rows), :] = _rope(k_ref[pl.ds(r0, rope_rows), :].astype(F32), cs, sn) * scale
        return carry

    lax.fori_loop(0, seq // rope_rows, rope_blk, 0)

    def scan_blk(n, carry):
        s, t = carry
        r0 = pl.multiple_of(n * c, c)
        kvf = _dot_tn((kr_ref[pl.ds(r0, c), :] * kdf).astype(BF16), v_ref[pl.ds(r0, c), :])
        sf_ref[n] = s.astype(BF16)
        m = n_chunks - 1 - n
        m0 = pl.multiple_of(m * c, c)
        kvb = _dot_tn((kr_ref[pl.ds(m0, c), :] * kdb).astype(BF16), v_ref[pl.ds(m0, c), :])
        sb_ref[m] = t.astype(BF16)
        return s * cdf + kvf, t * cdb + kvb

    lax.fori_loop(0, n_chunks, scan_blk, (s_f, s_b), unroll=4)

    gn = gn_ref[...]

    def out_blk(n, carry):
        r0 = pl.multiple_of(n * c, c)
        qc = qr_ref[pl.ds(r0, c), :]
        kc = kr_ref[pl.ds(r0, c), :]
        a = _dot_nt(qc.astype(BF16), kc.astype(BF16))
        o = (_dot((a * dmat).astype(BF16), v_ref[pl.ds(r0, c), :])
             + _dot((qc * qdf).astype(BF16), sf_ref[n])
             + _dot((qc * qdb).astype(BF16), sb_ref[n]))
        mu = jnp.mean(o, axis=-1, keepdims=True)
        var = jnp.mean(jnp.square(o - mu), axis=-1, keepdims=True)
        y = (o - mu) * lax.rsqrt(var + EPS) * gn
        o_ref[pl.ds(r0, c), :] = (y * _silu(g_ref[pl.ds(r0, c), :].astype(F32))).astype(o_ref.dtype)
        return carry

    lax.fori_loop(0, n_chunks, out_blk, 0, unroll=4)


def _rope_tables(seq):
    axis_dim = HEAD_DIM // 2
    inv_freq = ROPE_BASE ** (-jnp.arange(0, axis_dim, 2, dtype=F32) / axis_dim)
    rows = seq // GRID_W
    ang_r = jnp.arange(rows, dtype=F32)[:, None] * inv_freq
    ang_c = jnp.arange(GRID_W, dtype=F32)[:, None] * inv_freq
    by_row = lambda a: jnp.repeat(a, GRID_W, axis=0)
    by_col = lambda a: jnp.tile(a, (rows, 1))
    cr, sr, cc, sc = by_row(jnp.cos(ang_r)), by_row(jnp.sin(ang_r)), by_col(jnp.cos(ang_c)), by_col(jnp.sin(ang_c))
    return jnp.concatenate([cr, cr, cc, cc], axis=-1), jnp.concatenate([-sr, sr, -sc, sc], axis=-1)


def _retention(proj, cproj, decay_f, decay_b, gn_w, batch, seq, ctx_len):
    hb = RET_WIDTH // HEAD_DIM
    cos, sin = _rope_tables(seq)
    dec_f = jnp.broadcast_to(decay_f.astype(F32)[:, None, None], (RET_HEADS, 1, HEAD_DIM))
    dec_b = jnp.broadcast_to(decay_b.astype(F32)[:, None, None], (RET_HEADS, 1, HEAD_DIM))
    n_chunks = seq // RET_BLOCK
    blk = lambda rows, fn: pl.BlockSpec((rows, HEAD_DIM), fn)
    return pl.pallas_call(
        _ret_body,
        grid=(RET_HEADS, batch),
        in_specs=[
            pl.BlockSpec((1, 1, HEAD_DIM), lambda h, b: (h, 0, 0)),
            pl.BlockSpec((1, 1, HEAD_DIM), lambda h, b: (h, 0, 0)),
            blk(1, lambda h, b: (0, h)),
            blk(seq, lambda h, b: (0, 0)),
            blk(seq, lambda h, b: (0, 0)),
            blk(seq, lambda h, b: (b, 5 * hb + h)),
            blk(seq, lambda h, b: (b, 2 * hb + h)),
            blk(seq, lambda h, b: (b, 3 * hb + h)),
            blk(seq, lambda h, b: (b, 6 * hb + h)),
            blk(ctx_len, lambda h, b: (b, 2 * hb + h)),
            blk(ctx_len, lambda h, b: (b, 3 * hb + h)),
        ],
        out_specs=blk(seq, lambda h, b: (b, h)),
        out_shape=jax.ShapeDtypeStruct((batch * seq, RET_WIDTH), BF16),
        scratch_shapes=[
            pltpu.VMEM((seq, HEAD_DIM), F32),
            pltpu.VMEM((seq, HEAD_DIM), F32),
            pltpu.VMEM((n_chunks, HEAD_DIM, HEAD_DIM), BF16),
            pltpu.VMEM((n_chunks, HEAD_DIM, HEAD_DIM), BF16),
        ],
        compiler_params=_cparams(2),
        name="retention",
    )(dec_f, dec_b, gn_w.reshape(1, RET_WIDTH), cos, sin, proj, proj, proj, proj, cproj, cproj)


def _split_bf16(x):
    hi = x.astype(BF16)
    lo = (x - hi.astype(F32)).astype(BF16)
    return hi, lo


def _cast_body(x_ref, o_ref):
    o_ref[...] = x_ref[...].astype(o_ref.dtype)


def _cast_bf16(w, tm=512):
    rows, cols = w.shape
    return pl.pallas_call(
        _cast_body,
        grid=(rows // tm,),
        in_specs=[pl.BlockSpec((tm, cols), lambda i: (i, 0))],
        out_specs=pl.BlockSpec((tm, cols), lambda i: (i, 0)),
        out_shape=jax.ShapeDtypeStruct((rows, cols), BF16),
        compiler_params=_cparams(1),
        name="cast_bf16",
    )(w)


def _outproj_body(na_ref, ret_ref, w_ref, x_ref, gate_ref, nw_ref, sh_ref, sc_ref, wr_ref,
                  h1_ref, u2_ref, lg_ref):
    half = na_ref.shape[1]
    n_e = wr_ref.shape[0]
    mix = _dot(na_ref[...], w_ref[:half, :]) + _dot(ret_ref[...], w_ref[half:, :])
    h1 = x_ref[...] + gate_ref[0] * mix
    h1_ref[...] = h1
    u2 = _rms_mod(h1, nw_ref[...], sh_ref[0], sc_ref[0])
    u2_ref[...] = u2
    uh, ul = _split_bf16(u2)
    wh, wl = _split_bf16(wr_ref[...])
    both = _dot_nt(jnp.concatenate([wh, wl], axis=0), uh)
    lg_ref[...] = both[:n_e] + (both[n_e:] + _dot_nt(wh, ul))


def _out_projection(na, ret, w_out_bf16, x2d, mod3, norm_w, w_router_t, seq, tm=512):
    rows, d = x2d.shape
    half = na.shape[1]
    per_b = seq // tm
    return pl.pallas_call(
        _outproj_body,
        grid=(rows // tm,),
        in_specs=[
            pl.BlockSpec((tm, half), lambda i: (i, 0)),
            pl.BlockSpec((tm, half), lambda i: (i, 0)),
            pl.BlockSpec((2 * half, d), lambda i: (0, 0)),
            pl.BlockSpec((tm, d), lambda i: (i, 0)),
            pl.BlockSpec((1, 1, d), lambda i: (i // per_b, 0, 2)),
            pl.BlockSpec((1, d), lambda i: (0, 0)),
            pl.BlockSpec((1, 1, d), lambda i: (i // per_b, 0, 3)),
            pl.BlockSpec((1, 1, d), lambda i: (i // per_b, 0, 4)),
            pl.BlockSpec((N_EXPERTS, d), lambda i: (0, 0)),
        ],
        out_specs=[
            pl.BlockSpec((tm, d), lambda i: (i, 0)),
            pl.BlockSpec((tm, d), lambda i: (i, 0)),
            pl.BlockSpec((N_EXPERTS, tm), lambda i: (0, i)),
        ],
        out_shape=[
            jax.ShapeDtypeStruct((rows, d), F32),
            jax.ShapeDtypeStruct((rows, d), F32),
            jax.ShapeDtypeStruct((N_EXPERTS, rows), F32),
        ],
        compiler_params=_cparams(1),
        name="out_projection",
    )(na, ret, w_out_bf16, x2d, mod3, norm_w.reshape(1, d), mod3, mod3, w_router_t)


LANES = 128
SUBLANES = 8


def _prefix_incl_lanes(x, tri):
    r, l = x.shape
    nb = l // LANES
    xs = jnp.concatenate([x[:, t * LANES:(t + 1) * LANES] for t in range(nb)], axis=0).astype(BF16)
    p = _dot(xs, tri)
    outs, run = [], jnp.zeros((r, 1), F32)
    for t in range(nb):
        blk = p[t * r:(t + 1) * r] + run
        outs.append(blk)
        run = blk[:, LANES - 1:LANES]
    return jnp.concatenate(outs, axis=1)


def _route_body(lg_ref, gidx_ref, spos_ref, gate_ref, rstart_ref, cnt_ref, *, cap):
    b = pl.program_id(0)
    n_e, seq = lg_ref.shape
    kf = float(cap)
    lg = lg_ref[...]
    ex = jnp.exp(lg - jnp.max(lg, axis=0, keepdims=True))
    aff = ex / jnp.sum(ex, axis=0, keepdims=True)

    def cond(c):
        return (c[0] < 4096) & (c[5] > 0.5)

    def step(c):
        it, lo, hi, thr, done, _ = c
        mid = 0.5 * (lo + hi)
        above = jnp.sum((aff > mid).astype(F32), axis=1, keepdims=True)
        hit = above == kf
        stuck = (mid <= lo) | (mid >= hi)
        active = done < 0.5
        thr = jnp.where(active & hit, mid, jnp.where(active & stuck, hi, thr))
        go = active & ~(hit | stuck)
        ge = above >= kf
        lo = jnp.where(go & ge, mid, lo)
        hi = jnp.where(go & ~ge, mid, hi)
        done = jnp.where(active & (hit | stuck), 1.0, done)
        return it + 1, lo, hi, thr, done, jnp.sum(1.0 - done)

    col = lambda v: jnp.full((n_e, 1), v, F32)
    init = (jnp.int32(0), col(-1.0), col(2.0), col(0.0), col(0.0), jnp.float32(n_e))
    thr = lax.while_loop(cond, step, init)[3]

    ii = lax.broadcasted_iota(jnp.int32, (LANES, LANES), 0)
    jj = lax.broadcasted_iota(jnp.int32, (LANES, LANES), 1)
    tri = (ii <= jj).astype(BF16)
    gt = aff > thr
    eq = (aff == thr).astype(F32)
    need = kf - jnp.sum(gt.astype(F32), axis=1, keepdims=True)
    eq_before = _prefix_incl_lanes(eq, tri) - eq
    mask = jnp.where(gt | ((eq > 0.5) & (eq_before < need)), 1.0, 0.0)

    slot = _prefix_incl_lanes(mask, tri) - mask
    cnt = jnp.sum(mask, axis=0, keepdims=True)
    row0 = _prefix_incl_lanes(jnp.broadcast_to(cnt, (8, seq)), tri)[:1] - cnt
    ei = lax.broadcasted_iota(jnp.int32, (n_e, n_e), 0)
    ej = lax.broadcasted_iota(jnp.int32, (n_e, n_e), 1)
    rank = _dot((ej < ei).astype(BF16), mask.astype(BF16))
    base = (b * (n_e * cap)).astype(F32)
    pos = row0 + rank + base
    rstart_ref[0] = (row0 + base).astype(jnp.int32)
    cnt_ref[0] = cnt.astype(jnp.int32)

    tok = lax.broadcasted_iota(jnp.int32, (1, seq), 1).astype(F32)
    tok_hi = jnp.floor(tok * (1.0 / 64))
    tok_lo = tok - 64.0 * tok_hi
    n_hi = ROUTE_SLOT_HI
    n_lo = cap // n_hi
    hi_iota = lax.broadcasted_iota(jnp.int32, (n_hi, seq), 0).astype(F32)
    lo_iota = lax.broadcasted_iota(jnp.int32, (n_lo, seq), 0).astype(F32)
    for e in range(n_e):
        se = slot[e:e + 1]
        s_hi = jnp.floor(se * (1.0 / n_lo))
        s_lo = se - n_lo * s_hi
        in_hi = jnp.where((s_hi == hi_iota) & (mask[e:e + 1] > 0.5), 1.0, 0.0)
        in_lo = (s_lo == lo_iota).astype(BF16)
        pe = pos[e:e + 1]
        p_hi = jnp.floor(pe * (1.0 / 128))
        p_lo = pe - 128.0 * p_hi
        a = aff[e:e + 1]
        a_hi = a.astype(BF16).astype(F32)
        a_mid = (a - a_hi).astype(BF16).astype(F32)
        a_lo = a - a_hi - a_mid
        vals = (tok_hi, tok_lo, p_hi, p_lo, a_hi, a_mid, a_lo)
        lhs = jnp.concatenate([in_hi * v for v in vals], axis=0).astype(BF16)
        got = _dot_nt(lhs, in_lo)
        part = lambda r: got[r * n_hi:(r + 1) * n_hi]
        gidx_ref[0, e] = (part(0) * 64.0 + part(1)).astype(jnp.int32) + b * seq
        spos_ref[0, e] = (part(2) * 128.0 + part(3)).astype(jnp.int32)
        gate_ref[0, e] = part(4) + part(5) + part(6)


ROUTE_SLOT_HI = 16


def _routing(logits_t, batch, seq):
    n_e = logits_t.shape[0]
    cap = CAPACITY_FACTOR * seq // n_e
    n_hi, n_lo = ROUTE_SLOT_HI, cap // ROUTE_SLOT_HI
    bec = pl.BlockSpec((1, n_e, n_hi, n_lo), lambda b: (b, 0, 0, 0))
    b1l = pl.BlockSpec((1, 1, seq), lambda b: (b, 0, 0))
    gidx, spos, gates, rstart, cnt = pl.pallas_call(
        functools.partial(_route_body, cap=cap),
        grid=(batch,),
        in_specs=[pl.BlockSpec((n_e, seq), lambda b: (0, b))],
        out_specs=[bec, bec, bec, b1l, b1l],
        out_shape=[
            jax.ShapeDtypeStruct((batch, n_e, n_hi, n_lo), jnp.int32),
            jax.ShapeDtypeStruct((batch, n_e, n_hi, n_lo), jnp.int32),
            jax.ShapeDtypeStruct((batch, n_e, n_hi, n_lo), F32),
            jax.ShapeDtypeStruct((batch, 1, seq), jnp.int32),
            jax.ShapeDtypeStruct((batch, 1, seq), jnp.int32),
        ],
        compiler_params=_cparams(1),
        name="routing",
    )(logits_t)
    flat = lambda a: a.reshape(batch, n_e, cap)
    return flat(gidx), flat(spos), flat(gates), rstart, cnt


def _moe_body(gidx_ref, gnext_ref, spos_ref, gate_ref, u2_hbm, wg_ref, wu_ref, wd_ref, y_hbm,
              stage_ref, xe_ref, acc_ref, yout_ref, gsem, ssem, *, n_f):
    e = pl.program_id(0)
    f = pl.program_id(1)
    n_e = pl.num_programs(0)
    m, d = acc_ref.shape
    sub = SUBLANES
    q = m // n_f

    def hbm_row(ref, r):
        return ref.at[lax.shift_right_logical(r, 3), pl.ds(r & (sub - 1), 1)]

    def gather_copy(idx_ref, i, k):
        return pltpu.make_async_copy(hbm_row(u2_hbm, idx_ref[0, i * sub + k]), stage_ref.at[i, pl.ds(k, 1)], gsem)

    def scatter_copy(pos_ref, i, k):
        return pltpu.make_async_copy(yout_ref.at[i, pl.ds(k, 1)], hbm_row(y_hbm, pos_ref[0, i * sub + k]), ssem)

    def for_rows(fn, tiles_per_trip=4):
        def body(t, c):
            for kk in range(tiles_per_trip * sub):
                fn(t * tiles_per_trip + kk // sub, kk % sub)
            return c
        lax.fori_loop(0, m // (sub * tiles_per_trip), body, 0)

    wait_gather = lambda: for_rows(lambda i, k: gather_copy(gidx_ref, i, k).wait())
    wait_scatter = lambda: for_rows(lambda i, k: scatter_copy(spos_ref, i, k).wait())

    @pl.when(f == 0)
    def _():
        @pl.when(e == 0)
        def _():
            for_rows(lambda i, k: gather_copy(gidx_ref, i, k).start())

        wait_gather()
        xe_ref[...] = stage_ref[...].reshape(m, d).astype(BF16)
        acc_ref[...] = jnp.zeros_like(acc_ref)

    i0 = f * (q // sub)
    for kk in range(q):
        gather_copy(gnext_ref, i0 + kk // sub, kk % sub).start()
    x = xe_ref[...]
    hid = _silu(_dot(x, wg_ref[...].astype(BF16))) * _dot(x, wu_ref[...].astype(BF16))
    acc_ref[...] += _dot(hid.astype(BF16), wd_ref[...].astype(BF16))

    @pl.when(f == n_f - 1)
    def _():
        @pl.when(e > 0)
        def _():
            wait_scatter()

        g = jnp.transpose(jnp.broadcast_to(gate_ref[...], (LANES, m)))
        for k in range(d // LANES):
            cols = slice(k * LANES, (k + 1) * LANES)
            yout_ref[:, :, cols] = (acc_ref[:, cols] * g).reshape(m // sub, sub, LANES)
        for_rows(lambda i, k: scatter_copy(spos_ref, i, k).start())

        @pl.when(e == n_e - 1)
        def _():
            wait_scatter()
            wait_gather()


def _moe_ffn(gidx, spos, gates, u2, w_gate, w_up, w_down, tf=256):
    n_e, _, m = gidx.shape
    rows, d = u2.shape
    ff = w_gate.shape[2]
    n_f = ff // tf
    sub = SUBLANES
    smem = lambda fn: pl.BlockSpec((None, 1, m), fn, memory_space=pltpu.SMEM)
    y = pl.pallas_call(
        functools.partial(_moe_body, n_f=n_f),
        grid=(n_e, n_f),
        in_specs=[
            smem(lambda e, f: (e, 0, 0)),
            smem(lambda e, f: (jnp.minimum(e + 1, n_e - 1), 0, 0)),
            smem(lambda e, f: (e, 0, 0)),
            pl.BlockSpec((None, 1, m), lambda e, f: (e, 0, 0)),
            pl.BlockSpec(memory_space=pl.ANY),
            pl.BlockSpec((None, d, tf), lambda e, f: (e, 0, f)),
            pl.BlockSpec((None, d, tf), lambda e, f: (e, 0, f)),
            pl.BlockSpec((None, tf, d), lambda e, f: (e, f, 0)),
        ],
        out_specs=pl.BlockSpec(memory_space=pl.ANY),
        out_shape=jax.ShapeDtypeStruct((n_e * m // sub, sub, d), F32),
        scratch_shapes=[
            pltpu.VMEM((m // sub, sub, d), F32),
            pltpu.VMEM((m, d), BF16),
            pltpu.VMEM((m, d), F32),
            pltpu.VMEM((m // sub, sub, d), F32),
            pltpu.SemaphoreType.DMA,
            pltpu.SemaphoreType.DMA,
        ],
        compiler_params=_cparams(2),
        name="moe_ffn",
    )(gidx, gidx, spos, gates, u2.reshape(rows // sub, sub, d), w_gate, w_up, w_down)
    return y.reshape(n_e * m, d)


COMBINE_TOKENS = 256
COMBINE_ROWS = 256


def _combine_body(tile_ref, chunk_ref, flag_ref, h1_ref, y_ref, rs_ref, cnt_ref, gate_ref, nw_ref, o_ref, acc_ref):
    s = pl.program_id(0)
    t, ch = COMBINE_TOKENS, COMBINE_ROWS
    flags = flag_ref[s]
    is_first, is_last, is_valid = (flags & 1) == 1, (flags & 2) == 2, (flags & 4) == 4

    @pl.when(is_first)
    def _():
        acc_ref[...] = jnp.zeros_like(acc_ref)

    @pl.when(is_valid)
    def _():
        first = rs_ref[0]
        last = first + cnt_ref[0]
        row = chunk_ref[s] * ch + lax.broadcasted_iota(jnp.int32, (ch, t), 0)
        sel = ((row >= first) & (row < last)).astype(BF16)
        acc_ref[...] += _dot_tn(sel, y_ref[...].astype(BF16))

    @pl.when(is_last)
    def _():
        h2 = h1_ref[...] + gate_ref[0] * acc_ref[...]
        ms = jnp.mean(h2 * h2, axis=-1, keepdims=True)
        o_ref[...] = h2 * lax.rsqrt(ms + EPS) * nw_ref[...]


def _combine_schedule(row_prefix, n_tiles, n_chunks):
    t, ch = COMBINE_TOKENS, COMBINE_ROWS
    n_steps = n_tiles + n_chunks
    bounds = row_prefix[::t]
    lo = jnp.minimum(bounds[:-1] // ch, n_chunks - 1)
    hi = jnp.clip((bounds[1:] + ch - 1) // ch, lo + 1, n_chunks)
    n = hi - lo
    end = jnp.cumsum(n)
    start = end - n
    step = jnp.arange(n_steps, dtype=jnp.int32)
    tile = jnp.minimum(jnp.sum(step[:, None] >= end[None, :], axis=1), n_tiles - 1).astype(jnp.int32)
    valid = step < end[-1]
    k = jnp.minimum(step - start[tile], n[tile] - 1)
    chunk = (lo[tile] + k).astype(jnp.int32)
    flags = ((valid & (step == start[tile])).astype(jnp.int32)
             + 2 * (valid & (step == end[tile] - 1)).astype(jnp.int32) + 4 * valid.astype(jnp.int32))
    return tile, chunk, flags


def _combine(row_prefix, h1, y, rstart, cnt, mod3, final_w, seq):
    rows, d = h1.shape
    t, ch = COMBINE_TOKENS, COMBINE_ROWS
    per_b = seq // t
    n_tiles, n_chunks = rows // t, y.shape[0] // ch
    tile, chunk, flags = _combine_schedule(row_prefix, n_tiles, n_chunks)
    grid_spec = pltpu.PrefetchScalarGridSpec(
        num_scalar_prefetch=3,
        grid=(n_tiles + n_chunks,),
        in_specs=[
            pl.BlockSpec((t, d), lambda s, ti, ci, fl: (ti[s], 0)),
            pl.BlockSpec((ch, d), lambda s, ti, ci, fl: (ci[s], 0)),
            pl.BlockSpec((1, 1, t), lambda s, ti, ci, fl: (ti[s], 0, 0)),
            pl.BlockSpec((1, 1, t), lambda s, ti, ci, fl: (ti[s], 0, 0)),
            pl.BlockSpec((1, 1, d), lambda s, ti, ci, fl: (ti[s] // per_b, 0, 5)),
            pl.BlockSpec((1, d), lambda s, ti, ci, fl: (0, 0)),
        ],
        out_specs=pl.BlockSpec((t, d), lambda s, ti, ci, fl: (ti[s], 0)),
        scratch_shapes=[pltpu.VMEM((t, d), F32)],
    )
    return pl.pallas_call(
        _combine_body,
        grid_spec=grid_spec,
        out_shape=jax.ShapeDtypeStruct((rows, d), F32),
        compiler_params=_cparams(1),
        name="combine",
    )(tile, chunk, flags, h1, y, rstart, cnt, mod3, final_w.reshape(1, d))


def kernel(x, c, ctx, c_ctx, w_mod, b_mod, norm_mix_w, norm_ffn_w, w_in, na_rpb, ret_decay_fwd,
           ret_decay_bwd, ret_gn_w, w_out, w_router, w_gate, w_up, w_down, final_norm_w):
    batch, seq, d = x.shape
    ctx_len = ctx.shape[1]
    assert w_mod.shape[0] == 1, "one trunk layer"
    assert seq % (NA_QROWS * GRID_W) == 0 and seq // GRID_W >= 3 * NA_QROWS
    n_e = w_router.shape[2]
    cap = CAPACITY_FACTOR * seq // n_e

    mod_rows = 8
    cc = jnp.concatenate([c, c_ctx[None], jnp.zeros((mod_rows - batch - 1, d), c.dtype)], axis=0)
    mod3 = _modulation(cc, w_mod[0], b_mod[0]).reshape(mod_rows, 1, N_MOD * d)

    x2d = x.reshape(batch * seq, d)
    tm = 1024
    w_in_bf16 = _cast_bf16(w_in[0], tm=256)
    proj = _in_projection(x2d, norm_mix_w[0], mod3, lambda i: i // (seq // tm), w_in_bf16,
                          w_in.shape[2], tm, w_in.shape[2] // 4)
    cproj = _in_projection(ctx.reshape(batch * ctx_len, d), norm_mix_w[0], mod3, lambda i: batch,
                           w_in_bf16, KV_COLS, batch * ctx_len, 1024)

    na = _neighbourhood_attention(proj, cproj, na_rpb[0], batch, seq, ctx_len)
    ret = _retention(proj, cproj, ret_decay_fwd[0], ret_decay_bwd[0], ret_gn_w[0], batch, seq, ctx_len)

    h1, u2, logits_t = _out_projection(na, ret, _cast_bf16(w_out[0]), x2d, mod3, norm_ffn_w[0],
                                       w_router[0].T, seq)

    gidx, spos, gates, rstart, cnt = _routing(logits_t, batch, seq)
    per_expert = lambda a: a.transpose(1, 0, 2).reshape(n_e, 1, batch * cap)
    y = _moe_ffn(per_expert(gidx), per_expert(spos), per_expert(gates), u2, w_gate[0], w_up[0], w_down[0])

    row_prefix = jnp.concatenate([rstart.reshape(-1), jnp.full((1,), batch * n_e * cap, jnp.int32)])
    t = COMBINE_TOKENS
    out = _combine(row_prefix, h1, y, rstart.reshape(batch * seq // t, 1, t), cnt.reshape(batch * seq // t, 1, t),
                   mod3, final_norm_w, seq)
    return out.reshape(batch, seq, d)
```

```python
import functools

import jax
import jax.numpy as jnp
from jax import lax
from jax.experimental import pallas as pl
from jax.experimental.pallas import tpu as pltpu

GRID_W = 64
HEAD_DIM = 128
NA_HEADS = 8
RET_HEADS = 8
NA_WIDTH = NA_HEADS * HEAD_DIM
RET_WIDTH = RET_HEADS * HEAD_DIM
WIN_ROWS = 8
WIN_COLS = 16
RET_BLOCK = 256
ROPE_BASE = 10000.0
N_EXPERTS = 16
CAPACITY_FACTOR = 2
N_MOD = 6
EPS = 1e-6
NEG_INF = -1e30
KV_COLS = 2 * NA_WIDTH + 2 * RET_WIDTH

F32 = jnp.float32
BF16 = jnp.bfloat16
MIB = 1024 * 1024
VMEM_LIMIT_V7X = 56 * MIB


def _cparams(n_axes):
    return pltpu.CompilerParams(
        dimension_semantics=("arbitrary",) * n_axes, vmem_limit_bytes=VMEM_LIMIT_V7X)


def _silu(x):
    return x * jax.nn.sigmoid(x)


def _dot(a, b):
    return jnp.dot(a, b, preferred_element_type=F32)


def _dot_nt(a, b):
    return lax.dot_general(a, b, (((1,), (1,)), ((), ())), preferred_element_type=F32)


def _dot_tn(a, b):
    return lax.dot_general(a, b, (((0,), (0,)), ((), ())), preferred_element_type=F32)


def _mod_body(c_ref, w_ref, b_ref, o_ref):
    a = _silu(c_ref[...]).astype(BF16)
    o_ref[...] = _dot(a, w_ref[...].astype(BF16)) + b_ref[...]


def _modulation(cc, w_mod, b_mod, tn=1024):
    rows, d = cc.shape
    n = w_mod.shape[1]
    return pl.pallas_call(
        _mod_body,
        grid=(n // tn,),
        in_specs=[
            pl.BlockSpec((rows, d), lambda j: (0, 0)),
            pl.BlockSpec((d, tn), lambda j: (0, j)),
            pl.BlockSpec((1, tn), lambda j: (0, j)),
        ],
        out_specs=pl.BlockSpec((rows, tn), lambda j: (0, j)),
        out_shape=jax.ShapeDtypeStruct((rows, n), F32),
        compiler_params=_cparams(1),
        name="modulation",
    )(cc, w_mod, b_mod.reshape(1, n))


def _rms_mod(x, nw, shift, scale):
    ms = jnp.mean(x * x, axis=-1, keepdims=True)
    y = x * lax.rsqrt(ms + EPS) * nw
    return y * (1.0 + scale) + shift


def _inproj_body(x_ref, nw_ref, sh_ref, sc_ref, w_ref, o_ref, u_ref):
    @pl.when(pl.program_id(1) == 0)
    def _():
        u_ref[...] = _rms_mod(x_ref[...], nw_ref[...], sh_ref[0], sc_ref[0]).astype(BF16)

    o_ref[...] = _dot(u_ref[...], w_ref[...].astype(BF16)).astype(o_ref.dtype)


def _in_projection(x2d, norm_w, mod3, mod_row_fn, w_in, n_cols, tm, tn):
    rows, d = x2d.shape
    return pl.pallas_call(
        _inproj_body,
        grid=(rows // tm, n_cols // tn),
        in_specs=[
            pl.BlockSpec((tm, d), lambda i, j: (i, 0)),
            pl.BlockSpec((1, d), lambda i, j: (0, 0)),
            pl.BlockSpec((1, 1, d), lambda i, j: (mod_row_fn(i), 0, 0)),
            pl.BlockSpec((1, 1, d), lambda i, j: (mod_row_fn(i), 0, 1)),
            pl.BlockSpec((d, tn), lambda i, j: (0, j)),
        ],
        out_specs=pl.BlockSpec((tm, tn), lambda i, j: (i, j)),
        out_shape=jax.ShapeDtypeStruct((rows, n_cols), BF16),
        scratch_shapes=[pltpu.VMEM((tm, d), BF16)],
        compiler_params=_cparams(2),
        name="in_projection",
    )(x2d, norm_w.reshape(1, d), mod3, mod3, w_in)


NA_QROWS = 8
NA_KROWS = 2 * NA_QROWS


def _na_row_offset(tile_kind, i, w, rows):
    half = WIN_ROWS // 2
    if tile_kind == 0:
        r, key = i, w
    elif tile_kind == 1:
        r, key = NA_QROWS + i, NA_QROWS - half + w
    else:
        r, key = rows - NA_QROWS + i, rows - NA_KROWS + w
    start = min(max(r - half, 0), rows - WIN_ROWS)
    if not (start <= key < start + WIN_ROWS):
        return None
    return key - r + (WIN_ROWS - 1)


def _na_build_bias(rpb_ref, bias_ref, h, rows):
    w = GRID_W
    cq = lax.broadcasted_iota(jnp.int32, (w, 2 * w), 0)
    ck = lax.broadcasted_iota(jnp.int32, (w, 2 * w), 1) % w
    col_start = jnp.clip(cq - WIN_COLS // 2, 0, w - WIN_COLS)
    col_ok = (ck >= col_start) & (ck < col_start + WIN_COLS)
    col_off = jnp.clip(ck - cq, -(WIN_COLS - 1), WIN_COLS - 1) + (WIN_COLS - 1)
    neg = jnp.full((w, 2 * w), NEG_INF, F32)
    n_ro, n_co = 2 * WIN_ROWS - 1, 2 * WIN_COLS - 1
    tabs = []
    for ro in range(n_ro):
        t = jnp.zeros((w, 2 * w), F32)
        for j in range(n_co):
            t = jnp.where(col_off == j, rpb_ref[h, ro * n_co + j], t)
        tabs.append(jnp.where(col_ok, t, neg))
    left = lax.broadcasted_iota(jnp.int32, (w, 2 * w), 1) < w
    for kind in range(3):
        for i in range(NA_QROWS):
            for wp in range(NA_KROWS // 2):
                ra = _na_row_offset(kind, i, 2 * wp, rows)
                rb = _na_row_offset(kind, i, 2 * wp + 1, rows)
                ta = neg if ra is None else tabs[ra]
                tb = neg if rb is None else tabs[rb]
                blk = ta if ra == rb else jnp.where(left, ta, tb)
                bias_ref[kind, i * w:(i + 1) * w, wp * 2 * w:(wp + 1) * 2 * w] = blk


def _na_body(rpb_ref, q_ref, k_ref, v_ref, ck_ref, cv_ref, o_ref, bias_ref, sa_ref, sb_ref, *, rows):
    h = pl.program_id(0)
    w = GRID_W
    tq, tk = NA_QROWS * w, NA_KROWS * w
    n_tiles = rows // NA_QROWS
    scale = HEAD_DIM ** -0.5

    @pl.when(pl.program_id(1) == 0)
    def _():
        _na_build_bias(rpb_ref, bias_ref, h, rows)

    def offsets(t):
        krow0 = jnp.clip(t * NA_QROWS - WIN_ROWS // 2, 0, rows - NA_KROWS)
        return pl.multiple_of(t * tq, tq), pl.multiple_of(krow0 * w, 4 * w)

    def scores(t, s_ref):
        kind = jnp.where(t == 0, 0, jnp.where(t == n_tiles - 1, 2, 1))
        q0, k0 = offsets(t)
        q = q_ref[pl.ds(q0, tq), :]
        s_ref[:, :tk] = _dot_nt(q, k_ref[pl.ds(k0, tk), :]) * scale + bias_ref[kind]
        s_ref[:, tk:] = _dot_nt(q, ck_ref[...]) * scale

    def attend(t, s_ref):
        q0, k0 = offsets(t)
        s = s_ref[...]
        p = jnp.exp(s - jnp.max(s, axis=-1, keepdims=True))
        l = jnp.sum(p, axis=-1, keepdims=True)
        pb = p.astype(BF16)
        o = _dot(pb[:, :tk], v_ref[pl.ds(k0, tk), :]) + _dot(pb[:, tk:], cv_ref[...])
        o_ref[pl.ds(q0, tq), :] = (o / l).astype(o_ref.dtype)

    scores(jnp.int32(0), sa_ref)

    def pair(i, carry):
        t = 2 * i
        scores(t + 1, sb_ref)
        attend(t, sa_ref)
        scores(t + 2, sa_ref)
        attend(t + 1, sb_ref)
        return carry

    lax.fori_loop(0, n_tiles // 2 - 1, pair, 0)
    scores(jnp.int32(n_tiles - 1), sb_ref)
    attend(jnp.int32(n_tiles - 2), sa_ref)
    attend(jnp.int32(n_tiles - 1), sb_ref)


def _neighbourhood_attention(proj, cproj, rpb, batch, seq, ctx_len):
    rows = seq // GRID_W
    hb = NA_WIDTH // HEAD_DIM
    n_rpb = (2 * WIN_ROWS - 1) * (2 * WIN_COLS - 1)
    tq, tk = NA_QROWS * GRID_W, NA_KROWS * GRID_W
    grid_spec = pltpu.PrefetchScalarGridSpec(
        num_scalar_prefetch=0,
        grid=(NA_HEADS, batch),
        in_specs=[
            pl.BlockSpec(memory_space=pltpu.SMEM),
            pl.BlockSpec((seq, HEAD_DIM), lambda h, b: (b, 4 * hb + h)),
            pl.BlockSpec((seq, HEAD_DIM), lambda h, b: (b, h)),
            pl.BlockSpec((seq, HEAD_DIM), lambda h, b: (b, hb + h)),
            pl.BlockSpec((ctx_len, HEAD_DIM), lambda h, b: (b, h)),
            pl.BlockSpec((ctx_len, HEAD_DIM), lambda h, b: (b, hb + h)),
        ],
        out_specs=pl.BlockSpec((seq, HEAD_DIM), lambda h, b: (b, h)),
        scratch_shapes=[
            pltpu.VMEM((3, tq, tk), F32),
            pltpu.VMEM((tq, tk + ctx_len), F32),
            pltpu.VMEM((tq, tk + ctx_len), F32),
        ],
    )
    return pl.pallas_call(
        functools.partial(_na_body, rows=rows),
        grid_spec=grid_spec,
        out_shape=jax.ShapeDtypeStruct((batch * seq, NA_WIDTH), BF16),
        compiler_params=_cparams(2),
        name="neighbourhood_attention",
    )(rpb.reshape(NA_HEADS, n_rpb), proj, proj, proj, cproj, cproj)


def _log_sigmoid(x):
    return -(jnp.maximum(-x, 0.0) + jnp.log1p(jnp.exp(-jnp.abs(x))))


def _rope(x, cos, sin_signed):
    lane = lax.broadcasted_iota(jnp.int32, x.shape, 1)
    quarter = HEAD_DIM // 4
    partner = jnp.where(lane % (2 * quarter) < quarter,
                        pltpu.roll(x, HEAD_DIM - quarter, 1), pltpu.roll(x, quarter, 1))
    return x * cos + partner * sin_signed


def _ret_body(df_ref, db_ref, gn_ref, cos_ref, sin_ref, q_ref, k_ref, v_ref, g_ref, ck_ref, cv_ref,
              o_ref, qr_ref, kr_ref, sf_ref, sb_ref):
    c, d = RET_BLOCK, HEAD_DIM
    seq = q_ref.shape[0]
    n_chunks = seq // c
    ctx_len = ck_ref.shape[0]
    scale = HEAD_DIM ** -0.5
    lgf_row = _log_sigmoid(df_ref[0])
    lgb_row = _log_sigmoid(db_ref[0])
    lgf = jnp.broadcast_to(lgf_row, (c, d))
    lgb = jnp.broadcast_to(lgb_row, (c, d))
    pos = lax.broadcasted_iota(jnp.int32, (c, d), 0).astype(F32)
    kdf = jnp.exp(lgf * (c - 1.0 - pos))
    kdb = jnp.exp(lgb * pos)
    qdf = jnp.exp(lgf * (pos + 1.0))
    qdb = jnp.exp(lgb * (c - pos))
    cdf = jnp.exp(lgf_row * float(c))
    cdb = jnp.exp(lgb_row * float(c))
    diff = (lax.broadcasted_iota(jnp.int32, (c, c), 0) - lax.broadcasted_iota(jnp.int32, (c, c), 1)).astype(F32)
    lgf_cc = jnp.broadcast_to(lgf_row[:, :1], (c, c))
    lgb_cc = jnp.broadcast_to(lgb_row[:, :1], (c, c))
    dmat = (jnp.where(diff >= 0, jnp.exp(lgf_cc * jnp.maximum(diff, 0.0)), 0.0)
            + jnp.where(diff <= 0, jnp.exp(lgb_cc * jnp.maximum(-diff, 0.0)), 0.0))

    cpos = lax.broadcasted_iota(jnp.int32, (ctx_len, d), 0).astype(F32)
    ckf = ck_ref[...].astype(F32) * scale
    cv = cv_ref[...]
    wf = jnp.exp(jnp.broadcast_to(lgf_row, (ctx_len, d)) * (ctx_len - 1.0 - cpos))
    wb = jnp.exp(jnp.broadcast_to(lgb_row, (ctx_len, d)) * cpos)
    s_f = _dot_tn((ckf * wf).astype(BF16), cv)
    s_b = _dot_tn((ckf * wb).astype(BF16), cv)

    rope_rows = 512

    def rope_blk(i, carry):
        r0 = pl.multiple_of(i * rope_rows, rope_rows)
        cs = cos_ref[pl.ds(r0, rope_rows), :]
        sn = sin_ref[pl.ds(r0, rope_rows), :]
        qr_ref[pl.ds(r0, rope_rows), :] = _rope(q_ref[pl.ds(r0, rope_rows), :].astype(F32), cs, sn)
        kr_ref[pl.ds(r0, rope_rows), :] = _rope(k_ref[pl.ds(r0, rope_rows), :].astype(F32), cs, sn) * scale
        return carry

    lax.fori_loop(0, seq // rope_rows, rope_blk, 0)

    def scan_blk(n, carry):
        s, t = carry
        r0 = pl.multiple_of(n * c, c)
        kvf = _dot_tn((kr_ref[pl.ds(r0, c), :] * kdf).astype(BF16), v_ref[pl.ds(r0, c), :])
        sf_ref[n] = s.astype(BF16)
        m = n_chunks - 1 - n
        m0 = pl.multiple_of(m * c, c)
        kvb = _dot_tn((kr_ref[pl.ds(m0, c), :] * kdb).astype(BF16), v_ref[pl.ds(m0, c), :])
        sb_ref[m] = t.astype(BF16)
        return s * cdf + kvf, t * cdb + kvb

    lax.fori_loop(0, n_chunks, scan_blk, (s_f, s_b), unroll=4)

    gn = gn_ref[...]

    def out_blk(n, carry):
        r0 = pl.multiple_of(n * c, c)
        qc = qr_ref[pl.ds(r0, c), :]
        kc = kr_ref[pl.ds(r0, c), :]
        a = _dot_nt(qc.astype(BF16), kc.astype(BF16))
        o = (_dot((a * dmat).astype(BF16), v_ref[pl.ds(r0, c), :])
             + _dot((qc * qdf).astype(BF16), sf_ref[n])
             + _dot((qc * qdb).astype(BF16), sb_ref[n]))
        mu = jnp.mean(o, axis=-1, keepdims=True)
        var = jnp.mean(jnp.square(o - mu), axis=-1, keepdims=True)
        y = (o - mu) * lax.rsqrt(var + EPS) * gn
        o_ref[pl.ds(r0, c), :] = (y * _silu(g_ref[pl.ds(r0, c), :].astype(F32))).astype(o_ref.dtype)
        return carry

    lax.fori_loop(0, n_chunks, out_blk, 0, unroll=4)


def _rope_tables(seq):
    axis_dim = HEAD_DIM // 2
    inv_freq = ROPE_BASE ** (-jnp.arange(0, axis_dim, 2, dtype=F32) / axis_dim)
    rows = seq // GRID_W
    ang_r = jnp.arange(rows, dtype=F32)[:, None] * inv_freq
    ang_c = jnp.arange(GRID_W, dtype=F32)[:, None] * inv_freq
    by_row = lambda a: jnp.repeat(a, GRID_W, axis=0)
    by_col = lambda a: jnp.tile(a, (rows, 1))
    cr, sr, cc, sc = by_row(jnp.cos(ang_r)), by_row(jnp.sin(ang_r)), by_col(jnp.cos(ang_c)), by_col(jnp.sin(ang_c))
    return jnp.concatenate([cr, cr, cc, cc], axis=-1), jnp.concatenate([-sr, sr, -sc, sc], axis=-1)


def _retention(proj, cproj, decay_f, decay_b, gn_w, batch, seq, ctx_len):
    hb = RET_WIDTH // HEAD_DIM
    cos, sin = _rope_tables(seq)
    dec_f = jnp.broadcast_to(decay_f.astype(F32)[:, None, None], (RET_HEADS, 1, HEAD_DIM))
    dec_b = jnp.broadcast_to(decay_b.astype(F32)[:, None, None], (RET_HEADS, 1, HEAD_DIM))
    n_chunks = seq // RET_BLOCK
    blk = lambda rows, fn: pl.BlockSpec((rows, HEAD_DIM), fn)
    return pl.pallas_call(
        _ret_body,
        grid=(RET_HEADS, batch),
        in_specs=[
            pl.BlockSpec((1, 1, HEAD_DIM), lambda h, b: (h, 0, 0)),
            pl.BlockSpec((1, 1, HEAD_DIM), lambda h, b: (h, 0, 0)),
            blk(1, lambda h, b: (0, h)),
            blk(seq, lambda h, b: (0, 0)),
            blk(seq, lambda h, b: (0, 0)),
            blk(seq, lambda h, b: (b, 5 * hb + h)),
            blk(seq, lambda h, b: (b, 2 * hb + h)),
            blk(seq, lambda h, b: (b, 3 * hb + h)),
            blk(seq, lambda h, b: (b, 6 * hb + h)),
            blk(ctx_len, lambda h, b: (b, 2 * hb + h)),
            blk(ctx_len, lambda h, b: (b, 3 * hb + h)),
        ],
        out_specs=blk(seq, lambda h, b: (b, h)),
        out_shape=jax.ShapeDtypeStruct((batch * seq, RET_WIDTH), BF16),
        scratch_shapes=[
            pltpu.VMEM((seq, HEAD_DIM), F32),
            pltpu.VMEM((seq, HEAD_DIM), F32),
            pltpu.VMEM((n_chunks, HEAD_DIM, HEAD_DIM), BF16),
            pltpu.VMEM((n_chunks, HEAD_DIM, HEAD_DIM), BF16),
        ],
        compiler_params=_cparams(2),
        name="retention",
    )(dec_f, dec_b, gn_w.reshape(1, RET_WIDTH), cos, sin, proj, proj, proj, proj, cproj, cproj)


def _split_bf16(x):
    hi = x.astype(BF16)
    lo = (x - hi.astype(F32)).astype(BF16)
    return hi, lo


def _cast_body(x_ref, o_ref):
    o_ref[...] = x_ref[...].astype(o_ref.dtype)


def _cast_bf16(w, tm=512):
    rows, cols = w.shape
    return pl.pallas_call(
        _cast_body,
        grid=(rows // tm,),
        in_specs=[pl.BlockSpec((tm, cols), lambda i: (i, 0))],
        out_specs=pl.BlockSpec((tm, cols), lambda i: (i, 0)),
        out_shape=jax.ShapeDtypeStruct((rows, cols), BF16),
        compiler_params=_cparams(1),
        name="cast_bf16",
    )(w)


def _outproj_body(na_ref, ret_ref, w_ref, x_ref, gate_ref, nw_ref, sh_ref, sc_ref, wr_ref,
                  h1_ref, u2_ref, lg_ref):
    half = na_ref.shape[1]
    n_e = wr_ref.shape[0]
    mix = _dot(na_ref[...], w_ref[:half, :]) + _dot(ret_ref[...], w_ref[half:, :])
    h1 = x_ref[...] + gate_ref[0] * mix
    h1_ref[...] = h1
    u2 = _rms_mod(h1, nw_ref[...], sh_ref[0], sc_ref[0])
    u2_ref[...] = u2
    uh, ul = _split_bf16(u2)
    wh, wl = _split_bf16(wr_ref[...])
    both = _dot_nt(jnp.concatenate([wh, wl], axis=0), uh)
    lg_ref[...] = both[:n_e] + (both[n_e:] + _dot_nt(wh, ul))


def _out_projection(na, ret, w_out_bf16, x2d, mod3, norm_w, w_router_t, seq, tm=512):
    rows, d = x2d.shape
    half = na.shape[1]
    per_b = seq // tm
    return pl.pallas_call(
        _outproj_body,
        grid=(rows // tm,),
        in_specs=[
            pl.BlockSpec((tm, half), lambda i: (i, 0)),
            pl.BlockSpec((tm, half), lambda i: (i, 0)),
            pl.BlockSpec((2 * half, d), lambda i: (0, 0)),
            pl.BlockSpec((tm, d), lambda i: (i, 0)),
            pl.BlockSpec((1, 1, d), lambda i: (i // per_b, 0, 2)),
            pl.BlockSpec((1, d), lambda i: (0, 0)),
            pl.BlockSpec((1, 1, d), lambda i: (i // per_b, 0, 3)),
            pl.BlockSpec((1, 1, d), lambda i: (i // per_b, 0, 4)),
            pl.BlockSpec((N_EXPERTS, d), lambda i: (0, 0)),
        ],
        out_specs=[
            pl.BlockSpec((tm, d), lambda i: (i, 0)),
            pl.BlockSpec((tm, d), lambda i: (i, 0)),
            pl.BlockSpec((N_EXPERTS, tm), lambda i: (0, i)),
        ],
        out_shape=[
            jax.ShapeDtypeStruct((rows, d), F32),
            jax.ShapeDtypeStruct((rows, d), F32),
            jax.ShapeDtypeStruct((N_EXPERTS, rows), F32),
        ],
        compiler_params=_cparams(1),
        name="out_projection",
    )(na, ret, w_out_bf16, x2d, mod3, norm_w.reshape(1, d), mod3, mod3, w_router_t)


LANES = 128
SUBLANES = 8


def _prefix_incl_lanes(x, tri):
    r, l = x.shape
    nb = l // LANES
    xs = jnp.concatenate([x[:, t * LANES:(t + 1) * LANES] for t in range(nb)], axis=0).astype(BF16)
    p = _dot(xs, tri)
    outs, run = [], jnp.zeros((r, 1), F32)
    for t in range(nb):
        blk = p[t * r:(t + 1) * r] + run
        outs.append(blk)
        run = blk[:, LANES - 1:LANES]
    return jnp.concatenate(outs, axis=1)


NOT_ROUTED = -(1 << 20)


def _route_body(lg_ref, gidx_ref, gate_ref, slot_ref, before_ref, *, cap):
    b = pl.program_id(0)
    n_e, seq = lg_ref.shape
    kf = float(cap)
    lg = lg_ref[...]
    ex = jnp.exp(lg - jnp.max(lg, axis=0, keepdims=True))
    aff = ex / jnp.sum(ex, axis=0, keepdims=True)

    def cond(c):
        return (c[0] < 4096) & (c[5] > 0.5)

    def step(c):
        it, lo, hi, thr, done, _ = c
        mid = 0.5 * (lo + hi)
        above = jnp.sum((aff > mid).astype(F32), axis=1, keepdims=True)
        hit = above == kf
        stuck = (mid <= lo) | (mid >= hi)
        active = done < 0.5
        thr = jnp.where(active & hit, mid, jnp.where(active & stuck, hi, thr))
        go = active & ~(hit | stuck)
        ge = above >= kf
        lo = jnp.where(go & ge, mid, lo)
        hi = jnp.where(go & ~ge, mid, hi)
        done = jnp.where(active & (hit | stuck), 1.0, done)
        return it + 1, lo, hi, thr, done, jnp.sum(1.0 - done)

    col = lambda v: jnp.full((n_e, 1), v, F32)
    init = (jnp.int32(0), col(-1.0), col(2.0), col(0.0), col(0.0), jnp.float32(n_e))
    thr = lax.while_loop(cond, step, init)[3]

    ii = lax.broadcasted_iota(jnp.int32, (LANES, LANES), 0)
    jj = lax.broadcasted_iota(jnp.int32, (LANES, LANES), 1)
    tri = (ii <= jj).astype(BF16)
    gt = aff > thr
    eq = (aff == thr).astype(F32)
    need = kf - jnp.sum(gt.astype(F32), axis=1, keepdims=True)
    eq_before = _prefix_incl_lanes(eq, tri) - eq
    mask = jnp.where(gt | ((eq > 0.5) & (eq_before < need)), 1.0, 0.0)

    slot = _prefix_incl_lanes(mask, tri) - mask
    before_ref[0] = slot.astype(jnp.int32)
    slot_ref[0] = jnp.where(mask > 0.5, slot, float(NOT_ROUTED)).astype(jnp.int32)

    tok = lax.broadcasted_iota(jnp.int32, (1, seq), 1).astype(F32)
    tok_hi = jnp.floor(tok * (1.0 / 64))
    tok_lo = tok - 64.0 * tok_hi
    n_hi = ROUTE_SLOT_HI
    n_lo = cap // n_hi
    hi_iota = lax.broadcasted_iota(jnp.int32, (n_hi, seq), 0).astype(F32)
    lo_iota = lax.broadcasted_iota(jnp.int32, (n_lo, seq), 0).astype(F32)
    for e in range(n_e):
        se = slot[e:e + 1]
        s_hi = jnp.floor(se * (1.0 / n_lo))
        s_lo = se - n_lo * s_hi
        in_hi = jnp.where((s_hi == hi_iota) & (mask[e:e + 1] > 0.5), 1.0, 0.0)
        in_lo = (s_lo == lo_iota).astype(BF16)
        a = aff[e:e + 1]
        a_hi = a.astype(BF16).astype(F32)
        a_mid = (a - a_hi).astype(BF16).astype(F32)
        a_lo = a - a_hi - a_mid
        vals = (tok_hi, tok_lo, a_hi, a_mid, a_lo)
        lhs = jnp.concatenate([in_hi * v for v in vals], axis=0).astype(BF16)
        got = _dot_nt(lhs, in_lo)
        part = lambda r: got[r * n_hi:(r + 1) * n_hi]
        gidx_ref[0, e] = (part(0) * 64.0 + part(1)).astype(jnp.int32) + b * seq
        gate_ref[0, e] = part(2) + part(3) + part(4)


ROUTE_SLOT_HI = 16


def _routing(logits_t, batch, seq):
    n_e = logits_t.shape[0]
    cap = CAPACITY_FACTOR * seq // n_e
    n_hi, n_lo = ROUTE_SLOT_HI, cap // ROUTE_SLOT_HI
    bec = pl.BlockSpec((1, n_e, n_hi, n_lo), lambda b: (b, 0, 0, 0))
    bel = pl.BlockSpec((1, n_e, seq), lambda b: (b, 0, 0))
    gidx, gates, slot, before = pl.pallas_call(
        functools.partial(_route_body, cap=cap),
        grid=(batch,),
        in_specs=[pl.BlockSpec((n_e, seq), lambda b: (0, b))],
        out_specs=[bec, bec, bel, bel],
        out_shape=[
            jax.ShapeDtypeStruct((batch, n_e, n_hi, n_lo), jnp.int32),
            jax.ShapeDtypeStruct((batch, n_e, n_hi, n_lo), F32),
            jax.ShapeDtypeStruct((batch, n_e, seq), jnp.int32),
            jax.ShapeDtypeStruct((batch, n_e, seq), jnp.int32),
        ],
        compiler_params=_cparams(1),
        name="routing",
    )(logits_t)
    flat = lambda a: a.reshape(batch, n_e, cap)
    return flat(gidx), flat(gates), slot, before


def _hbm_row(ref, r):
    return ref.at[lax.shift_right_logical(r, 3), pl.ds(r & (SUBLANES - 1), 1)]


def _moe_body(gidx_ref, gnext_ref, gate_ref, u2_hbm, wg_ref, wu_ref, wd_ref, y_ref,
              stage_ref, xe_ref, gsem, *, n_f):
    e = pl.program_id(0)
    f = pl.program_id(1)
    n_e = pl.num_programs(0)
    m, d = y_ref.shape
    sub = SUBLANES
    q = m // n_f

    def gather_copy(idx_ref, i, k):
        return pltpu.make_async_copy(_hbm_row(u2_hbm, idx_ref[0, i * sub + k]), stage_ref.at[i, pl.ds(k, 1)], gsem)

    def for_rows(fn, tiles_per_trip=4):
        def body(t, c):
            for kk in range(tiles_per_trip * sub):
                fn(t * tiles_per_trip + kk // sub, kk % sub)
            return c
        lax.fori_loop(0, m // (sub * tiles_per_trip), body, 0)

    wait_gather = lambda: for_rows(lambda i, k: gather_copy(gidx_ref, i, k).wait())

    @pl.when(f == 0)
    def _():
        @pl.when(e == 0)
        def _():
            for_rows(lambda i, k: gather_copy(gidx_ref, i, k).start())

        wait_gather()
        xe_ref[...] = stage_ref[...].reshape(m, d).astype(BF16)
        y_ref[...] = jnp.zeros_like(y_ref)

    i0 = f * (q // sub)
    for kk in range(q):
        gather_copy(gnext_ref, i0 + kk // sub, kk % sub).start()
    x = xe_ref[...]
    hid = _silu(_dot(x, wg_ref[...].astype(BF16))) * _dot(x, wu_ref[...].astype(BF16))
    y_ref[...] += _dot(hid.astype(BF16), wd_ref[...].astype(BF16))

    @pl.when(f == n_f - 1)
    def _():
        g = jnp.transpose(jnp.broadcast_to(gate_ref[...], (LANES, m)))
        for k in range(d // LANES):
            cols = slice(k * LANES, (k + 1) * LANES)
            y_ref[:, cols] = y_ref[:, cols] * g

        @pl.when(e == n_e - 1)
        def _():
            wait_gather()


def _moe_ffn(gidx, gates, u2, w_gate, w_up, w_down, tf=256):
    n_e, _, m = gidx.shape
    rows, d = u2.shape
    ff = w_gate.shape[2]
    n_f = ff // tf
    sub = SUBLANES
    smem = lambda fn: pl.BlockSpec((None, 1, m), fn, memory_space=pltpu.SMEM)
    return pl.pallas_call(
        functools.partial(_moe_body, n_f=n_f),
        grid=(n_e, n_f),
        in_specs=[
            smem(lambda e, f: (e, 0, 0)),
            smem(lambda e, f: (jnp.minimum(e + 1, n_e - 1), 0, 0)),
            pl.BlockSpec((None, 1, m), lambda e, f: (e, 0, 0)),
            pl.BlockSpec(memory_space=pl.ANY),
            pl.BlockSpec((None, d, tf), lambda e, f: (e, 0, f)),
            pl.BlockSpec((None, d, tf), lambda e, f: (e, 0, f)),
            pl.BlockSpec((None, tf, d), lambda e, f: (e, f, 0)),
        ],
        out_specs=pl.BlockSpec((m, d), lambda e, f: (e, 0)),
        out_shape=jax.ShapeDtypeStruct((n_e * m, d), F32),
        scratch_shapes=[
            pltpu.VMEM((m // sub, sub, d), F32),
            pltpu.VMEM((m, d), BF16),
            pltpu.SemaphoreType.DMA,
        ],
        compiler_params=_cparams(2),
        name="moe_ffn",
    )(gidx, gidx, gates, u2.reshape(rows // sub, sub, d), w_gate, w_up, w_down)


COMBINE_TOKENS = 256
COMBINE_PIECE = 56


def _combine_body(start_ref, npiece_ref, h1_ref, y_hbm, slot_ref, gate_ref, nw_ref, o_ref,
                  acc_ref, buf_ref, xbuf_ref, sem, xsem, *, m, cap, per_b):
    i = pl.program_id(0)
    n_tiles = pl.num_programs(0)
    n_e = slot_ref.shape[1]
    t, p = COMBINE_TOKENS, COMBINE_PIECE
    total = n_e * m
    cur = i % 2
    b = i // per_b

    def piece_copy(tile, e, dst):
        st = pl.multiple_of(start_ref[tile * n_e + e], SUBLANES)
        return pltpu.make_async_copy(y_hbm.at[pl.ds(st, p)], buf_ref.at[dst, pl.ds(e * p, p)], sem.at[dst])

    @pl.when(i == 0)
    def _():
        for e in range(n_e):
            piece_copy(0, e, 0).start()

    @pl.when(i + 1 < n_tiles)
    def _():
        for e in range(n_e):
            piece_copy(i + 1, e, 1 - cur).start()

    for e in range(n_e):
        piece_copy(i, e, cur).wait()

    slots = slot_ref[0]
    riota = lax.broadcasted_iota(jnp.int32, (p, t), 0)

    def local_row(e, st):
        return slots[e:e + 1, :] + (e * m + b * cap - st)

    sel = jnp.concatenate(
        [(local_row(e, start_ref[i * n_e + e]) == riota).astype(BF16) for e in range(n_e)], axis=0)
    acc_ref[...] = _dot_tn(sel, buf_ref[cur].astype(BF16))

    for e in range(n_e):
        st0 = start_ref[i * n_e + e]

        def extra(k, carry, e=e, st0=st0):
            want = st0 + k * p
            st = pl.multiple_of(jnp.minimum(want, total - p), SUBLANES)
            cp = pltpu.make_async_copy(y_hbm.at[pl.ds(st, p)], xbuf_ref, xsem)
            cp.start()
            cp.wait()
            blk = ((local_row(e, st) == riota) & (riota >= want - st)).astype(BF16)
            acc_ref[...] += _dot_tn(blk, xbuf_ref[...].astype(BF16))
            return carry

        lax.fori_loop(1, npiece_ref[i * n_e + e], extra, 0)

    h2 = h1_ref[...] + gate_ref[0] * acc_ref[...]
    ms = jnp.mean(h2 * h2, axis=-1, keepdims=True)
    o_ref[...] = h2 * lax.rsqrt(ms + EPS) * nw_ref[...]


def _combine_pieces(before, cap):
    batch, n_e, seq = before.shape
    t, p = COMBINE_TOKENS, COMBINE_PIECE
    m = batch * cap
    total = n_e * m
    bounds = jnp.concatenate([before[:, :, ::t], jnp.full((batch, n_e, 1), cap, jnp.int32)], axis=2)
    base = (jnp.arange(n_e, dtype=jnp.int32) * m)[None, :, None] + (jnp.arange(batch, dtype=jnp.int32) * cap)[:, None, None]
    first = base + bounds[:, :, :-1]
    end = base + bounds[:, :, 1:]
    start = jnp.minimum(first // SUBLANES * SUBLANES, total - p)
    npiece = jnp.maximum((end - start + p - 1) // p, 1)
    by_tile = lambda a: a.transpose(0, 2, 1).reshape(-1)
    return by_tile(start), by_tile(npiece)


def _combine(h1, y, slot, before, mod3, final_w, seq, cap):
    rows, d = h1.shape
    batch, n_e, _ = slot.shape
    t, p = COMBINE_TOKENS, COMBINE_PIECE
    per_b = seq // t
    start, npiece = _combine_pieces(before, cap)
    grid_spec = pltpu.PrefetchScalarGridSpec(
        num_scalar_prefetch=2,
        grid=(rows // t,),
        in_specs=[
            pl.BlockSpec((t, d), lambda i, st, npc: (i, 0)),
            pl.BlockSpec(memory_space=pl.ANY),
            pl.BlockSpec((1, n_e, t), lambda i, st, npc: (i // per_b, 0, i % per_b)),
            pl.BlockSpec((1, 1, d), lambda i, st, npc: (i // per_b, 0, 5)),
            pl.BlockSpec((1, d), lambda i, st, npc: (0, 0)),
        ],
        out_specs=pl.BlockSpec((t, d), lambda i, st, npc: (i, 0)),
        scratch_shapes=[
            pltpu.VMEM((t, d), F32),
            pltpu.VMEM((2, n_e * p, d), F32),
            pltpu.VMEM((p, d), F32),
            pltpu.SemaphoreType.DMA((2,)),
            pltpu.SemaphoreType.DMA,
        ],
    )
    return pl.pallas_call(
        functools.partial(_combine_body, m=batch * cap, cap=cap, per_b=per_b),
        grid_spec=grid_spec,
        out_shape=jax.ShapeDtypeStruct((rows, d), F32),
        compiler_params=_cparams(1),
        name="combine",
    )(start, npiece, h1, y, slot, mod3, final_w.reshape(1, d))


def kernel(x, c, ctx, c_ctx, w_mod, b_mod, norm_mix_w, norm_ffn_w, w_in, na_rpb, ret_decay_fwd,
           ret_decay_bwd, ret_gn_w, w_out, w_router, w_gate, w_up, w_down, final_norm_w):
    batch, seq, d = x.shape
    ctx_len = ctx.shape[1]
    assert w_mod.shape[0] == 1, "one trunk layer"
    assert seq % (NA_QROWS * GRID_W) == 0 and seq // GRID_W >= 3 * NA_QROWS
    n_e = w_router.shape[2]
    cap = CAPACITY_FACTOR * seq // n_e

    mod_rows = 8
    cc = jnp.concatenate([c, c_ctx[None], jnp.zeros((mod_rows - batch - 1, d), c.dtype)], axis=0)
    mod3 = _modulation(cc, w_mod[0], b_mod[0]).reshape(mod_rows, 1, N_MOD * d)

    x2d = x.reshape(batch * seq, d)
    tm = 1024
    w_in_bf16 = _cast_bf16(w_in[0], tm=256)
    proj = _in_projection(x2d, norm_mix_w[0], mod3, lambda i: i // (seq // tm), w_in_bf16,
                          w_in.shape[2], tm, w_in.shape[2] // 4)
    cproj = _in_projection(ctx.reshape(batch * ctx_len, d), norm_mix_w[0], mod3, lambda i: batch,
                           w_in_bf16, KV_COLS, batch * ctx_len, 1024)

    na = _neighbourhood_attention(proj, cproj, na_rpb[0], batch, seq, ctx_len)
    ret = _retention(proj, cproj, ret_decay_fwd[0], ret_decay_bwd[0], ret_gn_w[0], batch, seq, ctx_len)

    h1, u2, logits_t = _out_projection(na, ret, _cast_bf16(w_out[0]), x2d, mod3, norm_ffn_w[0],
                                       w_router[0].T, seq)

    gidx, gates, slot, before = _routing(logits_t, batch, seq)
    per_expert = lambda a: a.transpose(1, 0, 2).reshape(n_e, 1, batch * cap)
    y = _moe_ffn(per_expert(gidx), per_expert(gates), u2, w_gate[0], w_up[0], w_down[0])
    out = _combine(h1, y, slot, before, mod3, final_norm_w, seq, cap)
    return out.reshape(batch, seq, d)
```

```python
import functools

import jax
import jax.numpy as jnp
from jax import lax
from jax.experimental import pallas as pl
from jax.experimental.pallas import tpu as pltpu

GRID_W = 64
HEAD_DIM = 128
NA_HEADS = 8
RET_HEADS = 8
NA_WIDTH = NA_HEADS * HEAD_DIM
RET_WIDTH = RET_HEADS * HEAD_DIM
WIN_ROWS = 8
WIN_COLS = 16
RET_BLOCK = 256
ROPE_BASE = 10000.0
N_EXPERTS = 16
CAPACITY_FACTOR = 2
N_MOD = 6
EPS = 1e-6
NEG_INF = -1e30
KV_COLS = 2 * NA_WIDTH + 2 * RET_WIDTH

F32 = jnp.float32
BF16 = jnp.bfloat16
MIB = 1024 * 1024
VMEM_LIMIT_V7X = 56 * MIB


def _cparams(n_axes):
    return pltpu.CompilerParams(
        dimension_semantics=("arbitrary",) * n_axes, vmem_limit_bytes=VMEM_LIMIT_V7X)


def _silu(x):
    return x * jax.nn.sigmoid(x)


def _dot(a, b):
    return jnp.dot(a, b, preferred_element_type=F32)


def _dot_nt(a, b):
    return lax.dot_general(a, b, (((1,), (1,)), ((), ())), preferred_element_type=F32)


def _dot_tn(a, b):
    return lax.dot_general(a, b, (((0,), (0,)), ((), ())), preferred_element_type=F32)


def _mod_body(c_ref, w_ref, b_ref, o_ref):
    a = _silu(c_ref[...]).astype(BF16)
    o_ref[...] = _dot(a, w_ref[...].astype(BF16)) + b_ref[...]


def _modulation(cc, w_mod, b_mod, tn=1024):
    rows, d = cc.shape
    n = w_mod.shape[1]
    return pl.pallas_call(
        _mod_body,
        grid=(n // tn,),
        in_specs=[
            pl.BlockSpec((rows, d), lambda j: (0, 0)),
            pl.BlockSpec((d, tn), lambda j: (0, j)),
            pl.BlockSpec((1, tn), lambda j: (0, j)),
        ],
        out_specs=pl.BlockSpec((rows, tn), lambda j: (0, j)),
        out_shape=jax.ShapeDtypeStruct((rows, n), F32),
        compiler_params=_cparams(1),
        name="modulation",
    )(cc, w_mod, b_mod.reshape(1, n))


def _rms_mod(x, nw, shift, scale):
    ms = jnp.mean(x * x, axis=-1, keepdims=True)
    y = x * lax.rsqrt(ms + EPS) * nw
    return y * (1.0 + scale) + shift


def _inproj_body(x_ref, nw_ref, sh_ref, sc_ref, w_ref, o_ref, u_ref):
    @pl.when(pl.program_id(1) == 0)
    def _():
        u_ref[...] = _rms_mod(x_ref[...], nw_ref[...], sh_ref[0], sc_ref[0]).astype(BF16)

    o_ref[...] = _dot(u_ref[...], w_ref[...].astype(BF16)).astype(o_ref.dtype)


def _in_projection(x2d, norm_w, mod3, mod_row_fn, w_in, n_cols, tm, tn):
    rows, d = x2d.shape
    return pl.pallas_call(
        _inproj_body,
        grid=(rows // tm, n_cols // tn),
        in_specs=[
            pl.BlockSpec((tm, d), lambda i, j: (i, 0)),
            pl.BlockSpec((1, d), lambda i, j: (0, 0)),
            pl.BlockSpec((1, 1, d), lambda i, j: (mod_row_fn(i), 0, 0)),
            pl.BlockSpec((1, 1, d), lambda i, j: (mod_row_fn(i), 0, 1)),
            pl.BlockSpec((d, tn), lambda i, j: (0, j)),
        ],
        out_specs=pl.BlockSpec((tm, tn), lambda i, j: (i, j)),
        out_shape=jax.ShapeDtypeStruct((rows, n_cols), BF16),
        scratch_shapes=[pltpu.VMEM((tm, d), BF16)],
        compiler_params=_cparams(2),
        name="in_projection",
    )(x2d, norm_w.reshape(1, d), mod3, mod3, w_in)


NA_QROWS = 8
NA_KROWS = 2 * NA_QROWS


def _na_row_offset(tile_kind, i, w, rows):
    half = WIN_ROWS // 2
    if tile_kind == 0:
        r, key = i, w
    elif tile_kind == 1:
        r, key = NA_QROWS + i, NA_QROWS - half + w
    else:
        r, key = rows - NA_QROWS + i, rows - NA_KROWS + w
    start = min(max(r - half, 0), rows - WIN_ROWS)
    if not (start <= key < start + WIN_ROWS):
        return None
    return key - r + (WIN_ROWS - 1)


def _na_build_bias(rpb_ref, bias_ref, h, rows):
    w = GRID_W
    cq = lax.broadcasted_iota(jnp.int32, (w, 2 * w), 0)
    ck = lax.broadcasted_iota(jnp.int32, (w, 2 * w), 1) % w
    col_start = jnp.clip(cq - WIN_COLS // 2, 0, w - WIN_COLS)
    col_ok = (ck >= col_start) & (ck < col_start + WIN_COLS)
    col_off = jnp.clip(ck - cq, -(WIN_COLS - 1), WIN_COLS - 1) + (WIN_COLS - 1)
    neg = jnp.full((w, 2 * w), NEG_INF, F32)
    n_ro, n_co = 2 * WIN_ROWS - 1, 2 * WIN_COLS - 1
    tabs = []
    for ro in range(n_ro):
        t = jnp.zeros((w, 2 * w), F32)
        for j in range(n_co):
            t = jnp.where(col_off == j, rpb_ref[h, ro * n_co + j], t)
        tabs.append(jnp.where(col_ok, t, neg))
    left = lax.broadcasted_iota(jnp.int32, (w, 2 * w), 1) < w
    for kind in range(3):
        for i in range(NA_QROWS):
            for wp in range(NA_KROWS // 2):
                ra = _na_row_offset(kind, i, 2 * wp, rows)
                rb = _na_row_offset(kind, i, 2 * wp + 1, rows)
                ta = neg if ra is None else tabs[ra]
                tb = neg if rb is None else tabs[rb]
                blk = ta if ra == rb else jnp.where(left, ta, tb)
                bias_ref[kind, i * w:(i + 1) * w, wp * 2 * w:(wp + 1) * 2 * w] = blk


def _na_body(rpb_ref, q_ref, k_ref, v_ref, ck_ref, cv_ref, o_ref, bias_ref, sa_ref, sb_ref, *, rows):
    h = pl.program_id(0)
    w = GRID_W
    tq, tk = NA_QROWS * w, NA_KROWS * w
    n_tiles = rows // NA_QROWS
    scale = HEAD_DIM ** -0.5

    @pl.when(pl.program_id(1) == 0)
    def _():
        _na_build_bias(rpb_ref, bias_ref, h, rows)

    def offsets(t):
        krow0 = jnp.clip(t * NA_QROWS - WIN_ROWS // 2, 0, rows - NA_KROWS)
        return pl.multiple_of(t * tq, tq), pl.multiple_of(krow0 * w, 4 * w)

    def scores(t, s_ref):
        kind = jnp.where(t == 0, 0, jnp.where(t == n_tiles - 1, 2, 1))
        q0, k0 = offsets(t)
        q = q_ref[pl.ds(q0, tq), :]
        s_ref[:, :tk] = _dot_nt(q, k_ref[pl.ds(k0, tk), :]) * scale + bias_ref[kind]
        s_ref[:, tk:] = _dot_nt(q, ck_ref[...]) * scale

    def attend(t, s_ref):
        q0, k0 = offsets(t)
        s = s_ref[...]
        p = jnp.exp(s - jnp.max(s, axis=-1, keepdims=True))
        l = jnp.sum(p, axis=-1, keepdims=True)
        pb = p.astype(BF16)
        o = _dot(pb[:, :tk], v_ref[pl.ds(k0, tk), :]) + _dot(pb[:, tk:], cv_ref[...])
        o_ref[pl.ds(q0, tq), :] = (o / l).astype(o_ref.dtype)

    scores(jnp.int32(0), sa_ref)

    def pair(i, carry):
        t = 2 * i
        scores(t + 1, sb_ref)
        attend(t, sa_ref)
        scores(t + 2, sa_ref)
        attend(t + 1, sb_ref)
        return carry

    lax.fori_loop(0, n_tiles // 2 - 1, pair, 0)
    scores(jnp.int32(n_tiles - 1), sb_ref)
    attend(jnp.int32(n_tiles - 2), sa_ref)
    attend(jnp.int32(n_tiles - 1), sb_ref)


def _neighbourhood_attention(proj, cproj, rpb, batch, seq, ctx_len):
    rows = seq // GRID_W
    hb = NA_WIDTH // HEAD_DIM
    n_rpb = (2 * WIN_ROWS - 1) * (2 * WIN_COLS - 1)
    tq, tk = NA_QROWS * GRID_W, NA_KROWS * GRID_W
    grid_spec = pltpu.PrefetchScalarGridSpec(
        num_scalar_prefetch=0,
        grid=(NA_HEADS, batch),
        in_specs=[
            pl.BlockSpec(memory_space=pltpu.SMEM),
            pl.BlockSpec((seq, HEAD_DIM), lambda h, b: (b, 4 * hb + h)),
            pl.BlockSpec((seq, HEAD_DIM), lambda h, b: (b, h)),
            pl.BlockSpec((seq, HEAD_DIM), lambda h, b: (b, hb + h)),
            pl.BlockSpec((ctx_len, HEAD_DIM), lambda h, b: (b, h)),
            pl.BlockSpec((ctx_len, HEAD_DIM), lambda h, b: (b, hb + h)),
        ],
        out_specs=pl.BlockSpec((seq, HEAD_DIM), lambda h, b: (b, h)),
        scratch_shapes=[
            pltpu.VMEM((3, tq, tk), F32),
            pltpu.VMEM((tq, tk + ctx_len), F32),
            pltpu.VMEM((tq, tk + ctx_len), F32),
        ],
    )
    return pl.pallas_call(
        functools.partial(_na_body, rows=rows),
        grid_spec=grid_spec,
        out_shape=jax.ShapeDtypeStruct((batch * seq, NA_WIDTH), BF16),
        compiler_params=_cparams(2),
        name="neighbourhood_attention",
    )(rpb.reshape(NA_HEADS, n_rpb), proj, proj, proj, cproj, cproj)


def _log_sigmoid(x):
    return -(jnp.maximum(-x, 0.0) + jnp.log1p(jnp.exp(-jnp.abs(x))))


def _rope_partner_matrix():
    quarter = HEAD_DIM // 4
    src = lax.broadcasted_iota(jnp.int32, (HEAD_DIM, HEAD_DIM), 0)
    dst = lax.broadcasted_iota(jnp.int32, (HEAD_DIM, HEAD_DIM), 1)
    want = jnp.where(dst % (2 * quarter) < quarter, dst + quarter, dst - quarter)
    return (src == want).astype(BF16)


def _rope(x_bf16, partner_matrix, cos, sin_signed):
    return x_bf16.astype(F32) * cos + _dot(x_bf16, partner_matrix) * sin_signed


def _ret_body(df_ref, db_ref, gn_ref, cos_ref, sin_ref, q_ref, k_ref, v_ref, g_ref, ck_ref, cv_ref,
              o_ref, qr_ref, kr_ref, sf_ref, sb_ref):
    c, d = RET_BLOCK, HEAD_DIM
    seq = q_ref.shape[0]
    n_chunks = seq // c
    ctx_len = ck_ref.shape[0]
    scale = HEAD_DIM ** -0.5
    lgf_row = _log_sigmoid(df_ref[0])
    lgb_row = _log_sigmoid(db_ref[0])
    lgf = jnp.broadcast_to(lgf_row, (c, d))
    lgb = jnp.broadcast_to(lgb_row, (c, d))
    pos = lax.broadcasted_iota(jnp.int32, (c, d), 0).astype(F32)
    kdf = jnp.exp(lgf * (c - 1.0 - pos))
    kdb = jnp.exp(lgb * pos)
    qdf = jnp.exp(lgf * (pos + 1.0))
    qdb = jnp.exp(lgb * (c - pos))
    cdf = jnp.exp(lgf_row * float(c))
    cdb = jnp.exp(lgb_row * float(c))
    diff = (lax.broadcasted_iota(jnp.int32, (c, c), 0) - lax.broadcasted_iota(jnp.int32, (c, c), 1)).astype(F32)
    lgf_cc = jnp.broadcast_to(lgf_row[:, :1], (c, c))
    lgb_cc = jnp.broadcast_to(lgb_row[:, :1], (c, c))
    dmat = (jnp.where(diff >= 0, jnp.exp(lgf_cc * jnp.maximum(diff, 0.0)), 0.0)
            + jnp.where(diff <= 0, jnp.exp(lgb_cc * jnp.maximum(-diff, 0.0)), 0.0))

    cpos = lax.broadcasted_iota(jnp.int32, (ctx_len, d), 0).astype(F32)
    ckf = ck_ref[...].astype(F32) * scale
    cv = cv_ref[...]
    wf = jnp.exp(jnp.broadcast_to(lgf_row, (ctx_len, d)) * (ctx_len - 1.0 - cpos))
    wb = jnp.exp(jnp.broadcast_to(lgb_row, (ctx_len, d)) * cpos)
    s_f = _dot_tn((ckf * wf).astype(BF16), cv)
    s_b = _dot_tn((ckf * wb).astype(BF16), cv)

    rope_rows = 512
    pmat = _rope_partner_matrix()

    def rope_blk(i, carry):
        r0 = pl.multiple_of(i * rope_rows, rope_rows)
        cs = cos_ref[pl.ds(r0, rope_rows), :]
        sn = sin_ref[pl.ds(r0, rope_rows), :]
        qr_ref[pl.ds(r0, rope_rows), :] = _rope(q_ref[pl.ds(r0, rope_rows), :], pmat, cs, sn)
        kr_ref[pl.ds(r0, rope_rows), :] = _rope(k_ref[pl.ds(r0, rope_rows), :], pmat, cs, sn) * scale
        return carry

    lax.fori_loop(0, seq // rope_rows, rope_blk, 0)

    def scan_blk(n, carry):
        s, t = carry
        r0 = pl.multiple_of(n * c, c)
        kvf = _dot_tn((kr_ref[pl.ds(r0, c), :] * kdf).astype(BF16), v_ref[pl.ds(r0, c), :])
        sf_ref[n] = s.astype(BF16)
        m = n_chunks - 1 - n
        m0 = pl.multiple_of(m * c, c)
        kvb = _dot_tn((kr_ref[pl.ds(m0, c), :] * kdb).astype(BF16), v_ref[pl.ds(m0, c), :])
        sb_ref[m] = t.astype(BF16)
        return s * cdf + kvf, t * cdb + kvb

    lax.fori_loop(0, n_chunks, scan_blk, (s_f, s_b), unroll=4)

    gn = gn_ref[...]

    def out_blk(n, carry):
        r0 = pl.multiple_of(n * c, c)
        qc = qr_ref[pl.ds(r0, c), :]
        kc = kr_ref[pl.ds(r0, c), :]
        a = _dot_nt(qc.astype(BF16), kc.astype(BF16))
        o = (_dot((a * dmat).astype(BF16), v_ref[pl.ds(r0, c), :])
             + _dot((qc * qdf).astype(BF16), sf_ref[n])
             + _dot((qc * qdb).astype(BF16), sb_ref[n]))
        mu = jnp.mean(o, axis=-1, keepdims=True)
        var = jnp.mean(jnp.square(o - mu), axis=-1, keepdims=True)
        y = (o - mu) * lax.rsqrt(var + EPS) * gn
        o_ref[pl.ds(r0, c), :] = (y * _silu(g_ref[pl.ds(r0, c), :].astype(F32))).astype(o_ref.dtype)
        return carry

    lax.fori_loop(0, n_chunks, out_blk, 0, unroll=4)


def _rope_tables(seq):
    axis_dim = HEAD_DIM // 2
    inv_freq = ROPE_BASE ** (-jnp.arange(0, axis_dim, 2, dtype=F32) / axis_dim)
    rows = seq // GRID_W
    ang_r = jnp.arange(rows, dtype=F32)[:, None] * inv_freq
    ang_c = jnp.arange(GRID_W, dtype=F32)[:, None] * inv_freq
    by_row = lambda a: jnp.repeat(a, GRID_W, axis=0)
    by_col = lambda a: jnp.tile(a, (rows, 1))
    cr, sr, cc, sc = by_row(jnp.cos(ang_r)), by_row(jnp.sin(ang_r)), by_col(jnp.cos(ang_c)), by_col(jnp.sin(ang_c))
    return jnp.concatenate([cr, cr, cc, cc], axis=-1), jnp.concatenate([-sr, sr, -sc, sc], axis=-1)


def _retention(proj, cproj, decay_f, decay_b, gn_w, batch, seq, ctx_len):
    hb = RET_WIDTH // HEAD_DIM
    cos, sin = _rope_tables(seq)
    dec_f = jnp.broadcast_to(decay_f.astype(F32)[:, None, None], (RET_HEADS, 1, HEAD_DIM))
    dec_b = jnp.broadcast_to(decay_b.astype(F32)[:, None, None], (RET_HEADS, 1, HEAD_DIM))
    n_chunks = seq // RET_BLOCK
    blk = lambda rows, fn: pl.BlockSpec((rows, HEAD_DIM), fn)
    return pl.pallas_call(
        _ret_body,
        grid=(RET_HEADS, batch),
        in_specs=[
            pl.BlockSpec((1, 1, HEAD_DIM), lambda h, b: (h, 0, 0)),
            pl.BlockSpec((1, 1, HEAD_DIM), lambda h, b: (h, 0, 0)),
            blk(1, lambda h, b: (0, h)),
            blk(seq, lambda h, b: (0, 0)),
            blk(seq, lambda h, b: (0, 0)),
            blk(seq, lambda h, b: (b, 5 * hb + h)),
            blk(seq, lambda h, b: (b, 2 * hb + h)),
            blk(seq, lambda h, b: (b, 3 * hb + h)),
            blk(seq, lambda h, b: (b, 6 * hb + h)),
            blk(ctx_len, lambda h, b: (b, 2 * hb + h)),
            blk(ctx_len, lambda h, b: (b, 3 * hb + h)),
        ],
        out_specs=blk(seq, lambda h, b: (b, h)),
        out_shape=jax.ShapeDtypeStruct((batch * seq, RET_WIDTH), BF16),
        scratch_shapes=[
            pltpu.VMEM((seq, HEAD_DIM), F32),
            pltpu.VMEM((seq, HEAD_DIM), F32),
            pltpu.VMEM((n_chunks, HEAD_DIM, HEAD_DIM), BF16),
            pltpu.VMEM((n_chunks, HEAD_DIM, HEAD_DIM), BF16),
        ],
        compiler_params=_cparams(2),
        name="retention",
    )(dec_f, dec_b, gn_w.reshape(1, RET_WIDTH), cos, sin, proj, proj, proj, proj, cproj, cproj)


def _split_bf16(x):
    hi = x.astype(BF16)
    lo = (x - hi.astype(F32)).astype(BF16)
    return hi, lo


def _cast_body(x_ref, o_ref):
    o_ref[...] = x_ref[...].astype(o_ref.dtype)


def _cast_bf16(w, tm=512):
    rows, cols = w.shape
    return pl.pallas_call(
        _cast_body,
        grid=(rows // tm,),
        in_specs=[pl.BlockSpec((tm, cols), lambda i: (i, 0))],
        out_specs=pl.BlockSpec((tm, cols), lambda i: (i, 0)),
        out_shape=jax.ShapeDtypeStruct((rows, cols), BF16),
        compiler_params=_cparams(1),
        name="cast_bf16",
    )(w)


OUTPROJ_ROW_SPLITS = 2


def _outproj_body(na_ref, ret_ref, w_ref, x_ref, gate_ref, nw_ref, sh_ref, sc_ref, wr_ref,
                  h1_ref, u2_ref, lg_ref):
    half = na_ref.shape[1]
    n_e = wr_ref.shape[0]
    tm = x_ref.shape[0]
    wh, wl = _split_bf16(wr_ref[...])
    w_router = jnp.concatenate([wh, wl], axis=0)
    for r in range(OUTPROJ_ROW_SPLITS):
        rows = pl.ds(r * (tm // OUTPROJ_ROW_SPLITS), tm // OUTPROJ_ROW_SPLITS)
        mix = _dot(na_ref[rows, :], w_ref[:half, :]) + _dot(ret_ref[rows, :], w_ref[half:, :])
        h1 = x_ref[rows, :] + gate_ref[0] * mix
        h1_ref[rows, :] = h1
        u2 = _rms_mod(h1, nw_ref[...], sh_ref[0], sc_ref[0])
        u2_ref[rows, :] = u2
        both = _dot_nt(w_router, u2.astype(BF16))
        lg_ref[:, rows] = both[:n_e] + both[n_e:]


def _out_projection(na, ret, w_out_bf16, x2d, mod3, norm_w, w_router_t, seq, tm=512):
    rows, d = x2d.shape
    half = na.shape[1]
    per_b = seq // tm
    return pl.pallas_call(
        _outproj_body,
        grid=(rows // tm,),
        in_specs=[
            pl.BlockSpec((tm, half), lambda i: (i, 0)),
            pl.BlockSpec((tm, half), lambda i: (i, 0)),
            pl.BlockSpec((2 * half, d), lambda i: (0, 0)),
            pl.BlockSpec((tm, d), lambda i: (i, 0)),
            pl.BlockSpec((1, 1, d), lambda i: (i // per_b, 0, 2)),
            pl.BlockSpec((1, d), lambda i: (0, 0)),
            pl.BlockSpec((1, 1, d), lambda i: (i // per_b, 0, 3)),
            pl.BlockSpec((1, 1, d), lambda i: (i // per_b, 0, 4)),
            pl.BlockSpec((N_EXPERTS, d), lambda i: (0, 0)),
        ],
        out_specs=[
            pl.BlockSpec((tm, d), lambda i: (i, 0)),
            pl.BlockSpec((tm, d), lambda i: (i, 0)),
            pl.BlockSpec((N_EXPERTS, tm), lambda i: (0, i)),
        ],
        out_shape=[
            jax.ShapeDtypeStruct((rows, d), F32),
            jax.ShapeDtypeStruct((rows, d), F32),
            jax.ShapeDtypeStruct((N_EXPERTS, rows), F32),
        ],
        compiler_params=_cparams(1),
        name="out_projection",
    )(na, ret, w_out_bf16, x2d, mod3, norm_w.reshape(1, d), mod3, mod3, w_router_t)


LANES = 128
SUBLANES = 8


def _prefix_incl_lanes(x, tri):
    r, l = x.shape
    nb = l // LANES
    xs = jnp.concatenate([x[:, t * LANES:(t + 1) * LANES] for t in range(nb)], axis=0).astype(BF16)
    p = _dot(xs, tri)
    outs, run = [], jnp.zeros((r, 1), F32)
    for t in range(nb):
        blk = p[t * r:(t + 1) * r] + run
        outs.append(blk)
        run = blk[:, LANES - 1:LANES]
    return jnp.concatenate(outs, axis=1)


NOT_ROUTED = -(1 << 20)


def _route_body(lg_ref, gidx_ref, gate_ref, slot_ref, before_ref, *, cap):
    b = pl.program_id(0)
    n_e, seq = lg_ref.shape
    kf = float(cap)
    lg = lg_ref[...]
    ex = jnp.exp(lg - jnp.max(lg, axis=0, keepdims=True))
    aff = ex / jnp.sum(ex, axis=0, keepdims=True)

    def cond(c):
        return (c[0] < 4096) & (c[5] > 0.5)

    def step(c):
        it, lo, hi, thr, done, _ = c
        mid = 0.5 * (lo + hi)
        above = jnp.sum((aff > mid).astype(F32), axis=1, keepdims=True)
        hit = above == kf
        stuck = (mid <= lo) | (mid >= hi)
        active = done < 0.5
        thr = jnp.where(active & hit, mid, jnp.where(active & stuck, hi, thr))
        go = active & ~(hit | stuck)
        ge = above >= kf
        lo = jnp.where(go & ge, mid, lo)
        hi = jnp.where(go & ~ge, mid, hi)
        done = jnp.where(active & (hit | stuck), 1.0, done)
        return it + 1, lo, hi, thr, done, jnp.sum(1.0 - done)

    col = lambda v: jnp.full((n_e, 1), v, F32)
    init = (jnp.int32(0), col(-1.0), col(2.0), col(0.0), col(0.0), jnp.float32(n_e))
    thr = lax.while_loop(cond, step, init)[3]

    ii = lax.broadcasted_iota(jnp.int32, (LANES, LANES), 0)
    jj = lax.broadcasted_iota(jnp.int32, (LANES, LANES), 1)
    tri = (ii <= jj).astype(BF16)
    gt = aff > thr
    eq = (aff == thr).astype(F32)
    need = kf - jnp.sum(gt.astype(F32), axis=1, keepdims=True)
    eq_before = _prefix_incl_lanes(eq, tri) - eq
    mask = jnp.where(gt | ((eq > 0.5) & (eq_before < need)), 1.0, 0.0)

    slot = _prefix_incl_lanes(mask, tri) - mask
    before_ref[0] = slot.astype(jnp.int32)
    slot_ref[0] = jnp.where(mask > 0.5, slot, float(NOT_ROUTED)).astype(jnp.int32)

    tok = lax.broadcasted_iota(jnp.int32, (1, seq), 1).astype(F32)
    tok_hi = jnp.floor(tok * (1.0 / 64))
    tok_lo = tok - 64.0 * tok_hi
    n_hi = ROUTE_SLOT_HI
    n_lo = cap // n_hi
    hi_iota = lax.broadcasted_iota(jnp.int32, (n_hi, seq), 0).astype(F32)
    lo_iota = lax.broadcasted_iota(jnp.int32, (n_lo, seq), 0).astype(F32)
    for e in range(n_e):
        se = slot[e:e + 1]
        s_hi = jnp.floor(se * (1.0 / n_lo))
        s_lo = se - n_lo * s_hi
        in_hi = jnp.where((s_hi == hi_iota) & (mask[e:e + 1] > 0.5), 1.0, 0.0)
        in_lo = (s_lo == lo_iota).astype(BF16)
        a = aff[e:e + 1]
        a_hi = a.astype(BF16).astype(F32)
        a_mid = (a - a_hi).astype(BF16).astype(F32)
        a_lo = a - a_hi - a_mid
        vals = (tok_hi, tok_lo, a_hi, a_mid, a_lo)
        lhs = jnp.concatenate([in_hi * v for v in vals], axis=0).astype(BF16)
        got = _dot_nt(lhs, in_lo)
        part = lambda r: got[r * n_hi:(r + 1) * n_hi]
        gidx_ref[0, e] = (part(0) * 64.0 + part(1)).astype(jnp.int32) + b * seq
        gate_ref[0, e] = part(2) + part(3) + part(4)


ROUTE_SLOT_HI = 16


def _routing(logits_t, batch, seq):
    n_e = logits_t.shape[0]
    cap = CAPACITY_FACTOR * seq // n_e
    n_hi, n_lo = ROUTE_SLOT_HI, cap // ROUTE_SLOT_HI
    bec = pl.BlockSpec((1, n_e, n_hi, n_lo), lambda b: (b, 0, 0, 0))
    bel = pl.BlockSpec((1, n_e, seq), lambda b: (b, 0, 0))
    gidx, gates, slot, before = pl.pallas_call(
        functools.partial(_route_body, cap=cap),
        grid=(batch,),
        in_specs=[pl.BlockSpec((n_e, seq), lambda b: (0, b))],
        out_specs=[bec, bec, bel, bel],
        out_shape=[
            jax.ShapeDtypeStruct((batch, n_e, n_hi, n_lo), jnp.int32),
            jax.ShapeDtypeStruct((batch, n_e, n_hi, n_lo), F32),
            jax.ShapeDtypeStruct((batch, n_e, seq), jnp.int32),
            jax.ShapeDtypeStruct((batch, n_e, seq), jnp.int32),
        ],
        compiler_params=_cparams(1),
        name="routing",
    )(logits_t)
    flat = lambda a: a.reshape(batch, n_e, cap)
    return flat(gidx), flat(gates), slot, before


def _hbm_row(ref, r):
    return ref.at[lax.shift_right_logical(r, 3), pl.ds(r & (SUBLANES - 1), 1)]


MOE_ROW_SPLITS = 2


def _moe_body(gidx_ref, gnext_ref, gate_ref, u2_hbm, wg_ref, wu_ref, wd_ref, y_ref,
              stage_ref, xe_ref, acc_ref, gsem, *, n_f):
    e = pl.program_id(0)
    f = pl.program_id(1)
    n_e = pl.num_programs(0)
    m, d = y_ref.shape
    sub = SUBLANES
    q = m // n_f

    def gather_copy(idx_ref, i, k):
        return pltpu.make_async_copy(_hbm_row(u2_hbm, idx_ref[0, i * sub + k]), stage_ref.at[i, pl.ds(k, 1)], gsem)

    def for_rows(fn, tiles_per_trip=4):
        def body(t, c):
            for kk in range(tiles_per_trip * sub):
                fn(t * tiles_per_trip + kk // sub, kk % sub)
            return c
        lax.fori_loop(0, m // (sub * tiles_per_trip), body, 0)

    wait_gather = lambda: for_rows(lambda i, k: gather_copy(gidx_ref, i, k).wait())

    @pl.when(f == 0)
    def _():
        @pl.when(e == 0)
        def _():
            for_rows(lambda i, k: gather_copy(gidx_ref, i, k).start())

        wait_gather()
        xe_ref[...] = stage_ref[...].reshape(m, d).astype(BF16)
        acc_ref[...] = jnp.zeros_like(acc_ref)

    i0 = f * (q // sub)
    for kk in range(q):
        gather_copy(gnext_ref, i0 + kk // sub, kk % sub).start()
    wg, wu, wd = wg_ref[...].astype(BF16), wu_ref[...].astype(BF16), wd_ref[...].astype(BF16)
    for r in range(MOE_ROW_SPLITS):
        rows = pl.ds(r * (m // MOE_ROW_SPLITS), m // MOE_ROW_SPLITS)
        x = xe_ref[rows, :]
        hid = _silu(_dot(x, wg)) * _dot(x, wu)
        acc_ref[rows, :] += _dot(hid.astype(BF16), wd)

    @pl.when(f == n_f - 1)
    def _():
        g = jnp.transpose(jnp.broadcast_to(gate_ref[...], (LANES, m)))
        for k in range(d // LANES):
            cols = slice(k * LANES, (k + 1) * LANES)
            y_ref[:, cols] = (acc_ref[:, cols] * g).astype(y_ref.dtype)

        @pl.when(e == n_e - 1)
        def _():
            wait_gather()


def _moe_ffn(gidx, gates, u2, w_gate, w_up, w_down, tf=256):
    n_e, _, m = gidx.shape
    rows, d = u2.shape
    ff = w_gate.shape[2]
    n_f = ff // tf
    sub = SUBLANES
    smem = lambda fn: pl.BlockSpec((None, 1, m), fn, memory_space=pltpu.SMEM)
    return pl.pallas_call(
        functools.partial(_moe_body, n_f=n_f),
        grid=(n_e, n_f),
        in_specs=[
            smem(lambda e, f: (e, 0, 0)),
            smem(lambda e, f: (jnp.minimum(e + 1, n_e - 1), 0, 0)),
            pl.BlockSpec((None, 1, m), lambda e, f: (e, 0, 0)),
            pl.BlockSpec(memory_space=pl.ANY),
            pl.BlockSpec((None, d, tf), lambda e, f: (e, 0, f)),
            pl.BlockSpec((None, d, tf), lambda e, f: (e, 0, f)),
            pl.BlockSpec((None, tf, d), lambda e, f: (e, f, 0)),
        ],
        out_specs=pl.BlockSpec((m, d), lambda e, f: (e, 0)),
        out_shape=jax.ShapeDtypeStruct((n_e * m, d), BF16),
        scratch_shapes=[
            pltpu.VMEM((m // sub, sub, d), F32),
            pltpu.VMEM((m, d), BF16),
            pltpu.VMEM((m, d), F32),
            pltpu.SemaphoreType.DMA,
        ],
        compiler_params=_cparams(2),
        name="moe_ffn",
    )(gidx, gidx, gates, u2.reshape(rows // sub, sub, d), w_gate, w_up, w_down)


COMBINE_TOKENS = 256
COMBINE_PIECE = 64
BF16_ROWS = 16


def _combine_body(start_ref, npiece_ref, h1_ref, y_hbm, slot_ref, gate_ref, nw_ref, o_ref,
                  acc_ref, buf_ref, xbuf_ref, sem, xsem, *, m, cap, per_b):
    i = pl.program_id(0)
    n_tiles = pl.num_programs(0)
    n_e = slot_ref.shape[1]
    t, p = COMBINE_TOKENS, COMBINE_PIECE
    total = n_e * m
    cur = i % 2
    b = i // per_b

    def piece_copy(tile, e, dst):
        st = pl.multiple_of(start_ref[tile * n_e + e], BF16_ROWS)
        return pltpu.make_async_copy(y_hbm.at[pl.ds(st, p)], buf_ref.at[dst, pl.ds(e * p, p)], sem.at[dst])

    @pl.when(i == 0)
    def _():
        for e in range(n_e):
            piece_copy(0, e, 0).start()

    @pl.when(i + 1 < n_tiles)
    def _():
        for e in range(n_e):
            piece_copy(i + 1, e, 1 - cur).start()

    for e in range(n_e):
        piece_copy(i, e, cur).wait()

    slots = slot_ref[0]
    riota = lax.broadcasted_iota(jnp.int32, (p, t), 0)

    def local_row(e, st):
        return slots[e:e + 1, :] + (e * m + b * cap - st)

    sel = jnp.concatenate(
        [(local_row(e, start_ref[i * n_e + e]) == riota).astype(BF16) for e in range(n_e)], axis=0)
    acc_ref[...] = _dot_tn(sel, buf_ref[cur])

    for e in range(n_e):
        st0 = start_ref[i * n_e + e]

        def extra(k, carry, e=e, st0=st0):
            want = st0 + k * p
            st = pl.multiple_of(jnp.minimum(want, total - p), BF16_ROWS)
            cp = pltpu.make_async_copy(y_hbm.at[pl.ds(st, p)], xbuf_ref, xsem)
            cp.start()
            cp.wait()
            blk = ((local_row(e, st) == riota) & (riota >= want - st)).astype(BF16)
            acc_ref[...] += _dot_tn(blk, xbuf_ref[...])
            return carry

        lax.fori_loop(1, npiece_ref[i * n_e + e], extra, 0)

    h2 = h1_ref[...] + gate_ref[0] * acc_ref[...]
    ms = jnp.mean(h2 * h2, axis=-1, keepdims=True)
    o_ref[...] = h2 * lax.rsqrt(ms + EPS) * nw_ref[...]


def _combine_pieces(before, cap):
    batch, n_e, seq = before.shape
    t, p = COMBINE_TOKENS, COMBINE_PIECE
    m = batch * cap
    total = n_e * m
    bounds = jnp.concatenate([before[:, :, ::t], jnp.full((batch, n_e, 1), cap, jnp.int32)], axis=2)
    base = (jnp.arange(n_e, dtype=jnp.int32) * m)[None, :, None] + (jnp.arange(batch, dtype=jnp.int32) * cap)[:, None, None]
    first = base + bounds[:, :, :-1]
    end = base + bounds[:, :, 1:]
    start = jnp.minimum(first // BF16_ROWS * BF16_ROWS, total - p)
    npiece = jnp.maximum((end - start + p - 1) // p, 1)
    by_tile = lambda a: a.transpose(0, 2, 1).reshape(-1)
    return by_tile(start), by_tile(npiece)


def _combine(h1, y, slot, before, mod3, final_w, seq, cap):
    rows, d = h1.shape
    batch, n_e, _ = slot.shape
    t, p = COMBINE_TOKENS, COMBINE_PIECE
    per_b = seq // t
    start, npiece = _combine_pieces(before, cap)
    grid_spec = pltpu.PrefetchScalarGridSpec(
        num_scalar_prefetch=2,
        grid=(rows // t,),
        in_specs=[
            pl.BlockSpec((t, d), lambda i, st, npc: (i, 0)),
            pl.BlockSpec(memory_space=pl.ANY),
            pl.BlockSpec((1, n_e, t), lambda i, st, npc: (i // per_b, 0, i % per_b)),
            pl.BlockSpec((1, 1, d), lambda i, st, npc: (i // per_b, 0, 5)),
            pl.BlockSpec((1, d), lambda i, st, npc: (0, 0)),
        ],
        out_specs=pl.BlockSpec((t, d), lambda i, st, npc: (i, 0)),
        scratch_shapes=[
            pltpu.VMEM((t, d), F32),
            pltpu.VMEM((2, n_e * p, d), y.dtype),
            pltpu.VMEM((p, d), y.dtype),
            pltpu.SemaphoreType.DMA((2,)),
            pltpu.SemaphoreType.DMA,
        ],
    )
    return pl.pallas_call(
        functools.partial(_combine_body, m=batch * cap, cap=cap, per_b=per_b),
        grid_spec=grid_spec,
        out_shape=jax.ShapeDtypeStruct((rows, d), F32),
        compiler_params=_cparams(1),
        name="combine",
    )(start, npiece, h1, y, slot, mod3, final_w.reshape(1, d))


def kernel(x, c, ctx, c_ctx, w_mod, b_mod, norm_mix_w, norm_ffn_w, w_in, na_rpb, ret_decay_fwd,
           ret_decay_bwd, ret_gn_w, w_out, w_router, w_gate, w_up, w_down, final_norm_w):
    batch, seq, d = x.shape
    ctx_len = ctx.shape[1]
    assert w_mod.shape[0] == 1, "one trunk layer"
    assert seq % (NA_QROWS * GRID_W) == 0 and seq // GRID_W >= 3 * NA_QROWS
    n_e = w_router.shape[2]
    cap = CAPACITY_FACTOR * seq // n_e

    mod_rows = 8
    cc = jnp.concatenate([c, c_ctx[None], jnp.zeros((mod_rows - batch - 1, d), c.dtype)], axis=0)
    mod3 = _modulation(cc, w_mod[0], b_mod[0]).reshape(mod_rows, 1, N_MOD * d)

    x2d = x.reshape(batch * seq, d)
    tm = 1024
    w_in_bf16 = _cast_bf16(w_in[0], tm=256)
    proj = _in_projection(x2d, norm_mix_w[0], mod3, lambda i: i // (seq // tm), w_in_bf16,
                          w_in.shape[2], tm, w_in.shape[2] // 4)
    cproj = _in_projection(ctx.reshape(batch * ctx_len, d), norm_mix_w[0], mod3, lambda i: batch,
                           w_in_bf16, KV_COLS, batch * ctx_len, 1024)

    na = _neighbourhood_attention(proj, cproj, na_rpb[0], batch, seq, ctx_len)
    ret = _retention(proj, cproj, ret_decay_fwd[0], ret_decay_bwd[0], ret_gn_w[0], batch, seq, ctx_len)

    h1, u2, logits_t = _out_projection(na, ret, _cast_bf16(w_out[0]), x2d, mod3, norm_ffn_w[0],
                                       w_router[0].T, seq)

    gidx, gates, slot, before = _routing(logits_t, batch, seq)
    per_expert = lambda a: a.transpose(1, 0, 2).reshape(n_e, 1, batch * cap)
    y = _moe_ffn(per_expert(gidx), per_expert(gates), u2, w_gate[0], w_up[0], w_down[0])
    out = _combine(h1, y, slot, before, mod3, final_norm_w, seq, cap)
    return out.reshape(batch, seq, d)
```

```python
import functools

import jax
import jax.numpy as jnp
from jax import lax
from jax.experimental import pallas as pl
from jax.experimental.pallas import tpu as pltpu

GRID_W = 64
HEAD_DIM = 128
NA_HEADS = 8
RET_HEADS = 8
NA_WIDTH = NA_HEADS * HEAD_DIM
RET_WIDTH = RET_HEADS * HEAD_DIM
WIN_ROWS = 8
WIN_COLS = 16
RET_BLOCK = 256
ROPE_BASE = 10000.0
N_EXPERTS = 16
CAPACITY_FACTOR = 2
N_MOD = 6
MOD_EARLY = 2
MOD_LATE_GATE_MIX, MOD_LATE_SHIFT_FFN, MOD_LATE_SCALE_FFN, MOD_LATE_GATE_FFN = 0, 1, 2, 3
EPS = 1e-6
NEG_INF = -1e30
KV_COLS = 2 * NA_WIDTH + 2 * RET_WIDTH

F32 = jnp.float32
BF16 = jnp.bfloat16
MIB = 1024 * 1024
VMEM_LIMIT_V7X = 56 * MIB


def _cparams(n_axes):
    return pltpu.CompilerParams(
        dimension_semantics=("arbitrary",) * n_axes, vmem_limit_bytes=VMEM_LIMIT_V7X)


def _silu(x):
    return x * jax.nn.sigmoid(x)


def _dot(a, b):
    return jnp.dot(a, b, preferred_element_type=F32)


def _dot_nt(a, b):
    return lax.dot_general(a, b, (((1,), (1,)), ((), ())), preferred_element_type=F32)


def _dot_tn(a, b):
    return lax.dot_general(a, b, (((0,), (0,)), ((), ())), preferred_element_type=F32)


def _mod_body(c_ref, w_ref, b_ref, o_ref):
    a = _silu(c_ref[...]).astype(BF16)
    o_ref[...] = _dot(a, w_ref[...].astype(BF16)) + b_ref[...]


def _modulation(cc, w_mod, b_mod, n_cols, tn=1024):
    rows, d = cc.shape
    n = n_cols
    return pl.pallas_call(
        _mod_body,
        grid=(n // tn,),
        in_specs=[
            pl.BlockSpec((rows, d), lambda j: (0, 0)),
            pl.BlockSpec((d, tn), lambda j: (0, j)),
            pl.BlockSpec((1, tn), lambda j: (0, j)),
        ],
        out_specs=pl.BlockSpec((rows, tn), lambda j: (0, j)),
        out_shape=jax.ShapeDtypeStruct((rows, n), F32),
        compiler_params=_cparams(1),
        name="modulation",
    )(cc, w_mod, b_mod.reshape(1, -1))


def _rms_mod(x, nw, shift, scale):
    ms = jnp.mean(x * x, axis=-1, keepdims=True)
    y = x * lax.rsqrt(ms + EPS) * nw
    return y * (1.0 + scale) + shift


def _inproj_body(x_ref, nw_ref, sh_ref, sc_ref, w_ref, o_ref, u_ref):
    @pl.when(pl.program_id(1) == 0)
    def _():
        u_ref[...] = _rms_mod(x_ref[...], nw_ref[...], sh_ref[0], sc_ref[0]).astype(BF16)

    o_ref[...] = _dot(u_ref[...], w_ref[...].astype(BF16)).astype(o_ref.dtype)


def _in_projection(x2d, norm_w, mod3, mod_row_fn, w_in, n_cols, tm, tn):
    rows, d = x2d.shape
    return pl.pallas_call(
        _inproj_body,
        grid=(rows // tm, n_cols // tn),
        in_specs=[
            pl.BlockSpec((tm, d), lambda i, j: (i, 0)),
            pl.BlockSpec((1, d), lambda i, j: (0, 0)),
            pl.BlockSpec((1, 1, d), lambda i, j: (mod_row_fn(i), 0, 0)),
            pl.BlockSpec((1, 1, d), lambda i, j: (mod_row_fn(i), 0, 1)),
            pl.BlockSpec((d, tn), lambda i, j: (0, j)),
        ],
        out_specs=pl.BlockSpec((tm, tn), lambda i, j: (i, j)),
        out_shape=jax.ShapeDtypeStruct((rows, n_cols), BF16),
        scratch_shapes=[pltpu.VMEM((tm, d), BF16)],
        compiler_params=_cparams(2),
        name="in_projection",
    )(x2d, norm_w.reshape(1, d), mod3, mod3, w_in)


NA_QROWS = 8
NA_KROWS = 2 * NA_QROWS


def _na_row_offset(tile_kind, i, w, rows):
    half = WIN_ROWS // 2
    if tile_kind == 0:
        r, key = i, w
    elif tile_kind == 1:
        r, key = NA_QROWS + i, NA_QROWS - half + w
    else:
        r, key = rows - NA_QROWS + i, rows - NA_KROWS + w
    start = min(max(r - half, 0), rows - WIN_ROWS)
    if not (start <= key < start + WIN_ROWS):
        return None
    return key - r + (WIN_ROWS - 1)


def _na_build_bias(rpb_ref, bias_ref, h, rows):
    w = GRID_W
    cq = lax.broadcasted_iota(jnp.int32, (w, 2 * w), 0)
    ck = lax.broadcasted_iota(jnp.int32, (w, 2 * w), 1) % w
    col_start = jnp.clip(cq - WIN_COLS // 2, 0, w - WIN_COLS)
    col_ok = (ck >= col_start) & (ck < col_start + WIN_COLS)
    col_off = jnp.clip(ck - cq, -(WIN_COLS - 1), WIN_COLS - 1) + (WIN_COLS - 1)
    neg = jnp.full((w, 2 * w), NEG_INF, F32)
    n_ro, n_co = 2 * WIN_ROWS - 1, 2 * WIN_COLS - 1
    tabs = []
    for ro in range(n_ro):
        t = jnp.zeros((w, 2 * w), F32)
        for j in range(n_co):
            t = jnp.where(col_off == j, rpb_ref[h, ro * n_co + j], t)
        tabs.append(jnp.where(col_ok, t, neg))
    left = lax.broadcasted_iota(jnp.int32, (w, 2 * w), 1) < w
    for kind in range(3):
        for i in range(NA_QROWS):
            for wp in range(NA_KROWS // 2):
                ra = _na_row_offset(kind, i, 2 * wp, rows)
                rb = _na_row_offset(kind, i, 2 * wp + 1, rows)
                ta = neg if ra is None else tabs[ra]
                tb = neg if rb is None else tabs[rb]
                blk = ta if ra == rb else jnp.where(left, ta, tb)
                bias_ref[kind, i * w:(i + 1) * w, wp * 2 * w:(wp + 1) * 2 * w] = blk


def _na_body(rpb_ref, q_ref, k_ref, v_ref, ck_ref, cv_ref, o_ref, bias_ref, sa_ref, sb_ref, *, rows):
    h = pl.program_id(0)
    w = GRID_W
    tq, tk = NA_QROWS * w, NA_KROWS * w
    n_tiles = rows // NA_QROWS
    scale = HEAD_DIM ** -0.5

    @pl.when(pl.program_id(1) == 0)
    def _():
        _na_build_bias(rpb_ref, bias_ref, h, rows)

    def offsets(t):
        krow0 = jnp.clip(t * NA_QROWS - WIN_ROWS // 2, 0, rows - NA_KROWS)
        return pl.multiple_of(t * tq, tq), pl.multiple_of(krow0 * w, 4 * w)

    def scores(t, s_ref):
        kind = jnp.where(t == 0, 0, jnp.where(t == n_tiles - 1, 2, 1))
        q0, k0 = offsets(t)
        q = q_ref[pl.ds(q0, tq), :]
        s_ref[:, :tk] = _dot_nt(q, k_ref[pl.ds(k0, tk), :]) * scale + bias_ref[kind]
        s_ref[:, tk:] = _dot_nt(q, ck_ref[...]) * scale

    def attend(t, s_ref):
        q0, k0 = offsets(t)
        s = s_ref[...]
        p = jnp.exp(s - jnp.max(s, axis=-1, keepdims=True))
        l = jnp.sum(p, axis=-1, keepdims=True)
        pb = p.astype(BF16)
        o = _dot(pb[:, :tk], v_ref[pl.ds(k0, tk), :]) + _dot(pb[:, tk:], cv_ref[...])
        o_ref[pl.ds(q0, tq), :] = (o / l).astype(o_ref.dtype)

    scores(jnp.int32(0), sa_ref)

    def pair(i, carry):
        t = 2 * i
        scores(t + 1, sb_ref)
        attend(t, sa_ref)
        scores(t + 2, sa_ref)
        attend(t + 1, sb_ref)
        return carry

    lax.fori_loop(0, n_tiles // 2 - 1, pair, 0)
    scores(jnp.int32(n_tiles - 1), sb_ref)
    attend(jnp.int32(n_tiles - 2), sa_ref)
    attend(jnp.int32(n_tiles - 1), sb_ref)


def _neighbourhood_attention(proj, cproj, rpb, batch, seq, ctx_len):
    rows = seq // GRID_W
    hb = NA_WIDTH // HEAD_DIM
    n_rpb = (2 * WIN_ROWS - 1) * (2 * WIN_COLS - 1)
    tq, tk = NA_QROWS * GRID_W, NA_KROWS * GRID_W
    grid_spec = pltpu.PrefetchScalarGridSpec(
        num_scalar_prefetch=0,
        grid=(NA_HEADS, batch),
        in_specs=[
            pl.BlockSpec(memory_space=pltpu.SMEM),
            pl.BlockSpec((seq, HEAD_DIM), lambda h, b: (b, 4 * hb + h)),
            pl.BlockSpec((seq, HEAD_DIM), lambda h, b: (b, h)),
            pl.BlockSpec((seq, HEAD_DIM), lambda h, b: (b, hb + h)),
            pl.BlockSpec((ctx_len, HEAD_DIM), lambda h, b: (b, h)),
            pl.BlockSpec((ctx_len, HEAD_DIM), lambda h, b: (b, hb + h)),
        ],
        out_specs=pl.BlockSpec((seq, HEAD_DIM), lambda h, b: (b, h)),
        scratch_shapes=[
            pltpu.VMEM((3, tq, tk), F32),
            pltpu.VMEM((tq, tk + ctx_len), F32),
            pltpu.VMEM((tq, tk + ctx_len), F32),
        ],
    )
    return pl.pallas_call(
        functools.partial(_na_body, rows=rows),
        grid_spec=grid_spec,
        out_shape=jax.ShapeDtypeStruct((batch * seq, NA_WIDTH), BF16),
        compiler_params=_cparams(2),
        name="neighbourhood_attention",
    )(rpb.reshape(NA_HEADS, n_rpb), proj, proj, proj, cproj, cproj)


def _log_sigmoid(x):
    return -(jnp.maximum(-x, 0.0) + jnp.log1p(jnp.exp(-jnp.abs(x))))


def _rope_partner_matrix():
    quarter = HEAD_DIM // 4
    src = lax.broadcasted_iota(jnp.int32, (HEAD_DIM, HEAD_DIM), 0)
    dst = lax.broadcasted_iota(jnp.int32, (HEAD_DIM, HEAD_DIM), 1)
    want = jnp.where(dst % (2 * quarter) < quarter, dst + quarter, dst - quarter)
    return (src == want).astype(BF16)


def _rope(x_bf16, partner_matrix, cos, sin_signed):
    return x_bf16.astype(F32) * cos + _dot(x_bf16, partner_matrix) * sin_signed


def _ret_body(df_ref, db_ref, gn_ref, cos_ref, sin_ref, q_ref, k_ref, v_ref, g_ref, ck_ref, cv_ref,
              cc_ref, wm_ref, bm_ref, wo_ref,
              o_ref, modl_ref, wob_ref, qr_ref, kr_ref, sf_ref, sb_ref):
    modl_ref[...] = _dot(_silu(cc_ref[...]).astype(BF16), wm_ref[...].astype(BF16)) + bm_ref[...]
    wob_ref[...] = wo_ref[...].astype(BF16)

    c, d = RET_BLOCK, HEAD_DIM
    seq = q_ref.shape[0]
    n_chunks = seq // c
    ctx_len = ck_ref.shape[0]
    scale = HEAD_DIM ** -0.5
    lgf_row = _log_sigmoid(df_ref[0])
    lgb_row = _log_sigmoid(db_ref[0])
    lgf = jnp.broadcast_to(lgf_row, (c, d))
    lgb = jnp.broadcast_to(lgb_row, (c, d))
    pos = lax.broadcasted_iota(jnp.int32, (c, d), 0).astype(F32)
    kdf = jnp.exp(lgf * (c - 1.0 - pos))
    kdb = jnp.exp(lgb * pos)
    qdf = jnp.exp(lgf * (pos + 1.0))
    qdb = jnp.exp(lgb * (c - pos))
    cdf = jnp.exp(lgf_row * float(c))
    cdb = jnp.exp(lgb_row * float(c))
    diff = (lax.broadcasted_iota(jnp.int32, (c, c), 0) - lax.broadcasted_iota(jnp.int32, (c, c), 1)).astype(F32)
    lgf_cc = jnp.broadcast_to(lgf_row[:, :1], (c, c))
    lgb_cc = jnp.broadcast_to(lgb_row[:, :1], (c, c))
    dmat = (jnp.where(diff >= 0, jnp.exp(lgf_cc * jnp.maximum(diff, 0.0)), 0.0)
            + jnp.where(diff <= 0, jnp.exp(lgb_cc * jnp.maximum(-diff, 0.0)), 0.0))

    cpos = lax.broadcasted_iota(jnp.int32, (ctx_len, d), 0).astype(F32)
    ckf = ck_ref[...].astype(F32) * scale
    cv = cv_ref[...]
    wf = jnp.exp(jnp.broadcast_to(lgf_row, (ctx_len, d)) * (ctx_len - 1.0 - cpos))
    wb = jnp.exp(jnp.broadcast_to(lgb_row, (ctx_len, d)) * cpos)
    s_f = _dot_tn((ckf * wf).astype(BF16), cv)
    s_b = _dot_tn((ckf * wb).astype(BF16), cv)

    rope_rows = 512
    pmat = _rope_partner_matrix()

    def rope_blk(i, carry):
        r0 = pl.multiple_of(i * rope_rows, rope_rows)
        cs = cos_ref[pl.ds(r0, rope_rows), :]
        sn = sin_ref[pl.ds(r0, rope_rows), :]
        qr_ref[pl.ds(r0, rope_rows), :] = _rope(q_ref[pl.ds(r0, rope_rows), :], pmat, cs, sn)
        kr_ref[pl.ds(r0, rope_rows), :] = _rope(k_ref[pl.ds(r0, rope_rows), :], pmat, cs, sn) * scale
        return carry

    lax.fori_loop(0, seq // rope_rows, rope_blk, 0)

    def scan_blk(n, carry):
        s, t = carry
        r0 = pl.multiple_of(n * c, c)
        kvf = _dot_tn((kr_ref[pl.ds(r0, c), :] * kdf).astype(BF16), v_ref[pl.ds(r0, c), :])
        sf_ref[n] = s.astype(BF16)
        m = n_chunks - 1 - n
        m0 = pl.multiple_of(m * c, c)
        kvb = _dot_tn((kr_ref[pl.ds(m0, c), :] * kdb).astype(BF16), v_ref[pl.ds(m0, c), :])
        sb_ref[m] = t.astype(BF16)
        return s * cdf + kvf, t * cdb + kvb

    lax.fori_loop(0, n_chunks, scan_blk, (s_f, s_b), unroll=4)

    gn = gn_ref[...]

    def out_blk(n, carry):
        r0 = pl.multiple_of(n * c, c)
        qc = qr_ref[pl.ds(r0, c), :]
        kc = kr_ref[pl.ds(r0, c), :]
        a = _dot_nt(qc.astype(BF16), kc.astype(BF16))
        o = (_dot((a * dmat).astype(BF16), v_ref[pl.ds(r0, c), :])
             + _dot((qc * qdf).astype(BF16), sf_ref[n])
             + _dot((qc * qdb).astype(BF16), sb_ref[n]))
        mu = jnp.mean(o, axis=-1, keepdims=True)
        var = jnp.mean(jnp.square(o - mu), axis=-1, keepdims=True)
        y = (o - mu) * lax.rsqrt(var + EPS) * gn
        o_ref[pl.ds(r0, c), :] = (y * _silu(g_ref[pl.ds(r0, c), :].astype(F32))).astype(o_ref.dtype)
        return carry

    lax.fori_loop(0, n_chunks, out_blk, 0, unroll=4)


def _rope_tables(seq):
    axis_dim = HEAD_DIM // 2
    inv_freq = ROPE_BASE ** (-jnp.arange(0, axis_dim, 2, dtype=F32) / axis_dim)
    rows = seq // GRID_W
    ang_r = jnp.arange(rows, dtype=F32)[:, None] * inv_freq
    ang_c = jnp.arange(GRID_W, dtype=F32)[:, None] * inv_freq
    by_row = lambda a: jnp.repeat(a, GRID_W, axis=0)
    by_col = lambda a: jnp.tile(a, (rows, 1))
    cr, sr, cc, sc = by_row(jnp.cos(ang_r)), by_row(jnp.sin(ang_r)), by_col(jnp.cos(ang_c)), by_col(jnp.sin(ang_c))
    return jnp.concatenate([cr, cr, cc, cc], axis=-1), jnp.concatenate([-sr, sr, -sc, sc], axis=-1)


def _retention(proj, cproj, decay_f, decay_b, gn_w, batch, seq, ctx_len, cc, w_mod, b_mod, mod_from, w_out):
    hb = RET_WIDTH // HEAD_DIM
    cos, sin = _rope_tables(seq)
    dec_f = jnp.broadcast_to(decay_f.astype(F32)[:, None, None], (RET_HEADS, 1, HEAD_DIM))
    dec_b = jnp.broadcast_to(decay_b.astype(F32)[:, None, None], (RET_HEADS, 1, HEAD_DIM))
    n_chunks = seq // RET_BLOCK
    blk = lambda rows, fn: pl.BlockSpec((rows, HEAD_DIM), fn)
    n_steps = RET_HEADS * batch
    step = lambda h, b: h * batch + b
    mod_rows, d = cc.shape
    late = w_mod.shape[1] - mod_from
    mcols, orows = late // n_steps, w_out.shape[0] // n_steps
    side_in = [
        pl.BlockSpec((mod_rows, d), lambda h, b: (0, 0)),
        pl.BlockSpec((d, mcols), lambda h, b: (0, mod_from // mcols + step(h, b))),
        pl.BlockSpec((1, mcols), lambda h, b: (0, mod_from // mcols + step(h, b))),
        pl.BlockSpec((orows, w_out.shape[1]), lambda h, b: (step(h, b), 0)),
    ]
    side_out = [
        pl.BlockSpec((mod_rows, mcols), lambda h, b: (0, step(h, b))),
        pl.BlockSpec((orows, w_out.shape[1]), lambda h, b: (step(h, b), 0)),
    ]
    side_shape = [jax.ShapeDtypeStruct((mod_rows, late), F32), jax.ShapeDtypeStruct(w_out.shape, BF16)]
    return pl.pallas_call(
        _ret_body,
        grid=(RET_HEADS, batch),
        in_specs=[
            pl.BlockSpec((1, 1, HEAD_DIM), lambda h, b: (h, 0, 0)),
            pl.BlockSpec((1, 1, HEAD_DIM), lambda h, b: (h, 0, 0)),
            blk(1, lambda h, b: (0, h)),
            blk(seq, lambda h, b: (0, 0)),
            blk(seq, lambda h, b: (0, 0)),
            blk(seq, lambda h, b: (b, 5 * hb + h)),
            blk(seq, lambda h, b: (b, 2 * hb + h)),
            blk(seq, lambda h, b: (b, 3 * hb + h)),
            blk(seq, lambda h, b: (b, 6 * hb + h)),
            blk(ctx_len, lambda h, b: (b, 2 * hb + h)),
            blk(ctx_len, lambda h, b: (b, 3 * hb + h)),
        ] + side_in,
        out_specs=[blk(seq, lambda h, b: (b, h))] + side_out,
        out_shape=[jax.ShapeDtypeStruct((batch * seq, RET_WIDTH), BF16)] + side_shape,
        scratch_shapes=[
            pltpu.VMEM((seq, HEAD_DIM), F32),
            pltpu.VMEM((seq, HEAD_DIM), F32),
            pltpu.VMEM((n_chunks, HEAD_DIM, HEAD_DIM), BF16),
            pltpu.VMEM((n_chunks, HEAD_DIM, HEAD_DIM), BF16),
        ],
        compiler_params=_cparams(2),
        name="retention",
    )(dec_f, dec_b, gn_w.reshape(1, RET_WIDTH), cos, sin, proj, proj, proj, proj, cproj, cproj,
      cc, w_mod, b_mod.reshape(1, -1), w_out)


def _split_bf16(x):
    hi = x.astype(BF16)
    lo = (x - hi.astype(F32)).astype(BF16)
    return hi, lo


def _cast_body(x_ref, o_ref):
    o_ref[...] = x_ref[...].astype(o_ref.dtype)


def _cast_bf16(w, tm=512):
    rows, cols = w.shape
    return pl.pallas_call(
        _cast_body,
        grid=(rows // tm,),
        in_specs=[pl.BlockSpec((tm, cols), lambda i: (i, 0))],
        out_specs=pl.BlockSpec((tm, cols), lambda i: (i, 0)),
        out_shape=jax.ShapeDtypeStruct((rows, cols), BF16),
        compiler_params=_cparams(1),
        name="cast_bf16",
    )(w)


OUTPROJ_ROW_SPLITS = 2


def _outproj_body(na_ref, ret_ref, w_ref, x_ref, gate_ref, nw_ref, sh_ref, sc_ref, wr_ref,
                  h1_ref, u2_ref, lg_ref):
    half = na_ref.shape[1]
    n_e = wr_ref.shape[0]
    tm = x_ref.shape[0]
    wh, wl = _split_bf16(wr_ref[...])
    w_router = jnp.concatenate([wh, wl], axis=0)
    for r in range(OUTPROJ_ROW_SPLITS):
        rows = pl.ds(r * (tm // OUTPROJ_ROW_SPLITS), tm // OUTPROJ_ROW_SPLITS)
        mix = _dot(na_ref[rows, :], w_ref[:half, :]) + _dot(ret_ref[rows, :], w_ref[half:, :])
        h1 = x_ref[rows, :] + gate_ref[0] * mix
        h1_ref[rows, :] = h1
        u2 = _rms_mod(h1, nw_ref[...], sh_ref[0], sc_ref[0])
        u2_ref[rows, :] = u2
        both = _dot_nt(w_router, u2.astype(BF16))
        lg_ref[:, rows] = both[:n_e] + both[n_e:]


def _out_projection(na, ret, w_out_bf16, x2d, mod3, norm_w, w_router_t, seq, tm=512):
    rows, d = x2d.shape
    half = na.shape[1]
    per_b = seq // tm
    return pl.pallas_call(
        _outproj_body,
        grid=(rows // tm,),
        in_specs=[
            pl.BlockSpec((tm, half), lambda i: (i, 0)),
            pl.BlockSpec((tm, half), lambda i: (i, 0)),
            pl.BlockSpec((2 * half, d), lambda i: (0, 0)),
            pl.BlockSpec((tm, d), lambda i: (i, 0)),
            pl.BlockSpec((1, 1, d), lambda i: (i // per_b, 0, MOD_LATE_GATE_MIX)),
            pl.BlockSpec((1, d), lambda i: (0, 0)),
            pl.BlockSpec((1, 1, d), lambda i: (i // per_b, 0, MOD_LATE_SHIFT_FFN)),
            pl.BlockSpec((1, 1, d), lambda i: (i // per_b, 0, MOD_LATE_SCALE_FFN)),
            pl.BlockSpec((N_EXPERTS, d), lambda i: (0, 0)),
        ],
        out_specs=[
            pl.BlockSpec((tm, d), lambda i: (i, 0)),
            pl.BlockSpec((tm, d), lambda i: (i, 0)),
            pl.BlockSpec((N_EXPERTS, tm), lambda i: (0, i)),
        ],
        out_shape=[
            jax.ShapeDtypeStruct((rows, d), F32),
            jax.ShapeDtypeStruct((rows, d), F32),
            jax.ShapeDtypeStruct((N_EXPERTS, rows), F32),
        ],
        compiler_params=_cparams(1),
        name="out_projection",
    )(na, ret, w_out_bf16, x2d, mod3, norm_w.reshape(1, d), mod3, mod3, w_router_t)


LANES = 128
SUBLANES = 8


def _prefix_incl_lanes(x, tri):
    r, l = x.shape
    nb = l // LANES
    xs = jnp.concatenate([x[:, t * LANES:(t + 1) * LANES] for t in range(nb)], axis=0).astype(BF16)
    p = _dot(xs, tri)
    outs, run = [], jnp.zeros((r, 1), F32)
    for t in range(nb):
        blk = p[t * r:(t + 1) * r] + run
        outs.append(blk)
        run = blk[:, LANES - 1:LANES]
    return jnp.concatenate(outs, axis=1)


NOT_ROUTED = -(1 << 20)


def _route_body(lg_ref, gidx_ref, gate_ref, slot_ref, before_ref, *, cap):
    b = pl.program_id(0)
    n_e, seq = lg_ref.shape
    kf = float(cap)
    lg = lg_ref[...]
    ex = jnp.exp(lg - jnp.max(lg, axis=0, keepdims=True))
    aff = ex / jnp.sum(ex, axis=0, keepdims=True)

    def cond(c):
        return (c[0] < 4096) & (c[5] > 0.5)

    def step(c):
        it, lo, hi, thr, done, _ = c
        mid = 0.5 * (lo + hi)
        above = jnp.sum((aff > mid).astype(F32), axis=1, keepdims=True)
        hit = above == kf
        stuck = (mid <= lo) | (mid >= hi)
        active = done < 0.5
        thr = jnp.where(active & hit, mid, jnp.where(active & stuck, hi, thr))
        go = active & ~(hit | stuck)
        ge = above >= kf
        lo = jnp.where(go & ge, mid, lo)
        hi = jnp.where(go & ~ge, mid, hi)
        done = jnp.where(active & (hit | stuck), 1.0, done)
        return it + 1, lo, hi, thr, done, jnp.sum(1.0 - done)

    col = lambda v: jnp.full((n_e, 1), v, F32)
    init = (jnp.int32(0), col(-1.0), col(2.0), col(0.0), col(0.0), jnp.float32(n_e))
    thr = lax.while_loop(cond, step, init)[3]

    ii = lax.broadcasted_iota(jnp.int32, (LANES, LANES), 0)
    jj = lax.broadcasted_iota(jnp.int32, (LANES, LANES), 1)
    tri = (ii <= jj).astype(BF16)
    gt = aff > thr
    eq = (aff == thr).astype(F32)
    need = kf - jnp.sum(gt.astype(F32), axis=1, keepdims=True)
    eq_before = _prefix_incl_lanes(eq, tri) - eq
    mask = jnp.where(gt | ((eq > 0.5) & (eq_before < need)), 1.0, 0.0)

    slot = _prefix_incl_lanes(mask, tri) - mask
    before_ref[0] = slot.astype(jnp.int32)
    slot_ref[0] = jnp.where(mask > 0.5, slot, float(NOT_ROUTED)).astype(jnp.int32)

    tok = lax.broadcasted_iota(jnp.int32, (1, seq), 1).astype(F32)
    tok_hi = jnp.floor(tok * (1.0 / 64))
    tok_lo = tok - 64.0 * tok_hi
    n_hi = ROUTE_SLOT_HI
    n_lo = cap // n_hi
    hi_iota = lax.broadcasted_iota(jnp.int32, (n_hi, seq), 0).astype(F32)
    lo_iota = lax.broadcasted_iota(jnp.int32, (n_lo, seq), 0).astype(F32)
    for e in range(n_e):
        se = slot[e:e + 1]
        s_hi = jnp.floor(se * (1.0 / n_lo))
        s_lo = se - n_lo * s_hi
        in_hi = jnp.where((s_hi == hi_iota) & (mask[e:e + 1] > 0.5), 1.0, 0.0)
        in_lo = (s_lo == lo_iota).astype(BF16)
        a = aff[e:e + 1]
        a_hi = a.astype(BF16).astype(F32)
        a_mid = (a - a_hi).astype(BF16).astype(F32)
        a_lo = a - a_hi - a_mid
        vals = (tok_hi, tok_lo, a_hi, a_mid, a_lo)
        lhs = jnp.concatenate([in_hi * v for v in vals], axis=0).astype(BF16)
        got = _dot_nt(lhs, in_lo)
        part = lambda r: got[r * n_hi:(r + 1) * n_hi]
        gidx_ref[0, e] = (part(0) * 64.0 + part(1)).astype(jnp.int32) + b * seq
        gate_ref[0, e] = part(2) + part(3) + part(4)


ROUTE_SLOT_HI = 16


def _routing(logits_t, batch, seq):
    n_e = logits_t.shape[0]
    cap = CAPACITY_FACTOR * seq // n_e
    n_hi, n_lo = ROUTE_SLOT_HI, cap // ROUTE_SLOT_HI
    bec = pl.BlockSpec((1, n_e, n_hi, n_lo), lambda b: (b, 0, 0, 0))
    bel = pl.BlockSpec((1, n_e, seq), lambda b: (b, 0, 0))
    gidx, gates, slot, before = pl.pallas_call(
        functools.partial(_route_body, cap=cap),
        grid=(batch,),
        in_specs=[pl.BlockSpec((n_e, seq), lambda b: (0, b))],
        out_specs=[bec, bec, bel, bel],
        out_shape=[
            jax.ShapeDtypeStruct((batch, n_e, n_hi, n_lo), jnp.int32),
            jax.ShapeDtypeStruct((batch, n_e, n_hi, n_lo), F32),
            jax.ShapeDtypeStruct((batch, n_e, seq), jnp.int32),
            jax.ShapeDtypeStruct((batch, n_e, seq), jnp.int32),
        ],
        compiler_params=_cparams(1),
        name="routing",
    )(logits_t)
    flat = lambda a: a.reshape(batch, n_e, cap)
    return flat(gidx), flat(gates), slot, before


def _hbm_row(ref, r):
    return ref.at[lax.shift_right_logical(r, 3), pl.ds(r & (SUBLANES - 1), 1)]


MOE_ROW_SPLITS = 2


def _moe_body(gidx_ref, gnext_ref, gate_ref, u2_hbm, wg_ref, wu_ref, wd_ref, y_ref,
              stage_ref, xe_ref, acc_ref, gsem, *, n_f):
    e = pl.program_id(0)
    f = pl.program_id(1)
    n_e = pl.num_programs(0)
    m, d = y_ref.shape
    sub = SUBLANES
    n_tiles = m // sub
    assert n_f >= 2
    per_step = -(-n_tiles // (n_f - 1))
    last_count = n_tiles - per_step * (n_f - 2)

    def gather_copy(idx_ref, i, k):
        return pltpu.make_async_copy(_hbm_row(u2_hbm, idx_ref[0, i * sub + k]), stage_ref.at[i, pl.ds(k, 1)], gsem)

    def for_rows(fn, tiles_per_trip=4):
        def body(t, c):
            for kk in range(tiles_per_trip * sub):
                fn(t * tiles_per_trip + kk // sub, kk % sub)
            return c
        lax.fori_loop(0, n_tiles // tiles_per_trip, body, 0)

    wait_gather = lambda: for_rows(lambda i, k: gather_copy(gidx_ref, i, k).wait())

    def ffn_step(first, gather_tiles):
        i0 = (f - 1) * per_step
        for kk in range(gather_tiles * sub):
            gather_copy(gnext_ref, i0 + kk // sub, kk % sub).start()
        wg, wu, wd = wg_ref[...].astype(BF16), wu_ref[...].astype(BF16), wd_ref[...].astype(BF16)
        part = m // MOE_ROW_SPLITS
        for r in range(MOE_ROW_SPLITS):
            rows = pl.ds(r * part, part)
            if first:
                x = stage_ref[pl.ds(r * (part // sub), part // sub)].reshape(part, d).astype(BF16)
                xe_ref[rows, :] = x
            else:
                x = xe_ref[rows, :]
            hid = _silu(_dot(x, wg)) * _dot(x, wu)
            out = _dot(hid.astype(BF16), wd)
            acc_ref[rows, :] = out if first else acc_ref[rows, :] + out

    @pl.when(f == 0)
    def _():
        @pl.when(e == 0)
        def _():
            for_rows(lambda i, k: gather_copy(gidx_ref, i, k).start())

        wait_gather()
        ffn_step(True, 0)

    if n_f > 2:
        pl.when((f > 0) & (f < n_f - 1))(lambda: ffn_step(False, per_step))

    @pl.when(f == n_f - 1)
    def _():
        ffn_step(False, last_count)

        @pl.when(f > 0)
        def _():
            g = jnp.transpose(jnp.broadcast_to(gate_ref[...], (LANES, m)))
            for k in range(d // LANES):
                cols = slice(k * LANES, (k + 1) * LANES)
                y_ref[:, cols] = (acc_ref[:, cols] * g).astype(y_ref.dtype)

        @pl.when(e == n_e - 1)
        def _():
            wait_gather()


def _moe_ffn(gidx, gates, u2, w_gate, w_up, w_down, tf=256):
    n_e, _, m = gidx.shape
    rows, d = u2.shape
    ff = w_gate.shape[2]
    n_f = ff // tf
    sub = SUBLANES
    smem = lambda fn: pl.BlockSpec((None, 1, m), fn, memory_space=pltpu.SMEM)
    return pl.pallas_call(
        functools.partial(_moe_body, n_f=n_f),
        grid=(n_e, n_f),
        in_specs=[
            smem(lambda e, f: (e, 0, 0)),
            smem(lambda e, f: (jnp.minimum(e + 1, n_e - 1), 0, 0)),
            pl.BlockSpec((None, 1, m), lambda e, f: (e, 0, 0)),
            pl.BlockSpec(memory_space=pl.ANY),
            pl.BlockSpec((None, d, tf), lambda e, f: (e, 0, f)),
            pl.BlockSpec((None, d, tf), lambda e, f: (e, 0, f)),
            pl.BlockSpec((None, tf, d), lambda e, f: (e, f, 0)),
        ],
        out_specs=pl.BlockSpec((m, d), lambda e, f: (e, 0)),
        out_shape=jax.ShapeDtypeStruct((n_e * m, d), BF16),
        scratch_shapes=[
            pltpu.VMEM((m // sub, sub, d), F32),
            pltpu.VMEM((m, d), BF16),
            pltpu.VMEM((m, d), F32),
            pltpu.SemaphoreType.DMA,
        ],
        compiler_params=_cparams(2),
        name="moe_ffn",
    )(gidx, gidx, gates, u2.reshape(rows // sub, sub, d), w_gate, w_up, w_down)


COMBINE_TOKENS = 256
COMBINE_PIECE = 64
BF16_ROWS = 16


def _combine_body(start_ref, npiece_ref, h1_ref, y_hbm, slot_ref, gate_ref, nw_ref, o_ref,
                  acc_ref, buf_ref, xbuf_ref, sem, xsem, *, m, cap, per_b):
    i = pl.program_id(0)
    n_tiles = pl.num_programs(0)
    n_e = slot_ref.shape[1]
    t, p = COMBINE_TOKENS, COMBINE_PIECE
    total = n_e * m
    cur = i % 2
    b = i // per_b

    def piece_copy(tile, e, dst):
        st = pl.multiple_of(start_ref[tile * n_e + e], BF16_ROWS)
        return pltpu.make_async_copy(y_hbm.at[pl.ds(st, p)], buf_ref.at[dst, pl.ds(e * p, p)], sem.at[dst])

    @pl.when(i == 0)
    def _():
        for e in range(n_e):
            piece_copy(0, e, 0).start()

    @pl.when(i + 1 < n_tiles)
    def _():
        for e in range(n_e):
            piece_copy(i + 1, e, 1 - cur).start()

    for e in range(n_e):
        piece_copy(i, e, cur).wait()

    slots = slot_ref[0]
    riota = lax.broadcasted_iota(jnp.int32, (p, t), 0)

    def local_row(e, st):
        return slots[e:e + 1, :] + (e * m + b * cap - st)

    sel = jnp.concatenate(
        [(local_row(e, start_ref[i * n_e + e]) == riota).astype(BF16) for e in range(n_e)], axis=0)
    acc_ref[...] = _dot_tn(sel, buf_ref[cur])

    for e in range(n_e):
        st0 = start_ref[i * n_e + e]

        def extra(k, carry, e=e, st0=st0):
            want = st0 + k * p
            st = pl.multiple_of(jnp.minimum(want, total - p), BF16_ROWS)
            cp = pltpu.make_async_copy(y_hbm.at[pl.ds(st, p)], xbuf_ref, xsem)
            cp.start()
            cp.wait()
            blk = ((local_row(e, st) == riota) & (riota >= want - st)).astype(BF16)
            acc_ref[...] += _dot_tn(blk, xbuf_ref[...])
            return carry

        lax.fori_loop(1, npiece_ref[i * n_e + e], extra, 0)

    h2 = h1_ref[...] + gate_ref[0] * acc_ref[...]
    ms = jnp.mean(h2 * h2, axis=-1, keepdims=True)
    o_ref[...] = h2 * lax.rsqrt(ms + EPS) * nw_ref[...]


def _combine_pieces(before, cap):
    batch, n_e, seq = before.shape
    t, p = COMBINE_TOKENS, COMBINE_PIECE
    m = batch * cap
    total = n_e * m
    bounds = jnp.concatenate([before[:, :, ::t], jnp.full((batch, n_e, 1), cap, jnp.int32)], axis=2)
    base = (jnp.arange(n_e, dtype=jnp.int32) * m)[None, :, None] + (jnp.arange(batch, dtype=jnp.int32) * cap)[:, None, None]
    first = base + bounds[:, :, :-1]
    end = base + bounds[:, :, 1:]
    start = jnp.minimum(first // BF16_ROWS * BF16_ROWS, total - p)
    npiece = jnp.maximum((end - start + p - 1) // p, 1)
    by_tile = lambda a: a.transpose(0, 2, 1).reshape(-1)
    return by_tile(start), by_tile(npiece)


def _combine(h1, y, slot, before, mod3, final_w, seq, cap):
    rows, d = h1.shape
    batch, n_e, _ = slot.shape
    t, p = COMBINE_TOKENS, COMBINE_PIECE
    per_b = seq // t
    start, npiece = _combine_pieces(before, cap)
    grid_spec = pltpu.PrefetchScalarGridSpec(
        num_scalar_prefetch=2,
        grid=(rows // t,),
        in_specs=[
            pl.BlockSpec((t, d), lambda i, st, npc: (i, 0)),
            pl.BlockSpec(memory_space=pl.ANY),
            pl.BlockSpec((1, n_e, t), lambda i, st, npc: (i // per_b, 0, i % per_b)),
            pl.BlockSpec((1, 1, d), lambda i, st, npc: (i // per_b, 0, MOD_LATE_GATE_FFN)),
            pl.BlockSpec((1, d), lambda i, st, npc: (0, 0)),
        ],
        out_specs=pl.BlockSpec((t, d), lambda i, st, npc: (i, 0)),
        scratch_shapes=[
            pltpu.VMEM((t, d), F32),
            pltpu.VMEM((2, n_e * p, d), y.dtype),
            pltpu.VMEM((p, d), y.dtype),
            pltpu.SemaphoreType.DMA((2,)),
            pltpu.SemaphoreType.DMA,
        ],
    )
    return pl.pallas_call(
        functools.partial(_combine_body, m=batch * cap, cap=cap, per_b=per_b),
        grid_spec=grid_spec,
        out_shape=jax.ShapeDtypeStruct((rows, d), F32),
        compiler_params=_cparams(1),
        name="combine",
    )(start, npiece, h1, y, slot, mod3, final_w.reshape(1, d))


def kernel(x, c, ctx, c_ctx, w_mod, b_mod, norm_mix_w, norm_ffn_w, w_in, na_rpb, ret_decay_fwd,
           ret_decay_bwd, ret_gn_w, w_out, w_router, w_gate, w_up, w_down, final_norm_w):
    batch, seq, d = x.shape
    ctx_len = ctx.shape[1]
    assert w_mod.shape[0] == 1, "one trunk layer"
    assert seq % (NA_QROWS * GRID_W) == 0 and seq // GRID_W >= 3 * NA_QROWS
    n_e = w_router.shape[2]
    cap = CAPACITY_FACTOR * seq // n_e

    mod_rows = 8
    cc = jnp.concatenate([c, c_ctx[None], jnp.zeros((mod_rows - batch - 1, d), c.dtype)], axis=0)
    mod3 = _modulation(cc, w_mod[0], b_mod[0], MOD_EARLY * d).reshape(mod_rows, 1, MOD_EARLY * d)

    x2d = x.reshape(batch * seq, d)
    tm = 1024
    w_in_bf16 = _cast_bf16(w_in[0], tm=256)
    proj = _in_projection(x2d, norm_mix_w[0], mod3, lambda i: i // (seq // tm), w_in_bf16,
                          w_in.shape[2], tm, w_in.shape[2] // 4)
    cproj = _in_projection(ctx.reshape(batch * ctx_len, d), norm_mix_w[0], mod3, lambda i: batch,
                           w_in_bf16, KV_COLS, batch * ctx_len, 1024)

    na = _neighbourhood_attention(proj, cproj, na_rpb[0], batch, seq, ctx_len)
    ret, mod_late, w_out_bf16 = _retention(proj, cproj, ret_decay_fwd[0], ret_decay_bwd[0], ret_gn_w[0],
                                           batch, seq, ctx_len, cc, w_mod[0], b_mod[0], MOD_EARLY * d, w_out[0])
    mod_late3 = mod_late.reshape(mod_rows, 1, (N_MOD - MOD_EARLY) * d)

    h1, u2, logits_t = _out_projection(na, ret, w_out_bf16, x2d, mod_late3, norm_ffn_w[0], w_router[0].T, seq)

    gidx, gates, slot, before = _routing(logits_t, batch, seq)
    per_expert = lambda a: a.transpose(1, 0, 2).reshape(n_e, 1, batch * cap)
    y = _moe_ffn(per_expert(gidx), per_expert(gates), u2, w_gate[0], w_up[0], w_down[0])
    out = _combine(h1, y, slot, before, mod_late3, final_norm_w, seq, cap)
    return out.reshape(batch, seq, d)
```

```python
import functools

import jax
import jax.numpy as jnp
from jax import lax
from jax.experimental import pallas as pl
from jax.experimental.pallas import tpu as pltpu

GRID_W = 64
HEAD_DIM = 128
NA_HEADS = 8
RET_HEADS = 8
NA_WIDTH = NA_HEADS * HEAD_DIM
RET_WIDTH = RET_HEADS * HEAD_DIM
WIN_ROWS = 8
WIN_COLS = 16
RET_BLOCK = 256
ROPE_BASE = 10000.0
N_EXPERTS = 16
CAPACITY_FACTOR = 2
N_MOD = 6
MOD_EARLY = 2
MOD_LATE_GATE_MIX, MOD_LATE_SHIFT_FFN, MOD_LATE_SCALE_FFN, MOD_LATE_GATE_FFN = 0, 1, 2, 3
EPS = 1e-6
NEG_INF = -1e30
KV_COLS = 2 * NA_WIDTH + 2 * RET_WIDTH

F32 = jnp.float32
BF16 = jnp.bfloat16
MIB = 1024 * 1024
VMEM_LIMIT_V7X = 56 * MIB


def _cparams(n_axes):
    return pltpu.CompilerParams(
        dimension_semantics=("arbitrary",) * n_axes, vmem_limit_bytes=VMEM_LIMIT_V7X)


def _silu(x):
    return x * jax.nn.sigmoid(x)


def _dot(a, b):
    return jnp.dot(a, b, preferred_element_type=F32)


def _dot_nt(a, b):
    return lax.dot_general(a, b, (((1,), (1,)), ((), ())), preferred_element_type=F32)


def _dot_tn(a, b):
    return lax.dot_general(a, b, (((0,), (0,)), ((), ())), preferred_element_type=F32)


def _mod_body(c_ref, w_ref, b_ref, o_ref):
    a = _silu(c_ref[...]).astype(BF16)
    o_ref[...] = _dot(a, w_ref[...].astype(BF16)) + b_ref[...]


def _modulation(cc, w_mod, b_mod, n_cols, tn=1024):
    rows, d = cc.shape
    n = n_cols
    return pl.pallas_call(
        _mod_body,
        grid=(n // tn,),
        in_specs=[
            pl.BlockSpec((rows, d), lambda j: (0, 0)),
            pl.BlockSpec((d, tn), lambda j: (0, j)),
            pl.BlockSpec((1, tn), lambda j: (0, j)),
        ],
        out_specs=pl.BlockSpec((rows, tn), lambda j: (0, j)),
        out_shape=jax.ShapeDtypeStruct((rows, n), F32),
        compiler_params=_cparams(1),
        name="modulation",
    )(cc, w_mod, b_mod.reshape(1, -1))


def _rms_mod(x, nw, shift, scale):
    ms = jnp.mean(x * x, axis=-1, keepdims=True)
    y = x * lax.rsqrt(ms + EPS) * nw
    return y * (1.0 + scale) + shift


INPROJ_NORM_SPLITS = 4


def _inproj_body(x_ref, nw_ref, sh_ref, sc_ref, w_ref, o_ref, u_ref):
    j = pl.program_id(1)
    tm = x_ref.shape[0]

    @pl.when(j == 0)
    def _():
        part = tm // INPROJ_NORM_SPLITS
        for r in range(INPROJ_NORM_SPLITS):
            rows = pl.ds(r * part, part)
            u = _rms_mod(x_ref[rows, :], nw_ref[...], sh_ref[0], sc_ref[0]).astype(BF16)
            u_ref[rows, :] = u
            o_ref[rows, :] = _dot(u, w_ref[...]).astype(o_ref.dtype)

    @pl.when(j > 0)
    def _():
        o_ref[...] = _dot(u_ref[...], w_ref[...]).astype(o_ref.dtype)


def _in_projection(x2d, norm_w, mod3, mod_row_fn, w_in, n_cols, tm, tn):
    rows, d = x2d.shape
    return pl.pallas_call(
        _inproj_body,
        grid=(rows // tm, n_cols // tn),
        in_specs=[
            pl.BlockSpec((tm, d), lambda i, j: (i, 0)),
            pl.BlockSpec((1, d), lambda i, j: (0, 0)),
            pl.BlockSpec((1, 1, d), lambda i, j: (mod_row_fn(i), 0, 0)),
            pl.BlockSpec((1, 1, d), lambda i, j: (mod_row_fn(i), 0, 1)),
            pl.BlockSpec((d, tn), lambda i, j: (0, j)),
        ],
        out_specs=pl.BlockSpec((tm, tn), lambda i, j: (i, j)),
        out_shape=jax.ShapeDtypeStruct((rows, n_cols), BF16),
        scratch_shapes=[pltpu.VMEM((tm, d), BF16)],
        compiler_params=_cparams(2),
        name="in_projection",
    )(x2d, norm_w.reshape(1, d), mod3, mod3, w_in)


NA_QROWS = 8
NA_KROWS = 2 * NA_QROWS


def _na_row_offset(tile_kind, i, w, rows):
    half = WIN_ROWS // 2
    if tile_kind == 0:
        r, key = i, w
    elif tile_kind == 1:
        r, key = NA_QROWS + i, NA_QROWS - half + w
    else:
        r, key = rows - NA_QROWS + i, rows - NA_KROWS + w
    start = min(max(r - half, 0), rows - WIN_ROWS)
    if not (start <= key < start + WIN_ROWS):
        return None
    return key - r + (WIN_ROWS - 1)


def _na_build_bias(rpb_ref, bias_ref, h, rows):
    w = GRID_W
    cq = lax.broadcasted_iota(jnp.int32, (w, 2 * w), 0)
    ck = lax.broadcasted_iota(jnp.int32, (w, 2 * w), 1) % w
    col_start = jnp.clip(cq - WIN_COLS // 2, 0, w - WIN_COLS)
    col_ok = (ck >= col_start) & (ck < col_start + WIN_COLS)
    col_off = jnp.clip(ck - cq, -(WIN_COLS - 1), WIN_COLS - 1) + (WIN_COLS - 1)
    neg = jnp.full((w, 2 * w), NEG_INF, F32)
    n_ro, n_co = 2 * WIN_ROWS - 1, 2 * WIN_COLS - 1
    tabs = []
    for ro in range(n_ro):
        t = jnp.zeros((w, 2 * w), F32)
        for j in range(n_co):
            t = jnp.where(col_off == j, rpb_ref[h, ro * n_co + j], t)
        tabs.append(jnp.where(col_ok, t, neg))
    left = lax.broadcasted_iota(jnp.int32, (w, 2 * w), 1) < w
    for kind in range(3):
        for i in range(NA_QROWS):
            for wp in range(NA_KROWS // 2):
                ra = _na_row_offset(kind, i, 2 * wp, rows)
                rb = _na_row_offset(kind, i, 2 * wp + 1, rows)
                ta = neg if ra is None else tabs[ra]
                tb = neg if rb is None else tabs[rb]
                blk = ta if ra == rb else jnp.where(left, ta, tb)
                bias_ref[kind, i * w:(i + 1) * w, wp * 2 * w:(wp + 1) * 2 * w] = blk


def _na_body(rpb_ref, q_ref, k_ref, v_ref, ck_ref, cv_ref, o_ref, bias_ref, sa_ref, sb_ref, *, rows):
    h = pl.program_id(0)
    w = GRID_W
    tq, tk = NA_QROWS * w, NA_KROWS * w
    n_tiles = rows // NA_QROWS
    scale = HEAD_DIM ** -0.5

    @pl.when(pl.program_id(1) == 0)
    def _():
        _na_build_bias(rpb_ref, bias_ref, h, rows)

    def offsets(t):
        krow0 = jnp.clip(t * NA_QROWS - WIN_ROWS // 2, 0, rows - NA_KROWS)
        return pl.multiple_of(t * tq, tq), pl.multiple_of(krow0 * w, 4 * w)

    def scores(t, s_ref):
        kind = jnp.where(t == 0, 0, jnp.where(t == n_tiles - 1, 2, 1))
        q0, k0 = offsets(t)
        q = q_ref[pl.ds(q0, tq), :]
        s_ref[:, :tk] = _dot_nt(q, k_ref[pl.ds(k0, tk), :]) * scale + bias_ref[kind]
        s_ref[:, tk:] = _dot_nt(q, ck_ref[...]) * scale

    def attend(t, s_ref):
        q0, k0 = offsets(t)
        s = s_ref[...]
        p = jnp.exp(s - jnp.max(s, axis=-1, keepdims=True))
        l = jnp.sum(p, axis=-1, keepdims=True)
        pb = p.astype(BF16)
        o = _dot(pb[:, :tk], v_ref[pl.ds(k0, tk), :]) + _dot(pb[:, tk:], cv_ref[...])
        o_ref[pl.ds(q0, tq), :] = (o / l).astype(o_ref.dtype)

    scores(jnp.int32(0), sa_ref)

    def pair(i, carry):
        t = 2 * i
        scores(t + 1, sb_ref)
        attend(t, sa_ref)
        scores(t + 2, sa_ref)
        attend(t + 1, sb_ref)
        return carry

    lax.fori_loop(0, n_tiles // 2 - 1, pair, 0)
    scores(jnp.int32(n_tiles - 1), sb_ref)
    attend(jnp.int32(n_tiles - 2), sa_ref)
    attend(jnp.int32(n_tiles - 1), sb_ref)


def _neighbourhood_attention(proj, cproj, rpb, batch, seq, ctx_len):
    rows = seq // GRID_W
    hb = NA_WIDTH // HEAD_DIM
    n_rpb = (2 * WIN_ROWS - 1) * (2 * WIN_COLS - 1)
    tq, tk = NA_QROWS * GRID_W, NA_KROWS * GRID_W
    grid_spec = pltpu.PrefetchScalarGridSpec(
        num_scalar_prefetch=0,
        grid=(NA_HEADS, batch),
        in_specs=[
            pl.BlockSpec(memory_space=pltpu.SMEM),
            pl.BlockSpec((seq, HEAD_DIM), lambda h, b: (b, 4 * hb + h)),
            pl.BlockSpec((seq, HEAD_DIM), lambda h, b: (b, h)),
            pl.BlockSpec((seq, HEAD_DIM), lambda h, b: (b, hb + h)),
            pl.BlockSpec((ctx_len, HEAD_DIM), lambda h, b: (b, h)),
            pl.BlockSpec((ctx_len, HEAD_DIM), lambda h, b: (b, hb + h)),
        ],
        out_specs=pl.BlockSpec((seq, HEAD_DIM), lambda h, b: (b, h)),
        scratch_shapes=[
            pltpu.VMEM((3, tq, tk), F32),
            pltpu.VMEM((tq, tk + ctx_len), F32),
            pltpu.VMEM((tq, tk + ctx_len), F32),
        ],
    )
    return pl.pallas_call(
        functools.partial(_na_body, rows=rows),
        grid_spec=grid_spec,
        out_shape=jax.ShapeDtypeStruct((batch * seq, NA_WIDTH), BF16),
        compiler_params=_cparams(2),
        name="neighbourhood_attention",
    )(rpb.reshape(NA_HEADS, n_rpb), proj, proj, proj, cproj, cproj)


def _log_sigmoid(x):
    return -(jnp.maximum(-x, 0.0) + jnp.log1p(jnp.exp(-jnp.abs(x))))


def _rope_partner_matrix():
    quarter = HEAD_DIM // 4
    src = lax.broadcasted_iota(jnp.int32, (HEAD_DIM, HEAD_DIM), 0)
    dst = lax.broadcasted_iota(jnp.int32, (HEAD_DIM, HEAD_DIM), 1)
    want = jnp.where(dst % (2 * quarter) < quarter, dst + quarter, dst - quarter)
    return (src == want).astype(BF16)


def _rope(x_bf16, partner_matrix, cos, sin_signed):
    return x_bf16.astype(F32) * cos + _dot(x_bf16, partner_matrix) * sin_signed


def _ret_body(df_ref, db_ref, gn_ref, cos_ref, sin_ref, q_ref, k_ref, v_ref, g_ref, ck_ref, cv_ref,
              cc_ref, wm_ref, bm_ref, wo_ref,
              o_ref, modl_ref, wob_ref, qr_ref, kr_ref, sf_ref, sb_ref):
    modl_ref[...] = _dot(_silu(cc_ref[...]).astype(BF16), wm_ref[...].astype(BF16)) + bm_ref[...]
    wob_ref[...] = wo_ref[...].astype(BF16)

    c, d = RET_BLOCK, HEAD_DIM
    seq = q_ref.shape[0]
    n_chunks = seq // c
    ctx_len = ck_ref.shape[0]
    scale = HEAD_DIM ** -0.5
    lgf_row = _log_sigmoid(df_ref[0])
    lgb_row = _log_sigmoid(db_ref[0])
    lgf = jnp.broadcast_to(lgf_row, (c, d))
    lgb = jnp.broadcast_to(lgb_row, (c, d))
    pos = lax.broadcasted_iota(jnp.int32, (c, d), 0).astype(F32)
    kdf = jnp.exp(lgf * (c - 1.0 - pos))
    kdb = jnp.exp(lgb * pos)
    qdf = jnp.exp(lgf * (pos + 1.0))
    qdb = jnp.exp(lgb * (c - pos))
    cdf = jnp.exp(lgf_row * float(c))
    cdb = jnp.exp(lgb_row * float(c))
    diff = (lax.broadcasted_iota(jnp.int32, (c, c), 0) - lax.broadcasted_iota(jnp.int32, (c, c), 1)).astype(F32)
    lgf_cc = jnp.broadcast_to(lgf_row[:, :1], (c, c))
    lgb_cc = jnp.broadcast_to(lgb_row[:, :1], (c, c))
    dmat = (jnp.where(diff >= 0, jnp.exp(lgf_cc * jnp.maximum(diff, 0.0)), 0.0)
            + jnp.where(diff <= 0, jnp.exp(lgb_cc * jnp.maximum(-diff, 0.0)), 0.0))

    cpos = lax.broadcasted_iota(jnp.int32, (ctx_len, d), 0).astype(F32)
    ckf = ck_ref[...].astype(F32) * scale
    cv = cv_ref[...]
    wf = jnp.exp(jnp.broadcast_to(lgf_row, (ctx_len, d)) * (ctx_len - 1.0 - cpos))
    wb = jnp.exp(jnp.broadcast_to(lgb_row, (ctx_len, d)) * cpos)
    s_f = _dot_tn((ckf * wf).astype(BF16), cv)
    s_b = _dot_tn((ckf * wb).astype(BF16), cv)

    rope_rows = 512
    pmat = _rope_partner_matrix()

    def rope_blk(i, carry):
        r0 = pl.multiple_of(i * rope_rows, rope_rows)
        cs = cos_ref[pl.ds(r0, rope_rows), :]
        sn = sin_ref[pl.ds(r0, rope_rows), :]
        qr_ref[pl.ds(r0, rope_rows), :] = _rope(q_ref[pl.ds(r0, rope_rows), :], pmat, cs, sn)
        kr_ref[pl.ds(r0, rope_rows), :] = _rope(k_ref[pl.ds(r0, rope_rows), :], pmat, cs, sn) * scale
        return carry

    lax.fori_loop(0, seq // rope_rows, rope_blk, 0)

    def scan_blk(n, carry):
        s, t = carry
        r0 = pl.multiple_of(n * c, c)
        kvf = _dot_tn((kr_ref[pl.ds(r0, c), :] * kdf).astype(BF16), v_ref[pl.ds(r0, c), :])
        sf_ref[n] = s.astype(BF16)
        m = n_chunks - 1 - n
        m0 = pl.multiple_of(m * c, c)
        kvb = _dot_tn((kr_ref[pl.ds(m0, c), :] * kdb).astype(BF16), v_ref[pl.ds(m0, c), :])
        sb_ref[m] = t.astype(BF16)
        return s * cdf + kvf, t * cdb + kvb

    lax.fori_loop(0, n_chunks, scan_blk, (s_f, s_b), unroll=4)

    gn = gn_ref[...]

    def out_blk(n, carry):
        r0 = pl.multiple_of(n * c, c)
        qc = qr_ref[pl.ds(r0, c), :]
        kc = kr_ref[pl.ds(r0, c), :]
        a = _dot_nt(qc.astype(BF16), kc.astype(BF16))
        o = (_dot((a * dmat).astype(BF16), v_ref[pl.ds(r0, c), :])
             + _dot((qc * qdf).astype(BF16), sf_ref[n])
             + _dot((qc * qdb).astype(BF16), sb_ref[n]))
        mu = jnp.mean(o, axis=-1, keepdims=True)
        var = jnp.mean(jnp.square(o - mu), axis=-1, keepdims=True)
        y = (o - mu) * lax.rsqrt(var + EPS) * gn
        o_ref[pl.ds(r0, c), :] = (y * _silu(g_ref[pl.ds(r0, c), :].astype(F32))).astype(o_ref.dtype)
        return carry

    lax.fori_loop(0, n_chunks, out_blk, 0, unroll=4)


def _rope_tables(seq):
    axis_dim = HEAD_DIM // 2
    inv_freq = ROPE_BASE ** (-jnp.arange(0, axis_dim, 2, dtype=F32) / axis_dim)
    rows = seq // GRID_W
    ang_r = jnp.arange(rows, dtype=F32)[:, None] * inv_freq
    ang_c = jnp.arange(GRID_W, dtype=F32)[:, None] * inv_freq
    by_row = lambda a: jnp.repeat(a, GRID_W, axis=0)
    by_col = lambda a: jnp.tile(a, (rows, 1))
    cr, sr, cc, sc = by_row(jnp.cos(ang_r)), by_row(jnp.sin(ang_r)), by_col(jnp.cos(ang_c)), by_col(jnp.sin(ang_c))
    return jnp.concatenate([cr, cr, cc, cc], axis=-1), jnp.concatenate([-sr, sr, -sc, sc], axis=-1)


def _retention(proj, cproj, decay_f, decay_b, gn_w, batch, seq, ctx_len, cc, w_mod, b_mod, mod_from, w_out):
    hb = RET_WIDTH // HEAD_DIM
    cos, sin = _rope_tables(seq)
    dec_f = jnp.broadcast_to(decay_f.astype(F32)[:, None, None], (RET_HEADS, 1, HEAD_DIM))
    dec_b = jnp.broadcast_to(decay_b.astype(F32)[:, None, None], (RET_HEADS, 1, HEAD_DIM))
    n_chunks = seq // RET_BLOCK
    blk = lambda rows, fn: pl.BlockSpec((rows, HEAD_DIM), fn)
    n_steps = RET_HEADS * batch
    step = lambda h, b: h * batch + b
    mod_rows, d = cc.shape
    late = w_mod.shape[1] - mod_from
    mcols, orows = late // n_steps, w_out.shape[0] // n_steps
    side_in = [
        pl.BlockSpec((mod_rows, d), lambda h, b: (0, 0)),
        pl.BlockSpec((d, mcols), lambda h, b: (0, mod_from // mcols + step(h, b))),
        pl.BlockSpec((1, mcols), lambda h, b: (0, mod_from // mcols + step(h, b))),
        pl.BlockSpec((orows, w_out.shape[1]), lambda h, b: (step(h, b), 0)),
    ]
    side_out = [
        pl.BlockSpec((mod_rows, mcols), lambda h, b: (0, step(h, b))),
        pl.BlockSpec((orows, w_out.shape[1]), lambda h, b: (step(h, b), 0)),
    ]
    side_shape = [jax.ShapeDtypeStruct((mod_rows, late), F32), jax.ShapeDtypeStruct(w_out.shape, BF16)]
    return pl.pallas_call(
        _ret_body,
        grid=(RET_HEADS, batch),
        in_specs=[
            pl.BlockSpec((1, 1, HEAD_DIM), lambda h, b: (h, 0, 0)),
            pl.BlockSpec((1, 1, HEAD_DIM), lambda h, b: (h, 0, 0)),
            blk(1, lambda h, b: (0, h)),
            blk(seq, lambda h, b: (0, 0)),
            blk(seq, lambda h, b: (0, 0)),
            blk(seq, lambda h, b: (b, 5 * hb + h)),
            blk(seq, lambda h, b: (b, 2 * hb + h)),
            blk(seq, lambda h, b: (b, 3 * hb + h)),
            blk(seq, lambda h, b: (b, 6 * hb + h)),
            blk(ctx_len, lambda h, b: (b, 2 * hb + h)),
            blk(ctx_len, lambda h, b: (b, 3 * hb + h)),
        ] + side_in,
        out_specs=[blk(seq, lambda h, b: (b, h))] + side_out,
        out_shape=[jax.ShapeDtypeStruct((batch * seq, RET_WIDTH), BF16)] + side_shape,
        scratch_shapes=[
            pltpu.VMEM((seq, HEAD_DIM), F32),
            pltpu.VMEM((seq, HEAD_DIM), F32),
            pltpu.VMEM((n_chunks, HEAD_DIM, HEAD_DIM), BF16),
            pltpu.VMEM((n_chunks, HEAD_DIM, HEAD_DIM), BF16),
        ],
        compiler_params=_cparams(2),
        name="retention",
    )(dec_f, dec_b, gn_w.reshape(1, RET_WIDTH), cos, sin, proj, proj, proj, proj, cproj, cproj,
      cc, w_mod, b_mod.reshape(1, -1), w_out)


def _split_bf16(x):
    hi = x.astype(BF16)
    lo = (x - hi.astype(F32)).astype(BF16)
    return hi, lo


def _cast_body(x_ref, o_ref):
    o_ref[...] = x_ref[...].astype(o_ref.dtype)


def _cast_bf16(w, tm=512):
    rows, cols = w.shape
    return pl.pallas_call(
        _cast_body,
        grid=(rows // tm,),
        in_specs=[pl.BlockSpec((tm, cols), lambda i: (i, 0))],
        out_specs=pl.BlockSpec((tm, cols), lambda i: (i, 0)),
        out_shape=jax.ShapeDtypeStruct((rows, cols), BF16),
        compiler_params=_cparams(1),
        name="cast_bf16",
    )(w)


OUTPROJ_ROW_SPLITS = 2


def _outproj_body(na_ref, ret_ref, w_ref, x_ref, gate_ref, nw_ref, sh_ref, sc_ref, wr_ref,
                  h1_ref, u2_ref, lg_ref):
    half = na_ref.shape[1]
    n_e = wr_ref.shape[0]
    tm = x_ref.shape[0]
    wh, wl = _split_bf16(wr_ref[...])
    w_router = jnp.concatenate([wh, wl], axis=0)
    for r in range(OUTPROJ_ROW_SPLITS):
        rows = pl.ds(r * (tm // OUTPROJ_ROW_SPLITS), tm // OUTPROJ_ROW_SPLITS)
        mix = _dot(na_ref[rows, :], w_ref[:half, :]) + _dot(ret_ref[rows, :], w_ref[half:, :])
        h1 = x_ref[rows, :] + gate_ref[0] * mix
        h1_ref[rows, :] = h1
        u2 = _rms_mod(h1, nw_ref[...], sh_ref[0], sc_ref[0])
        u2_ref[rows, :] = u2
        both = _dot_nt(w_router, u2.astype(BF16))
        lg_ref[:, rows] = both[:n_e] + both[n_e:]


def _out_projection(na, ret, w_out_bf16, x2d, mod3, norm_w, w_router_t, seq, tm=512):
    rows, d = x2d.shape
    half = na.shape[1]
    per_b = seq // tm
    return pl.pallas_call(
        _outproj_body,
        grid=(rows // tm,),
        in_specs=[
            pl.BlockSpec((tm, half), lambda i: (i, 0)),
            pl.BlockSpec((tm, half), lambda i: (i, 0)),
            pl.BlockSpec((2 * half, d), lambda i: (0, 0)),
            pl.BlockSpec((tm, d), lambda i: (i, 0)),
            pl.BlockSpec((1, 1, d), lambda i: (i // per_b, 0, MOD_LATE_GATE_MIX)),
            pl.BlockSpec((1, d), lambda i: (0, 0)),
            pl.BlockSpec((1, 1, d), lambda i: (i // per_b, 0, MOD_LATE_SHIFT_FFN)),
            pl.BlockSpec((1, 1, d), lambda i: (i // per_b, 0, MOD_LATE_SCALE_FFN)),
            pl.BlockSpec((N_EXPERTS, d), lambda i: (0, 0)),
        ],
        out_specs=[
            pl.BlockSpec((tm, d), lambda i: (i, 0)),
            pl.BlockSpec((tm, d), lambda i: (i, 0)),
            pl.BlockSpec((N_EXPERTS, tm), lambda i: (0, i)),
        ],
        out_shape=[
            jax.ShapeDtypeStruct((rows, d), F32),
            jax.ShapeDtypeStruct((rows, d), F32),
            jax.ShapeDtypeStruct((N_EXPERTS, rows), F32),
        ],
        compiler_params=_cparams(1),
        name="out_projection",
    )(na, ret, w_out_bf16, x2d, mod3, norm_w.reshape(1, d), mod3, mod3, w_router_t)


LANES = 128
SUBLANES = 8


def _prefix_incl_lanes(x, tri):
    r, l = x.shape
    nb = l // LANES
    xs = jnp.concatenate([x[:, t * LANES:(t + 1) * LANES] for t in range(nb)], axis=0).astype(BF16)
    p = _dot(xs, tri)
    outs, run = [], jnp.zeros((r, 1), F32)
    for t in range(nb):
        blk = p[t * r:(t + 1) * r] + run
        outs.append(blk)
        run = blk[:, LANES - 1:LANES]
    return jnp.concatenate(outs, axis=1)


NOT_ROUTED = -(1 << 20)


def _route_body(lg_ref, gidx_ref, gate_ref, slot_ref, before_ref, *, cap):
    b = pl.program_id(0)
    n_e, seq = lg_ref.shape
    kf = float(cap)
    lg = lg_ref[...]
    ex = jnp.exp(lg - jnp.max(lg, axis=0, keepdims=True))
    aff = ex / jnp.sum(ex, axis=0, keepdims=True)

    def cond(c):
        return (c[0] < 4096) & (c[5] > 0.5)

    def step(c):
        it, lo, hi, thr, done, _ = c
        mid = 0.5 * (lo + hi)
        above = jnp.sum((aff > mid).astype(F32), axis=1, keepdims=True)
        hit = above == kf
        stuck = (mid <= lo) | (mid >= hi)
        active = done < 0.5
        thr = jnp.where(active & hit, mid, jnp.where(active & stuck, hi, thr))
        go = active & ~(hit | stuck)
        ge = above >= kf
        lo = jnp.where(go & ge, mid, lo)
        hi = jnp.where(go & ~ge, mid, hi)
        done = jnp.where(active & (hit | stuck), 1.0, done)
        return it + 1, lo, hi, thr, done, jnp.sum(1.0 - done)

    col = lambda v: jnp.full((n_e, 1), v, F32)
    init = (jnp.int32(0), col(-1.0), col(2.0), col(0.0), col(0.0), jnp.float32(n_e))
    thr = lax.while_loop(cond, step, init)[3]

    ii = lax.broadcasted_iota(jnp.int32, (LANES, LANES), 0)
    jj = lax.broadcasted_iota(jnp.int32, (LANES, LANES), 1)
    tri = (ii <= jj).astype(BF16)
    gt = aff > thr
    eq = (aff == thr).astype(F32)
    need = kf - jnp.sum(gt.astype(F32), axis=1, keepdims=True)
    eq_before = _prefix_incl_lanes(eq, tri) - eq
    mask = jnp.where(gt | ((eq > 0.5) & (eq_before < need)), 1.0, 0.0)

    slot = _prefix_incl_lanes(mask, tri) - mask
    before_ref[0] = slot.astype(jnp.int32)
    slot_ref[0] = jnp.where(mask > 0.5, slot, float(NOT_ROUTED)).astype(jnp.int32)

    tok = lax.broadcasted_iota(jnp.int32, (1, seq), 1).astype(F32)
    tok_hi = jnp.floor(tok * (1.0 / 64))
    tok_lo = tok - 64.0 * tok_hi
    n_hi = ROUTE_SLOT_HI
    n_lo = cap // n_hi
    hi_iota = lax.broadcasted_iota(jnp.int32, (n_hi, seq), 0).astype(F32)
    lo_iota = lax.broadcasted_iota(jnp.int32, (n_lo, seq), 0).astype(F32)
    for e in range(n_e):
        se = slot[e:e + 1]
        s_hi = jnp.floor(se * (1.0 / n_lo))
        s_lo = se - n_lo * s_hi
        in_hi = jnp.where((s_hi == hi_iota) & (mask[e:e + 1] > 0.5), 1.0, 0.0)
        in_lo = (s_lo == lo_iota).astype(BF16)
        a = aff[e:e + 1]
        a_hi = a.astype(BF16).astype(F32)
        a_mid = (a - a_hi).astype(BF16).astype(F32)
        a_lo = a - a_hi - a_mid
        vals = (tok_hi, tok_lo, a_hi, a_mid, a_lo)
        lhs = jnp.concatenate([in_hi * v for v in vals], axis=0).astype(BF16)
        got = _dot_nt(lhs, in_lo)
        part = lambda r: got[r * n_hi:(r + 1) * n_hi]
        gidx_ref[0, e] = (part(0) * 64.0 + part(1)).astype(jnp.int32) + b * seq
        gate_ref[0, e] = part(2) + part(3) + part(4)


ROUTE_SLOT_HI = 16


def _routing(logits_t, batch, seq):
    n_e = logits_t.shape[0]
    cap = CAPACITY_FACTOR * seq // n_e
    n_hi, n_lo = ROUTE_SLOT_HI, cap // ROUTE_SLOT_HI
    bec = pl.BlockSpec((1, n_e, n_hi, n_lo), lambda b: (b, 0, 0, 0))
    bel = pl.BlockSpec((1, n_e, seq), lambda b: (b, 0, 0))
    gidx, gates, slot, before = pl.pallas_call(
        functools.partial(_route_body, cap=cap),
        grid=(batch,),
        in_specs=[pl.BlockSpec((n_e, seq), lambda b: (0, b))],
        out_specs=[bec, bec, bel, bel],
        out_shape=[
            jax.ShapeDtypeStruct((batch, n_e, n_hi, n_lo), jnp.int32),
            jax.ShapeDtypeStruct((batch, n_e, n_hi, n_lo), F32),
            jax.ShapeDtypeStruct((batch, n_e, seq), jnp.int32),
            jax.ShapeDtypeStruct((batch, n_e, seq), jnp.int32),
        ],
        compiler_params=_cparams(1),
        name="routing",
    )(logits_t)
    flat = lambda a: a.reshape(batch, n_e, cap)
    return flat(gidx), flat(gates), slot, before


def _hbm_row(ref, r):
    return ref.at[lax.shift_right_logical(r, 3), pl.ds(r & (SUBLANES - 1), 1)]


MOE_ROW_SPLITS = 2


def _moe_body(gidx_ref, gnext_ref, gate_ref, u2_hbm, wg_ref, wu_ref, wd_ref, y_ref,
              stage_ref, xe_ref, acc_ref, gsem, *, n_f):
    e = pl.program_id(0)
    f = pl.program_id(1)
    n_e = pl.num_programs(0)
    m, d = y_ref.shape
    sub = SUBLANES
    n_tiles = m // sub
    assert n_f >= 2
    per_step = -(-n_tiles // (n_f - 1))
    last_count = n_tiles - per_step * (n_f - 2)

    def gather_copy(idx_ref, i, k):
        return pltpu.make_async_copy(_hbm_row(u2_hbm, idx_ref[0, i * sub + k]), stage_ref.at[i, pl.ds(k, 1)], gsem)

    def for_rows(fn, tiles_per_trip=4):
        def body(t, c):
            for kk in range(tiles_per_trip * sub):
                fn(t * tiles_per_trip + kk // sub, kk % sub)
            return c
        lax.fori_loop(0, n_tiles // tiles_per_trip, body, 0)

    wait_gather = lambda: for_rows(lambda i, k: gather_copy(gidx_ref, i, k).wait())

    def ffn_step(first, gather_tiles):
        i0 = (f - 1) * per_step
        for kk in range(gather_tiles * sub):
            gather_copy(gnext_ref, i0 + kk // sub, kk % sub).start(priority=1)
        wg, wu, wd = wg_ref[...].astype(BF16), wu_ref[...].astype(BF16), wd_ref[...].astype(BF16)
        part = m // MOE_ROW_SPLITS
        for r in range(MOE_ROW_SPLITS):
            rows = pl.ds(r * part, part)
            if first:
                x = stage_ref[pl.ds(r * (part // sub), part // sub)].reshape(part, d).astype(BF16)
                xe_ref[rows, :] = x
            else:
                x = xe_ref[rows, :]
            hid = _silu(_dot(x, wg)) * _dot(x, wu)
            out = _dot(hid.astype(BF16), wd)
            acc_ref[rows, :] = out if first else acc_ref[rows, :] + out

    @pl.when(f == 0)
    def _():
        @pl.when(e == 0)
        def _():
            for_rows(lambda i, k: gather_copy(gidx_ref, i, k).start())

        wait_gather()
        ffn_step(True, 0)

    if n_f > 2:
        pl.when((f > 0) & (f < n_f - 1))(lambda: ffn_step(False, per_step))

    @pl.when(f == n_f - 1)
    def _():
        ffn_step(False, last_count)

        @pl.when(f > 0)
        def _():
            g = jnp.transpose(jnp.broadcast_to(gate_ref[...], (LANES, m)))
            for k in range(d // LANES):
                cols = slice(k * LANES, (k + 1) * LANES)
                y_ref[:, cols] = (acc_ref[:, cols] * g).astype(y_ref.dtype)

        @pl.when(e == n_e - 1)
        def _():
            wait_gather()


def _moe_ffn(gidx, gates, u2, w_gate, w_up, w_down, tf=256):
    n_e, _, m = gidx.shape
    rows, d = u2.shape
    ff = w_gate.shape[2]
    n_f = ff // tf
    sub = SUBLANES
    smem = lambda fn: pl.BlockSpec((None, 1, m), fn, memory_space=pltpu.SMEM)
    return pl.pallas_call(
        functools.partial(_moe_body, n_f=n_f),
        grid=(n_e, n_f),
        in_specs=[
            smem(lambda e, f: (e, 0, 0)),
            smem(lambda e, f: (jnp.minimum(e + 1, n_e - 1), 0, 0)),
            pl.BlockSpec((None, 1, m), lambda e, f: (e, 0, 0)),
            pl.BlockSpec(memory_space=pl.ANY),
            pl.BlockSpec((None, d, tf), lambda e, f: (e, 0, f)),
            pl.BlockSpec((None, d, tf), lambda e, f: (e, 0, f)),
            pl.BlockSpec((None, tf, d), lambda e, f: (e, f, 0)),
        ],
        out_specs=pl.BlockSpec((m, d), lambda e, f: (e, 0)),
        out_shape=jax.ShapeDtypeStruct((n_e * m, d), BF16),
        scratch_shapes=[
            pltpu.VMEM((m // sub, sub, d), F32),
            pltpu.VMEM((m, d), BF16),
            pltpu.VMEM((m, d), F32),
            pltpu.SemaphoreType.DMA,
        ],
        compiler_params=_cparams(2),
        name="moe_ffn",
    )(gidx, gidx, gates, u2.reshape(rows // sub, sub, d), w_gate, w_up, w_down)


COMBINE_TOKENS = 256
COMBINE_PIECE = 64
BF16_ROWS = 16


def _combine_body(start_ref, npiece_ref, h1_ref, y_hbm, slot_ref, gate_ref, nw_ref, o_ref,
                  acc_ref, buf_ref, xbuf_ref, sem, xsem, *, m, cap, per_b):
    i = pl.program_id(0)
    n_tiles = pl.num_programs(0)
    n_e = slot_ref.shape[1]
    t, p = COMBINE_TOKENS, COMBINE_PIECE
    total = n_e * m
    cur = i % 2
    b = i // per_b

    def piece_copy(tile, e, dst):
        st = pl.multiple_of(start_ref[tile * n_e + e], BF16_ROWS)
        return pltpu.make_async_copy(y_hbm.at[pl.ds(st, p)], buf_ref.at[dst, pl.ds(e * p, p)], sem.at[dst])

    @pl.when(i == 0)
    def _():
        for e in range(n_e):
            piece_copy(0, e, 0).start()

    @pl.when(i + 1 < n_tiles)
    def _():
        for e in range(n_e):
            piece_copy(i + 1, e, 1 - cur).start()

    for e in range(n_e):
        piece_copy(i, e, cur).wait()

    slots = slot_ref[0]
    riota = lax.broadcasted_iota(jnp.int32, (p, t), 0)

    def local_row(e, st):
        return slots[e:e + 1, :] + (e * m + b * cap - st)

    sel = jnp.concatenate(
        [(local_row(e, start_ref[i * n_e + e]) == riota).astype(BF16) for e in range(n_e)], axis=0)
    acc_ref[...] = _dot_tn(sel, buf_ref[cur])

    for e in range(n_e):
        st0 = start_ref[i * n_e + e]

        def extra(k, carry, e=e, st0=st0):
            want = st0 + k * p
            st = pl.multiple_of(jnp.minimum(want, total - p), BF16_ROWS)
            cp = pltpu.make_async_copy(y_hbm.at[pl.ds(st, p)], xbuf_ref, xsem)
            cp.start()
            cp.wait()
            blk = ((local_row(e, st) == riota) & (riota >= want - st)).astype(BF16)
            acc_ref[...] += _dot_tn(blk, xbuf_ref[...])
            return carry

        lax.fori_loop(1, npiece_ref[i * n_e + e], extra, 0)

    h2 = h1_ref[...] + gate_ref[0] * acc_ref[...]
    ms = jnp.mean(h2 * h2, axis=-1, keepdims=True)
    o_ref[...] = h2 * lax.rsqrt(ms + EPS) * nw_ref[...]


def _combine_pieces(before, cap):
    batch, n_e, seq = before.shape
    t, p = COMBINE_TOKENS, COMBINE_PIECE
    m = batch * cap
    total = n_e * m
    bounds = jnp.concatenate([before[:, :, ::t], jnp.full((batch, n_e, 1), cap, jnp.int32)], axis=2)
    base = (jnp.arange(n_e, dtype=jnp.int32) * m)[None, :, None] + (jnp.arange(batch, dtype=jnp.int32) * cap)[:, None, None]
    first = base + bounds[:, :, :-1]
    end = base + bounds[:, :, 1:]
    start = jnp.minimum(first // BF16_ROWS * BF16_ROWS, total - p)
    npiece = jnp.maximum((end - start + p - 1) // p, 1)
    by_tile = lambda a: a.transpose(0, 2, 1).reshape(-1)
    return by_tile(start), by_tile(npiece)


def _combine(h1, y, slot, before, mod3, final_w, seq, cap):
    rows, d = h1.shape
    batch, n_e, _ = slot.shape
    t, p = COMBINE_TOKENS, COMBINE_PIECE
    per_b = seq // t
    start, npiece = _combine_pieces(before, cap)
    grid_spec = pltpu.PrefetchScalarGridSpec(
        num_scalar_prefetch=2,
        grid=(rows // t,),
        in_specs=[
            pl.BlockSpec((t, d), lambda i, st, npc: (i, 0)),
            pl.BlockSpec(memory_space=pl.ANY),
            pl.BlockSpec((1, n_e, t), lambda i, st, npc: (i // per_b, 0, i % per_b)),
            pl.BlockSpec((1, 1, d), lambda i, st, npc: (i // per_b, 0, MOD_LATE_GATE_FFN)),
            pl.BlockSpec((1, d), lambda i, st, npc: (0, 0)),
        ],
        out_specs=pl.BlockSpec((t, d), lambda i, st, npc: (i, 0)),
        scratch_shapes=[
            pltpu.VMEM((t, d), F32),
            pltpu.VMEM((2, n_e * p, d), y.dtype),
            pltpu.VMEM((p, d), y.dtype),
            pltpu.SemaphoreType.DMA((2,)),
            pltpu.SemaphoreType.DMA,
        ],
    )
    return pl.pallas_call(
        functools.partial(_combine_body, m=batch * cap, cap=cap, per_b=per_b),
        grid_spec=grid_spec,
        out_shape=jax.ShapeDtypeStruct((rows, d), F32),
        compiler_params=_cparams(1),
        name="combine",
    )(start, npiece, h1, y, slot, mod3, final_w.reshape(1, d))


def kernel(x, c, ctx, c_ctx, w_mod, b_mod, norm_mix_w, norm_ffn_w, w_in, na_rpb, ret_decay_fwd,
           ret_decay_bwd, ret_gn_w, w_out, w_router, w_gate, w_up, w_down, final_norm_w):
    batch, seq, d = x.shape
    ctx_len = ctx.shape[1]
    assert w_mod.shape[0] == 1, "one trunk layer"
    assert seq % (NA_QROWS * GRID_W) == 0 and seq // GRID_W >= 3 * NA_QROWS
    n_e = w_router.shape[2]
    cap = CAPACITY_FACTOR * seq // n_e

    mod_rows = 8
    cc = jnp.concatenate([c, c_ctx[None], jnp.zeros((mod_rows - batch - 1, d), c.dtype)], axis=0)
    mod3 = _modulation(cc, w_mod[0], b_mod[0], MOD_EARLY * d).reshape(mod_rows, 1, MOD_EARLY * d)

    x2d = x.reshape(batch * seq, d)
    tm = 1024
    w_in_bf16 = _cast_bf16(w_in[0], tm=256)
    proj = _in_projection(x2d, norm_mix_w[0], mod3, lambda i: i // (seq // tm), w_in_bf16,
                          w_in.shape[2], tm, w_in.shape[2] // 4)
    cproj = _in_projection(ctx.reshape(batch * ctx_len, d), norm_mix_w[0], mod3, lambda i: batch,
                           w_in_bf16, KV_COLS, batch * ctx_len, 1024)

    na = _neighbourhood_attention(proj, cproj, na_rpb[0], batch, seq, ctx_len)
    ret, mod_late, w_out_bf16 = _retention(proj, cproj, ret_decay_fwd[0], ret_decay_bwd[0], ret_gn_w[0],
                                           batch, seq, ctx_len, cc, w_mod[0], b_mod[0], MOD_EARLY * d, w_out[0])
    mod_late3 = mod_late.reshape(mod_rows, 1, (N_MOD - MOD_EARLY) * d)

    h1, u2, logits_t = _out_projection(na, ret, w_out_bf16, x2d, mod_late3, norm_ffn_w[0], w_router[0].T, seq)

    gidx, gates, slot, before = _routing(logits_t, batch, seq)
    per_expert = lambda a: a.transpose(1, 0, 2).reshape(n_e, 1, batch * cap)
    y = _moe_ffn(per_expert(gidx), per_expert(gates), u2, w_gate[0], w_up[0], w_down[0])
    out = _combine(h1, y, slot, before, mod_late3, final_norm_w, seq, cap)
    return out.reshape(batch, seq, d)
```

```python
import functools

import jax
import jax.numpy as jnp
from jax import lax
from jax.experimental import pallas as pl
from jax.experimental.pallas import tpu as pltpu

GRID_W = 64
HEAD_DIM = 128
NA_HEADS = 8
RET_HEADS = 8
NA_WIDTH = NA_HEADS * HEAD_DIM
RET_WIDTH = RET_HEADS * HEAD_DIM
WIN_ROWS = 8
WIN_COLS = 16
RET_BLOCK = 256
ROPE_BASE = 10000.0
N_EXPERTS = 16
CAPACITY_FACTOR = 2
N_MOD = 6
MOD_EARLY = 2
MOD_LATE_GATE_MIX, MOD_LATE_SHIFT_FFN, MOD_LATE_SCALE_FFN, MOD_LATE_GATE_FFN = 0, 1, 2, 3
EPS = 1e-6
NEG_INF = -1e30
KV_COLS = 2 * NA_WIDTH + 2 * RET_WIDTH

F32 = jnp.float32
BF16 = jnp.bfloat16
MIB = 1024 * 1024
VMEM_LIMIT_V7X = 56 * MIB


def _cparams(n_axes, vmem_limit_bytes=VMEM_LIMIT_V7X):
    return pltpu.CompilerParams(
        dimension_semantics=("arbitrary",) * n_axes, vmem_limit_bytes=vmem_limit_bytes)


def _silu(x):
    return x * jax.nn.sigmoid(x)


def _dot(a, b):
    return jnp.dot(a, b, preferred_element_type=F32)


def _dot_nt(a, b):
    return lax.dot_general(a, b, (((1,), (1,)), ((), ())), preferred_element_type=F32)


def _dot_tn(a, b):
    return lax.dot_general(a, b, (((0,), (0,)), ((), ())), preferred_element_type=F32)


def _mod_body(c_ref, w_ref, b_ref, o_ref):
    a = _silu(c_ref[...]).astype(BF16)
    o_ref[...] = _dot(a, w_ref[...].astype(BF16)) + b_ref[...]


def _modulation(cc, w_mod, b_mod, n_cols, tn=1024):
    rows, d = cc.shape
    n = n_cols
    return pl.pallas_call(
        _mod_body,
        grid=(n // tn,),
        in_specs=[
            pl.BlockSpec((rows, d), lambda j: (0, 0)),
            pl.BlockSpec((d, tn), lambda j: (0, j)),
            pl.BlockSpec((1, tn), lambda j: (0, j)),
        ],
        out_specs=pl.BlockSpec((rows, tn), lambda j: (0, j)),
        out_shape=jax.ShapeDtypeStruct((rows, n), F32),
        compiler_params=_cparams(1),
        name="modulation",
    )(cc, w_mod, b_mod.reshape(1, -1))


def _rms_mod(x, nw, shift, scale):
    ms = jnp.mean(x * x, axis=-1, keepdims=True)
    y = x * lax.rsqrt(ms + EPS) * nw
    return y * (1.0 + scale) + shift


INPROJ_NORM_SPLITS = 4


def _inproj_body(x_ref, nw_ref, sh_ref, sc_ref, w_ref, o_ref, u_ref):
    j = pl.program_id(1)
    tm = x_ref.shape[0]

    @pl.when(j == 0)
    def _():
        part = tm // INPROJ_NORM_SPLITS
        for r in range(INPROJ_NORM_SPLITS):
            rows = pl.ds(r * part, part)
            u = _rms_mod(x_ref[rows, :], nw_ref[...], sh_ref[0], sc_ref[0]).astype(BF16)
            u_ref[rows, :] = u
            o_ref[rows, :] = _dot(u, w_ref[...]).astype(o_ref.dtype)

    @pl.when(j > 0)
    def _():
        o_ref[...] = _dot(u_ref[...], w_ref[...]).astype(o_ref.dtype)


def _ctx_proj_body(x_ref, nw_ref, sh_ref, sc_ref, w_ref, o_ref, wb_ref, u_ref, *, kv_tiles):
    j = pl.program_id(0)

    @pl.when(j == 0)
    def _():
        u_ref[...] = _rms_mod(x_ref[...], nw_ref[...], sh_ref[0], sc_ref[0]).astype(BF16)

    wb_ref[...] = w_ref[...].astype(BF16)

    @pl.when(j < kv_tiles)
    def _():
        o_ref[...] = _dot(u_ref[...], wb_ref[...]).astype(o_ref.dtype)


def _ctx_projection(x2d, norm_w, mod3, mod_row, w_in, kv_cols, tn=1024):
    rows, d = x2d.shape
    n = w_in.shape[1]
    kv_tiles = kv_cols // tn
    return pl.pallas_call(
        functools.partial(_ctx_proj_body, kv_tiles=kv_tiles),
        grid=(n // tn,),
        in_specs=[
            pl.BlockSpec((rows, d), lambda j: (0, 0)),
            pl.BlockSpec((1, d), lambda j: (0, 0)),
            pl.BlockSpec((1, 1, d), lambda j: (mod_row, 0, 0)),
            pl.BlockSpec((1, 1, d), lambda j: (mod_row, 0, 1)),
            pl.BlockSpec((d, tn), lambda j: (0, j)),
        ],
        out_specs=[
            pl.BlockSpec((rows, tn), lambda j: (0, jnp.minimum(j, kv_tiles - 1))),
            pl.BlockSpec((d, tn), lambda j: (0, j)),
        ],
        out_shape=[jax.ShapeDtypeStruct((rows, kv_cols), BF16), jax.ShapeDtypeStruct((d, n), BF16)],
        scratch_shapes=[pltpu.VMEM((rows, d), BF16)],
        compiler_params=_cparams(1),
        name="ctx_projection",
    )(x2d, norm_w.reshape(1, d), mod3, mod3, w_in)


def _in_projection(x2d, norm_w, mod3, mod_row_fn, w_in, n_cols, tm, tn):
    rows, d = x2d.shape
    return pl.pallas_call(
        _inproj_body,
        grid=(rows // tm, n_cols // tn),
        in_specs=[
            pl.BlockSpec((tm, d), lambda i, j: (i, 0)),
            pl.BlockSpec((1, d), lambda i, j: (0, 0)),
            pl.BlockSpec((1, 1, d), lambda i, j: (mod_row_fn(i), 0, 0)),
            pl.BlockSpec((1, 1, d), lambda i, j: (mod_row_fn(i), 0, 1)),
            pl.BlockSpec((d, tn), lambda i, j: (0, j)),
        ],
        out_specs=pl.BlockSpec((tm, tn), lambda i, j: (i, j)),
        out_shape=jax.ShapeDtypeStruct((rows, n_cols), BF16),
        scratch_shapes=[pltpu.VMEM((tm, d), BF16)],
        compiler_params=_cparams(2),
        name="in_projection",
    )(x2d, norm_w.reshape(1, d), mod3, mod3, w_in)


NA_QROWS = 8
NA_KROWS = 2 * NA_QROWS


def _na_row_offset(tile_kind, i, w, rows):
    half = WIN_ROWS // 2
    if tile_kind == 0:
        r, key = i, w
    elif tile_kind == 1:
        r, key = NA_QROWS + i, NA_QROWS - half + w
    else:
        r, key = rows - NA_QROWS + i, rows - NA_KROWS + w
    start = min(max(r - half, 0), rows - WIN_ROWS)
    if not (start <= key < start + WIN_ROWS):
        return None
    return key - r + (WIN_ROWS - 1)


def _na_build_bias(rpb_ref, bias_ref, h, rows):
    w = GRID_W
    cq = lax.broadcasted_iota(jnp.int32, (w, 2 * w), 0)
    ck = lax.broadcasted_iota(jnp.int32, (w, 2 * w), 1) % w
    col_start = jnp.clip(cq - WIN_COLS // 2, 0, w - WIN_COLS)
    col_ok = (ck >= col_start) & (ck < col_start + WIN_COLS)
    col_off = jnp.clip(ck - cq, -(WIN_COLS - 1), WIN_COLS - 1) + (WIN_COLS - 1)
    neg = jnp.full((w, 2 * w), NEG_INF, F32)
    n_ro, n_co = 2 * WIN_ROWS - 1, 2 * WIN_COLS - 1
    tabs = []
    for ro in range(n_ro):
        t = jnp.zeros((w, 2 * w), F32)
        for j in range(n_co):
            t = jnp.where(col_off == j, rpb_ref[h, ro * n_co + j], t)
        tabs.append(jnp.where(col_ok, t, neg))
    left = lax.broadcasted_iota(jnp.int32, (w, 2 * w), 1) < w
    for kind in range(3):
        for i in range(NA_QROWS):
            for wp in range(NA_KROWS // 2):
                ra = _na_row_offset(kind, i, 2 * wp, rows)
                rb = _na_row_offset(kind, i, 2 * wp + 1, rows)
                ta = neg if ra is None else tabs[ra]
                tb = neg if rb is None else tabs[rb]
                blk = ta if ra == rb else jnp.where(left, ta, tb)
                bias_ref[kind, i * w:(i + 1) * w, wp * 2 * w:(wp + 1) * 2 * w] = blk


def _na_body(rpb_ref, q_ref, k_ref, v_ref, ck_ref, cv_ref, o_ref, bias_ref, sa_ref, sb_ref, *, rows):
    h = pl.program_id(0)
    w = GRID_W
    tq, tk = NA_QROWS * w, NA_KROWS * w
    n_tiles = rows // NA_QROWS
    scale = HEAD_DIM ** -0.5

    @pl.when(pl.program_id(1) == 0)
    def _():
        _na_build_bias(rpb_ref, bias_ref, h, rows)

    def offsets(t):
        krow0 = jnp.clip(t * NA_QROWS - WIN_ROWS // 2, 0, rows - NA_KROWS)
        return pl.multiple_of(t * tq, tq), pl.multiple_of(krow0 * w, 4 * w)

    def scores(t, s_ref):
        kind = jnp.where(t == 0, 0, jnp.where(t == n_tiles - 1, 2, 1))
        q0, k0 = offsets(t)
        q = q_ref[pl.ds(q0, tq), :]
        s_ref[:, :tk] = _dot_nt(q, k_ref[pl.ds(k0, tk), :]) * scale + bias_ref[kind]
        s_ref[:, tk:] = _dot_nt(q, ck_ref[...]) * scale

    def attend(t, s_ref):
        q0, k0 = offsets(t)
        s = s_ref[...]
        p = jnp.exp(s - jnp.max(s, axis=-1, keepdims=True))
        l = jnp.sum(p, axis=-1, keepdims=True)
        pb = p.astype(BF16)
        o = _dot(pb[:, :tk], v_ref[pl.ds(k0, tk), :]) + _dot(pb[:, tk:], cv_ref[...])
        o_ref[pl.ds(q0, tq), :] = (o / l).astype(o_ref.dtype)

    scores(jnp.int32(0), sa_ref)

    def pair(i, carry):
        t = 2 * i
        scores(t + 1, sb_ref)
        attend(t, sa_ref)
        scores(t + 2, sa_ref)
        attend(t + 1, sb_ref)
        return carry

    lax.fori_loop(0, n_tiles // 2 - 1, pair, 0)
    scores(jnp.int32(n_tiles - 1), sb_ref)
    attend(jnp.int32(n_tiles - 2), sa_ref)
    attend(jnp.int32(n_tiles - 1), sb_ref)


def _neighbourhood_attention(proj, cproj, rpb, batch, seq, ctx_len):
    rows = seq // GRID_W
    hb = NA_WIDTH // HEAD_DIM
    n_rpb = (2 * WIN_ROWS - 1) * (2 * WIN_COLS - 1)
    tq, tk = NA_QROWS * GRID_W, NA_KROWS * GRID_W
    grid_spec = pltpu.PrefetchScalarGridSpec(
        num_scalar_prefetch=0,
        grid=(NA_HEADS, batch),
        in_specs=[
            pl.BlockSpec(memory_space=pltpu.SMEM),
            pl.BlockSpec((seq, HEAD_DIM), lambda h, b: (b, 4 * hb + h)),
            pl.BlockSpec((seq, HEAD_DIM), lambda h, b: (b, h)),
            pl.BlockSpec((seq, HEAD_DIM), lambda h, b: (b, hb + h)),
            pl.BlockSpec((ctx_len, HEAD_DIM), lambda h, b: (b, h)),
            pl.BlockSpec((ctx_len, HEAD_DIM), lambda h, b: (b, hb + h)),
        ],
        out_specs=pl.BlockSpec((seq, HEAD_DIM), lambda h, b: (b, h)),
        scratch_shapes=[
            pltpu.VMEM((3, tq, tk), F32),
            pltpu.VMEM((tq, tk + ctx_len), F32),
            pltpu.VMEM((tq, tk + ctx_len), F32),
        ],
    )
    return pl.pallas_call(
        functools.partial(_na_body, rows=rows),
        grid_spec=grid_spec,
        out_shape=jax.ShapeDtypeStruct((batch * seq, NA_WIDTH), BF16),
        compiler_params=_cparams(2),
        name="neighbourhood_attention",
    )(rpb.reshape(NA_HEADS, n_rpb), proj, proj, proj, cproj, cproj)


def _log_sigmoid(x):
    return -(jnp.maximum(-x, 0.0) + jnp.log1p(jnp.exp(-jnp.abs(x))))


def _rope_partner_matrix():
    quarter = HEAD_DIM // 4
    src = lax.broadcasted_iota(jnp.int32, (HEAD_DIM, HEAD_DIM), 0)
    dst = lax.broadcasted_iota(jnp.int32, (HEAD_DIM, HEAD_DIM), 1)
    want = jnp.where(dst % (2 * quarter) < quarter, dst + quarter, dst - quarter)
    return (src == want).astype(BF16)


def _rope(x_bf16, partner_matrix, cos, sin_signed):
    return x_bf16.astype(F32) * cos + _dot(x_bf16, partner_matrix) * sin_signed


def _ret_body(df_ref, db_ref, gn_ref, cos_ref, sin_ref, q_ref, k_ref, v_ref, g_ref, ck_ref, cv_ref,
              cc_ref, wm_ref, bm_ref, wo_ref,
              o_ref, modl_ref, wob_ref, qr_ref, kr_ref, sf_ref, sb_ref):
    modl_ref[...] = _dot(_silu(cc_ref[...]).astype(BF16), wm_ref[...].astype(BF16)) + bm_ref[...]
    wob_ref[...] = wo_ref[...].astype(BF16)

    c, d = RET_BLOCK, HEAD_DIM
    seq = q_ref.shape[0]
    n_chunks = seq // c
    ctx_len = ck_ref.shape[0]
    scale = HEAD_DIM ** -0.5
    lgf_row = _log_sigmoid(df_ref[0])
    lgb_row = _log_sigmoid(db_ref[0])
    lgf = jnp.broadcast_to(lgf_row, (c, d))
    lgb = jnp.broadcast_to(lgb_row, (c, d))
    pos = lax.broadcasted_iota(jnp.int32, (c, d), 0).astype(F32)
    kdf = jnp.exp(lgf * (c - 1.0 - pos))
    kdb = jnp.exp(lgb * pos)
    qdf = jnp.exp(lgf * (pos + 1.0))
    qdb = jnp.exp(lgb * (c - pos))
    cdf = jnp.exp(lgf_row * float(c))
    cdb = jnp.exp(lgb_row * float(c))
    diff = (lax.broadcasted_iota(jnp.int32, (c, c), 0) - lax.broadcasted_iota(jnp.int32, (c, c), 1)).astype(F32)
    lgf_cc = jnp.broadcast_to(lgf_row[:, :1], (c, c))
    lgb_cc = jnp.broadcast_to(lgb_row[:, :1], (c, c))
    dmat = (jnp.where(diff >= 0, jnp.exp(lgf_cc * jnp.maximum(diff, 0.0)), 0.0)
            + jnp.where(diff <= 0, jnp.exp(lgb_cc * jnp.maximum(-diff, 0.0)), 0.0))

    cpos = lax.broadcasted_iota(jnp.int32, (ctx_len, d), 0).astype(F32)
    ckf = ck_ref[...].astype(F32) * scale
    cv = cv_ref[...]
    wf = jnp.exp(jnp.broadcast_to(lgf_row, (ctx_len, d)) * (ctx_len - 1.0 - cpos))
    wb = jnp.exp(jnp.broadcast_to(lgb_row, (ctx_len, d)) * cpos)
    s_f = _dot_tn((ckf * wf).astype(BF16), cv)
    s_b = _dot_tn((ckf * wb).astype(BF16), cv)

    rope_rows = 512
    pmat = _rope_partner_matrix()

    def rope_blk(i, carry):
        r0 = pl.multiple_of(i * rope_rows, rope_rows)
        cs = cos_ref[pl.ds(r0, rope_rows), :]
        sn = sin_ref[pl.ds(r0, rope_rows), :]
        qr_ref[pl.ds(r0, rope_rows), :] = _rope(q_ref[pl.ds(r0, rope_rows), :], pmat, cs, sn)
        kr_ref[pl.ds(r0, rope_rows), :] = _rope(k_ref[pl.ds(r0, rope_rows), :], pmat, cs, sn) * scale
        return carry

    lax.fori_loop(0, seq // rope_rows, rope_blk, 0)

    def scan_blk(n, carry):
        s, t = carry
        r0 = pl.multiple_of(n * c, c)
        kvf = _dot_tn((kr_ref[pl.ds(r0, c), :] * kdf).astype(BF16), v_ref[pl.ds(r0, c), :])
        sf_ref[n] = s.astype(BF16)
        m = n_chunks - 1 - n
        m0 = pl.multiple_of(m * c, c)
        kvb = _dot_tn((kr_ref[pl.ds(m0, c), :] * kdb).astype(BF16), v_ref[pl.ds(m0, c), :])
        sb_ref[m] = t.astype(BF16)
        return s * cdf + kvf, t * cdb + kvb

    lax.fori_loop(0, n_chunks, scan_blk, (s_f, s_b), unroll=4)

    gn = gn_ref[...]

    def out_blk(n, carry):
        r0 = pl.multiple_of(n * c, c)
        qc = qr_ref[pl.ds(r0, c), :]
        kc = kr_ref[pl.ds(r0, c), :]
        a = _dot_nt(qc.astype(BF16), kc.astype(BF16))
        o = (_dot((a * dmat).astype(BF16), v_ref[pl.ds(r0, c), :])
             + _dot((qc * qdf).astype(BF16), sf_ref[n])
             + _dot((qc * qdb).astype(BF16), sb_ref[n]))
        mu = jnp.mean(o, axis=-1, keepdims=True)
        var = jnp.mean(jnp.square(o - mu), axis=-1, keepdims=True)
        y = (o - mu) * lax.rsqrt(var + EPS) * gn
        o_ref[pl.ds(r0, c), :] = (y * _silu(g_ref[pl.ds(r0, c), :].astype(F32))).astype(o_ref.dtype)
        return carry

    lax.fori_loop(0, n_chunks, out_blk, 0, unroll=4)


def _rope_tables(seq):
    axis_dim = HEAD_DIM // 2
    inv_freq = ROPE_BASE ** (-jnp.arange(0, axis_dim, 2, dtype=F32) / axis_dim)
    rows = seq // GRID_W
    ang_r = jnp.arange(rows, dtype=F32)[:, None] * inv_freq
    ang_c = jnp.arange(GRID_W, dtype=F32)[:, None] * inv_freq
    by_row = lambda a: jnp.repeat(a, GRID_W, axis=0)
    by_col = lambda a: jnp.tile(a, (rows, 1))
    cr, sr, cc, sc = by_row(jnp.cos(ang_r)), by_row(jnp.sin(ang_r)), by_col(jnp.cos(ang_c)), by_col(jnp.sin(ang_c))
    return jnp.concatenate([cr, cr, cc, cc], axis=-1), jnp.concatenate([-sr, sr, -sc, sc], axis=-1)


def _retention(proj, cproj, decay_f, decay_b, gn_w, batch, seq, ctx_len, cc, w_mod, b_mod, mod_from, w_out):
    hb = RET_WIDTH // HEAD_DIM
    cos, sin = _rope_tables(seq)
    dec_f = jnp.broadcast_to(decay_f.astype(F32)[:, None, None], (RET_HEADS, 1, HEAD_DIM))
    dec_b = jnp.broadcast_to(decay_b.astype(F32)[:, None, None], (RET_HEADS, 1, HEAD_DIM))
    n_chunks = seq // RET_BLOCK
    blk = lambda rows, fn: pl.BlockSpec((rows, HEAD_DIM), fn)
    n_steps = RET_HEADS * batch
    step = lambda h, b: h * batch + b
    mod_rows, d = cc.shape
    late = w_mod.shape[1] - mod_from
    mcols, orows = late // n_steps, w_out.shape[0] // n_steps
    side_in = [
        pl.BlockSpec((mod_rows, d), lambda h, b: (0, 0)),
        pl.BlockSpec((d, mcols), lambda h, b: (0, mod_from // mcols + step(h, b))),
        pl.BlockSpec((1, mcols), lambda h, b: (0, mod_from // mcols + step(h, b))),
        pl.BlockSpec((orows, w_out.shape[1]), lambda h, b: (step(h, b), 0)),
    ]
    side_out = [
        pl.BlockSpec((mod_rows, mcols), lambda h, b: (0, step(h, b))),
        pl.BlockSpec((orows, w_out.shape[1]), lambda h, b: (step(h, b), 0)),
    ]
    side_shape = [jax.ShapeDtypeStruct((mod_rows, late), F32), jax.ShapeDtypeStruct(w_out.shape, BF16)]
    return pl.pallas_call(
        _ret_body,
        grid=(RET_HEADS, batch),
        in_specs=[
            pl.BlockSpec((1, 1, HEAD_DIM), lambda h, b: (h, 0, 0)),
            pl.BlockSpec((1, 1, HEAD_DIM), lambda h, b: (h, 0, 0)),
            blk(1, lambda h, b: (0, h)),
            blk(seq, lambda h, b: (0, 0)),
            blk(seq, lambda h, b: (0, 0)),
            blk(seq, lambda h, b: (b, 5 * hb + h)),
            blk(seq, lambda h, b: (b, 2 * hb + h)),
            blk(seq, lambda h, b: (b, 3 * hb + h)),
            blk(seq, lambda h, b: (b, 6 * hb + h)),
            blk(ctx_len, lambda h, b: (b, 2 * hb + h)),
            blk(ctx_len, lambda h, b: (b, 3 * hb + h)),
        ] + side_in,
        out_specs=[blk(seq, lambda h, b: (b, h))] + side_out,
        out_shape=[jax.ShapeDtypeStruct((batch * seq, RET_WIDTH), BF16)] + side_shape,
        scratch_shapes=[
            pltpu.VMEM((seq, HEAD_DIM), F32),
            pltpu.VMEM((seq, HEAD_DIM), F32),
            pltpu.VMEM((n_chunks, HEAD_DIM, HEAD_DIM), BF16),
            pltpu.VMEM((n_chunks, HEAD_DIM, HEAD_DIM), BF16),
        ],
        compiler_params=_cparams(2),
        name="retention",
    )(dec_f, dec_b, gn_w.reshape(1, RET_WIDTH), cos, sin, proj, proj, proj, proj, cproj, cproj,
      cc, w_mod, b_mod.reshape(1, -1), w_out)


def _split_bf16(x):
    hi = x.astype(BF16)
    lo = (x - hi.astype(F32)).astype(BF16)
    return hi, lo


def _cast_body(x_ref, o_ref):
    o_ref[...] = x_ref[...].astype(o_ref.dtype)


def _cast_bf16(w, tm=512):
    rows, cols = w.shape
    return pl.pallas_call(
        _cast_body,
        grid=(rows // tm,),
        in_specs=[pl.BlockSpec((tm, cols), lambda i: (i, 0))],
        out_specs=pl.BlockSpec((tm, cols), lambda i: (i, 0)),
        out_shape=jax.ShapeDtypeStruct((rows, cols), BF16),
        compiler_params=_cparams(1),
        name="cast_bf16",
    )(w)


OUTPROJ_ROW_SPLITS = 2


def _outproj_body(na_ref, ret_ref, w_ref, x_ref, gate_ref, nw_ref, sh_ref, sc_ref, wr_ref,
                  h1_ref, u2_ref, lg_ref):
    half = na_ref.shape[1]
    n_e = wr_ref.shape[0]
    tm = x_ref.shape[0]
    wh, wl = _split_bf16(wr_ref[...])
    w_router = jnp.concatenate([wh, wl], axis=0)
    for r in range(OUTPROJ_ROW_SPLITS):
        rows = pl.ds(r * (tm // OUTPROJ_ROW_SPLITS), tm // OUTPROJ_ROW_SPLITS)
        mix = _dot(na_ref[rows, :], w_ref[:half, :]) + _dot(ret_ref[rows, :], w_ref[half:, :])
        h1 = x_ref[rows, :] + gate_ref[0] * mix
        h1_ref[rows, :] = h1
        u2 = _rms_mod(h1, nw_ref[...], sh_ref[0], sc_ref[0])
        u2_ref[rows, :] = u2
        both = _dot_nt(w_router, u2.astype(BF16))
        lg_ref[:, rows] = both[:n_e] + both[n_e:]


def _out_projection(na, ret, w_out_bf16, x2d, mod3, norm_w, w_router_t, seq, tm=512):
    rows, d = x2d.shape
    half = na.shape[1]
    per_b = seq // tm
    return pl.pallas_call(
        _outproj_body,
        grid=(rows // tm,),
        in_specs=[
            pl.BlockSpec((tm, half), lambda i: (i, 0)),
            pl.BlockSpec((tm, half), lambda i: (i, 0)),
            pl.BlockSpec((2 * half, d), lambda i: (0, 0)),
            pl.BlockSpec((tm, d), lambda i: (i, 0)),
            pl.BlockSpec((1, 1, d), lambda i: (i // per_b, 0, MOD_LATE_GATE_MIX)),
            pl.BlockSpec((1, d), lambda i: (0, 0)),
            pl.BlockSpec((1, 1, d), lambda i: (i // per_b, 0, MOD_LATE_SHIFT_FFN)),
            pl.BlockSpec((1, 1, d), lambda i: (i // per_b, 0, MOD_LATE_SCALE_FFN)),
            pl.BlockSpec((N_EXPERTS, d), lambda i: (0, 0)),
        ],
        out_specs=[
            pl.BlockSpec((tm, d), lambda i: (i, 0)),
            pl.BlockSpec((tm, d), lambda i: (i, 0)),
            pl.BlockSpec((N_EXPERTS, tm), lambda i: (0, i)),
        ],
        out_shape=[
            jax.ShapeDtypeStruct((rows, d), F32),
            jax.ShapeDtypeStruct((rows, d), F32),
            jax.ShapeDtypeStruct((N_EXPERTS, rows), F32),
        ],
        compiler_params=_cparams(1),
        name="out_projection",
    )(na, ret, w_out_bf16, x2d, mod3, norm_w.reshape(1, d), mod3, mod3, w_router_t)


LANES = 128
SUBLANES = 8


def _prefix_incl_lanes(x, tri):
    r, l = x.shape
    nb = l // LANES
    xs = jnp.concatenate([x[:, t * LANES:(t + 1) * LANES] for t in range(nb)], axis=0).astype(BF16)
    p = _dot(xs, tri)
    outs, run = [], jnp.zeros((r, 1), F32)
    for t in range(nb):
        blk = p[t * r:(t + 1) * r] + run
        outs.append(blk)
        run = blk[:, LANES - 1:LANES]
    return jnp.concatenate(outs, axis=1)


NOT_ROUTED = -(1 << 20)


def _route_body(lg_ref, gidx_ref, gate_ref, slot_ref, before_ref, *, cap):
    b = pl.program_id(0)
    n_e, seq = lg_ref.shape
    kf = float(cap)
    lg = lg_ref[...]
    ex = jnp.exp(lg - jnp.max(lg, axis=0, keepdims=True))
    aff = ex / jnp.sum(ex, axis=0, keepdims=True)

    def cond(c):
        return (c[0] < 4096) & (c[5] > 0.5)

    def step(c):
        it, lo, hi, thr, done, _ = c
        mid = 0.5 * (lo + hi)
        above = jnp.sum((aff > mid).astype(F32), axis=1, keepdims=True)
        hit = above == kf
        stuck = (mid <= lo) | (mid >= hi)
        active = done < 0.5
        thr = jnp.where(active & hit, mid, jnp.where(active & stuck, hi, thr))
        go = active & ~(hit | stuck)
        ge = above >= kf
        lo = jnp.where(go & ge, mid, lo)
        hi = jnp.where(go & ~ge, mid, hi)
        done = jnp.where(active & (hit | stuck), 1.0, done)
        return it + 1, lo, hi, thr, done, jnp.sum(1.0 - done)

    col = lambda v: jnp.full((n_e, 1), v, F32)
    init = (jnp.int32(0), col(-1.0), col(2.0), col(0.0), col(0.0), jnp.float32(n_e))
    thr = lax.while_loop(cond, step, init)[3]

    ii = lax.broadcasted_iota(jnp.int32, (LANES, LANES), 0)
    jj = lax.broadcasted_iota(jnp.int32, (LANES, LANES), 1)
    tri = (ii <= jj).astype(BF16)
    gt = aff > thr
    eq = (aff == thr).astype(F32)
    need = kf - jnp.sum(gt.astype(F32), axis=1, keepdims=True)
    eq_before = _prefix_incl_lanes(eq, tri) - eq
    mask = jnp.where(gt | ((eq > 0.5) & (eq_before < need)), 1.0, 0.0)

    slot = _prefix_incl_lanes(mask, tri) - mask
    before_ref[0] = slot.astype(jnp.int32)
    slot_ref[0] = jnp.where(mask > 0.5, slot, float(NOT_ROUTED)).astype(jnp.int32)

    tok = lax.broadcasted_iota(jnp.int32, (1, seq), 1).astype(F32)
    tok_hi = jnp.floor(tok * (1.0 / 64))
    tok_lo = tok - 64.0 * tok_hi
    n_hi = ROUTE_SLOT_HI
    n_lo = cap // n_hi
    hi_iota = lax.broadcasted_iota(jnp.int32, (n_hi, seq), 0).astype(F32)
    lo_iota = lax.broadcasted_iota(jnp.int32, (n_lo, seq), 0).astype(F32)
    for e in range(n_e):
        se = slot[e:e + 1]
        s_hi = jnp.floor(se * (1.0 / n_lo))
        s_lo = se - n_lo * s_hi
        in_hi = jnp.where((s_hi == hi_iota) & (mask[e:e + 1] > 0.5), 1.0, 0.0)
        in_lo = (s_lo == lo_iota).astype(BF16)
        a = aff[e:e + 1]
        a_hi = a.astype(BF16).astype(F32)
        a_mid = (a - a_hi).astype(BF16).astype(F32)
        a_lo = a - a_hi - a_mid
        vals = (tok_hi, tok_lo, a_hi, a_mid, a_lo)
        lhs = jnp.concatenate([in_hi * v for v in vals], axis=0).astype(BF16)
        got = _dot_nt(lhs, in_lo)
        part = lambda r: got[r * n_hi:(r + 1) * n_hi]
        gidx_ref[0, e] = (part(0) * 64.0 + part(1)).astype(jnp.int32) + b * seq
        gate_ref[0, e] = part(2) + part(3) + part(4)


ROUTE_SLOT_HI = 16


def _routing(logits_t, batch, seq):
    n_e = logits_t.shape[0]
    cap = CAPACITY_FACTOR * seq // n_e
    n_hi, n_lo = ROUTE_SLOT_HI, cap // ROUTE_SLOT_HI
    bec = pl.BlockSpec((1, n_e, n_hi, n_lo), lambda b: (b, 0, 0, 0))
    bel = pl.BlockSpec((1, n_e, seq), lambda b: (b, 0, 0))
    gidx, gates, slot, before = pl.pallas_call(
        functools.partial(_route_body, cap=cap),
        grid=(batch,),
        in_specs=[pl.BlockSpec((n_e, seq), lambda b: (0, b))],
        out_specs=[bec, bec, bel, bel],
        out_shape=[
            jax.ShapeDtypeStruct((batch, n_e, n_hi, n_lo), jnp.int32),
            jax.ShapeDtypeStruct((batch, n_e, n_hi, n_lo), F32),
            jax.ShapeDtypeStruct((batch, n_e, seq), jnp.int32),
            jax.ShapeDtypeStruct((batch, n_e, seq), jnp.int32),
        ],
        compiler_params=_cparams(1),
        name="routing",
    )(logits_t)
    flat = lambda a: a.reshape(batch, n_e, cap)
    return flat(gidx), flat(gates), slot, before


def _hbm_row(ref, r):
    return ref.at[lax.shift_right_logical(r, 3), pl.ds(r & (SUBLANES - 1), 1)]


MOE_ROW_SPLITS = 2


def _moe_body(gidx_ref, gnext_ref, gate_ref, u2_hbm, wg_ref, wu_ref, wd_ref, y_ref,
              stage_ref, xe_ref, acc_ref, gsem, *, n_f):
    e = pl.program_id(0)
    f = pl.program_id(1)
    n_e = pl.num_programs(0)
    m, d = y_ref.shape
    sub = SUBLANES
    n_tiles = m // sub
    assert n_f >= 2
    per_step = -(-n_tiles // (n_f - 1))
    last_count = n_tiles - per_step * (n_f - 2)

    def gather_copy(idx_ref, i, k):
        return pltpu.make_async_copy(_hbm_row(u2_hbm, idx_ref[0, i * sub + k]), stage_ref.at[i, pl.ds(k, 1)], gsem)

    def for_rows(fn, tiles_per_trip=4):
        def body(t, c):
            for kk in range(tiles_per_trip * sub):
                fn(t * tiles_per_trip + kk // sub, kk % sub)
            return c
        lax.fori_loop(0, n_tiles // tiles_per_trip, body, 0)

    wait_gather = lambda: for_rows(lambda i, k: gather_copy(gidx_ref, i, k).wait())

    def ffn_step(first, gather_tiles):
        i0 = (f - 1) * per_step
        for kk in range(gather_tiles * sub):
            gather_copy(gnext_ref, i0 + kk // sub, kk % sub).start()
        wg, wu, wd = wg_ref[...].astype(BF16), wu_ref[...].astype(BF16), wd_ref[...].astype(BF16)
        part = m // MOE_ROW_SPLITS
        for r in range(MOE_ROW_SPLITS):
            rows = pl.ds(r * part, part)
            if first:
                x = stage_ref[pl.ds(r * (part // sub), part // sub)].reshape(part, d).astype(BF16)
                xe_ref[rows, :] = x
            else:
                x = xe_ref[rows, :]
            hid = _silu(_dot(x, wg)) * _dot(x, wu)
            out = _dot(hid.astype(BF16), wd)
            acc_ref[rows, :] = out if first else acc_ref[rows, :] + out

    @pl.when(f == 0)
    def _():
        @pl.when(e == 0)
        def _():
            for_rows(lambda i, k: gather_copy(gidx_ref, i, k).start())

        wait_gather()
        ffn_step(True, 0)

    if n_f > 2:
        pl.when((f > 0) & (f < n_f - 1))(lambda: ffn_step(False, per_step))

    @pl.when(f == n_f - 1)
    def _():
        ffn_step(False, last_count)

        @pl.when(f > 0)
        def _():
            g = jnp.transpose(jnp.broadcast_to(gate_ref[...], (LANES, m)))
            for k in range(d // LANES):
                cols = slice(k * LANES, (k + 1) * LANES)
                y_ref[:, cols] = (acc_ref[:, cols] * g).astype(y_ref.dtype)

        @pl.when(e == n_e - 1)
        def _():
            wait_gather()


MOE_VMEM_LIMIT_V7X = 60 * MIB


def _moe_ffn(gidx, gates, u2, w_gate, w_up, w_down, tf=512):
    n_e, _, m = gidx.shape
    rows, d = u2.shape
    ff = w_gate.shape[2]
    n_f = ff // tf
    sub = SUBLANES
    smem = lambda fn: pl.BlockSpec((None, 1, m), fn, memory_space=pltpu.SMEM)
    return pl.pallas_call(
        functools.partial(_moe_body, n_f=n_f),
        grid=(n_e, n_f),
        in_specs=[
            smem(lambda e, f: (e, 0, 0)),
            smem(lambda e, f: (jnp.minimum(e + 1, n_e - 1), 0, 0)),
            pl.BlockSpec((None, 1, m), lambda e, f: (e, 0, 0)),
            pl.BlockSpec(memory_space=pl.ANY),
            pl.BlockSpec((None, d, tf), lambda e, f: (e, 0, f)),
            pl.BlockSpec((None, d, tf), lambda e, f: (e, 0, f)),
            pl.BlockSpec((None, tf, d), lambda e, f: (e, f, 0)),
        ],
        out_specs=pl.BlockSpec((m, d), lambda e, f: (e, 0)),
        out_shape=jax.ShapeDtypeStruct((n_e * m, d), BF16),
        scratch_shapes=[
            pltpu.VMEM((m // sub, sub, d), F32),
            pltpu.VMEM((m, d), BF16),
            pltpu.VMEM((m, d), F32),
            pltpu.SemaphoreType.DMA,
        ],
        compiler_params=_cparams(2, MOE_VMEM_LIMIT_V7X),
        name="moe_ffn",
    )(gidx, gidx, gates, u2.reshape(rows // sub, sub, d), w_gate, w_up, w_down)


COMBINE_TOKENS = 256
COMBINE_PIECE = 64
BF16_ROWS = 16


def _combine_body(start_ref, npiece_ref, h1_ref, y_hbm, slot_ref, gate_ref, nw_ref, o_ref,
                  acc_ref, buf_ref, xbuf_ref, sem, xsem, *, m, cap, per_b):
    i = pl.program_id(0)
    n_tiles = pl.num_programs(0)
    n_e = slot_ref.shape[1]
    t, p = COMBINE_TOKENS, COMBINE_PIECE
    total = n_e * m
    cur = i % 2
    b = i // per_b

    def piece_copy(tile, e, dst):
        st = pl.multiple_of(start_ref[tile * n_e + e], BF16_ROWS)
        return pltpu.make_async_copy(y_hbm.at[pl.ds(st, p)], buf_ref.at[dst, pl.ds(e * p, p)], sem.at[dst])

    @pl.when(i == 0)
    def _():
        for e in range(n_e):
            piece_copy(0, e, 0).start()

    @pl.when(i + 1 < n_tiles)
    def _():
        for e in range(n_e):
            piece_copy(i + 1, e, 1 - cur).start()

    for e in range(n_e):
        piece_copy(i, e, cur).wait()

    slots = slot_ref[0]
    riota = lax.broadcasted_iota(jnp.int32, (p, t), 0)

    def local_row(e, st):
        return slots[e:e + 1, :] + (e * m + b * cap - st)

    sel = jnp.concatenate(
        [(local_row(e, start_ref[i * n_e + e]) == riota).astype(BF16) for e in range(n_e)], axis=0)
    acc_ref[...] = _dot_tn(sel, buf_ref[cur])

    for e in range(n_e):
        st0 = start_ref[i * n_e + e]

        def extra(k, carry, e=e, st0=st0):
            want = st0 + k * p
            st = pl.multiple_of(jnp.minimum(want, total - p), BF16_ROWS)
            cp = pltpu.make_async_copy(y_hbm.at[pl.ds(st, p)], xbuf_ref, xsem)
            cp.start()
            cp.wait()
            blk = ((local_row(e, st) == riota) & (riota >= want - st)).astype(BF16)
            acc_ref[...] += _dot_tn(blk, xbuf_ref[...])
            return carry

        lax.fori_loop(1, npiece_ref[i * n_e + e], extra, 0)

    h2 = h1_ref[...] + gate_ref[0] * acc_ref[...]
    ms = jnp.mean(h2 * h2, axis=-1, keepdims=True)
    o_ref[...] = h2 * lax.rsqrt(ms + EPS) * nw_ref[...]


def _combine_pieces(before, cap):
    batch, n_e, seq = before.shape
    t, p = COMBINE_TOKENS, COMBINE_PIECE
    m = batch * cap
    total = n_e * m
    bounds = jnp.concatenate([before[:, :, ::t], jnp.full((batch, n_e, 1), cap, jnp.int32)], axis=2)
    base = (jnp.arange(n_e, dtype=jnp.int32) * m)[None, :, None] + (jnp.arange(batch, dtype=jnp.int32) * cap)[:, None, None]
    first = base + bounds[:, :, :-1]
    end = base + bounds[:, :, 1:]
    start = jnp.minimum(first // BF16_ROWS * BF16_ROWS, total - p)
    npiece = jnp.maximum((end - start + p - 1) // p, 1)
    by_tile = lambda a: a.transpose(0, 2, 1).reshape(-1)
    return by_tile(start), by_tile(npiece)


def _combine(h1, y, slot, before, mod3, final_w, seq, cap):
    rows, d = h1.shape
    batch, n_e, _ = slot.shape
    t, p = COMBINE_TOKENS, COMBINE_PIECE
    per_b = seq // t
    start, npiece = _combine_pieces(before, cap)
    grid_spec = pltpu.PrefetchScalarGridSpec(
        num_scalar_prefetch=2,
        grid=(rows // t,),
        in_specs=[
            pl.BlockSpec((t, d), lambda i, st, npc: (i, 0)),
            pl.BlockSpec(memory_space=pl.ANY),
            pl.BlockSpec((1, n_e, t), lambda i, st, npc: (i // per_b, 0, i % per_b)),
            pl.BlockSpec((1, 1, d), lambda i, st, npc: (i // per_b, 0, MOD_LATE_GATE_FFN)),
            pl.BlockSpec((1, d), lambda i, st, npc: (0, 0)),
        ],
        out_specs=pl.BlockSpec((t, d), lambda i, st, npc: (i, 0)),
        scratch_shapes=[
            pltpu.VMEM((t, d), F32),
            pltpu.VMEM((2, n_e * p, d), y.dtype),
            pltpu.VMEM((p, d), y.dtype),
            pltpu.SemaphoreType.DMA((2,)),
            pltpu.SemaphoreType.DMA,
        ],
    )
    return pl.pallas_call(
        functools.partial(_combine_body, m=batch * cap, cap=cap, per_b=per_b),
        grid_spec=grid_spec,
        out_shape=jax.ShapeDtypeStruct((rows, d), F32),
        compiler_params=_cparams(1),
        name="combine",
    )(start, npiece, h1, y, slot, mod3, final_w.reshape(1, d))


def kernel(x, c, ctx, c_ctx, w_mod, b_mod, norm_mix_w, norm_ffn_w, w_in, na_rpb, ret_decay_fwd,
           ret_decay_bwd, ret_gn_w, w_out, w_router, w_gate, w_up, w_down, final_norm_w):
    batch, seq, d = x.shape
    ctx_len = ctx.shape[1]
    assert w_mod.shape[0] == 1, "one trunk layer"
    assert seq % (NA_QROWS * GRID_W) == 0 and seq // GRID_W >= 3 * NA_QROWS
    n_e = w_router.shape[2]
    cap = CAPACITY_FACTOR * seq // n_e

    mod_rows = 8
    cc = jnp.concatenate([c, c_ctx[None], jnp.zeros((mod_rows - batch - 1, d), c.dtype)], axis=0)
    mod3 = _modulation(cc, w_mod[0], b_mod[0], MOD_EARLY * d).reshape(mod_rows, 1, MOD_EARLY * d)

    x2d = x.reshape(batch * seq, d)
    tm = 1024
    cproj, w_in_bf16 = _ctx_projection(ctx.reshape(batch * ctx_len, d), norm_mix_w[0], mod3, batch,
                                       w_in[0], KV_COLS)
    proj = _in_projection(x2d, norm_mix_w[0], mod3, lambda i: i // (seq // tm), w_in_bf16,
                          w_in.shape[2], tm, w_in.shape[2] // 4)

    na = _neighbourhood_attention(proj, cproj, na_rpb[0], batch, seq, ctx_len)
    ret, mod_late, w_out_bf16 = _retention(proj, cproj, ret_decay_fwd[0], ret_decay_bwd[0], ret_gn_w[0],
                                           batch, seq, ctx_len, cc, w_mod[0], b_mod[0], MOD_EARLY * d, w_out[0])
    mod_late3 = mod_late.reshape(mod_rows, 1, (N_MOD - MOD_EARLY) * d)

    h1, u2, logits_t = _out_projection(na, ret, w_out_bf16, x2d, mod_late3, norm_ffn_w[0], w_router[0].T, seq)

    gidx, gates, slot, before = _routing(logits_t, batch, seq)
    per_expert = lambda a: a.transpose(1, 0, 2).reshape(n_e, 1, batch * cap)
    y = _moe_ffn(per_expert(gidx), per_expert(gates), u2, w_gate[0], w_up[0], w_down[0])
    out = _combine(h1, y, slot, before, mod_late3, final_norm_w, seq, cap)
    return out.reshape(batch, seq, d)
```

```python
import functools

import jax
import jax.numpy as jnp
from jax import lax
from jax.experimental import pallas as pl
from jax.experimental.pallas import tpu as pltpu

GRID_W = 64
HEAD_DIM = 128
NA_HEADS = 8
RET_HEADS = 8
NA_WIDTH = NA_HEADS * HEAD_DIM
RET_WIDTH = RET_HEADS * HEAD_DIM
WIN_ROWS = 8
WIN_COLS = 16
RET_BLOCK = 256
ROPE_BASE = 10000.0
N_EXPERTS = 16
CAPACITY_FACTOR = 2
N_MOD = 6
MOD_EARLY = 2
MOD_LATE_GATE_MIX, MOD_LATE_SHIFT_FFN, MOD_LATE_SCALE_FFN, MOD_LATE_GATE_FFN = 0, 1, 2, 3
EPS = 1e-6
NEG_INF = -1e30
KV_COLS = 2 * NA_WIDTH + 2 * RET_WIDTH

F32 = jnp.float32
BF16 = jnp.bfloat16
MIB = 1024 * 1024
VMEM_LIMIT_V7X = 56 * MIB


def _cparams(n_axes):
    return pltpu.CompilerParams(
        dimension_semantics=("arbitrary",) * n_axes, vmem_limit_bytes=VMEM_LIMIT_V7X)


def _silu(x):
    return x * jax.nn.sigmoid(x)


def _dot(a, b):
    return jnp.dot(a, b, preferred_element_type=F32)


def _dot_nt(a, b):
    return lax.dot_general(a, b, (((1,), (1,)), ((), ())), preferred_element_type=F32)


def _dot_tn(a, b):
    return lax.dot_general(a, b, (((0,), (0,)), ((), ())), preferred_element_type=F32)


def _mod_body(c_ref, w_ref, b_ref, o_ref):
    a = _silu(c_ref[...]).astype(BF16)
    o_ref[:, 0, :] = _dot(a, w_ref[...].astype(BF16)) + b_ref[...]


def _modulation(cc, w_mod, b_mod, n_cols, tn=1024):
    rows, d = cc.shape
    n = n_cols
    return pl.pallas_call(
        _mod_body,
        grid=(n // tn,),
        in_specs=[
            pl.BlockSpec((rows, d), lambda j: (0, 0)),
            pl.BlockSpec((d, tn), lambda j: (0, j)),
            pl.BlockSpec((1, tn), lambda j: (0, j)),
        ],
        out_specs=pl.BlockSpec((rows, 1, tn), lambda j: (0, 0, j)),
        out_shape=jax.ShapeDtypeStruct((rows, 1, n), F32),
        compiler_params=_cparams(1),
        name="modulation",
    )(cc, w_mod, b_mod.reshape(1, -1))


def _rms_mod(x, nw, shift, scale):
    ms = jnp.mean(x * x, axis=-1, keepdims=True)
    y = x * lax.rsqrt(ms + EPS) * nw
    return y * (1.0 + scale) + shift


INPROJ_NORM_SPLITS = 4


def _inproj_body(x_ref, nw_ref, sh_ref, sc_ref, w_ref, o_ref, u_ref):
    j = pl.program_id(1)
    tm = x_ref.shape[0]

    @pl.when(j == 0)
    def _():
        part = tm // INPROJ_NORM_SPLITS
        for r in range(INPROJ_NORM_SPLITS):
            rows = pl.ds(r * part, part)
            u = _rms_mod(x_ref[rows, :], nw_ref[...], sh_ref[0], sc_ref[0]).astype(BF16)
            u_ref[rows, :] = u
            o_ref[rows, :] = _dot(u, w_ref[...]).astype(o_ref.dtype)

    @pl.when(j > 0)
    def _():
        o_ref[...] = _dot(u_ref[...], w_ref[...]).astype(o_ref.dtype)


def _ctx_proj_body(x_ref, nw_ref, sh_ref, sc_ref, w_ref, o_ref, wb_ref, u_ref, *, kv_tiles):
    j = pl.program_id(0)

    @pl.when(j == 0)
    def _():
        u_ref[...] = _rms_mod(x_ref[...], nw_ref[...], sh_ref[0], sc_ref[0]).astype(BF16)

    wb_ref[...] = w_ref[...].astype(BF16)

    @pl.when(j < kv_tiles)
    def _():
        o_ref[...] = _dot(u_ref[...], wb_ref[...]).astype(o_ref.dtype)


def _ctx_projection(x2d, norm_w, mod3, mod_row, w_in, kv_cols, tn=1024):
    rows, d = x2d.shape
    n = w_in.shape[1]
    kv_tiles = kv_cols // tn
    return pl.pallas_call(
        functools.partial(_ctx_proj_body, kv_tiles=kv_tiles),
        grid=(n // tn,),
        in_specs=[
            pl.BlockSpec((rows, d), lambda j: (0, 0)),
            pl.BlockSpec((1, d), lambda j: (0, 0)),
            pl.BlockSpec((1, 1, d), lambda j: (mod_row, 0, 0)),
            pl.BlockSpec((1, 1, d), lambda j: (mod_row, 0, 1)),
            pl.BlockSpec((d, tn), lambda j: (0, j)),
        ],
        out_specs=[
            pl.BlockSpec((rows, tn), lambda j: (0, jnp.minimum(j, kv_tiles - 1))),
            pl.BlockSpec((d, tn), lambda j: (0, j)),
        ],
        out_shape=[jax.ShapeDtypeStruct((rows, kv_cols), BF16), jax.ShapeDtypeStruct((d, n), BF16)],
        scratch_shapes=[pltpu.VMEM((rows, d), BF16)],
        compiler_params=_cparams(1),
        name="ctx_projection",
    )(x2d, norm_w.reshape(1, d), mod3, mod3, w_in)


def _in_projection(x2d, norm_w, mod3, mod_row_fn, w_in, n_cols, tm, tn):
    rows, d = x2d.shape
    return pl.pallas_call(
        _inproj_body,
        grid=(rows // tm, n_cols // tn),
        in_specs=[
            pl.BlockSpec((tm, d), lambda i, j: (i, 0)),
            pl.BlockSpec((1, d), lambda i, j: (0, 0)),
            pl.BlockSpec((1, 1, d), lambda i, j: (mod_row_fn(i), 0, 0)),
            pl.BlockSpec((1, 1, d), lambda i, j: (mod_row_fn(i), 0, 1)),
            pl.BlockSpec((d, tn), lambda i, j: (0, j)),
        ],
        out_specs=pl.BlockSpec((tm, tn), lambda i, j: (i, j)),
        out_shape=jax.ShapeDtypeStruct((rows, n_cols), BF16),
        scratch_shapes=[pltpu.VMEM((tm, d), BF16)],
        compiler_params=_cparams(2),
        name="in_projection",
    )(x2d, norm_w.reshape(1, d), mod3, mod3, w_in)


NA_QROWS = 4
NA_KROWS = NA_QROWS + WIN_ROWS


def _na_row_offset(tile_kind, i, w, rows):
    half = WIN_ROWS // 2
    if tile_kind == 0:
        r, key = i, w
    elif tile_kind == 1:
        r, key = NA_QROWS + i, NA_QROWS - half + w
    else:
        r, key = rows - NA_QROWS + i, rows - NA_KROWS + w
    start = min(max(r - half, 0), rows - WIN_ROWS)
    if not (start <= key < start + WIN_ROWS):
        return None
    return key - r + (WIN_ROWS - 1)


def _na_build_bias(rpb_ref, bias_ref, h, rows):
    w = GRID_W
    cq = lax.broadcasted_iota(jnp.int32, (w, 2 * w), 0)
    ck = lax.broadcasted_iota(jnp.int32, (w, 2 * w), 1) % w
    col_start = jnp.clip(cq - WIN_COLS // 2, 0, w - WIN_COLS)
    col_ok = (ck >= col_start) & (ck < col_start + WIN_COLS)
    col_off = jnp.clip(ck - cq, -(WIN_COLS - 1), WIN_COLS - 1) + (WIN_COLS - 1)
    neg = jnp.full((w, 2 * w), NEG_INF, F32)
    n_ro, n_co = 2 * WIN_ROWS - 1, 2 * WIN_COLS - 1
    tabs = []
    for ro in range(n_ro):
        t = jnp.zeros((w, 2 * w), F32)
        for j in range(n_co):
            t = jnp.where(col_off == j, rpb_ref[h, ro * n_co + j], t)
        tabs.append(jnp.where(col_ok, t, neg))
    left = lax.broadcasted_iota(jnp.int32, (w, 2 * w), 1) < w
    for kind in range(3):
        for i in range(NA_QROWS):
            for wp in range(NA_KROWS // 2):
                ra = _na_row_offset(kind, i, 2 * wp, rows)
                rb = _na_row_offset(kind, i, 2 * wp + 1, rows)
                ta = neg if ra is None else tabs[ra]
                tb = neg if rb is None else tabs[rb]
                blk = ta if ra == rb else jnp.where(left, ta, tb)
                bias_ref[kind, i * w:(i + 1) * w, wp * 2 * w:(wp + 1) * 2 * w] = blk


def _na_body(rpb_ref, q_ref, k_ref, v_ref, ck_ref, cv_ref, o_ref, bias_ref, sa_ref, sb_ref, *, rows):
    h = pl.program_id(0)
    w = GRID_W
    tq, tk = NA_QROWS * w, NA_KROWS * w
    n_tiles = rows // NA_QROWS
    scale = HEAD_DIM ** -0.5

    @pl.when(pl.program_id(1) == 0)
    def _():
        _na_build_bias(rpb_ref, bias_ref, h, rows)

    def offsets(t):
        krow0 = jnp.clip(t * NA_QROWS - WIN_ROWS // 2, 0, rows - NA_KROWS)
        return pl.multiple_of(t * tq, tq), pl.multiple_of(krow0 * w, 4 * w)

    def scores(t, s_ref):
        kind = jnp.where(t == 0, 0, jnp.where(t == n_tiles - 1, 2, 1))
        q0, k0 = offsets(t)
        q = q_ref[pl.ds(q0, tq), :]
        s_ref[:, :tk] = _dot_nt(q, k_ref[pl.ds(k0, tk), :]) * scale + bias_ref[kind]
        s_ref[:, tk:] = _dot_nt(q, ck_ref[...]) * scale

    def attend(t, s_ref):
        q0, k0 = offsets(t)
        s = s_ref[...]
        p = jnp.exp(s - jnp.max(s, axis=-1, keepdims=True))
        l = jnp.sum(p, axis=-1, keepdims=True)
        pb = p.astype(BF16)
        o = _dot(pb[:, :tk], v_ref[pl.ds(k0, tk), :]) + _dot(pb[:, tk:], cv_ref[...])
        o_ref[pl.ds(q0, tq), :] = (o / l).astype(o_ref.dtype)

    scores(jnp.int32(0), sa_ref)

    def pair(i, carry):
        t = 2 * i
        scores(t + 1, sb_ref)
        attend(t, sa_ref)
        scores(t + 2, sa_ref)
        attend(t + 1, sb_ref)
        return carry

    lax.fori_loop(0, n_tiles // 2 - 1, pair, 0)
    scores(jnp.int32(n_tiles - 1), sb_ref)
    attend(jnp.int32(n_tiles - 2), sa_ref)
    attend(jnp.int32(n_tiles - 1), sb_ref)


def _neighbourhood_attention(proj, cproj, rpb, batch, seq, ctx_len):
    rows = seq // GRID_W
    hb = NA_WIDTH // HEAD_DIM
    n_rpb = (2 * WIN_ROWS - 1) * (2 * WIN_COLS - 1)
    tq, tk = NA_QROWS * GRID_W, NA_KROWS * GRID_W
    grid_spec = pltpu.PrefetchScalarGridSpec(
        num_scalar_prefetch=0,
        grid=(NA_HEADS, batch),
        in_specs=[
            pl.BlockSpec(memory_space=pltpu.SMEM),
            pl.BlockSpec((seq, HEAD_DIM), lambda h, b: (b, 4 * hb + h)),
            pl.BlockSpec((seq, HEAD_DIM), lambda h, b: (b, h)),
            pl.BlockSpec((seq, HEAD_DIM), lambda h, b: (b, hb + h)),
            pl.BlockSpec((ctx_len, HEAD_DIM), lambda h, b: (b, h)),
            pl.BlockSpec((ctx_len, HEAD_DIM), lambda h, b: (b, hb + h)),
        ],
        out_specs=pl.BlockSpec((seq, HEAD_DIM), lambda h, b: (b, h)),
        scratch_shapes=[
            pltpu.VMEM((3, tq, tk), F32),
            pltpu.VMEM((tq, tk + ctx_len), F32),
            pltpu.VMEM((tq, tk + ctx_len), F32),
        ],
    )
    return pl.pallas_call(
        functools.partial(_na_body, rows=rows),
        grid_spec=grid_spec,
        out_shape=jax.ShapeDtypeStruct((batch * seq, NA_WIDTH), BF16),
        compiler_params=_cparams(2),
        name="neighbourhood_attention",
    )(rpb.reshape(NA_HEADS, n_rpb), proj, proj, proj, cproj, cproj)


def _log_sigmoid(x):
    return -(jnp.maximum(-x, 0.0) + jnp.log1p(jnp.exp(-jnp.abs(x))))


def _rope_partner_matrix():
    quarter = HEAD_DIM // 4
    src = lax.broadcasted_iota(jnp.int32, (HEAD_DIM, HEAD_DIM), 0)
    dst = lax.broadcasted_iota(jnp.int32, (HEAD_DIM, HEAD_DIM), 1)
    want = jnp.where(dst % (2 * quarter) < quarter, dst + quarter, dst - quarter)
    return (src == want).astype(BF16)


def _rope(x_bf16, partner_matrix, cos, sin_signed):
    return x_bf16.astype(F32) * cos + _dot(x_bf16, partner_matrix) * sin_signed


def _ret_body(df_ref, db_ref, gn_ref, cos_ref, sin_ref, q_ref, k_ref, v_ref, g_ref, ck_ref, cv_ref,
              cc_ref, wm_ref, bm_ref, wo_ref,
              o_ref, modl_ref, wob_ref, qr_ref, kr_ref, sf_ref, sb_ref):
    modl_ref[:, 0, :] = _dot(_silu(cc_ref[...]).astype(BF16), wm_ref[...].astype(BF16)) + bm_ref[...]
    wob_ref[...] = wo_ref[...].astype(BF16)

    c, d = RET_BLOCK, HEAD_DIM
    seq = q_ref.shape[0]
    n_chunks = seq // c
    ctx_len = ck_ref.shape[0]
    scale = HEAD_DIM ** -0.5
    lgf_row = _log_sigmoid(df_ref[0])
    lgb_row = _log_sigmoid(db_ref[0])
    lgf = jnp.broadcast_to(lgf_row, (c, d))
    lgb = jnp.broadcast_to(lgb_row, (c, d))
    pos = lax.broadcasted_iota(jnp.int32, (c, d), 0).astype(F32)
    kdf = jnp.exp(lgf * (c - 1.0 - pos))
    kdb = jnp.exp(lgb * pos)
    qdf = jnp.exp(lgf * (pos + 1.0))
    qdb = jnp.exp(lgb * (c - pos))
    cdf = jnp.exp(lgf_row * float(c))
    cdb = jnp.exp(lgb_row * float(c))
    diff = (lax.broadcasted_iota(jnp.int32, (c, c), 0) - lax.broadcasted_iota(jnp.int32, (c, c), 1)).astype(F32)
    lgf_cc = jnp.broadcast_to(lgf_row[:, :1], (c, c))
    lgb_cc = jnp.broadcast_to(lgb_row[:, :1], (c, c))
    dmat = (jnp.where(diff >= 0, jnp.exp(lgf_cc * jnp.maximum(diff, 0.0)), 0.0)
            + jnp.where(diff <= 0, jnp.exp(lgb_cc * jnp.maximum(-diff, 0.0)), 0.0))

    cpos = lax.broadcasted_iota(jnp.int32, (ctx_len, d), 0).astype(F32)
    ckf = ck_ref[...].astype(F32) * scale
    cv = cv_ref[...]
    wf = jnp.exp(jnp.broadcast_to(lgf_row, (ctx_len, d)) * (ctx_len - 1.0 - cpos))
    wb = jnp.exp(jnp.broadcast_to(lgb_row, (ctx_len, d)) * cpos)
    s_f = _dot_tn((ckf * wf).astype(BF16), cv)
    s_b = _dot_tn((ckf * wb).astype(BF16), cv)

    rope_rows = 512
    pmat = _rope_partner_matrix()

    def rope_blk(i, carry):
        r0 = pl.multiple_of(i * rope_rows, rope_rows)
        cs = cos_ref[pl.ds(r0, rope_rows), :]
        sn = sin_ref[pl.ds(r0, rope_rows), :]
        qr_ref[pl.ds(r0, rope_rows), :] = _rope(q_ref[pl.ds(r0, rope_rows), :], pmat, cs, sn)
        kr_ref[pl.ds(r0, rope_rows), :] = _rope(k_ref[pl.ds(r0, rope_rows), :], pmat, cs, sn) * scale
        return carry

    lax.fori_loop(0, seq // rope_rows, rope_blk, 0)

    def scan_blk(n, carry):
        s, t = carry
        r0 = pl.multiple_of(n * c, c)
        kvf = _dot_tn((kr_ref[pl.ds(r0, c), :] * kdf).astype(BF16), v_ref[pl.ds(r0, c), :])
        sf_ref[n] = s.astype(BF16)
        m = n_chunks - 1 - n
        m0 = pl.multiple_of(m * c, c)
        kvb = _dot_tn((kr_ref[pl.ds(m0, c), :] * kdb).astype(BF16), v_ref[pl.ds(m0, c), :])
        sb_ref[m] = t.astype(BF16)
        return s * cdf + kvf, t * cdb + kvb

    lax.fori_loop(0, n_chunks, scan_blk, (s_f, s_b), unroll=4)

    gn = gn_ref[...]

    def out_blk(n, carry):
        r0 = pl.multiple_of(n * c, c)
        qc = qr_ref[pl.ds(r0, c), :]
        kc = kr_ref[pl.ds(r0, c), :]
        a = _dot_nt(qc.astype(BF16), kc.astype(BF16))
        o = (_dot((a * dmat).astype(BF16), v_ref[pl.ds(r0, c), :])
             + _dot((qc * qdf).astype(BF16), sf_ref[n])
             + _dot((qc * qdb).astype(BF16), sb_ref[n]))
        mu = jnp.mean(o, axis=-1, keepdims=True)
        var = jnp.mean(jnp.square(o - mu), axis=-1, keepdims=True)
        y = (o - mu) * lax.rsqrt(var + EPS) * gn
        o_ref[pl.ds(r0, c), :] = (y * _silu(g_ref[pl.ds(r0, c), :].astype(F32))).astype(o_ref.dtype)
        return carry

    lax.fori_loop(0, n_chunks, out_blk, 0, unroll=4)


def _rope_tables(seq):
    axis_dim = HEAD_DIM // 2
    inv_freq = ROPE_BASE ** (-jnp.arange(0, axis_dim, 2, dtype=F32) / axis_dim)
    rows = seq // GRID_W
    ang_r = jnp.arange(rows, dtype=F32)[:, None] * inv_freq
    ang_c = jnp.arange(GRID_W, dtype=F32)[:, None] * inv_freq
    by_row = lambda a: jnp.repeat(a, GRID_W, axis=0)
    by_col = lambda a: jnp.tile(a, (rows, 1))
    cr, sr, cc, sc = by_row(jnp.cos(ang_r)), by_row(jnp.sin(ang_r)), by_col(jnp.cos(ang_c)), by_col(jnp.sin(ang_c))
    return jnp.concatenate([cr, cr, cc, cc], axis=-1), jnp.concatenate([-sr, sr, -sc, sc], axis=-1)


def _retention(proj, cproj, decay_f, decay_b, gn_w, batch, seq, ctx_len, cc, w_mod, b_mod, mod_from, w_out):
    hb = RET_WIDTH // HEAD_DIM
    cos, sin = _rope_tables(seq)
    dec_f = jnp.broadcast_to(decay_f.astype(F32)[:, None, None], (RET_HEADS, 1, HEAD_DIM))
    dec_b = jnp.broadcast_to(decay_b.astype(F32)[:, None, None], (RET_HEADS, 1, HEAD_DIM))
    n_chunks = seq // RET_BLOCK
    blk = lambda rows, fn: pl.BlockSpec((rows, HEAD_DIM), fn)
    n_steps = RET_HEADS * batch
    step = lambda h, b: h * batch + b
    mod_rows, d = cc.shape
    late = w_mod.shape[1] - mod_from
    mcols, orows = late // n_steps, w_out.shape[0] // n_steps
    side_in = [
        pl.BlockSpec((mod_rows, d), lambda h, b: (0, 0)),
        pl.BlockSpec((d, mcols), lambda h, b: (0, mod_from // mcols + step(h, b))),
        pl.BlockSpec((1, mcols), lambda h, b: (0, mod_from // mcols + step(h, b))),
        pl.BlockSpec((orows, w_out.shape[1]), lambda h, b: (step(h, b), 0)),
    ]
    side_out = [
        pl.BlockSpec((mod_rows, 1, mcols), lambda h, b: (0, 0, step(h, b))),
        pl.BlockSpec((orows, w_out.shape[1]), lambda h, b: (step(h, b), 0)),
    ]
    side_shape = [jax.ShapeDtypeStruct((mod_rows, 1, late), F32), jax.ShapeDtypeStruct(w_out.shape, BF16)]
    return pl.pallas_call(
        _ret_body,
        grid=(RET_HEADS, batch),
        in_specs=[
            pl.BlockSpec((1, 1, HEAD_DIM), lambda h, b: (h, 0, 0)),
            pl.BlockSpec((1, 1, HEAD_DIM), lambda h, b: (h, 0, 0)),
            blk(1, lambda h, b: (0, h)),
            blk(seq, lambda h, b: (0, 0)),
            blk(seq, lambda h, b: (0, 0)),
            blk(seq, lambda h, b: (b, 5 * hb + h)),
            blk(seq, lambda h, b: (b, 2 * hb + h)),
            blk(seq, lambda h, b: (b, 3 * hb + h)),
            blk(seq, lambda h, b: (b, 6 * hb + h)),
            blk(ctx_len, lambda h, b: (b, 2 * hb + h)),
            blk(ctx_len, lambda h, b: (b, 3 * hb + h)),
        ] + side_in,
        out_specs=[blk(seq, lambda h, b: (b, h))] + side_out,
        out_shape=[jax.ShapeDtypeStruct((batch * seq, RET_WIDTH), BF16)] + side_shape,
        scratch_shapes=[
            pltpu.VMEM((seq, HEAD_DIM), F32),
            pltpu.VMEM((seq, HEAD_DIM), F32),
            pltpu.VMEM((n_chunks, HEAD_DIM, HEAD_DIM), BF16),
            pltpu.VMEM((n_chunks, HEAD_DIM, HEAD_DIM), BF16),
        ],
        compiler_params=_cparams(2),
        name="retention",
    )(dec_f, dec_b, gn_w.reshape(1, RET_WIDTH), cos, sin, proj, proj, proj, proj, cproj, cproj,
      cc, w_mod, b_mod.reshape(1, -1), w_out)


def _split_bf16(x):
    hi = x.astype(BF16)
    lo = (x - hi.astype(F32)).astype(BF16)
    return hi, lo


OUTPROJ_ROW_SPLITS = 2


def _outproj_body(na_ref, ret_ref, w_ref, x_ref, gate_ref, nw_ref, sh_ref, sc_ref, wr_ref,
                  h1_ref, u2_ref, lg_ref):
    half = na_ref.shape[1]
    n_e = wr_ref.shape[0]
    tm = x_ref.shape[0]
    wh, wl = _split_bf16(wr_ref[...])
    w_router = jnp.concatenate([wh, wl], axis=0)
    for r in range(OUTPROJ_ROW_SPLITS):
        rows = pl.ds(r * (tm // OUTPROJ_ROW_SPLITS), tm // OUTPROJ_ROW_SPLITS)
        mix = _dot(na_ref[rows, :], w_ref[:half, :]) + _dot(ret_ref[rows, :], w_ref[half:, :])
        h1 = x_ref[rows, :] + gate_ref[0] * mix
        h1_ref[rows, :] = h1
        u2 = _rms_mod(h1, nw_ref[...], sh_ref[0], sc_ref[0])
        u2_ref[rows, :] = u2
        both = _dot_nt(w_router, u2.astype(BF16))
        lg_ref[:, rows] = both[:n_e] + both[n_e:]


def _out_projection(na, ret, w_out_bf16, x2d, mod3, norm_w, w_router_t, seq, tm=512):
    rows, d = x2d.shape
    half = na.shape[1]
    per_b = seq // tm
    return pl.pallas_call(
        _outproj_body,
        grid=(rows // tm,),
        in_specs=[
            pl.BlockSpec((tm, half), lambda i: (i, 0)),
            pl.BlockSpec((tm, half), lambda i: (i, 0)),
            pl.BlockSpec((2 * half, d), lambda i: (0, 0)),
            pl.BlockSpec((tm, d), lambda i: (i, 0)),
            pl.BlockSpec((1, 1, d), lambda i: (i // per_b, 0, MOD_LATE_GATE_MIX)),
            pl.BlockSpec((1, d), lambda i: (0, 0)),
            pl.BlockSpec((1, 1, d), lambda i: (i // per_b, 0, MOD_LATE_SHIFT_FFN)),
            pl.BlockSpec((1, 1, d), lambda i: (i // per_b, 0, MOD_LATE_SCALE_FFN)),
            pl.BlockSpec((N_EXPERTS, d), lambda i: (0, 0)),
        ],
        out_specs=[
            pl.BlockSpec((tm, d), lambda i: (i, 0)),
            pl.BlockSpec((tm, d), lambda i: (i, 0)),
            pl.BlockSpec((N_EXPERTS, tm), lambda i: (0, i)),
        ],
        out_shape=[
            jax.ShapeDtypeStruct((rows, d), F32),
            jax.ShapeDtypeStruct((rows, d), F32),
            jax.ShapeDtypeStruct((N_EXPERTS, rows), F32),
        ],
        compiler_params=_cparams(1),
        name="out_projection",
    )(na, ret, w_out_bf16, x2d, mod3, norm_w.reshape(1, d), mod3, mod3, w_router_t)


LANES = 128
SUBLANES = 8


def _prefix_incl_lanes(x, tri):
    r, l = x.shape
    nb = l // LANES
    xs = jnp.concatenate([x[:, t * LANES:(t + 1) * LANES] for t in range(nb)], axis=0).astype(BF16)
    p = _dot(xs, tri)
    outs, run = [], jnp.zeros((r, 1), F32)
    for t in range(nb):
        blk = p[t * r:(t + 1) * r] + run
        outs.append(blk)
        run = blk[:, LANES - 1:LANES]
    return jnp.concatenate(outs, axis=1)


NOT_ROUTED = -(1 << 20)


def _route_body(lg_ref, gidx_ref, gate_ref, slot_ref, before_ref, *, cap):
    b = pl.program_id(0)
    n_e, seq = lg_ref.shape
    kf = float(cap)
    lg = lg_ref[...]
    ex = jnp.exp(lg - jnp.max(lg, axis=0, keepdims=True))
    aff = ex / jnp.sum(ex, axis=0, keepdims=True)

    def cond(c):
        return (c[0] < 4096) & (c[5] > 0.5)

    def step(c):
        it, lo, hi, thr, done, _ = c
        mid = 0.5 * (lo + hi)
        above = jnp.sum((aff > mid).astype(F32), axis=1, keepdims=True)
        hit = above == kf
        stuck = (mid <= lo) | (mid >= hi)
        active = done < 0.5
        thr = jnp.where(active & hit, mid, jnp.where(active & stuck, hi, thr))
        go = active & ~(hit | stuck)
        ge = above >= kf
        lo = jnp.where(go & ge, mid, lo)
        hi = jnp.where(go & ~ge, mid, hi)
        done = jnp.where(active & (hit | stuck), 1.0, done)
        return it + 1, lo, hi, thr, done, jnp.sum(1.0 - done)

    col = lambda v: jnp.full((n_e, 1), v, F32)
    init = (jnp.int32(0), col(-1.0), col(2.0), col(0.0), col(0.0), jnp.float32(n_e))
    thr = lax.while_loop(cond, step, init)[3]

    ii = lax.broadcasted_iota(jnp.int32, (LANES, LANES), 0)
    jj = lax.broadcasted_iota(jnp.int32, (LANES, LANES), 1)
    tri = (ii <= jj).astype(BF16)
    gt = aff > thr
    eq = (aff == thr).astype(F32)
    need = kf - jnp.sum(gt.astype(F32), axis=1, keepdims=True)
    eq_before = _prefix_incl_lanes(eq, tri) - eq
    mask = jnp.where(gt | ((eq > 0.5) & (eq_before < need)), 1.0, 0.0)

    slot = _prefix_incl_lanes(mask, tri) - mask
    before_ref[0] = slot.astype(jnp.int32)
    slot_ref[0] = jnp.where(mask > 0.5, slot, float(NOT_ROUTED)).astype(jnp.int32)

    tok = lax.broadcasted_iota(jnp.int32, (1, seq), 1).astype(F32)
    tok_hi = jnp.floor(tok * (1.0 / 64))
    tok_lo = tok - 64.0 * tok_hi
    n_hi = ROUTE_SLOT_HI
    n_lo = cap // n_hi
    hi_iota = lax.broadcasted_iota(jnp.int32, (n_hi, seq), 0).astype(F32)
    lo_iota = lax.broadcasted_iota(jnp.int32, (n_lo, seq), 0).astype(F32)
    for e in range(n_e):
        se = slot[e:e + 1]
        s_hi = jnp.floor(se * (1.0 / n_lo))
        s_lo = se - n_lo * s_hi
        in_hi = jnp.where((s_hi == hi_iota) & (mask[e:e + 1] > 0.5), 1.0, 0.0)
        in_lo = (s_lo == lo_iota).astype(BF16)
        a = aff[e:e + 1]
        a_hi = a.astype(BF16).astype(F32)
        a_mid = (a - a_hi).astype(BF16).astype(F32)
        a_lo = a - a_hi - a_mid
        vals = (tok_hi, tok_lo, a_hi, a_mid, a_lo)
        lhs = jnp.concatenate([in_hi * v for v in vals], axis=0).astype(BF16)
        got = _dot_nt(lhs, in_lo)
        part = lambda r: got[r * n_hi:(r + 1) * n_hi]
        gidx_ref[0, e] = (part(0) * 64.0 + part(1)).astype(jnp.int32) + b * seq
        gate_ref[0, e] = part(2) + part(3) + part(4)


ROUTE_SLOT_HI = 16


def _routing(logits_t, batch, seq):
    n_e = logits_t.shape[0]
    cap = CAPACITY_FACTOR * seq // n_e
    n_hi, n_lo = ROUTE_SLOT_HI, cap // ROUTE_SLOT_HI
    bec = pl.BlockSpec((1, n_e, n_hi, n_lo), lambda b: (b, 0, 0, 0))
    bel = pl.BlockSpec((1, n_e, seq), lambda b: (b, 0, 0))
    gidx, gates, slot, before = pl.pallas_call(
        functools.partial(_route_body, cap=cap),
        grid=(batch,),
        in_specs=[pl.BlockSpec((n_e, seq), lambda b: (0, b))],
        out_specs=[bec, bec, bel, bel],
        out_shape=[
            jax.ShapeDtypeStruct((batch, n_e, n_hi, n_lo), jnp.int32),
            jax.ShapeDtypeStruct((batch, n_e, n_hi, n_lo), F32),
            jax.ShapeDtypeStruct((batch, n_e, seq), jnp.int32),
            jax.ShapeDtypeStruct((batch, n_e, seq), jnp.int32),
        ],
        compiler_params=_cparams(1),
        name="routing",
    )(logits_t)
    flat = lambda a: a.reshape(batch, n_e, cap)
    return flat(gidx), flat(gates), slot, before


def _hbm_row(ref, r):
    return ref.at[lax.shift_right_logical(r, 3), pl.ds(r & (SUBLANES - 1), 1)]


MOE_ROW_SPLITS = 2


def _moe_body(gidx_ref, gnext_ref, gate_ref, u2_hbm, wg_ref, wu_ref, wd_ref, y_ref,
              stage_ref, xe_ref, acc_ref, gsem, *, n_f):
    e = pl.program_id(0)
    f = pl.program_id(1)
    n_e = pl.num_programs(0)
    m, d = y_ref.shape
    sub = SUBLANES
    n_tiles = m // sub
    assert n_f >= 2
    per_step = -(-n_tiles // (n_f - 1))
    last_count = n_tiles - per_step * (n_f - 2)

    def gather_copy(idx_ref, i, k):
        return pltpu.make_async_copy(_hbm_row(u2_hbm, idx_ref[0, i * sub + k]), stage_ref.at[i, pl.ds(k, 1)], gsem)

    def for_rows(fn, tiles_per_trip=4):
        def body(t, c):
            for kk in range(tiles_per_trip * sub):
                fn(t * tiles_per_trip + kk // sub, kk % sub)
            return c
        lax.fori_loop(0, n_tiles // tiles_per_trip, body, 0)

    wait_gather = lambda: for_rows(lambda i, k: gather_copy(gidx_ref, i, k).wait())

    def ffn_step(first, gather_tiles):
        i0 = (f - 1) * per_step
        for kk in range(gather_tiles * sub):
            gather_copy(gnext_ref, i0 + kk // sub, kk % sub).start()
        wg, wu, wd = wg_ref[...].astype(BF16), wu_ref[...].astype(BF16), wd_ref[...].astype(BF16)
        part = m // MOE_ROW_SPLITS
        for r in range(MOE_ROW_SPLITS):
            rows = pl.ds(r * part, part)
            if first:
                x = stage_ref[pl.ds(r * (part // sub), part // sub)].reshape(part, d).astype(BF16)
                xe_ref[rows, :] = x
            else:
                x = xe_ref[rows, :]
            hid = _silu(_dot(x, wg)) * _dot(x, wu)
            out = _dot(hid.astype(BF16), wd)
            acc_ref[rows, :] = out if first else acc_ref[rows, :] + out

    @pl.when(f == 0)
    def _():
        @pl.when(e == 0)
        def _():
            for_rows(lambda i, k: gather_copy(gidx_ref, i, k).start())

        wait_gather()
        ffn_step(True, 0)

    if n_f > 2:
        pl.when((f > 0) & (f < n_f - 1))(lambda: ffn_step(False, per_step))

    @pl.when(f == n_f - 1)
    def _():
        ffn_step(False, last_count)

        @pl.when(f > 0)
        def _():
            g = jnp.transpose(jnp.broadcast_to(gate_ref[...], (LANES, m)))
            for k in range(d // LANES):
                cols = slice(k * LANES, (k + 1) * LANES)
                y_ref[:, cols] = (acc_ref[:, cols] * g).astype(y_ref.dtype)

        @pl.when(e == n_e - 1)
        def _():
            wait_gather()


def _moe_ffn(gidx, gates, u2, w_gate, w_up, w_down, tf=256):
    n_e, _, m = gidx.shape
    rows, d = u2.shape
    ff = w_gate.shape[2]
    n_f = ff // tf
    sub = SUBLANES
    smem = lambda fn: pl.BlockSpec((None, 1, m), fn, memory_space=pltpu.SMEM)
    return pl.pallas_call(
        functools.partial(_moe_body, n_f=n_f),
        grid=(n_e, n_f),
        in_specs=[
            smem(lambda e, f: (e, 0, 0)),
            smem(lambda e, f: (jnp.minimum(e + 1, n_e - 1), 0, 0)),
            pl.BlockSpec((None, 1, m), lambda e, f: (e, 0, 0)),
            pl.BlockSpec(memory_space=pl.ANY),
            pl.BlockSpec((None, d, tf), lambda e, f: (e, 0, f)),
            pl.BlockSpec((None, d, tf), lambda e, f: (e, 0, f)),
            pl.BlockSpec((None, tf, d), lambda e, f: (e, f, 0)),
        ],
        out_specs=pl.BlockSpec((m, d), lambda e, f: (e, 0)),
        out_shape=jax.ShapeDtypeStruct((n_e * m, d), BF16),
        scratch_shapes=[
            pltpu.VMEM((m // sub, sub, d), F32),
            pltpu.VMEM((m, d), BF16),
            pltpu.VMEM((m, d), F32),
            pltpu.SemaphoreType.DMA,
        ],
        compiler_params=_cparams(2),
        name="moe_ffn",
    )(gidx, gidx, gates, u2.reshape(rows // sub, sub, d), w_gate, w_up, w_down)


COMBINE_TOKENS = 256
COMBINE_PIECE = 64
BF16_ROWS = 16


def _combine_body(start_ref, npiece_ref, h1_ref, y_hbm, slot_ref, gate_ref, nw_ref, o_ref,
                  acc_ref, buf_ref, xbuf_ref, sem, xsem, *, m, cap, per_b):
    i = pl.program_id(0)
    n_tiles = pl.num_programs(0)
    n_e = slot_ref.shape[1]
    t, p = COMBINE_TOKENS, COMBINE_PIECE
    total = n_e * m
    cur = i % 2
    b = i // per_b

    def piece_copy(tile, e, dst):
        st = pl.multiple_of(start_ref[tile * n_e + e], BF16_ROWS)
        return pltpu.make_async_copy(y_hbm.at[pl.ds(st, p)], buf_ref.at[dst, pl.ds(e * p, p)], sem.at[dst])

    @pl.when(i == 0)
    def _():
        for e in range(n_e):
            piece_copy(0, e, 0).start()

    @pl.when(i + 1 < n_tiles)
    def _():
        for e in range(n_e):
            piece_copy(i + 1, e, 1 - cur).start()

    for e in range(n_e):
        piece_copy(i, e, cur).wait()

    slots = slot_ref[0]
    riota = lax.broadcasted_iota(jnp.int32, (p, t), 0)

    def local_row(e, st):
        return slots[e:e + 1, :] + (e * m + b * cap - st)

    sel = jnp.concatenate(
        [(local_row(e, start_ref[i * n_e + e]) == riota).astype(BF16) for e in range(n_e)], axis=0)
    acc_ref[...] = _dot_tn(sel, buf_ref[cur])

    for e in range(n_e):
        st0 = start_ref[i * n_e + e]

        def extra(k, carry, e=e, st0=st0):
            want = st0 + k * p
            st = pl.multiple_of(jnp.minimum(want, total - p), BF16_ROWS)
            cp = pltpu.make_async_copy(y_hbm.at[pl.ds(st, p)], xbuf_ref, xsem)
            cp.start()
            cp.wait()
            blk = ((local_row(e, st) == riota) & (riota >= want - st)).astype(BF16)
            acc_ref[...] += _dot_tn(blk, xbuf_ref[...])
            return carry

        lax.fori_loop(1, npiece_ref[i * n_e + e], extra, 0)

    h2 = h1_ref[...] + gate_ref[0] * acc_ref[...]
    ms = jnp.mean(h2 * h2, axis=-1, keepdims=True)
    o_ref[...] = h2 * lax.rsqrt(ms + EPS) * nw_ref[...]


def _combine_pieces(before, cap):
    batch, n_e, seq = before.shape
    t, p = COMBINE_TOKENS, COMBINE_PIECE
    m = batch * cap
    total = n_e * m
    bounds = jnp.concatenate([before[:, :, ::t], jnp.full((batch, n_e, 1), cap, jnp.int32)], axis=2)
    base = (jnp.arange(n_e, dtype=jnp.int32) * m)[None, :, None] + (jnp.arange(batch, dtype=jnp.int32) * cap)[:, None, None]
    first = base + bounds[:, :, :-1]
    end = base + bounds[:, :, 1:]
    start = jnp.minimum(first // BF16_ROWS * BF16_ROWS, total - p)
    npiece = jnp.maximum((end - start + p - 1) // p, 1)
    by_tile = lambda a: a.transpose(0, 2, 1).reshape(-1)
    return by_tile(start), by_tile(npiece)


def _combine(h1, y, slot, before, mod3, final_w, seq, cap):
    rows, d = h1.shape
    batch, n_e, _ = slot.shape
    t, p = COMBINE_TOKENS, COMBINE_PIECE
    per_b = seq // t
    start, npiece = _combine_pieces(before, cap)
    grid_spec = pltpu.PrefetchScalarGridSpec(
        num_scalar_prefetch=2,
        grid=(rows // t,),
        in_specs=[
            pl.BlockSpec((t, d), lambda i, st, npc: (i, 0)),
            pl.BlockSpec(memory_space=pl.ANY),
            pl.BlockSpec((1, n_e, t), lambda i, st, npc: (i // per_b, 0, i % per_b)),
            pl.BlockSpec((1, 1, d), lambda i, st, npc: (i // per_b, 0, MOD_LATE_GATE_FFN)),
            pl.BlockSpec((1, d), lambda i, st, npc: (0, 0)),
        ],
        out_specs=pl.BlockSpec((t, d), lambda i, st, npc: (i, 0)),
        scratch_shapes=[
            pltpu.VMEM((t, d), F32),
            pltpu.VMEM((2, n_e * p, d), y.dtype),
            pltpu.VMEM((p, d), y.dtype),
            pltpu.SemaphoreType.DMA((2,)),
            pltpu.SemaphoreType.DMA,
        ],
    )
    return pl.pallas_call(
        functools.partial(_combine_body, m=batch * cap, cap=cap, per_b=per_b),
        grid_spec=grid_spec,
        out_shape=jax.ShapeDtypeStruct((rows, d), F32),
        compiler_params=_cparams(1),
        name="combine",
    )(start, npiece, h1, y, slot, mod3, final_w.reshape(1, d))


def kernel(x, c, ctx, c_ctx, w_mod, b_mod, norm_mix_w, norm_ffn_w, w_in, na_rpb, ret_decay_fwd,
           ret_decay_bwd, ret_gn_w, w_out, w_router, w_gate, w_up, w_down, final_norm_w):
    batch, seq, d = x.shape
    ctx_len = ctx.shape[1]
    assert w_mod.shape[0] == 1, "one trunk layer"
    assert seq % (NA_QROWS * GRID_W) == 0 and seq // GRID_W >= 3 * NA_QROWS
    n_e = w_router.shape[2]
    cap = CAPACITY_FACTOR * seq // n_e

    mod_rows = 8
    cc = jnp.concatenate([c, c_ctx[None], jnp.zeros((mod_rows - batch - 1, d), c.dtype)], axis=0)
    mod3 = _modulation(cc, w_mod[0], b_mod[0], MOD_EARLY * d)

    x2d = x.reshape(batch * seq, d)
    tm = 1024
    cproj, w_in_bf16 = _ctx_projection(ctx.reshape(batch * ctx_len, d), norm_mix_w[0], mod3, batch,
                                       w_in[0], KV_COLS)
    proj = _in_projection(x2d, norm_mix_w[0], mod3, lambda i: i // (seq // tm), w_in_bf16,
                          w_in.shape[2], tm, w_in.shape[2] // 4)

    na = _neighbourhood_attention(proj, cproj, na_rpb[0], batch, seq, ctx_len)
    ret, mod_late3, w_out_bf16 = _retention(proj, cproj, ret_decay_fwd[0], ret_decay_bwd[0], ret_gn_w[0],
                                            batch, seq, ctx_len, cc, w_mod[0], b_mod[0], MOD_EARLY * d, w_out[0])

    h1, u2, logits_t = _out_projection(na, ret, w_out_bf16, x2d, mod_late3, norm_ffn_w[0], w_router[0].T, seq)

    gidx, gates, slot, before = _routing(logits_t, batch, seq)
    per_expert = lambda a: a.transpose(1, 0, 2).reshape(n_e, 1, batch * cap)
    y = _moe_ffn(per_expert(gidx), per_expert(gates), u2, w_gate[0], w_up[0], w_down[0])
    out = _combine(h1, y, slot, before, mod_late3, final_norm_w, seq, cap)
    return out.reshape(batch, seq, d)
```

```python
import functools

import jax
import jax.numpy as jnp
from jax import lax
from jax.experimental import pallas as pl
from jax.experimental.pallas import tpu as pltpu

GRID_W = 64
HEAD_DIM = 128
NA_HEADS = 8
RET_HEADS = 8
NA_WIDTH = NA_HEADS * HEAD_DIM
RET_WIDTH = RET_HEADS * HEAD_DIM
WIN_ROWS = 8
WIN_COLS = 16
RET_BLOCK = 256
ROPE_BASE = 10000.0
N_EXPERTS = 16
CAPACITY_FACTOR = 2
N_MOD = 6
MOD_EARLY = 2
MOD_LATE_GATE_MIX, MOD_LATE_SHIFT_FFN, MOD_LATE_SCALE_FFN, MOD_LATE_GATE_FFN = 0, 1, 2, 3
EPS = 1e-6
NEG_INF = -1e30
LOG2_E = 1.4426950408889634
KV_COLS = 2 * NA_WIDTH + 2 * RET_WIDTH

F32 = jnp.float32
BF16 = jnp.bfloat16
MIB = 1024 * 1024
VMEM_LIMIT_V7X = 56 * MIB


def _cparams(n_axes):
    return pltpu.CompilerParams(
        dimension_semantics=("arbitrary",) * n_axes, vmem_limit_bytes=VMEM_LIMIT_V7X)


def _silu(x):
    return x * jax.nn.sigmoid(x)


def _dot(a, b):
    return jnp.dot(a, b, preferred_element_type=F32)


def _dot_nt(a, b):
    return lax.dot_general(a, b, (((1,), (1,)), ((), ())), preferred_element_type=F32)


def _dot_tn(a, b):
    return lax.dot_general(a, b, (((0,), (0,)), ((), ())), preferred_element_type=F32)


def _mod_body(c_ref, w_ref, b_ref, o_ref):
    a = _silu(c_ref[...]).astype(BF16)
    o_ref[:, 0, :] = _dot(a, w_ref[...].astype(BF16)) + b_ref[...]


def _modulation(cc, w_mod, b_mod, n_cols, tn=1024):
    rows, d = cc.shape
    n = n_cols
    return pl.pallas_call(
        _mod_body,
        grid=(n // tn,),
        in_specs=[
            pl.BlockSpec((rows, d), lambda j: (0, 0)),
            pl.BlockSpec((d, tn), lambda j: (0, j)),
            pl.BlockSpec((1, tn), lambda j: (0, j)),
        ],
        out_specs=pl.BlockSpec((rows, 1, tn), lambda j: (0, 0, j)),
        out_shape=jax.ShapeDtypeStruct((rows, 1, n), F32),
        compiler_params=_cparams(1),
        name="modulation",
    )(cc, w_mod, b_mod.reshape(1, -1))


def _rms_mod(x, nw, shift, scale):
    ms = jnp.mean(x * x, axis=-1, keepdims=True)
    y = x * lax.rsqrt(ms + EPS) * nw
    return y * (1.0 + scale) + shift


INPROJ_NORM_SPLITS = 4


def _inproj_body(x_ref, nw_ref, sh_ref, sc_ref, w_ref, o_ref, u_ref):
    j = pl.program_id(1)
    tm = x_ref.shape[0]

    @pl.when(j == 0)
    def _():
        part = tm // INPROJ_NORM_SPLITS
        for r in range(INPROJ_NORM_SPLITS):
            rows = pl.ds(r * part, part)
            u = _rms_mod(x_ref[rows, :], nw_ref[...], sh_ref[0], sc_ref[0]).astype(BF16)
            u_ref[rows, :] = u
            o_ref[rows, :] = _dot(u, w_ref[...]).astype(o_ref.dtype)

    @pl.when(j > 0)
    def _():
        o_ref[...] = _dot(u_ref[...], w_ref[...]).astype(o_ref.dtype)


def _ctx_proj_body(x_ref, nw_ref, sh_ref, sc_ref, w_ref, o_ref, wb_ref, u_ref, *, kv_tiles):
    j = pl.program_id(0)

    @pl.when(j == 0)
    def _():
        u_ref[...] = _rms_mod(x_ref[...], nw_ref[...], sh_ref[0], sc_ref[0]).astype(BF16)

    wb_ref[...] = w_ref[...].astype(BF16)

    @pl.when(j < kv_tiles)
    def _():
        o_ref[...] = _dot(u_ref[...], wb_ref[...]).astype(o_ref.dtype)


def _ctx_projection(x2d, norm_w, mod3, mod_row, w_in, kv_cols, tn=1024):
    rows, d = x2d.shape
    n = w_in.shape[1]
    kv_tiles = kv_cols // tn
    return pl.pallas_call(
        functools.partial(_ctx_proj_body, kv_tiles=kv_tiles),
        grid=(n // tn,),
        in_specs=[
            pl.BlockSpec((rows, d), lambda j: (0, 0)),
            pl.BlockSpec((1, d), lambda j: (0, 0)),
            pl.BlockSpec((1, 1, d), lambda j: (mod_row, 0, 0)),
            pl.BlockSpec((1, 1, d), lambda j: (mod_row, 0, 1)),
            pl.BlockSpec((d, tn), lambda j: (0, j)),
        ],
        out_specs=[
            pl.BlockSpec((rows, tn), lambda j: (0, jnp.minimum(j, kv_tiles - 1))),
            pl.BlockSpec((d, tn), lambda j: (0, j)),
        ],
        out_shape=[jax.ShapeDtypeStruct((rows, kv_cols), BF16), jax.ShapeDtypeStruct((d, n), BF16)],
        scratch_shapes=[pltpu.VMEM((rows, d), BF16)],
        compiler_params=_cparams(1),
        name="ctx_projection",
    )(x2d, norm_w.reshape(1, d), mod3, mod3, w_in)


def _in_projection(x2d, norm_w, mod3, mod_row_fn, w_in, n_cols, tm, tn):
    rows, d = x2d.shape
    return pl.pallas_call(
        _inproj_body,
        grid=(rows // tm, n_cols // tn),
        in_specs=[
            pl.BlockSpec((tm, d), lambda i, j: (i, 0)),
            pl.BlockSpec((1, d), lambda i, j: (0, 0)),
            pl.BlockSpec((1, 1, d), lambda i, j: (mod_row_fn(i), 0, 0)),
            pl.BlockSpec((1, 1, d), lambda i, j: (mod_row_fn(i), 0, 1)),
            pl.BlockSpec((d, tn), lambda i, j: (0, j)),
        ],
        out_specs=pl.BlockSpec((tm, tn), lambda i, j: (i, j)),
        out_shape=jax.ShapeDtypeStruct((rows, n_cols), BF16),
        scratch_shapes=[pltpu.VMEM((tm, d), BF16)],
        compiler_params=_cparams(2),
        name="in_projection",
    )(x2d, norm_w.reshape(1, d), mod3, mod3, w_in)


NA_QROWS = 4
NA_KROWS = NA_QROWS + WIN_ROWS


def _na_row_offset(tile_kind, i, w, rows):
    half = WIN_ROWS // 2
    if tile_kind == 0:
        r, key = i, w
    elif tile_kind == 1:
        r, key = NA_QROWS + i, NA_QROWS - half + w
    else:
        r, key = rows - NA_QROWS + i, rows - NA_KROWS + w
    start = min(max(r - half, 0), rows - WIN_ROWS)
    if not (start <= key < start + WIN_ROWS):
        return None
    return key - r + (WIN_ROWS - 1)


def _na_build_bias(rpb_ref, bias_ref, h, rows):
    w = GRID_W
    cq = lax.broadcasted_iota(jnp.int32, (w, 2 * w), 0)
    ck = lax.broadcasted_iota(jnp.int32, (w, 2 * w), 1) % w
    col_start = jnp.clip(cq - WIN_COLS // 2, 0, w - WIN_COLS)
    col_ok = (ck >= col_start) & (ck < col_start + WIN_COLS)
    col_off = jnp.clip(ck - cq, -(WIN_COLS - 1), WIN_COLS - 1) + (WIN_COLS - 1)
    neg = jnp.full((w, 2 * w), NEG_INF, F32)
    n_ro, n_co = 2 * WIN_ROWS - 1, 2 * WIN_COLS - 1
    tabs = []
    for ro in range(n_ro):
        t = jnp.zeros((w, 2 * w), F32)
        for j in range(n_co):
            t = jnp.where(col_off == j, rpb_ref[h, ro * n_co + j] * (HEAD_DIM ** 0.5), t)
        tabs.append(jnp.where(col_ok, t, neg))
    left = lax.broadcasted_iota(jnp.int32, (w, 2 * w), 1) < w
    for kind in range(3):
        for i in range(NA_QROWS):
            for wp in range(NA_KROWS // 2):
                ra = _na_row_offset(kind, i, 2 * wp, rows)
                rb = _na_row_offset(kind, i, 2 * wp + 1, rows)
                ta = neg if ra is None else tabs[ra]
                tb = neg if rb is None else tabs[rb]
                blk = ta if ra == rb else jnp.where(left, ta, tb)
                bias_ref[kind, i * w:(i + 1) * w, wp * 2 * w:(wp + 1) * 2 * w] = blk


def _na_body(rpb_ref, q_ref, k_ref, v_ref, ck_ref, cv_ref, o_ref, bias_ref, sa_ref, sb_ref, *, rows):
    h = pl.program_id(0)
    w = GRID_W
    tq, tk = NA_QROWS * w, NA_KROWS * w
    n_tiles = rows // NA_QROWS
    scale = HEAD_DIM ** -0.5

    @pl.when(pl.program_id(1) == 0)
    def _():
        _na_build_bias(rpb_ref, bias_ref, h, rows)

    def offsets(t):
        krow0 = jnp.clip(t * NA_QROWS - WIN_ROWS // 2, 0, rows - NA_KROWS)
        return pl.multiple_of(t * tq, tq), pl.multiple_of(krow0 * w, 4 * w)

    def scores(t, s_ref):
        kind = jnp.where(t == 0, 0, jnp.where(t == n_tiles - 1, 2, 1))
        q0, k0 = offsets(t)
        q = q_ref[pl.ds(q0, tq), :]
        s_ref[:, :tk] = _dot_nt(q, k_ref[pl.ds(k0, tk), :]) + bias_ref[kind]
        s_ref[:, tk:] = _dot_nt(q, ck_ref[...])

    def attend(t, s_ref):
        q0, k0 = offsets(t)
        s = s_ref[...]
        p = jnp.exp2((s - jnp.max(s, axis=-1, keepdims=True)) * (scale * LOG2_E))
        l = jnp.sum(p, axis=-1, keepdims=True)
        pb = p.astype(BF16)
        o = _dot(pb[:, :tk], v_ref[pl.ds(k0, tk), :]) + _dot(pb[:, tk:], cv_ref[...])
        o_ref[pl.ds(q0, tq), :] = (o / l).astype(o_ref.dtype)

    scores(jnp.int32(0), sa_ref)

    def pair(i, carry):
        t = 2 * i
        scores(t + 1, sb_ref)
        attend(t, sa_ref)
        scores(t + 2, sa_ref)
        attend(t + 1, sb_ref)
        return carry

    lax.fori_loop(0, n_tiles // 2 - 1, pair, 0, unroll=True)
    scores(jnp.int32(n_tiles - 1), sb_ref)
    attend(jnp.int32(n_tiles - 2), sa_ref)
    attend(jnp.int32(n_tiles - 1), sb_ref)


def _neighbourhood_attention(proj, cproj, rpb, batch, seq, ctx_len):
    rows = seq // GRID_W
    hb = NA_WIDTH // HEAD_DIM
    n_rpb = (2 * WIN_ROWS - 1) * (2 * WIN_COLS - 1)
    tq, tk = NA_QROWS * GRID_W, NA_KROWS * GRID_W
    grid_spec = pltpu.PrefetchScalarGridSpec(
        num_scalar_prefetch=0,
        grid=(NA_HEADS, batch),
        in_specs=[
            pl.BlockSpec(memory_space=pltpu.SMEM),
            pl.BlockSpec((seq, HEAD_DIM), lambda h, b: (b, 4 * hb + h)),
            pl.BlockSpec((seq, HEAD_DIM), lambda h, b: (b, h)),
            pl.BlockSpec((seq, HEAD_DIM), lambda h, b: (b, hb + h)),
            pl.BlockSpec((ctx_len, HEAD_DIM), lambda h, b: (b, h)),
            pl.BlockSpec((ctx_len, HEAD_DIM), lambda h, b: (b, hb + h)),
        ],
        out_specs=pl.BlockSpec((seq, HEAD_DIM), lambda h, b: (b, h)),
        scratch_shapes=[
            pltpu.VMEM((3, tq, tk), F32),
            pltpu.VMEM((tq, tk + ctx_len), F32),
            pltpu.VMEM((tq, tk + ctx_len), F32),
        ],
    )
    return pl.pallas_call(
        functools.partial(_na_body, rows=rows),
        grid_spec=grid_spec,
        out_shape=jax.ShapeDtypeStruct((batch * seq, NA_WIDTH), BF16),
        compiler_params=_cparams(2),
        name="neighbourhood_attention",
    )(rpb.reshape(NA_HEADS, n_rpb), proj, proj, proj, cproj, cproj)


def _log_sigmoid(x):
    return -(jnp.maximum(-x, 0.0) + jnp.log1p(jnp.exp(-jnp.abs(x))))


def _rope_partner_matrix():
    quarter = HEAD_DIM // 4
    src = lax.broadcasted_iota(jnp.int32, (HEAD_DIM, HEAD_DIM), 0)
    dst = lax.broadcasted_iota(jnp.int32, (HEAD_DIM, HEAD_DIM), 1)
    want = jnp.where(dst % (2 * quarter) < quarter, dst + quarter, dst - quarter)
    return (src == want).astype(BF16)


def _rope(x_bf16, partner_matrix, cos, sin_signed):
    return x_bf16.astype(F32) * cos + _dot(x_bf16, partner_matrix) * sin_signed


def _ret_body(df_ref, db_ref, gn_ref, cos_ref, sin_ref, q_ref, k_ref, v_ref, g_ref, ck_ref, cv_ref,
              cc_ref, wm_ref, bm_ref, wo_ref,
              o_ref, modl_ref, wob_ref, qr_ref, kr_ref, sf_ref, sb_ref):
    modl_ref[:, 0, :] = _dot(_silu(cc_ref[...]).astype(BF16), wm_ref[...].astype(BF16)) + bm_ref[...]
    wob_ref[...] = wo_ref[...].astype(BF16)

    c, d = RET_BLOCK, HEAD_DIM
    seq = q_ref.shape[0]
    n_chunks = seq // c
    ctx_len = ck_ref.shape[0]
    scale = HEAD_DIM ** -0.5
    lgf_row = _log_sigmoid(df_ref[0])
    lgb_row = _log_sigmoid(db_ref[0])
    lgf = jnp.broadcast_to(lgf_row, (c, d))
    lgb = jnp.broadcast_to(lgb_row, (c, d))
    pos = lax.broadcasted_iota(jnp.int32, (c, d), 0).astype(F32)
    kdf = jnp.exp(lgf * (c - 1.0 - pos))
    kdb = jnp.exp(lgb * pos)
    qdf = jnp.exp(lgf * (pos + 1.0))
    qdb = jnp.exp(lgb * (c - pos))
    cdf = jnp.exp(lgf_row * float(c))
    cdb = jnp.exp(lgb_row * float(c))
    diff = (lax.broadcasted_iota(jnp.int32, (c, c), 0) - lax.broadcasted_iota(jnp.int32, (c, c), 1)).astype(F32)
    lgf_cc = jnp.broadcast_to(lgf_row[:, :1], (c, c))
    lgb_cc = jnp.broadcast_to(lgb_row[:, :1], (c, c))
    dmat = (jnp.where(diff >= 0, jnp.exp(lgf_cc * jnp.maximum(diff, 0.0)), 0.0)
            + jnp.where(diff <= 0, jnp.exp(lgb_cc * jnp.maximum(-diff, 0.0)), 0.0))

    cpos = lax.broadcasted_iota(jnp.int32, (ctx_len, d), 0).astype(F32)
    ckf = ck_ref[...].astype(F32) * scale
    cv = cv_ref[...]
    wf = jnp.exp(jnp.broadcast_to(lgf_row, (ctx_len, d)) * (ctx_len - 1.0 - cpos))
    wb = jnp.exp(jnp.broadcast_to(lgb_row, (ctx_len, d)) * cpos)
    s_f = _dot_tn((ckf * wf).astype(BF16), cv)
    s_b = _dot_tn((ckf * wb).astype(BF16), cv)

    rope_rows = 512
    pmat = _rope_partner_matrix()

    def rope_blk(i, carry):
        r0 = pl.multiple_of(i * rope_rows, rope_rows)
        cs = cos_ref[pl.ds(r0, rope_rows), :]
        sn = sin_ref[pl.ds(r0, rope_rows), :]
        qr_ref[pl.ds(r0, rope_rows), :] = _rope(q_ref[pl.ds(r0, rope_rows), :], pmat, cs, sn)
        kr_ref[pl.ds(r0, rope_rows), :] = _rope(k_ref[pl.ds(r0, rope_rows), :], pmat, cs, sn) * scale
        return carry

    lax.fori_loop(0, seq // rope_rows, rope_blk, 0, unroll=4)

    def scan_blk(n, carry):
        s, t = carry
        r0 = pl.multiple_of(n * c, c)
        kvf = _dot_tn((kr_ref[pl.ds(r0, c), :] * kdf).astype(BF16), v_ref[pl.ds(r0, c), :])
        sf_ref[n] = s.astype(BF16)
        m = n_chunks - 1 - n
        m0 = pl.multiple_of(m * c, c)
        kvb = _dot_tn((kr_ref[pl.ds(m0, c), :] * kdb).astype(BF16), v_ref[pl.ds(m0, c), :])
        sb_ref[m] = t.astype(BF16)
        return s * cdf + kvf, t * cdb + kvb

    lax.fori_loop(0, n_chunks, scan_blk, (s_f, s_b), unroll=16)

    gn = gn_ref[...]

    def out_blk(n, carry):
        r0 = pl.multiple_of(n * c, c)
        qc = qr_ref[pl.ds(r0, c), :]
        kc = kr_ref[pl.ds(r0, c), :]
        a = _dot_nt(qc.astype(BF16), kc.astype(BF16))
        o = (_dot((a * dmat).astype(BF16), v_ref[pl.ds(r0, c), :])
             + _dot((qc * qdf).astype(BF16), sf_ref[n])
             + _dot((qc * qdb).astype(BF16), sb_ref[n]))
        mu = jnp.mean(o, axis=-1, keepdims=True)
        var = jnp.mean(jnp.square(o - mu), axis=-1, keepdims=True)
        y = (o - mu) * lax.rsqrt(var + EPS) * gn
        o_ref[pl.ds(r0, c), :] = (y * _silu(g_ref[pl.ds(r0, c), :].astype(F32))).astype(o_ref.dtype)
        return carry

    lax.fori_loop(0, n_chunks, out_blk, 0, unroll=16)


def _rope_tables(seq):
    axis_dim = HEAD_DIM // 2
    inv_freq = ROPE_BASE ** (-jnp.arange(0, axis_dim, 2, dtype=F32) / axis_dim)
    rows = seq // GRID_W
    ang_r = jnp.arange(rows, dtype=F32)[:, None] * inv_freq
    ang_c = jnp.arange(GRID_W, dtype=F32)[:, None] * inv_freq
    by_row = lambda a: jnp.repeat(a, GRID_W, axis=0)
    by_col = lambda a: jnp.tile(a, (rows, 1))
    cr, sr, cc, sc = by_row(jnp.cos(ang_r)), by_row(jnp.sin(ang_r)), by_col(jnp.cos(ang_c)), by_col(jnp.sin(ang_c))
    return jnp.concatenate([cr, cr, cc, cc], axis=-1), jnp.concatenate([-sr, sr, -sc, sc], axis=-1)


def _retention(proj, cproj, decay_f, decay_b, gn_w, batch, seq, ctx_len, cc, w_mod, b_mod, mod_from, w_out):
    hb = RET_WIDTH // HEAD_DIM
    cos, sin = _rope_tables(seq)
    dec_f = jnp.broadcast_to(decay_f.astype(F32)[:, None, None], (RET_HEADS, 1, HEAD_DIM))
    dec_b = jnp.broadcast_to(decay_b.astype(F32)[:, None, None], (RET_HEADS, 1, HEAD_DIM))
    n_chunks = seq // RET_BLOCK
    blk = lambda rows, fn: pl.BlockSpec((rows, HEAD_DIM), fn)
    n_steps = RET_HEADS * batch
    step = lambda h, b: h * batch + b
    mod_rows, d = cc.shape
    late = w_mod.shape[1] - mod_from
    mcols, orows = late // n_steps, w_out.shape[0] // n_steps
    side_in = [
        pl.BlockSpec((mod_rows, d), lambda h, b: (0, 0)),
        pl.BlockSpec((d, mcols), lambda h, b: (0, mod_from // mcols + step(h, b))),
        pl.BlockSpec((1, mcols), lambda h, b: (0, mod_from // mcols + step(h, b))),
        pl.BlockSpec((orows, w_out.shape[1]), lambda h, b: (step(h, b), 0)),
    ]
    side_out = [
        pl.BlockSpec((mod_rows, 1, mcols), lambda h, b: (0, 0, step(h, b))),
        pl.BlockSpec((orows, w_out.shape[1]), lambda h, b: (step(h, b), 0)),
    ]
    side_shape = [jax.ShapeDtypeStruct((mod_rows, 1, late), F32), jax.ShapeDtypeStruct(w_out.shape, BF16)]
    return pl.pallas_call(
        _ret_body,
        grid=(RET_HEADS, batch),
        in_specs=[
            pl.BlockSpec((1, 1, HEAD_DIM), lambda h, b: (h, 0, 0)),
            pl.BlockSpec((1, 1, HEAD_DIM), lambda h, b: (h, 0, 0)),
            blk(1, lambda h, b: (0, h)),
            blk(seq, lambda h, b: (0, 0)),
            blk(seq, lambda h, b: (0, 0)),
            blk(seq, lambda h, b: (b, 5 * hb + h)),
            blk(seq, lambda h, b: (b, 2 * hb + h)),
            blk(seq, lambda h, b: (b, 3 * hb + h)),
            blk(seq, lambda h, b: (b, 6 * hb + h)),
            blk(ctx_len, lambda h, b: (b, 2 * hb + h)),
            blk(ctx_len, lambda h, b: (b, 3 * hb + h)),
        ] + side_in,
        out_specs=[blk(seq, lambda h, b: (b, h))] + side_out,
        out_shape=[jax.ShapeDtypeStruct((batch * seq, RET_WIDTH), BF16)] + side_shape,
        scratch_shapes=[
            pltpu.VMEM((seq, HEAD_DIM), F32),
            pltpu.VMEM((seq, HEAD_DIM), F32),
            pltpu.VMEM((n_chunks, HEAD_DIM, HEAD_DIM), BF16),
            pltpu.VMEM((n_chunks, HEAD_DIM, HEAD_DIM), BF16),
        ],
        compiler_params=_cparams(2),
        name="retention",
    )(dec_f, dec_b, gn_w.reshape(1, RET_WIDTH), cos, sin, proj, proj, proj, proj, cproj, cproj,
      cc, w_mod, b_mod.reshape(1, -1), w_out)


def _split_bf16(x):
    hi = x.astype(BF16)
    lo = (x - hi.astype(F32)).astype(BF16)
    return hi, lo


OUTPROJ_ROW_SPLITS = 2


def _outproj_body(na_ref, ret_ref, w_ref, x_ref, gate_ref, nw_ref, sh_ref, sc_ref, wr_ref,
                  h1_ref, u2_ref, lg_ref):
    half = na_ref.shape[1]
    n_e = wr_ref.shape[0]
    tm = x_ref.shape[0]
    wh, wl = _split_bf16(wr_ref[...])
    w_router = jnp.concatenate([wh, wl], axis=0)
    for r in range(OUTPROJ_ROW_SPLITS):
        rows = pl.ds(r * (tm // OUTPROJ_ROW_SPLITS), tm // OUTPROJ_ROW_SPLITS)
        mix = _dot(na_ref[rows, :], w_ref[:half, :]) + _dot(ret_ref[rows, :], w_ref[half:, :])
        h1 = x_ref[rows, :] + gate_ref[0] * mix
        h1_ref[rows, :] = h1
        u2 = _rms_mod(h1, nw_ref[...], sh_ref[0], sc_ref[0])
        u2_ref[rows, :] = u2
        both = _dot_nt(w_router, u2.astype(BF16))
        lg_ref[:, rows] = both[:n_e] + both[n_e:]


def _out_projection(na, ret, w_out_bf16, x2d, mod3, norm_w, w_router_t, seq, tm=512):
    rows, d = x2d.shape
    half = na.shape[1]
    per_b = seq // tm
    return pl.pallas_call(
        _outproj_body,
        grid=(rows // tm,),
        in_specs=[
            pl.BlockSpec((tm, half), lambda i: (i, 0)),
            pl.BlockSpec((tm, half), lambda i: (i, 0)),
            pl.BlockSpec((2 * half, d), lambda i: (0, 0)),
            pl.BlockSpec((tm, d), lambda i: (i, 0)),
            pl.BlockSpec((1, 1, d), lambda i: (i // per_b, 0, MOD_LATE_GATE_MIX)),
            pl.BlockSpec((1, d), lambda i: (0, 0)),
            pl.BlockSpec((1, 1, d), lambda i: (i // per_b, 0, MOD_LATE_SHIFT_FFN)),
            pl.BlockSpec((1, 1, d), lambda i: (i // per_b, 0, MOD_LATE_SCALE_FFN)),
            pl.BlockSpec((N_EXPERTS, d), lambda i: (0, 0)),
        ],
        out_specs=[
            pl.BlockSpec((tm, d), lambda i: (i, 0)),
            pl.BlockSpec((tm, d), lambda i: (i, 0)),
            pl.BlockSpec((N_EXPERTS, tm), lambda i: (0, i)),
        ],
        out_shape=[
            jax.ShapeDtypeStruct((rows, d), F32),
            jax.ShapeDtypeStruct((rows, d), F32),
            jax.ShapeDtypeStruct((N_EXPERTS, rows), F32),
        ],
        compiler_params=_cparams(1),
        name="out_projection",
    )(na, ret, w_out_bf16, x2d, mod3, norm_w.reshape(1, d), mod3, mod3, w_router_t)


LANES = 128
SUBLANES = 8


def _prefix_incl_lanes(x, tri):
    r, l = x.shape
    nb = l // LANES
    xs = jnp.concatenate([x[:, t * LANES:(t + 1) * LANES] for t in range(nb)], axis=0).astype(BF16)
    p = _dot(xs, tri)
    outs, run = [], jnp.zeros((r, 1), F32)
    for t in range(nb):
        blk = p[t * r:(t + 1) * r] + run
        outs.append(blk)
        run = blk[:, LANES - 1:LANES]
    return jnp.concatenate(outs, axis=1)


NOT_ROUTED = -(1 << 20)


def _route_body(lg_ref, gidx_ref, gate_ref, slot_ref, before_ref, *, cap):
    b = pl.program_id(0)
    n_e, seq = lg_ref.shape
    kf = float(cap)
    lg = lg_ref[...]
    ex = jnp.exp(lg - jnp.max(lg, axis=0, keepdims=True))
    aff = ex / jnp.sum(ex, axis=0, keepdims=True)

    def cond(c):
        return (c[0] < 4096) & (c[5] > 0.5)

    def step(c):
        it, lo, hi, thr, done, _ = c
        mid = 0.5 * (lo + hi)
        above = jnp.sum((aff > mid).astype(F32), axis=1, keepdims=True)
        hit = above == kf
        stuck = (mid <= lo) | (mid >= hi)
        active = done < 0.5
        thr = jnp.where(active & hit, mid, jnp.where(active & stuck, hi, thr))
        go = active & ~(hit | stuck)
        ge = above >= kf
        lo = jnp.where(go & ge, mid, lo)
        hi = jnp.where(go & ~ge, mid, hi)
        done = jnp.where(active & (hit | stuck), 1.0, done)
        return it + 1, lo, hi, thr, done, jnp.sum(1.0 - done)

    col = lambda v: jnp.full((n_e, 1), v, F32)
    init = (jnp.int32(0), col(-1.0), col(2.0), col(0.0), col(0.0), jnp.float32(n_e))
    thr = lax.while_loop(cond, step, init)[3]

    ii = lax.broadcasted_iota(jnp.int32, (LANES, LANES), 0)
    jj = lax.broadcasted_iota(jnp.int32, (LANES, LANES), 1)
    tri = (ii <= jj).astype(BF16)
    gt = aff > thr
    eq = (aff == thr).astype(F32)
    need = kf - jnp.sum(gt.astype(F32), axis=1, keepdims=True)
    eq_before = _prefix_incl_lanes(eq, tri) - eq
    mask = jnp.where(gt | ((eq > 0.5) & (eq_before < need)), 1.0, 0.0)

    slot = _prefix_incl_lanes(mask, tri) - mask
    before_ref[0] = slot.astype(jnp.int32)
    slot_ref[0] = jnp.where(mask > 0.5, slot, float(NOT_ROUTED)).astype(jnp.int32)

    tok = lax.broadcasted_iota(jnp.int32, (1, seq), 1).astype(F32)
    tok_hi = jnp.floor(tok * (1.0 / 64))
    tok_lo = tok - 64.0 * tok_hi
    n_hi = ROUTE_SLOT_HI
    n_lo = cap // n_hi
    hi_iota = lax.broadcasted_iota(jnp.int32, (n_hi, seq), 0).astype(F32)
    lo_iota = lax.broadcasted_iota(jnp.int32, (n_lo, seq), 0).astype(F32)
    for e in range(n_e):
        se = slot[e:e + 1]
        s_hi = jnp.floor(se * (1.0 / n_lo))
        s_lo = se - n_lo * s_hi
        in_hi = jnp.where((s_hi == hi_iota) & (mask[e:e + 1] > 0.5), 1.0, 0.0)
        in_lo = (s_lo == lo_iota).astype(BF16)
        a = aff[e:e + 1]
        a_hi = a.astype(BF16).astype(F32)
        a_mid = (a - a_hi).astype(BF16).astype(F32)
        a_lo = a - a_hi - a_mid
        vals = (tok_hi, tok_lo, a_hi, a_mid, a_lo)
        lhs = jnp.concatenate([in_hi * v for v in vals], axis=0).astype(BF16)
        got = _dot_nt(lhs, in_lo)
        part = lambda r: got[r * n_hi:(r + 1) * n_hi]
        gidx_ref[0, e] = (part(0) * 64.0 + part(1)).astype(jnp.int32) + b * seq
        gate_ref[0, e] = part(2) + part(3) + part(4)


ROUTE_SLOT_HI = 16


def _routing(logits_t, batch, seq):
    n_e = logits_t.shape[0]
    cap = CAPACITY_FACTOR * seq // n_e
    n_hi, n_lo = ROUTE_SLOT_HI, cap // ROUTE_SLOT_HI
    bec = pl.BlockSpec((1, n_e, n_hi, n_lo), lambda b: (b, 0, 0, 0))
    bel = pl.BlockSpec((1, n_e, seq), lambda b: (b, 0, 0))
    gidx, gates, slot, before = pl.pallas_call(
        functools.partial(_route_body, cap=cap),
        grid=(batch,),
        in_specs=[pl.BlockSpec((n_e, seq), lambda b: (0, b))],
        out_specs=[bec, bec, bel, bel],
        out_shape=[
            jax.ShapeDtypeStruct((batch, n_e, n_hi, n_lo), jnp.int32),
            jax.ShapeDtypeStruct((batch, n_e, n_hi, n_lo), F32),
            jax.ShapeDtypeStruct((batch, n_e, seq), jnp.int32),
            jax.ShapeDtypeStruct((batch, n_e, seq), jnp.int32),
        ],
        compiler_params=_cparams(1),
        name="routing",
    )(logits_t)
    flat = lambda a: a.reshape(batch, n_e, cap)
    return flat(gidx), flat(gates), slot, before


def _hbm_row(ref, r):
    return ref.at[lax.shift_right_logical(r, 3), pl.ds(r & (SUBLANES - 1), 1)]


MOE_ROW_SPLITS = 2


def _moe_body(gidx_ref, gnext_ref, gate_ref, u2_hbm, wg_ref, wu_ref, wd_ref, y_ref,
              stage_ref, xe_ref, acc_ref, gsem, *, n_f):
    e = pl.program_id(0)
    f = pl.program_id(1)
    n_e = pl.num_programs(0)
    m, d = y_ref.shape
    sub = SUBLANES
    n_tiles = m // sub
    assert n_f >= 2
    per_step = -(-n_tiles // (n_f - 1))
    last_count = n_tiles - per_step * (n_f - 2)

    def gather_copy(idx_ref, i, k):
        return pltpu.make_async_copy(_hbm_row(u2_hbm, idx_ref[0, i * sub + k]), stage_ref.at[i, pl.ds(k, 1)], gsem)

    def for_rows(fn, tiles_per_trip=4):
        def body(t, c):
            for kk in range(tiles_per_trip * sub):
                fn(t * tiles_per_trip + kk // sub, kk % sub)
            return c
        lax.fori_loop(0, n_tiles // tiles_per_trip, body, 0)

    wait_gather = lambda: for_rows(lambda i, k: gather_copy(gidx_ref, i, k).wait())

    def ffn_step(first, gather_tiles):
        i0 = (f - 1) * per_step
        for kk in range(gather_tiles * sub):
            gather_copy(gnext_ref, i0 + kk // sub, kk % sub).start()
        wg, wu, wd = wg_ref[...].astype(BF16), wu_ref[...].astype(BF16), wd_ref[...].astype(BF16)
        part = m // MOE_ROW_SPLITS
        for r in range(MOE_ROW_SPLITS):
            rows = pl.ds(r * part, part)
            if first:
                x = stage_ref[pl.ds(r * (part // sub), part // sub)].reshape(part, d).astype(BF16)
                xe_ref[rows, :] = x
            else:
                x = xe_ref[rows, :]
            hid = _silu(_dot(x, wg)) * _dot(x, wu)
            out = _dot(hid.astype(BF16), wd)
            acc_ref[rows, :] = out if first else acc_ref[rows, :] + out

    @pl.when(f == 0)
    def _():
        @pl.when(e == 0)
        def _():
            for_rows(lambda i, k: gather_copy(gidx_ref, i, k).start())

        wait_gather()
        ffn_step(True, 0)

    if n_f > 2:
        pl.when((f > 0) & (f < n_f - 1))(lambda: ffn_step(False, per_step))

    @pl.when(f == n_f - 1)
    def _():
        ffn_step(False, last_count)

        @pl.when(f > 0)
        def _():
            g = jnp.transpose(jnp.broadcast_to(gate_ref[...], (LANES, m)))
            for k in range(d // LANES):
                cols = slice(k * LANES, (k + 1) * LANES)
                y_ref[:, cols] = (acc_ref[:, cols] * g).astype(y_ref.dtype)

        @pl.when(e == n_e - 1)
        def _():
            wait_gather()


def _moe_ffn(gidx, gates, u2, w_gate, w_up, w_down, tf=256):
    n_e, _, m = gidx.shape
    rows, d = u2.shape
    ff = w_gate.shape[2]
    n_f = ff // tf
    sub = SUBLANES
    smem = lambda fn: pl.BlockSpec((None, 1, m), fn, memory_space=pltpu.SMEM)
    return pl.pallas_call(
        functools.partial(_moe_body, n_f=n_f),
        grid=(n_e, n_f),
        in_specs=[
            smem(lambda e, f: (e, 0, 0)),
            smem(lambda e, f: (jnp.minimum(e + 1, n_e - 1), 0, 0)),
            pl.BlockSpec((None, 1, m), lambda e, f: (e, 0, 0)),
            pl.BlockSpec(memory_space=pl.ANY),
            pl.BlockSpec((None, d, tf), lambda e, f: (e, 0, f)),
            pl.BlockSpec((None, d, tf), lambda e, f: (e, 0, f)),
            pl.BlockSpec((None, tf, d), lambda e, f: (e, f, 0)),
        ],
        out_specs=pl.BlockSpec((m, d), lambda e, f: (e, 0)),
        out_shape=jax.ShapeDtypeStruct((n_e * m, d), BF16),
        scratch_shapes=[
            pltpu.VMEM((m // sub, sub, d), F32),
            pltpu.VMEM((m, d), BF16),
            pltpu.VMEM((m, d), F32),
            pltpu.SemaphoreType.DMA,
        ],
        compiler_params=_cparams(2),
        name="moe_ffn",
    )(gidx, gidx, gates, u2.reshape(rows // sub, sub, d), w_gate, w_up, w_down)


COMBINE_TOKENS = 256
COMBINE_PIECE = 64
BF16_ROWS = 16


def _combine_body(start_ref, npiece_ref, h1_ref, y_hbm, slot_ref, gate_ref, nw_ref, o_ref,
                  acc_ref, buf_ref, xbuf_ref, sem, xsem, *, m, cap, per_b):
    i = pl.program_id(0)
    n_tiles = pl.num_programs(0)
    n_e = slot_ref.shape[1]
    t, p = COMBINE_TOKENS, COMBINE_PIECE
    total = n_e * m
    cur = i % 2
    b = i // per_b

    def piece_copy(tile, e, dst):
        st = pl.multiple_of(start_ref[tile * n_e + e], BF16_ROWS)
        return pltpu.make_async_copy(y_hbm.at[pl.ds(st, p)], buf_ref.at[dst, pl.ds(e * p, p)], sem.at[dst])

    @pl.when(i == 0)
    def _():
        for e in range(n_e):
            piece_copy(0, e, 0).start()

    @pl.when(i + 1 < n_tiles)
    def _():
        for e in range(n_e):
            piece_copy(i + 1, e, 1 - cur).start()

    for e in range(n_e):
        piece_copy(i, e, cur).wait()

    slots = slot_ref[0]
    riota = lax.broadcasted_iota(jnp.int32, (p, t), 0)

    def local_row(e, st):
        return slots[e:e + 1, :] + (e * m + b * cap - st)

    sel = jnp.concatenate(
        [(local_row(e, start_ref[i * n_e + e]) == riota).astype(BF16) for e in range(n_e)], axis=0)
    acc_ref[...] = _dot_tn(sel, buf_ref[cur])

    for e in range(n_e):
        st0 = start_ref[i * n_e + e]

        def extra(k, carry, e=e, st0=st0):
            want = st0 + k * p
            st = pl.multiple_of(jnp.minimum(want, total - p), BF16_ROWS)
            cp = pltpu.make_async_copy(y_hbm.at[pl.ds(st, p)], xbuf_ref, xsem)
            cp.start()
            cp.wait()
            blk = ((local_row(e, st) == riota) & (riota >= want - st)).astype(BF16)
            acc_ref[...] += _dot_tn(blk, xbuf_ref[...])
            return carry

        lax.fori_loop(1, npiece_ref[i * n_e + e], extra, 0)

    h2 = h1_ref[...] + gate_ref[0] * acc_ref[...]
    ms = jnp.mean(h2 * h2, axis=-1, keepdims=True)
    o_ref[...] = h2 * lax.rsqrt(ms + EPS) * nw_ref[...]


def _combine_pieces(before, cap):
    batch, n_e, seq = before.shape
    t, p = COMBINE_TOKENS, COMBINE_PIECE
    m = batch * cap
    total = n_e * m
    bounds = jnp.concatenate([before[:, :, ::t], jnp.full((batch, n_e, 1), cap, jnp.int32)], axis=2)
    base = (jnp.arange(n_e, dtype=jnp.int32) * m)[None, :, None] + (jnp.arange(batch, dtype=jnp.int32) * cap)[:, None, None]
    first = base + bounds[:, :, :-1]
    end = base + bounds[:, :, 1:]
    start = jnp.minimum(first // BF16_ROWS * BF16_ROWS, total - p)
    npiece = jnp.maximum((end - start + p - 1) // p, 1)
    by_tile = lambda a: a.transpose(0, 2, 1).reshape(-1)
    return by_tile(start), by_tile(npiece)


def _combine(h1, y, slot, before, mod3, final_w, seq, cap):
    rows, d = h1.shape
    batch, n_e, _ = slot.shape
    t, p = COMBINE_TOKENS, COMBINE_PIECE
    per_b = seq // t
    start, npiece = _combine_pieces(before, cap)
    grid_spec = pltpu.PrefetchScalarGridSpec(
        num_scalar_prefetch=2,
        grid=(rows // t,),
        in_specs=[
            pl.BlockSpec((t, d), lambda i, st, npc: (i, 0)),
            pl.BlockSpec(memory_space=pl.ANY),
            pl.BlockSpec((1, n_e, t), lambda i, st, npc: (i // per_b, 0, i % per_b)),
            pl.BlockSpec((1, 1, d), lambda i, st, npc: (i // per_b, 0, MOD_LATE_GATE_FFN)),
            pl.BlockSpec((1, d), lambda i, st, npc: (0, 0)),
        ],
        out_specs=pl.BlockSpec((t, d), lambda i, st, npc: (i, 0)),
        scratch_shapes=[
            pltpu.VMEM((t, d), F32),
            pltpu.VMEM((2, n_e * p, d), y.dtype),
            pltpu.VMEM((p, d), y.dtype),
            pltpu.SemaphoreType.DMA((2,)),
            pltpu.SemaphoreType.DMA,
        ],
    )
    return pl.pallas_call(
        functools.partial(_combine_body, m=batch * cap, cap=cap, per_b=per_b),
        grid_spec=grid_spec,
        out_shape=jax.ShapeDtypeStruct((rows, d), F32),
        compiler_params=_cparams(1),
        name="combine",
    )(start, npiece, h1, y, slot, mod3, final_w.reshape(1, d))


def kernel(x, c, ctx, c_ctx, w_mod, b_mod, norm_mix_w, norm_ffn_w, w_in, na_rpb, ret_decay_fwd,
           ret_decay_bwd, ret_gn_w, w_out, w_router, w_gate, w_up, w_down, final_norm_w):
    batch, seq, d = x.shape
    ctx_len = ctx.shape[1]
    assert w_mod.shape[0] == 1, "one trunk layer"
    assert seq % (NA_QROWS * GRID_W) == 0 and seq // GRID_W >= 3 * NA_QROWS
    n_e = w_router.shape[2]
    cap = CAPACITY_FACTOR * seq // n_e

    mod_rows = 8
    cc = jnp.concatenate([c, c_ctx[None], jnp.zeros((mod_rows - batch - 1, d), c.dtype)], axis=0)
    mod3 = _modulation(cc, w_mod[0], b_mod[0], MOD_EARLY * d)

    x2d = x.reshape(batch * seq, d)
    tm = 1024
    cproj, w_in_bf16 = _ctx_projection(ctx.reshape(batch * ctx_len, d), norm_mix_w[0], mod3, batch,
                                       w_in[0], KV_COLS)
    proj = _in_projection(x2d, norm_mix_w[0], mod3, lambda i: i // (seq // tm), w_in_bf16,
                          w_in.shape[2], tm, w_in.shape[2] // 4)

    na = _neighbourhood_attention(proj, cproj, na_rpb[0], batch, seq, ctx_len)
    ret, mod_late3, w_out_bf16 = _retention(proj, cproj, ret_decay_fwd[0], ret_decay_bwd[0], ret_gn_w[0],
                                            batch, seq, ctx_len, cc, w_mod[0], b_mod[0], MOD_EARLY * d, w_out[0])

    h1, u2, logits_t = _out_projection(na, ret, w_out_bf16, x2d, mod_late3, norm_ffn_w[0], w_router[0].T, seq)

    gidx, gates, slot, before = _routing(logits_t, batch, seq)
    per_expert = lambda a: a.transpose(1, 0, 2).reshape(n_e, 1, batch * cap)
    y = _moe_ffn(per_expert(gidx), per_expert(gates), u2, w_gate[0], w_up[0], w_down[0])
    out = _combine(h1, y, slot, before, mod_late3, final_norm_w, seq, cap)
    return out.reshape(batch, seq, d)
```

```python
import functools

import jax
import jax.numpy as jnp
from jax import lax
from jax.experimental import pallas as pl
from jax.experimental.pallas import tpu as pltpu

GRID_W = 64
HEAD_DIM = 128
NA_HEADS = 8
RET_HEADS = 8
NA_WIDTH = NA_HEADS * HEAD_DIM
RET_WIDTH = RET_HEADS * HEAD_DIM
WIN_ROWS = 8
WIN_COLS = 16
RET_BLOCK = 256
ROPE_BASE = 10000.0
N_EXPERTS = 16
CAPACITY_FACTOR = 2
N_MOD = 6
MOD_EARLY = 2
MOD_LATE_GATE_MIX, MOD_LATE_SHIFT_FFN, MOD_LATE_SCALE_FFN, MOD_LATE_GATE_FFN = 0, 1, 2, 3
EPS = 1e-6
NEG_INF = -1e30
LOG2_E = 1.4426950408889634
KV_COLS = 2 * NA_WIDTH + 2 * RET_WIDTH

F32 = jnp.float32
BF16 = jnp.bfloat16
MIB = 1024 * 1024
VMEM_LIMIT_V7X = 56 * MIB


def _cparams(n_axes):
    return pltpu.CompilerParams(
        dimension_semantics=("arbitrary",) * n_axes, vmem_limit_bytes=VMEM_LIMIT_V7X)


def _silu(x):
    return x * jax.nn.sigmoid(x)


def _dot(a, b):
    return jnp.dot(a, b, preferred_element_type=F32)


def _dot_nt(a, b):
    return lax.dot_general(a, b, (((1,), (1,)), ((), ())), preferred_element_type=F32)


def _dot_tn(a, b):
    return lax.dot_general(a, b, (((0,), (0,)), ((), ())), preferred_element_type=F32)


def _mod_body(c_ref, w_ref, b_ref, o_ref):
    a = _silu(c_ref[...]).astype(BF16)
    o_ref[:, 0, :] = _dot(a, w_ref[...].astype(BF16)) + b_ref[...]


def _modulation(cc, w_mod, b_mod, n_cols, tn=1024):
    rows, d = cc.shape
    n = n_cols
    return pl.pallas_call(
        _mod_body,
        grid=(n // tn,),
        in_specs=[
            pl.BlockSpec((rows, d), lambda j: (0, 0)),
            pl.BlockSpec((d, tn), lambda j: (0, j)),
            pl.BlockSpec((1, tn), lambda j: (0, j)),
        ],
        out_specs=pl.BlockSpec((rows, 1, tn), lambda j: (0, 0, j)),
        out_shape=jax.ShapeDtypeStruct((rows, 1, n), F32),
        compiler_params=_cparams(1),
        name="modulation",
    )(cc, w_mod, b_mod.reshape(1, -1))


def _rms_mod(x, nw, shift, scale):
    ms = jnp.mean(x * x, axis=-1, keepdims=True)
    y = x * lax.rsqrt(ms + EPS) * nw
    return y * (1.0 + scale) + shift


INPROJ_NORM_SPLITS = 4


def _inproj_body(x_ref, nw_ref, sh_ref, sc_ref, w_ref, o_ref, u_ref):
    j = pl.program_id(1)
    tm = x_ref.shape[0]

    @pl.when(j == 0)
    def _():
        part = tm // INPROJ_NORM_SPLITS
        for r in range(INPROJ_NORM_SPLITS):
            rows = pl.ds(r * part, part)
            u = _rms_mod(x_ref[rows, :], nw_ref[...], sh_ref[0], sc_ref[0]).astype(BF16)
            u_ref[rows, :] = u
            o_ref[rows, :] = _dot(u, w_ref[...]).astype(o_ref.dtype)

    @pl.when(j > 0)
    def _():
        o_ref[...] = _dot(u_ref[...], w_ref[...]).astype(o_ref.dtype)


def _ctx_proj_body(x_ref, nw_ref, sh_ref, sc_ref, w_ref, o_ref, wb_ref, u_ref, *, kv_tiles):
    j = pl.program_id(0)

    @pl.when(j == 0)
    def _():
        u_ref[...] = _rms_mod(x_ref[...], nw_ref[...], sh_ref[0], sc_ref[0]).astype(BF16)

    wb_ref[...] = w_ref[...].astype(BF16)

    @pl.when(j < kv_tiles)
    def _():
        o_ref[...] = _dot(u_ref[...], wb_ref[...]).astype(o_ref.dtype)


def _ctx_projection(x2d, norm_w, mod3, mod_row, w_in, kv_cols, tn=512):
    rows, d = x2d.shape
    n = w_in.shape[1]
    kv_tiles = kv_cols // tn
    return pl.pallas_call(
        functools.partial(_ctx_proj_body, kv_tiles=kv_tiles),
        grid=(n // tn,),
        in_specs=[
            pl.BlockSpec((rows, d), lambda j: (0, 0)),
            pl.BlockSpec((1, d), lambda j: (0, 0)),
            pl.BlockSpec((1, 1, d), lambda j: (mod_row, 0, 0)),
            pl.BlockSpec((1, 1, d), lambda j: (mod_row, 0, 1)),
            pl.BlockSpec((d, tn), lambda j: (0, j)),
        ],
        out_specs=[
            pl.BlockSpec((rows, tn), lambda j: (0, jnp.minimum(j, kv_tiles - 1))),
            pl.BlockSpec((d, tn), lambda j: (0, j)),
        ],
        out_shape=[jax.ShapeDtypeStruct((rows, kv_cols), BF16), jax.ShapeDtypeStruct((d, n), BF16)],
        scratch_shapes=[pltpu.VMEM((rows, d), BF16)],
        compiler_params=_cparams(1),
        name="ctx_projection",
    )(x2d, norm_w.reshape(1, d), mod3, mod3, w_in)


def _in_projection(x2d, norm_w, mod3, mod_row_fn, w_in, n_cols, tm, tn):
    rows, d = x2d.shape
    return pl.pallas_call(
        _inproj_body,
        grid=(rows // tm, n_cols // tn),
        in_specs=[
            pl.BlockSpec((tm, d), lambda i, j: (i, 0)),
            pl.BlockSpec((1, d), lambda i, j: (0, 0)),
            pl.BlockSpec((1, 1, d), lambda i, j: (mod_row_fn(i), 0, 0)),
            pl.BlockSpec((1, 1, d), lambda i, j: (mod_row_fn(i), 0, 1)),
            pl.BlockSpec((d, tn), lambda i, j: (0, j)),
        ],
        out_specs=pl.BlockSpec((tm, tn), lambda i, j: (i, j)),
        out_shape=jax.ShapeDtypeStruct((rows, n_cols), BF16),
        scratch_shapes=[pltpu.VMEM((tm, d), BF16)],
        compiler_params=_cparams(2),
        name="in_projection",
    )(x2d, norm_w.reshape(1, d), mod3, mod3, w_in)


NA_QROWS = 4
NA_KROWS = NA_QROWS + WIN_ROWS


def _na_row_offset(tile_kind, i, w, rows):
    half = WIN_ROWS // 2
    if tile_kind == 0:
        r, key = i, w
    elif tile_kind == 1:
        r, key = NA_QROWS + i, NA_QROWS - half + w
    else:
        r, key = rows - NA_QROWS + i, rows - NA_KROWS + w
    start = min(max(r - half, 0), rows - WIN_ROWS)
    if not (start <= key < start + WIN_ROWS):
        return None
    return key - r + (WIN_ROWS - 1)


def _na_build_bias(rpb_ref, bias_ref, h, rows):
    w = GRID_W
    cq = lax.broadcasted_iota(jnp.int32, (w, 2 * w), 0)
    ck = lax.broadcasted_iota(jnp.int32, (w, 2 * w), 1) % w
    col_start = jnp.clip(cq - WIN_COLS // 2, 0, w - WIN_COLS)
    col_ok = (ck >= col_start) & (ck < col_start + WIN_COLS)
    col_off = jnp.clip(ck - cq, -(WIN_COLS - 1), WIN_COLS - 1) + (WIN_COLS - 1)
    neg = jnp.full((w, 2 * w), NEG_INF, F32)
    n_ro, n_co = 2 * WIN_ROWS - 1, 2 * WIN_COLS - 1
    tabs = []
    for ro in range(n_ro):
        t = jnp.zeros((w, 2 * w), F32)
        for j in range(n_co):
            t = jnp.where(col_off == j, rpb_ref[h, ro * n_co + j] * (HEAD_DIM ** 0.5), t)
        tabs.append(jnp.where(col_ok, t, neg))
    left = lax.broadcasted_iota(jnp.int32, (w, 2 * w), 1) < w
    for kind in range(3):
        for i in range(NA_QROWS):
            for wp in range(NA_KROWS // 2):
                ra = _na_row_offset(kind, i, 2 * wp, rows)
                rb = _na_row_offset(kind, i, 2 * wp + 1, rows)
                ta = neg if ra is None else tabs[ra]
                tb = neg if rb is None else tabs[rb]
                blk = ta if ra == rb else jnp.where(left, ta, tb)
                bias_ref[kind, i * w:(i + 1) * w, wp * 2 * w:(wp + 1) * 2 * w] = blk


def _na_body(rpb_ref, q_ref, k_ref, v_ref, ck_ref, cv_ref, o_ref, bias_ref, sa_ref, sb_ref, *, rows):
    h = pl.program_id(0)
    w = GRID_W
    tq, tk = NA_QROWS * w, NA_KROWS * w
    n_tiles = rows // NA_QROWS
    scale = HEAD_DIM ** -0.5

    @pl.when(pl.program_id(1) == 0)
    def _():
        _na_build_bias(rpb_ref, bias_ref, h, rows)

    def offsets(t):
        krow0 = jnp.clip(t * NA_QROWS - WIN_ROWS // 2, 0, rows - NA_KROWS)
        return pl.multiple_of(t * tq, tq), pl.multiple_of(krow0 * w, 4 * w)

    def scores(t, s_ref):
        kind = jnp.where(t == 0, 0, jnp.where(t == n_tiles - 1, 2, 1))
        q0, k0 = offsets(t)
        q = q_ref[pl.ds(q0, tq), :]
        s_ref[:, :tk] = _dot_nt(q, k_ref[pl.ds(k0, tk), :]) + bias_ref[kind]
        s_ref[:, tk:] = _dot_nt(q, ck_ref[...])

    def attend(t, s_ref):
        q0, k0 = offsets(t)
        s = s_ref[...]
        p = jnp.exp2((s - jnp.max(s, axis=-1, keepdims=True)) * (scale * LOG2_E))
        l = jnp.sum(p, axis=-1, keepdims=True)
        pb = p.astype(BF16)
        o = _dot(pb[:, :tk], v_ref[pl.ds(k0, tk), :]) + _dot(pb[:, tk:], cv_ref[...])
        o_ref[pl.ds(q0, tq), :] = (o / l).astype(o_ref.dtype)

    scores(jnp.int32(0), sa_ref)

    def pair(i, carry):
        t = 2 * i
        scores(t + 1, sb_ref)
        attend(t, sa_ref)
        scores(t + 2, sa_ref)
        attend(t + 1, sb_ref)
        return carry

    lax.fori_loop(0, n_tiles // 2 - 1, pair, 0, unroll=True)
    scores(jnp.int32(n_tiles - 1), sb_ref)
    attend(jnp.int32(n_tiles - 2), sa_ref)
    attend(jnp.int32(n_tiles - 1), sb_ref)


def _neighbourhood_attention(proj, cproj, rpb, batch, seq, ctx_len):
    rows = seq // GRID_W
    hb = NA_WIDTH // HEAD_DIM
    n_rpb = (2 * WIN_ROWS - 1) * (2 * WIN_COLS - 1)
    tq, tk = NA_QROWS * GRID_W, NA_KROWS * GRID_W
    grid_spec = pltpu.PrefetchScalarGridSpec(
        num_scalar_prefetch=0,
        grid=(NA_HEADS, batch),
        in_specs=[
            pl.BlockSpec(memory_space=pltpu.SMEM),
            pl.BlockSpec((seq, HEAD_DIM), lambda h, b: (b, 4 * hb + h)),
            pl.BlockSpec((seq, HEAD_DIM), lambda h, b: (b, h)),
            pl.BlockSpec((seq, HEAD_DIM), lambda h, b: (b, hb + h)),
            pl.BlockSpec((ctx_len, HEAD_DIM), lambda h, b: (b, h)),
            pl.BlockSpec((ctx_len, HEAD_DIM), lambda h, b: (b, hb + h)),
        ],
        out_specs=pl.BlockSpec((seq, HEAD_DIM), lambda h, b: (b, h)),
        scratch_shapes=[
            pltpu.VMEM((3, tq, tk), F32),
            pltpu.VMEM((tq, tk + ctx_len), F32),
            pltpu.VMEM((tq, tk + ctx_len), F32),
        ],
    )
    return pl.pallas_call(
        functools.partial(_na_body, rows=rows),
        grid_spec=grid_spec,
        out_shape=jax.ShapeDtypeStruct((batch * seq, NA_WIDTH), BF16),
        compiler_params=_cparams(2),
        name="neighbourhood_attention",
    )(rpb.reshape(NA_HEADS, n_rpb), proj, proj, proj, cproj, cproj)


def _log_sigmoid(x):
    return -(jnp.maximum(-x, 0.0) + jnp.log1p(jnp.exp(-jnp.abs(x))))


def _rope_partner_matrix():
    quarter = HEAD_DIM // 4
    src = lax.broadcasted_iota(jnp.int32, (HEAD_DIM, HEAD_DIM), 0)
    dst = lax.broadcasted_iota(jnp.int32, (HEAD_DIM, HEAD_DIM), 1)
    want = jnp.where(dst % (2 * quarter) < quarter, dst + quarter, dst - quarter)
    return (src == want).astype(BF16)


def _rope(x_bf16, partner_matrix, cos, sin_signed):
    return x_bf16.astype(F32) * cos + _dot(x_bf16, partner_matrix) * sin_signed


def _ret_body(df_ref, db_ref, gn_ref, cos_ref, sin_ref, q_ref, k_ref, v_ref, g_ref, ck_ref, cv_ref,
              cc_ref, wm_ref, bm_ref, wo_ref,
              o_ref, modl_ref, wob_ref, qr_ref, kr_ref, sf_ref, sb_ref):
    modl_ref[:, 0, :] = _dot(_silu(cc_ref[...]).astype(BF16), wm_ref[...].astype(BF16)) + bm_ref[...]
    wob_ref[...] = wo_ref[...].astype(BF16)

    c, d = RET_BLOCK, HEAD_DIM
    seq = q_ref.shape[0]
    n_chunks = seq // c
    ctx_len = ck_ref.shape[0]
    scale = HEAD_DIM ** -0.5
    head = pl.program_id(0)
    lgf_row = _log_sigmoid(jnp.full((1, HEAD_DIM), df_ref[head], F32))
    lgb_row = _log_sigmoid(jnp.full((1, HEAD_DIM), db_ref[head], F32))
    lgf = jnp.broadcast_to(lgf_row, (c, d))
    lgb = jnp.broadcast_to(lgb_row, (c, d))
    pos = lax.broadcasted_iota(jnp.int32, (c, d), 0).astype(F32)
    kdf = jnp.exp(lgf * (c - 1.0 - pos))
    kdb = jnp.exp(lgb * pos)
    qdf = jnp.exp(lgf * (pos + 1.0))
    qdb = jnp.exp(lgb * (c - pos))
    cdf = jnp.exp(lgf_row * float(c))
    cdb = jnp.exp(lgb_row * float(c))
    diff = (lax.broadcasted_iota(jnp.int32, (c, c), 0) - lax.broadcasted_iota(jnp.int32, (c, c), 1)).astype(F32)
    lgf_cc = jnp.broadcast_to(lgf_row[:, :1], (c, c))
    lgb_cc = jnp.broadcast_to(lgb_row[:, :1], (c, c))
    dmat = (jnp.where(diff >= 0, jnp.exp(lgf_cc * jnp.maximum(diff, 0.0)), 0.0)
            + jnp.where(diff <= 0, jnp.exp(lgb_cc * jnp.maximum(-diff, 0.0)), 0.0))

    cpos = lax.broadcasted_iota(jnp.int32, (ctx_len, d), 0).astype(F32)
    ckf = ck_ref[...].astype(F32) * scale
    cv = cv_ref[...]
    wf = jnp.exp(jnp.broadcast_to(lgf_row, (ctx_len, d)) * (ctx_len - 1.0 - cpos))
    wb = jnp.exp(jnp.broadcast_to(lgb_row, (ctx_len, d)) * cpos)
    s_f = _dot_tn((ckf * wf).astype(BF16), cv)
    s_b = _dot_tn((ckf * wb).astype(BF16), cv)

    rope_rows = 512
    pmat = _rope_partner_matrix()

    def rope_blk(i, carry):
        r0 = pl.multiple_of(i * rope_rows, rope_rows)
        cs = cos_ref[pl.ds(r0, rope_rows), :]
        sn = sin_ref[pl.ds(r0, rope_rows), :]
        qr_ref[pl.ds(r0, rope_rows), :] = _rope(q_ref[pl.ds(r0, rope_rows), :], pmat, cs, sn)
        kr_ref[pl.ds(r0, rope_rows), :] = _rope(k_ref[pl.ds(r0, rope_rows), :], pmat, cs, sn) * scale
        return carry

    lax.fori_loop(0, seq // rope_rows, rope_blk, 0, unroll=4)

    def scan_blk(n, carry):
        s, t = carry
        r0 = pl.multiple_of(n * c, c)
        kvf = _dot_tn((kr_ref[pl.ds(r0, c), :] * kdf).astype(BF16), v_ref[pl.ds(r0, c), :])
        sf_ref[n] = s.astype(BF16)
        m = n_chunks - 1 - n
        m0 = pl.multiple_of(m * c, c)
        kvb = _dot_tn((kr_ref[pl.ds(m0, c), :] * kdb).astype(BF16), v_ref[pl.ds(m0, c), :])
        sb_ref[m] = t.astype(BF16)
        return s * cdf + kvf, t * cdb + kvb

    lax.fori_loop(0, n_chunks, scan_blk, (s_f, s_b), unroll=16)

    gn = gn_ref[...]

    def out_blk(n, carry):
        r0 = pl.multiple_of(n * c, c)
        qc = qr_ref[pl.ds(r0, c), :]
        kc = kr_ref[pl.ds(r0, c), :]
        a = _dot_nt(qc.astype(BF16), kc.astype(BF16))
        o = (_dot((a * dmat).astype(BF16), v_ref[pl.ds(r0, c), :])
             + _dot((qc * qdf).astype(BF16), sf_ref[n])
             + _dot((qc * qdb).astype(BF16), sb_ref[n]))
        mu = jnp.mean(o, axis=-1, keepdims=True)
        var = jnp.mean(jnp.square(o - mu), axis=-1, keepdims=True)
        y = (o - mu) * lax.rsqrt(var + EPS) * gn
        o_ref[pl.ds(r0, c), :] = (y * _silu(g_ref[pl.ds(r0, c), :].astype(F32))).astype(o_ref.dtype)
        return carry

    lax.fori_loop(0, n_chunks, out_blk, 0, unroll=16)


def _rope_tables(seq):
    axis_dim = HEAD_DIM // 2
    inv_freq = ROPE_BASE ** (-jnp.arange(0, axis_dim, 2, dtype=F32) / axis_dim)
    rows = seq // GRID_W
    ang_r = jnp.arange(rows, dtype=F32)[:, None] * inv_freq
    ang_c = jnp.arange(GRID_W, dtype=F32)[:, None] * inv_freq
    by_row = lambda a: jnp.repeat(a, GRID_W, axis=0)
    by_col = lambda a: jnp.tile(a, (rows, 1))
    cr, sr, cc, sc = by_row(jnp.cos(ang_r)), by_row(jnp.sin(ang_r)), by_col(jnp.cos(ang_c)), by_col(jnp.sin(ang_c))
    return jnp.concatenate([cr, cr, cc, cc], axis=-1), jnp.concatenate([-sr, sr, -sc, sc], axis=-1)


def _retention(proj, cproj, decay_f, decay_b, gn_w, batch, seq, ctx_len, cc, w_mod, b_mod, mod_from, w_out):
    hb = RET_WIDTH // HEAD_DIM
    cos, sin = _rope_tables(seq)
    n_chunks = seq // RET_BLOCK
    blk = lambda rows, fn: pl.BlockSpec((rows, HEAD_DIM), fn)
    n_steps = RET_HEADS * batch
    step = lambda h, b: h * batch + b
    mod_rows, d = cc.shape
    late = w_mod.shape[1] - mod_from
    mcols, orows = late // n_steps, w_out.shape[0] // n_steps
    side_in = [
        pl.BlockSpec((mod_rows, d), lambda h, b: (0, 0)),
        pl.BlockSpec((d, mcols), lambda h, b: (0, mod_from // mcols + step(h, b))),
        pl.BlockSpec((1, mcols), lambda h, b: (0, mod_from // mcols + step(h, b))),
        pl.BlockSpec((orows, w_out.shape[1]), lambda h, b: (step(h, b), 0)),
    ]
    side_out = [
        pl.BlockSpec((mod_rows, 1, mcols), lambda h, b: (0, 0, step(h, b))),
        pl.BlockSpec((orows, w_out.shape[1]), lambda h, b: (step(h, b), 0)),
    ]
    side_shape = [jax.ShapeDtypeStruct((mod_rows, 1, late), F32), jax.ShapeDtypeStruct(w_out.shape, BF16)]
    return pl.pallas_call(
        _ret_body,
        grid=(RET_HEADS, batch),
        in_specs=[
            pl.BlockSpec(memory_space=pltpu.SMEM),
            pl.BlockSpec(memory_space=pltpu.SMEM),
            blk(1, lambda h, b: (0, h)),
            blk(seq, lambda h, b: (0, 0)),
            blk(seq, lambda h, b: (0, 0)),
            blk(seq, lambda h, b: (b, 5 * hb + h)),
            blk(seq, lambda h, b: (b, 2 * hb + h)),
            blk(seq, lambda h, b: (b, 3 * hb + h)),
            blk(seq, lambda h, b: (b, 6 * hb + h)),
            blk(ctx_len, lambda h, b: (b, 2 * hb + h)),
            blk(ctx_len, lambda h, b: (b, 3 * hb + h)),
        ] + side_in,
        out_specs=[blk(seq, lambda h, b: (b, h))] + side_out,
        out_shape=[jax.ShapeDtypeStruct((batch * seq, RET_WIDTH), BF16)] + side_shape,
        scratch_shapes=[
            pltpu.VMEM((seq, HEAD_DIM), F32),
            pltpu.VMEM((seq, HEAD_DIM), F32),
            pltpu.VMEM((n_chunks, HEAD_DIM, HEAD_DIM), BF16),
            pltpu.VMEM((n_chunks, HEAD_DIM, HEAD_DIM), BF16),
        ],
        compiler_params=_cparams(2),
        name="retention",
    )(decay_f.astype(F32), decay_b.astype(F32), gn_w.reshape(1, RET_WIDTH), cos, sin, proj, proj, proj, proj, cproj, cproj,
      cc, w_mod, b_mod.reshape(1, -1), w_out)


def _split_bf16(x):
    hi = x.astype(BF16)
    lo = (x - hi.astype(F32)).astype(BF16)
    return hi, lo


OUTPROJ_ROW_SPLITS = 2


def _outproj_body(na_ref, ret_ref, w_ref, x_ref, gate_ref, nw_ref, sh_ref, sc_ref, wr_ref,
                  h1_ref, u2_ref, lg_ref):
    half = na_ref.shape[1]
    n_e = wr_ref.shape[0]
    tm = x_ref.shape[0]
    wh, wl = _split_bf16(wr_ref[...])
    w_router = jnp.concatenate([wh, wl], axis=0)
    for r in range(OUTPROJ_ROW_SPLITS):
        rows = pl.ds(r * (tm // OUTPROJ_ROW_SPLITS), tm // OUTPROJ_ROW_SPLITS)
        mix = _dot(na_ref[rows, :], w_ref[:half, :]) + _dot(ret_ref[rows, :], w_ref[half:, :])
        h1 = x_ref[rows, :] + gate_ref[0] * mix
        h1_ref[rows, :] = h1
        u2 = _rms_mod(h1, nw_ref[...], sh_ref[0], sc_ref[0])
        u2_ref[rows, :] = u2
        both = _dot_nt(w_router, u2.astype(BF16))
        lg_ref[:, rows] = both[:n_e] + both[n_e:]


def _out_projection(na, ret, w_out_bf16, x2d, mod3, norm_w, w_router_t, seq, tm=512):
    rows, d = x2d.shape
    half = na.shape[1]
    per_b = seq // tm
    return pl.pallas_call(
        _outproj_body,
        grid=(rows // tm,),
        in_specs=[
            pl.BlockSpec((tm, half), lambda i: (i, 0)),
            pl.BlockSpec((tm, half), lambda i: (i, 0)),
            pl.BlockSpec((2 * half, d), lambda i: (0, 0)),
            pl.BlockSpec((tm, d), lambda i: (i, 0)),
            pl.BlockSpec((1, 1, d), lambda i: (i // per_b, 0, MOD_LATE_GATE_MIX)),
            pl.BlockSpec((1, d), lambda i: (0, 0)),
            pl.BlockSpec((1, 1, d), lambda i: (i // per_b, 0, MOD_LATE_SHIFT_FFN)),
            pl.BlockSpec((1, 1, d), lambda i: (i // per_b, 0, MOD_LATE_SCALE_FFN)),
            pl.BlockSpec((N_EXPERTS, d), lambda i: (0, 0)),
        ],
        out_specs=[
            pl.BlockSpec((tm, d), lambda i: (i, 0)),
            pl.BlockSpec((tm, d), lambda i: (i, 0)),
            pl.BlockSpec((N_EXPERTS, tm), lambda i: (0, i)),
        ],
        out_shape=[
            jax.ShapeDtypeStruct((rows, d), F32),
            jax.ShapeDtypeStruct((rows, d), F32),
            jax.ShapeDtypeStruct((N_EXPERTS, rows), F32),
        ],
        compiler_params=_cparams(1),
        name="out_projection",
    )(na, ret, w_out_bf16, x2d, mod3, norm_w.reshape(1, d), mod3, mod3, w_router_t)


LANES = 128
SUBLANES = 8


def _prefix_incl_lanes(x, tri):
    r, l = x.shape
    nb = l // LANES
    xs = jnp.concatenate([x[:, t * LANES:(t + 1) * LANES] for t in range(nb)], axis=0).astype(BF16)
    p = _dot(xs, tri)
    outs, run = [], jnp.zeros((r, 1), F32)
    for t in range(nb):
        blk = p[t * r:(t + 1) * r] + run
        outs.append(blk)
        run = blk[:, LANES - 1:LANES]
    return jnp.concatenate(outs, axis=1)


NOT_ROUTED = -(1 << 20)


def _route_body(lg_ref, gidx_ref, gate_ref, slot_ref, before_ref, *, cap):
    b = pl.program_id(0)
    n_e, seq = lg_ref.shape
    kf = float(cap)
    lg = lg_ref[...]
    ex = jnp.exp(lg - jnp.max(lg, axis=0, keepdims=True))
    aff = ex / jnp.sum(ex, axis=0, keepdims=True)

    def cond(c):
        return (c[0] < 4096) & (c[5] > 0.5)

    def step(c):
        it, lo, hi, thr, done, _ = c
        mid = 0.5 * (lo + hi)
        above = jnp.sum((aff > mid).astype(F32), axis=1, keepdims=True)
        hit = above == kf
        stuck = (mid <= lo) | (mid >= hi)
        active = done < 0.5
        thr = jnp.where(active & hit, mid, jnp.where(active & stuck, hi, thr))
        go = active & ~(hit | stuck)
        ge = above >= kf
        lo = jnp.where(go & ge, mid, lo)
        hi = jnp.where(go & ~ge, mid, hi)
        done = jnp.where(active & (hit | stuck), 1.0, done)
        return it + 1, lo, hi, thr, done, jnp.sum(1.0 - done)

    col = lambda v: jnp.full((n_e, 1), v, F32)
    init = (jnp.int32(0), col(-1.0), col(2.0), col(0.0), col(0.0), jnp.float32(n_e))
    thr = lax.while_loop(cond, step, init)[3]

    ii = lax.broadcasted_iota(jnp.int32, (LANES, LANES), 0)
    jj = lax.broadcasted_iota(jnp.int32, (LANES, LANES), 1)
    tri = (ii <= jj).astype(BF16)
    gt = aff > thr
    eq = (aff == thr).astype(F32)
    need = kf - jnp.sum(gt.astype(F32), axis=1, keepdims=True)
    eq_before = _prefix_incl_lanes(eq, tri) - eq
    mask = jnp.where(gt | ((eq > 0.5) & (eq_before < need)), 1.0, 0.0)

    slot = _prefix_incl_lanes(mask, tri) - mask
    before_ref[0] = slot.astype(jnp.int32)
    slot_ref[0] = jnp.where(mask > 0.5, slot, float(NOT_ROUTED)).astype(jnp.int32)

    tok = lax.broadcasted_iota(jnp.int32, (1, seq), 1).astype(F32)
    tok_hi = jnp.floor(tok * (1.0 / 64))
    tok_lo = tok - 64.0 * tok_hi
    n_hi = ROUTE_SLOT_HI
    n_lo = cap // n_hi
    hi_iota = lax.broadcasted_iota(jnp.int32, (n_hi, seq), 0).astype(F32)
    lo_iota = lax.broadcasted_iota(jnp.int32, (n_lo, seq), 0).astype(F32)
    for e in range(n_e):
        se = slot[e:e + 1]
        s_hi = jnp.floor(se * (1.0 / n_lo))
        s_lo = se - n_lo * s_hi
        in_hi = jnp.where((s_hi == hi_iota) & (mask[e:e + 1] > 0.5), 1.0, 0.0)
        in_lo = (s_lo == lo_iota).astype(BF16)
        a = aff[e:e + 1]
        a_hi = a.astype(BF16).astype(F32)
        a_mid = (a - a_hi).astype(BF16).astype(F32)
        a_lo = a - a_hi - a_mid
        vals = (tok_hi, tok_lo, a_hi, a_mid, a_lo)
        lhs = jnp.concatenate([in_hi * v for v in vals], axis=0).astype(BF16)
        got = _dot_nt(lhs, in_lo)
        part = lambda r: got[r * n_hi:(r + 1) * n_hi]
        gidx_ref[0, e] = (part(0) * 64.0 + part(1)).astype(jnp.int32) + b * seq
        gate_ref[0, e] = part(2) + part(3) + part(4)


ROUTE_SLOT_HI = 16


def _routing(logits_t, batch, seq):
    n_e = logits_t.shape[0]
    cap = CAPACITY_FACTOR * seq // n_e
    n_hi, n_lo = ROUTE_SLOT_HI, cap // ROUTE_SLOT_HI
    bec = pl.BlockSpec((1, n_e, n_hi, n_lo), lambda b: (b, 0, 0, 0))
    bel = pl.BlockSpec((1, n_e, seq), lambda b: (b, 0, 0))
    gidx, gates, slot, before = pl.pallas_call(
        functools.partial(_route_body, cap=cap),
        grid=(batch,),
        in_specs=[pl.BlockSpec((n_e, seq), lambda b: (0, b))],
        out_specs=[bec, bec, bel, bel],
        out_shape=[
            jax.ShapeDtypeStruct((batch, n_e, n_hi, n_lo), jnp.int32),
            jax.ShapeDtypeStruct((batch, n_e, n_hi, n_lo), F32),
            jax.ShapeDtypeStruct((batch, n_e, seq), jnp.int32),
            jax.ShapeDtypeStruct((batch, n_e, seq), jnp.int32),
        ],
        compiler_params=_cparams(1),
        name="routing",
    )(logits_t)
    flat = lambda a: a.reshape(batch, n_e, cap)
    return flat(gidx), flat(gates), slot, before


def _hbm_row(ref, r):
    return ref.at[lax.shift_right_logical(r, 3), pl.ds(r & (SUBLANES - 1), 1)]


MOE_ROW_SPLITS = 2


def _moe_body(gidx_ref, gnext_ref, gate_ref, u2_hbm, wg_ref, wu_ref, wd_ref, y_ref,
              stage_ref, xe_ref, acc_ref, gsem, *, n_f):
    e = pl.program_id(0)
    f = pl.program_id(1)
    n_e = pl.num_programs(0)
    m, d = y_ref.shape
    sub = SUBLANES
    n_tiles = m // sub
    assert n_f >= 2
    per_step = -(-n_tiles // (n_f - 1))
    last_count = n_tiles - per_step * (n_f - 2)

    def gather_copy(idx_ref, i, k):
        return pltpu.make_async_copy(_hbm_row(u2_hbm, idx_ref[0, i * sub + k]), stage_ref.at[i, pl.ds(k, 1)], gsem)

    def for_rows(fn, tiles_per_trip=4):
        def body(t, c):
            for kk in range(tiles_per_trip * sub):
                fn(t * tiles_per_trip + kk // sub, kk % sub)
            return c
        lax.fori_loop(0, n_tiles // tiles_per_trip, body, 0)

    wait_gather = lambda: for_rows(lambda i, k: gather_copy(gidx_ref, i, k).wait())

    def ffn_step(first, gather_tiles):
        i0 = (f - 1) * per_step
        for kk in range(gather_tiles * sub):
            gather_copy(gnext_ref, i0 + kk // sub, kk % sub).start()
        wg, wu, wd = wg_ref[...].astype(BF16), wu_ref[...].astype(BF16), wd_ref[...].astype(BF16)
        part = m // MOE_ROW_SPLITS
        for r in range(MOE_ROW_SPLITS):
            rows = pl.ds(r * part, part)
            if first:
                x = stage_ref[pl.ds(r * (part // sub), part // sub)].reshape(part, d).astype(BF16)
                xe_ref[rows, :] = x
            else:
                x = xe_ref[rows, :]
            hid = _silu(_dot(x, wg)) * _dot(x, wu)
            out = _dot(hid.astype(BF16), wd)
            acc_ref[rows, :] = out if first else acc_ref[rows, :] + out

    @pl.when(f == 0)
    def _():
        @pl.when(e == 0)
        def _():
            for_rows(lambda i, k: gather_copy(gidx_ref, i, k).start())

        wait_gather()
        ffn_step(True, 0)

    if n_f > 2:
        pl.when((f > 0) & (f < n_f - 1))(lambda: ffn_step(False, per_step))

    @pl.when(f == n_f - 1)
    def _():
        ffn_step(False, last_count)

        @pl.when(f > 0)
        def _():
            g = jnp.transpose(jnp.broadcast_to(gate_ref[...], (LANES, m)))
            for k in range(d // LANES):
                cols = slice(k * LANES, (k + 1) * LANES)
                y_ref[:, cols] = (acc_ref[:, cols] * g).astype(y_ref.dtype)

        @pl.when(e == n_e - 1)
        def _():
            wait_gather()


def _moe_ffn(gidx, gates, u2, w_gate, w_up, w_down, tf=256):
    n_e, _, m = gidx.shape
    rows, d = u2.shape
    ff = w_gate.shape[2]
    n_f = ff // tf
    sub = SUBLANES
    smem = lambda fn: pl.BlockSpec((None, 1, m), fn, memory_space=pltpu.SMEM)
    return pl.pallas_call(
        functools.partial(_moe_body, n_f=n_f),
        grid=(n_e, n_f),
        in_specs=[
            smem(lambda e, f: (e, 0, 0)),
            smem(lambda e, f: (jnp.minimum(e + 1, n_e - 1), 0, 0)),
            pl.BlockSpec((None, 1, m), lambda e, f: (e, 0, 0)),
            pl.BlockSpec(memory_space=pl.ANY),
            pl.BlockSpec((None, d, tf), lambda e, f: (e, 0, f)),
            pl.BlockSpec((None, d, tf), lambda e, f: (e, 0, f)),
            pl.BlockSpec((None, tf, d), lambda e, f: (e, f, 0)),
        ],
        out_specs=pl.BlockSpec((m, d), lambda e, f: (e, 0)),
        out_shape=jax.ShapeDtypeStruct((n_e * m, d), BF16),
        scratch_shapes=[
            pltpu.VMEM((m // sub, sub, d), F32),
            pltpu.VMEM((m, d), BF16),
            pltpu.VMEM((m, d), F32),
            pltpu.SemaphoreType.DMA,
        ],
        compiler_params=_cparams(2),
        name="moe_ffn",
    )(gidx, gidx, gates, u2.reshape(rows // sub, sub, d), w_gate, w_up, w_down)


COMBINE_TOKENS = 512
COMBINE_PIECE = 96
BF16_ROWS = 16


def _combine_body(start_ref, npiece_ref, h1_ref, y_hbm, slot_ref, gate_ref, nw_ref, o_ref,
                  acc_ref, buf_ref, xbuf_ref, sem, xsem, *, m, cap, per_b):
    i = pl.program_id(0)
    n_tiles = pl.num_programs(0)
    n_e = slot_ref.shape[1]
    t, p = COMBINE_TOKENS, COMBINE_PIECE
    total = n_e * m
    cur = i % 2
    b = i // per_b

    def piece_copy(tile, e, dst):
        st = pl.multiple_of(start_ref[tile * n_e + e], BF16_ROWS)
        return pltpu.make_async_copy(y_hbm.at[pl.ds(st, p)], buf_ref.at[dst, pl.ds(e * p, p)], sem.at[dst])

    @pl.when(i == 0)
    def _():
        for e in range(n_e):
            piece_copy(0, e, 0).start()

    @pl.when(i + 1 < n_tiles)
    def _():
        for e in range(n_e):
            piece_copy(i + 1, e, 1 - cur).start()

    for e in range(n_e):
        piece_copy(i, e, cur).wait()

    slots = slot_ref[0]
    riota = lax.broadcasted_iota(jnp.int32, (p, t), 0)

    def local_row(e, st):
        return slots[e:e + 1, :] + (e * m + b * cap - st)

    sel = jnp.concatenate(
        [(local_row(e, start_ref[i * n_e + e]) == riota).astype(BF16) for e in range(n_e)], axis=0)
    acc_ref[...] = _dot_tn(sel, buf_ref[cur])

    for e in range(n_e):
        st0 = start_ref[i * n_e + e]

        def extra(k, carry, e=e, st0=st0):
            want = st0 + k * p
            st = pl.multiple_of(jnp.minimum(want, total - p), BF16_ROWS)
            cp = pltpu.make_async_copy(y_hbm.at[pl.ds(st, p)], xbuf_ref, xsem)
            cp.start()
            cp.wait()
            blk = ((local_row(e, st) == riota) & (riota >= want - st)).astype(BF16)
            acc_ref[...] += _dot_tn(blk, xbuf_ref[...])
            return carry

        lax.fori_loop(1, npiece_ref[i * n_e + e], extra, 0)

    h2 = h1_ref[...] + gate_ref[0] * acc_ref[...]
    ms = jnp.mean(h2 * h2, axis=-1, keepdims=True)
    o_ref[...] = h2 * lax.rsqrt(ms + EPS) * nw_ref[...]


def _combine_pieces(before, cap):
    batch, n_e, seq = before.shape
    t, p = COMBINE_TOKENS, COMBINE_PIECE
    m = batch * cap
    total = n_e * m
    bounds = jnp.concatenate([before[:, :, ::t], jnp.full((batch, n_e, 1), cap, jnp.int32)], axis=2)
    base = (jnp.arange(n_e, dtype=jnp.int32) * m)[None, :, None] + (jnp.arange(batch, dtype=jnp.int32) * cap)[:, None, None]
    first = base + bounds[:, :, :-1]
    end = base + bounds[:, :, 1:]
    start = jnp.minimum(first // BF16_ROWS * BF16_ROWS, total - p)
    npiece = jnp.maximum((end - start + p - 1) // p, 1)
    by_tile = lambda a: a.transpose(0, 2, 1).reshape(-1)
    return by_tile(start), by_tile(npiece)


def _combine(h1, y, slot, before, mod3, final_w, seq, cap):
    rows, d = h1.shape
    batch, n_e, _ = slot.shape
    t, p = COMBINE_TOKENS, COMBINE_PIECE
    per_b = seq // t
    start, npiece = _combine_pieces(before, cap)
    grid_spec = pltpu.PrefetchScalarGridSpec(
        num_scalar_prefetch=2,
        grid=(rows // t,),
        in_specs=[
            pl.BlockSpec((t, d), lambda i, st, npc: (i, 0)),
            pl.BlockSpec(memory_space=pl.ANY),
            pl.BlockSpec((1, n_e, t), lambda i, st, npc: (i // per_b, 0, i % per_b)),
            pl.BlockSpec((1, 1, d), lambda i, st, npc: (i // per_b, 0, MOD_LATE_GATE_FFN)),
            pl.BlockSpec((1, d), lambda i, st, npc: (0, 0)),
        ],
        out_specs=pl.BlockSpec((t, d), lambda i, st, npc: (i, 0)),
        scratch_shapes=[
            pltpu.VMEM((t, d), F32),
            pltpu.VMEM((2, n_e * p, d), y.dtype),
            pltpu.VMEM((p, d), y.dtype),
            pltpu.SemaphoreType.DMA((2,)),
            pltpu.SemaphoreType.DMA,
        ],
    )
    return pl.pallas_call(
        functools.partial(_combine_body, m=batch * cap, cap=cap, per_b=per_b),
        grid_spec=grid_spec,
        out_shape=jax.ShapeDtypeStruct((rows, d), F32),
        compiler_params=_cparams(1),
        name="combine",
    )(start, npiece, h1, y, slot, mod3, final_w.reshape(1, d))


def kernel(x, c, ctx, c_ctx, w_mod, b_mod, norm_mix_w, norm_ffn_w, w_in, na_rpb, ret_decay_fwd,
           ret_decay_bwd, ret_gn_w, w_out, w_router, w_gate, w_up, w_down, final_norm_w):
    batch, seq, d = x.shape
    ctx_len = ctx.shape[1]
    assert w_mod.shape[0] == 1, "one trunk layer"
    assert seq % (NA_QROWS * GRID_W) == 0 and seq // GRID_W >= 3 * NA_QROWS
    n_e = w_router.shape[2]
    cap = CAPACITY_FACTOR * seq // n_e

    mod_rows = 8
    cc = jnp.concatenate([c, c_ctx[None], jnp.zeros((mod_rows - batch - 1, d), c.dtype)], axis=0)
    mod3 = _modulation(cc, w_mod[0], b_mod[0], MOD_EARLY * d)

    x2d = x.reshape(batch * seq, d)
    tm = 1024
    cproj, w_in_bf16 = _ctx_projection(ctx.reshape(batch * ctx_len, d), norm_mix_w[0], mod3, batch,
                                       w_in[0], KV_COLS)
    proj = _in_projection(x2d, norm_mix_w[0], mod3, lambda i: i // (seq // tm), w_in_bf16,
                          w_in.shape[2], tm, w_in.shape[2] // 4)

    na = _neighbourhood_attention(proj, cproj, na_rpb[0], batch, seq, ctx_len)
    ret, mod_late3, w_out_bf16 = _retention(proj, cproj, ret_decay_fwd[0], ret_decay_bwd[0], ret_gn_w[0],
                                            batch, seq, ctx_len, cc, w_mod[0], b_mod[0], MOD_EARLY * d, w_out[0])

    h1, u2, logits_t = _out_projection(na, ret, w_out_bf16, x2d, mod_late3, norm_ffn_w[0], w_router[0].T, seq)

    gidx, gates, slot, before = _routing(logits_t, batch, seq)
    per_expert = lambda a: a.transpose(1, 0, 2).reshape(n_e, 1, batch * cap)
    y = _moe_ffn(per_expert(gidx), per_expert(gates), u2, w_gate[0], w_up[0], w_down[0])
    out = _combine(h1, y, slot, before, mod_late3, final_norm_w, seq, cap)
    return out.reshape(batch, seq, d)
```

```python
import functools

import jax
import jax.numpy as jnp
from jax import lax
from jax.experimental import pallas as pl
from jax.experimental.pallas import tpu as pltpu

GRID_W = 64
HEAD_DIM = 128
NA_HEADS = 8
RET_HEADS = 8
NA_WIDTH = NA_HEADS * HEAD_DIM
RET_WIDTH = RET_HEADS * HEAD_DIM
WIN_ROWS = 8
WIN_COLS = 16
RET_BLOCK = 256
ROPE_BASE = 10000.0
N_EXPERTS = 16
CAPACITY_FACTOR = 2
N_MOD = 6
MOD_EARLY = 2
MOD_LATE_GATE_MIX, MOD_LATE_SHIFT_FFN, MOD_LATE_SCALE_FFN, MOD_LATE_GATE_FFN = 0, 1, 2, 3
EPS = 1e-6
NEG_INF = -1e30
LOG2_E = 1.4426950408889634
KV_COLS = 2 * NA_WIDTH + 2 * RET_WIDTH

F32 = jnp.float32
BF16 = jnp.bfloat16
LANES = 128
SUBLANES = 8
BF16_ROWS = 16
ROUTE_SLOT_HI = 16
MIB = 1024 * 1024
VMEM_LIMIT_V7X = 56 * MIB

MOD_COL_TILE = 1024
CTX_COL_TILE = 1024
INPROJ_ROW_TILE = 1024
INPROJ_COL_TILES = 4
OUTPROJ_ROW_TILE = 512
MOE_F_TILE = 256


def _cparams(n_axes):
    return pltpu.CompilerParams(
        dimension_semantics=("arbitrary",) * n_axes, vmem_limit_bytes=VMEM_LIMIT_V7X)


def _silu(x):
    return x * jax.nn.sigmoid(x)


def _dot(a, b):
    return jnp.dot(a, b, preferred_element_type=F32)


def _dot_nt(a, b):
    return lax.dot_general(a, b, (((1,), (1,)), ((), ())), preferred_element_type=F32)


def _dot_tn(a, b):
    return lax.dot_general(a, b, (((0,), (0,)), ((), ())), preferred_element_type=F32)


def _mod_body(c_ref, w_ref, b_ref, o_ref):
    a = _silu(c_ref[...]).astype(BF16)
    o_ref[:, 0, :] = _dot(a, w_ref[...].astype(BF16)) + b_ref[...]


def _modulation(cc, w_mod, b_mod, n_cols, tn=MOD_COL_TILE):
    rows, d = cc.shape
    n = n_cols
    return pl.pallas_call(
        _mod_body,
        grid=(n // tn,),
        in_specs=[
            pl.BlockSpec((rows, d), lambda j: (0, 0)),
            pl.BlockSpec((d, tn), lambda j: (0, j)),
            pl.BlockSpec((1, tn), lambda j: (0, j)),
        ],
        out_specs=pl.BlockSpec((rows, 1, tn), lambda j: (0, 0, j)),
        out_shape=jax.ShapeDtypeStruct((rows, 1, n), F32),
        compiler_params=_cparams(1),
        name="modulation",
    )(cc, w_mod, b_mod.reshape(1, -1))


def _rms_mod(x, nw, shift, scale):
    ms = jnp.mean(x * x, axis=-1, keepdims=True)
    y = x * lax.rsqrt(ms + EPS) * nw
    return y * (1.0 + scale) + shift


INPROJ_NORM_SPLITS = 4


def _inproj_body(x_ref, nw_ref, sh_ref, sc_ref, w_ref, o_ref, u_ref):
    j = pl.program_id(1)
    tm = x_ref.shape[0]

    @pl.when(j == 0)
    def _():
        part = tm // INPROJ_NORM_SPLITS
        for r in range(INPROJ_NORM_SPLITS):
            rows = pl.ds(r * part, part)
            u = _rms_mod(x_ref[rows, :], nw_ref[...], sh_ref[0], sc_ref[0]).astype(BF16)
            u_ref[rows, :] = u
            o_ref[rows, :] = _dot(u, w_ref[...]).astype(o_ref.dtype)

    @pl.when(j > 0)
    def _():
        o_ref[...] = _dot(u_ref[...], w_ref[...]).astype(o_ref.dtype)


def _ctx_proj_body(x_ref, nw_ref, sh_ref, sc_ref, w_ref, o_ref, wb_ref, u_ref, *, kv_tiles):
    j = pl.program_id(0)

    @pl.when(j == 0)
    def _():
        u_ref[...] = _rms_mod(x_ref[...], nw_ref[...], sh_ref[0], sc_ref[0]).astype(BF16)

    wb_ref[...] = w_ref[...].astype(BF16)

    @pl.when(j < kv_tiles)
    def _():
        o_ref[...] = _dot(u_ref[...], wb_ref[...]).astype(o_ref.dtype)


def _ctx_projection(x2d, norm_w, mod3, mod_row, w_in, kv_cols, tn=CTX_COL_TILE):
    rows, d = x2d.shape
    n = w_in.shape[1]
    kv_tiles = kv_cols // tn
    return pl.pallas_call(
        functools.partial(_ctx_proj_body, kv_tiles=kv_tiles),
        grid=(n // tn,),
        in_specs=[
            pl.BlockSpec((rows, d), lambda j: (0, 0)),
            pl.BlockSpec((1, d), lambda j: (0, 0)),
            pl.BlockSpec((1, 1, d), lambda j: (mod_row, 0, 0)),
            pl.BlockSpec((1, 1, d), lambda j: (mod_row, 0, 1)),
            pl.BlockSpec((d, tn), lambda j: (0, j)),
        ],
        out_specs=[
            pl.BlockSpec((rows, tn), lambda j: (0, jnp.minimum(j, kv_tiles - 1))),
            pl.BlockSpec((d, tn), lambda j: (0, j)),
        ],
        out_shape=[jax.ShapeDtypeStruct((rows, kv_cols), BF16), jax.ShapeDtypeStruct((d, n), BF16)],
        scratch_shapes=[pltpu.VMEM((rows, d), BF16)],
        compiler_params=_cparams(1),
        name="ctx_projection",
    )(x2d, norm_w.reshape(1, d), mod3, mod3, w_in)


def _in_projection(x2d, norm_w, mod3, mod_row_fn, w_in, n_cols, tm, tn):
    rows, d = x2d.shape
    return pl.pallas_call(
        _inproj_body,
        grid=(rows // tm, n_cols // tn),
        in_specs=[
            pl.BlockSpec((tm, d), lambda i, j: (i, 0)),
            pl.BlockSpec((1, d), lambda i, j: (0, 0)),
            pl.BlockSpec((1, 1, d), lambda i, j: (mod_row_fn(i), 0, 0)),
            pl.BlockSpec((1, 1, d), lambda i, j: (mod_row_fn(i), 0, 1)),
            pl.BlockSpec((d, tn), lambda i, j: (0, j)),
        ],
        out_specs=pl.BlockSpec((tm, tn), lambda i, j: (i, j)),
        out_shape=jax.ShapeDtypeStruct((rows, n_cols), BF16),
        scratch_shapes=[pltpu.VMEM((tm, d), BF16)],
        compiler_params=_cparams(2),
        name="in_projection",
    )(x2d, norm_w.reshape(1, d), mod3, mod3, w_in)


NA_QROWS = 4
NA_KROWS = NA_QROWS + WIN_ROWS


def _na_row_offset(tile_kind, i, w, rows):
    half = WIN_ROWS // 2
    if tile_kind == 0:
        r, key = i, w
    elif tile_kind == 1:
        r, key = NA_QROWS + i, NA_QROWS - half + w
    else:
        r, key = rows - NA_QROWS + i, rows - NA_KROWS + w
    start = min(max(r - half, 0), rows - WIN_ROWS)
    if not (start <= key < start + WIN_ROWS):
        return None
    return key - r + (WIN_ROWS - 1)


def _na_build_bias(rpb_ref, bias_ref, h, rows):
    w = GRID_W
    cq = lax.broadcasted_iota(jnp.int32, (w, 2 * w), 0)
    ck = lax.broadcasted_iota(jnp.int32, (w, 2 * w), 1) % w
    col_start = jnp.clip(cq - WIN_COLS // 2, 0, w - WIN_COLS)
    col_ok = (ck >= col_start) & (ck < col_start + WIN_COLS)
    col_off = jnp.clip(ck - cq, -(WIN_COLS - 1), WIN_COLS - 1) + (WIN_COLS - 1)
    neg = jnp.full((w, 2 * w), NEG_INF, F32)
    n_ro, n_co = 2 * WIN_ROWS - 1, 2 * WIN_COLS - 1
    tabs = []
    for ro in range(n_ro):
        t = jnp.zeros((w, 2 * w), F32)
        for j in range(n_co):
            t = jnp.where(col_off == j, rpb_ref[h, ro * n_co + j] * (HEAD_DIM ** 0.5), t)
        tabs.append(jnp.where(col_ok, t, neg))
    left = lax.broadcasted_iota(jnp.int32, (w, 2 * w), 1) < w
    for kind in range(3):
        for i in range(NA_QROWS):
            for wp in range(NA_KROWS // 2):
                ra = _na_row_offset(kind, i, 2 * wp, rows)
                rb = _na_row_offset(kind, i, 2 * wp + 1, rows)
                ta = neg if ra is None else tabs[ra]
                tb = neg if rb is None else tabs[rb]
                blk = ta if ra == rb else jnp.where(left, ta, tb)
                bias_ref[kind, i * w:(i + 1) * w, wp * 2 * w:(wp + 1) * 2 * w] = blk


def _na_body(rpb_ref, q_ref, k_ref, v_ref, ck_ref, cv_ref, o_ref, bias_ref, sa_ref, sb_ref, *, rows):
    h = pl.program_id(0)
    w = GRID_W
    tq, tk = NA_QROWS * w, NA_KROWS * w
    n_tiles = rows // NA_QROWS
    scale = HEAD_DIM ** -0.5

    @pl.when(pl.program_id(1) == 0)
    def _():
        _na_build_bias(rpb_ref, bias_ref, h, rows)

    def offsets(t):
        krow0 = jnp.clip(t * NA_QROWS - WIN_ROWS // 2, 0, rows - NA_KROWS)
        return pl.multiple_of(t * tq, tq), pl.multiple_of(krow0 * w, 4 * w)

    def scores(t, s_ref):
        kind = jnp.where(t == 0, 0, jnp.where(t == n_tiles - 1, 2, 1))
        q0, k0 = offsets(t)
        q = q_ref[pl.ds(q0, tq), :]
        s_ref[:, :tk] = _dot_nt(q, k_ref[pl.ds(k0, tk), :]) + bias_ref[kind]
        s_ref[:, tk:] = _dot_nt(q, ck_ref[...])

    def attend(t, s_ref):
        q0, k0 = offsets(t)
        s = s_ref[...]
        p = jnp.exp2((s - jnp.max(s, axis=-1, keepdims=True)) * (scale * LOG2_E))
        l = jnp.sum(p, axis=-1, keepdims=True)
        pb = p.astype(BF16)
        o = _dot(pb[:, :tk], v_ref[pl.ds(k0, tk), :]) + _dot(pb[:, tk:], cv_ref[...])
        o_ref[pl.ds(q0, tq), :] = (o / l).astype(o_ref.dtype)

    scores(jnp.int32(0), sa_ref)

    def pair(i, carry):
        t = 2 * i
        scores(t + 1, sb_ref)
        attend(t, sa_ref)
        scores(t + 2, sa_ref)
        attend(t + 1, sb_ref)
        return carry

    lax.fori_loop(0, n_tiles // 2 - 1, pair, 0, unroll=True)
    scores(jnp.int32(n_tiles - 1), sb_ref)
    attend(jnp.int32(n_tiles - 2), sa_ref)
    attend(jnp.int32(n_tiles - 1), sb_ref)


def _neighbourhood_attention(proj, cproj, rpb, batch, seq, ctx_len):
    rows = seq // GRID_W
    assert rows % (2 * NA_QROWS) == 0 and rows >= NA_KROWS + 2 * NA_QROWS
    hb = NA_WIDTH // HEAD_DIM
    n_rpb = (2 * WIN_ROWS - 1) * (2 * WIN_COLS - 1)
    tq, tk = NA_QROWS * GRID_W, NA_KROWS * GRID_W
    grid_spec = pltpu.PrefetchScalarGridSpec(
        num_scalar_prefetch=0,
        grid=(NA_HEADS, batch),
        in_specs=[
            pl.BlockSpec(memory_space=pltpu.SMEM),
            pl.BlockSpec((seq, HEAD_DIM), lambda h, b: (b, 4 * hb + h)),
            pl.BlockSpec((seq, HEAD_DIM), lambda h, b: (b, h)),
            pl.BlockSpec((seq, HEAD_DIM), lambda h, b: (b, hb + h)),
            pl.BlockSpec((ctx_len, HEAD_DIM), lambda h, b: (b, h)),
            pl.BlockSpec((ctx_len, HEAD_DIM), lambda h, b: (b, hb + h)),
        ],
        out_specs=pl.BlockSpec((seq, HEAD_DIM), lambda h, b: (b, h)),
        scratch_shapes=[
            pltpu.VMEM((3, tq, tk), F32),
            pltpu.VMEM((tq, tk + ctx_len), F32),
            pltpu.VMEM((tq, tk + ctx_len), F32),
        ],
    )
    return pl.pallas_call(
        functools.partial(_na_body, rows=rows),
        grid_spec=grid_spec,
        out_shape=jax.ShapeDtypeStruct((batch * seq, NA_WIDTH), BF16),
        compiler_params=_cparams(2),
        name="neighbourhood_attention",
    )(rpb.reshape(NA_HEADS, n_rpb), proj, proj, proj, cproj, cproj)


def _log_sigmoid(x):
    return -(jnp.maximum(-x, 0.0) + jnp.log1p(jnp.exp(-jnp.abs(x))))


def _rope_partner_matrix():
    quarter = HEAD_DIM // 4
    src = lax.broadcasted_iota(jnp.int32, (HEAD_DIM, HEAD_DIM), 0)
    dst = lax.broadcasted_iota(jnp.int32, (HEAD_DIM, HEAD_DIM), 1)
    want = jnp.where(dst % (2 * quarter) < quarter, dst + quarter, dst - quarter)
    return (src == want).astype(BF16)


def _rope(x_bf16, partner_matrix, cos, sin_signed):
    return x_bf16.astype(F32) * cos + _dot(x_bf16, partner_matrix) * sin_signed


def _ret_body(df_ref, db_ref, gn_ref, cos_ref, sin_ref, q_ref, k_ref, v_ref, g_ref, ck_ref, cv_ref,
              cc_ref, wm_ref, bm_ref, wo_ref,
              o_ref, modl_ref, wob_ref, qr_ref, kr_ref, sf_ref, sb_ref):
    modl_ref[:, 0, :] = _dot(_silu(cc_ref[...]).astype(BF16), wm_ref[...].astype(BF16)) + bm_ref[...]
    wob_ref[...] = wo_ref[...].astype(BF16)

    c, d = RET_BLOCK, HEAD_DIM
    seq = q_ref.shape[0]
    n_chunks = seq // c
    ctx_len = ck_ref.shape[0]
    scale = HEAD_DIM ** -0.5
    head = pl.program_id(0)
    lgf_row = _log_sigmoid(jnp.full((1, HEAD_DIM), df_ref[head], F32))
    lgb_row = _log_sigmoid(jnp.full((1, HEAD_DIM), db_ref[head], F32))
    lgf = jnp.broadcast_to(lgf_row, (c, d))
    lgb = jnp.broadcast_to(lgb_row, (c, d))
    pos = lax.broadcasted_iota(jnp.int32, (c, d), 0).astype(F32)
    kdf = jnp.exp(lgf * (c - 1.0 - pos))
    kdb = jnp.exp(lgb * pos)
    qdf = jnp.exp(lgf * (pos + 1.0))
    qdb = jnp.exp(lgb * (c - pos))
    cdf = jnp.exp(lgf_row * float(c))
    cdb = jnp.exp(lgb_row * float(c))
    diff = (lax.broadcasted_iota(jnp.int32, (c, c), 0) - lax.broadcasted_iota(jnp.int32, (c, c), 1)).astype(F32)
    lgf_cc = jnp.broadcast_to(lgf_row[:, :1], (c, c))
    lgb_cc = jnp.broadcast_to(lgb_row[:, :1], (c, c))
    dmat = (jnp.where(diff >= 0, jnp.exp(lgf_cc * jnp.maximum(diff, 0.0)), 0.0)
            + jnp.where(diff <= 0, jnp.exp(lgb_cc * jnp.maximum(-diff, 0.0)), 0.0))

    cpos = lax.broadcasted_iota(jnp.int32, (ctx_len, d), 0).astype(F32)
    ckf = ck_ref[...].astype(F32) * scale
    cv = cv_ref[...]
    wf = jnp.exp(jnp.broadcast_to(lgf_row, (ctx_len, d)) * (ctx_len - 1.0 - cpos))
    wb = jnp.exp(jnp.broadcast_to(lgb_row, (ctx_len, d)) * cpos)
    s_f = _dot_tn((ckf * wf).astype(BF16), cv)
    s_b = _dot_tn((ckf * wb).astype(BF16), cv)

    rope_rows = 512
    pmat = _rope_partner_matrix()

    def rope_blk(i, carry):
        r0 = pl.multiple_of(i * rope_rows, rope_rows)
        cs = cos_ref[pl.ds(r0, rope_rows), :]
        sn = sin_ref[pl.ds(r0, rope_rows), :]
        qr_ref[pl.ds(r0, rope_rows), :] = _rope(q_ref[pl.ds(r0, rope_rows), :], pmat, cs, sn)
        kr_ref[pl.ds(r0, rope_rows), :] = _rope(k_ref[pl.ds(r0, rope_rows), :], pmat, cs, sn) * scale
        return carry

    lax.fori_loop(0, seq // rope_rows, rope_blk, 0, unroll=4)

    def scan_blk(n, carry):
        s, t = carry
        r0 = pl.multiple_of(n * c, c)
        kvf = _dot_tn((kr_ref[pl.ds(r0, c), :] * kdf).astype(BF16), v_ref[pl.ds(r0, c), :])
        sf_ref[n] = s.astype(BF16)
        m = n_chunks - 1 - n
        m0 = pl.multiple_of(m * c, c)
        kvb = _dot_tn((kr_ref[pl.ds(m0, c), :] * kdb).astype(BF16), v_ref[pl.ds(m0, c), :])
        sb_ref[m] = t.astype(BF16)
        return s * cdf + kvf, t * cdb + kvb

    lax.fori_loop(0, n_chunks, scan_blk, (s_f, s_b), unroll=16)

    gn = gn_ref[...]

    def out_blk(n, carry):
        r0 = pl.multiple_of(n * c, c)
        qc = qr_ref[pl.ds(r0, c), :]
        kc = kr_ref[pl.ds(r0, c), :]
        a = _dot_nt(qc.astype(BF16), kc.astype(BF16))
        o = (_dot((a * dmat).astype(BF16), v_ref[pl.ds(r0, c), :])
             + _dot((qc * qdf).astype(BF16), sf_ref[n])
             + _dot((qc * qdb).astype(BF16), sb_ref[n]))
        mu = jnp.mean(o, axis=-1, keepdims=True)
        var = jnp.mean(jnp.square(o - mu), axis=-1, keepdims=True)
        y = (o - mu) * lax.rsqrt(var + EPS) * gn
        o_ref[pl.ds(r0, c), :] = (y * _silu(g_ref[pl.ds(r0, c), :].astype(F32))).astype(o_ref.dtype)
        return carry

    lax.fori_loop(0, n_chunks, out_blk, 0, unroll=16)


def _rope_tables(seq):
    axis_dim = HEAD_DIM // 2
    inv_freq = ROPE_BASE ** (-jnp.arange(0, axis_dim, 2, dtype=F32) / axis_dim)
    rows = seq // GRID_W
    ang_r = jnp.arange(rows, dtype=F32)[:, None] * inv_freq
    ang_c = jnp.arange(GRID_W, dtype=F32)[:, None] * inv_freq
    by_row = lambda a: jnp.repeat(a, GRID_W, axis=0)
    by_col = lambda a: jnp.tile(a, (rows, 1))
    cr, sr, cc, sc = by_row(jnp.cos(ang_r)), by_row(jnp.sin(ang_r)), by_col(jnp.cos(ang_c)), by_col(jnp.sin(ang_c))
    return jnp.concatenate([cr, cr, cc, cc], axis=-1), jnp.concatenate([-sr, sr, -sc, sc], axis=-1)


def _retention(proj, cproj, decay_f, decay_b, gn_w, batch, seq, ctx_len, cc, w_mod, b_mod, mod_from, w_out):
    hb = RET_WIDTH // HEAD_DIM
    cos, sin = _rope_tables(seq)
    n_chunks = seq // RET_BLOCK
    blk = lambda rows, fn: pl.BlockSpec((rows, HEAD_DIM), fn)
    n_steps = RET_HEADS * batch
    step = lambda h, b: h * batch + b
    mod_rows, d = cc.shape
    late = w_mod.shape[1] - mod_from
    mcols, orows = late // n_steps, w_out.shape[0] // n_steps
    side_in = [
        pl.BlockSpec((mod_rows, d), lambda h, b: (0, 0)),
        pl.BlockSpec((d, mcols), lambda h, b: (0, mod_from // mcols + step(h, b))),
        pl.BlockSpec((1, mcols), lambda h, b: (0, mod_from // mcols + step(h, b))),
        pl.BlockSpec((orows, w_out.shape[1]), lambda h, b: (step(h, b), 0)),
    ]
    side_out = [
        pl.BlockSpec((mod_rows, 1, mcols), lambda h, b: (0, 0, step(h, b))),
        pl.BlockSpec((orows, w_out.shape[1]), lambda h, b: (step(h, b), 0)),
    ]
    side_shape = [jax.ShapeDtypeStruct((mod_rows, 1, late), F32), jax.ShapeDtypeStruct(w_out.shape, BF16)]
    return pl.pallas_call(
        _ret_body,
        grid=(RET_HEADS, batch),
        in_specs=[
            pl.BlockSpec(memory_space=pltpu.SMEM),
            pl.BlockSpec(memory_space=pltpu.SMEM),
            blk(1, lambda h, b: (0, h)),
            blk(seq, lambda h, b: (0, 0)),
            blk(seq, lambda h, b: (0, 0)),
            blk(seq, lambda h, b: (b, 5 * hb + h)),
            blk(seq, lambda h, b: (b, 2 * hb + h)),
            blk(seq, lambda h, b: (b, 3 * hb + h)),
            blk(seq, lambda h, b: (b, 6 * hb + h)),
            blk(ctx_len, lambda h, b: (b, 2 * hb + h)),
            blk(ctx_len, lambda h, b: (b, 3 * hb + h)),
        ] + side_in,
        out_specs=[blk(seq, lambda h, b: (b, h))] + side_out,
        out_shape=[jax.ShapeDtypeStruct((batch * seq, RET_WIDTH), BF16)] + side_shape,
        scratch_shapes=[
            pltpu.VMEM((seq, HEAD_DIM), F32),
            pltpu.VMEM((seq, HEAD_DIM), F32),
            pltpu.VMEM((n_chunks, HEAD_DIM, HEAD_DIM), BF16),
            pltpu.VMEM((n_chunks, HEAD_DIM, HEAD_DIM), BF16),
        ],
        compiler_params=_cparams(2),
        name="retention",
    )(decay_f.astype(F32), decay_b.astype(F32), gn_w.reshape(1, RET_WIDTH), cos, sin, proj, proj, proj, proj, cproj, cproj,
      cc, w_mod, b_mod.reshape(1, -1), w_out)


def _split_bf16(x):
    hi = x.astype(BF16)
    lo = (x - hi.astype(F32)).astype(BF16)
    return hi, lo


OUTPROJ_ROW_SPLITS = 2


def _outproj_body(na_ref, ret_ref, w_ref, x_ref, gate_ref, nw_ref, sh_ref, sc_ref, wr_ref,
                  h1_ref, u2_ref, lg_ref):
    half = na_ref.shape[1]
    n_e = wr_ref.shape[0]
    tm = x_ref.shape[0]
    wh, wl = _split_bf16(wr_ref[...])
    w_router = jnp.concatenate([wh, wl], axis=0)
    for r in range(OUTPROJ_ROW_SPLITS):
        rows = pl.ds(r * (tm // OUTPROJ_ROW_SPLITS), tm // OUTPROJ_ROW_SPLITS)
        mix = _dot(na_ref[rows, :], w_ref[:half, :]) + _dot(ret_ref[rows, :], w_ref[half:, :])
        h1 = x_ref[rows, :] + gate_ref[0] * mix
        h1_ref[rows, :] = h1
        u2 = _rms_mod(h1, nw_ref[...], sh_ref[0], sc_ref[0])
        u2_ref[rows, :] = u2
        both = _dot_nt(w_router, u2.astype(BF16))
        lg_ref[:, rows] = both[:n_e] + both[n_e:]


def _out_projection(na, ret, w_out_bf16, x2d, mod3, norm_w, w_router_t, seq, tm=OUTPROJ_ROW_TILE):
    rows, d = x2d.shape
    half = na.shape[1]
    per_b = seq // tm
    return pl.pallas_call(
        _outproj_body,
        grid=(rows // tm,),
        in_specs=[
            pl.BlockSpec((tm, half), lambda i: (i, 0)),
            pl.BlockSpec((tm, half), lambda i: (i, 0)),
            pl.BlockSpec((2 * half, d), lambda i: (0, 0)),
            pl.BlockSpec((tm, d), lambda i: (i, 0)),
            pl.BlockSpec((1, 1, d), lambda i: (i // per_b, 0, MOD_LATE_GATE_MIX)),
            pl.BlockSpec((1, d), lambda i: (0, 0)),
            pl.BlockSpec((1, 1, d), lambda i: (i // per_b, 0, MOD_LATE_SHIFT_FFN)),
            pl.BlockSpec((1, 1, d), lambda i: (i // per_b, 0, MOD_LATE_SCALE_FFN)),
            pl.BlockSpec((N_EXPERTS, d), lambda i: (0, 0)),
        ],
        out_specs=[
            pl.BlockSpec((tm, d), lambda i: (i, 0)),
            pl.BlockSpec((tm, d), lambda i: (i, 0)),
            pl.BlockSpec((N_EXPERTS, tm), lambda i: (0, i)),
        ],
        out_shape=[
            jax.ShapeDtypeStruct((rows, d), F32),
            jax.ShapeDtypeStruct((rows, d), F32),
            jax.ShapeDtypeStruct((N_EXPERTS, rows), F32),
        ],
        compiler_params=_cparams(1),
        name="out_projection",
    )(na, ret, w_out_bf16, x2d, mod3, norm_w.reshape(1, d), mod3, mod3, w_router_t)


def _prefix_incl_lanes(x, tri):
    r, l = x.shape
    nb = l // LANES
    xs = jnp.concatenate([x[:, t * LANES:(t + 1) * LANES] for t in range(nb)], axis=0).astype(BF16)
    p = _dot(xs, tri)
    outs, run = [], jnp.zeros((r, 1), F32)
    for t in range(nb):
        blk = p[t * r:(t + 1) * r] + run
        outs.append(blk)
        run = blk[:, LANES - 1:LANES]
    return jnp.concatenate(outs, axis=1)


NOT_ROUTED = -(1 << 20)


def _route_body(lg_ref, gidx_ref, gate_ref, slot_ref, before_ref, *, cap):
    b = pl.program_id(0)
    n_e, seq = lg_ref.shape
    kf = float(cap)
    lg = lg_ref[...]
    ex = jnp.exp(lg - jnp.max(lg, axis=0, keepdims=True))
    aff = ex / jnp.sum(ex, axis=0, keepdims=True)

    def cond(c):
        return (c[0] < 4096) & (c[5] > 0.5)

    def step(c):
        it, lo, hi, thr, done, _ = c
        mid = 0.5 * (lo + hi)
        above = jnp.sum((aff > mid).astype(F32), axis=1, keepdims=True)
        hit = above == kf
        stuck = (mid <= lo) | (mid >= hi)
        active = done < 0.5
        thr = jnp.where(active & hit, mid, jnp.where(active & stuck, hi, thr))
        go = active & ~(hit | stuck)
        ge = above >= kf
        lo = jnp.where(go & ge, mid, lo)
        hi = jnp.where(go & ~ge, mid, hi)
        done = jnp.where(active & (hit | stuck), 1.0, done)
        return it + 1, lo, hi, thr, done, jnp.sum(1.0 - done)

    col = lambda v: jnp.full((n_e, 1), v, F32)
    init = (jnp.int32(0), col(-1.0), col(2.0), col(0.0), col(0.0), jnp.float32(n_e))
    thr = lax.while_loop(cond, step, init)[3]

    ii = lax.broadcasted_iota(jnp.int32, (LANES, LANES), 0)
    jj = lax.broadcasted_iota(jnp.int32, (LANES, LANES), 1)
    tri = (ii <= jj).astype(BF16)
    gt = aff > thr
    eq = (aff == thr).astype(F32)
    need = kf - jnp.sum(gt.astype(F32), axis=1, keepdims=True)
    eq_before = _prefix_incl_lanes(eq, tri) - eq
    mask = jnp.where(gt | ((eq > 0.5) & (eq_before < need)), 1.0, 0.0)

    slot = _prefix_incl_lanes(mask, tri) - mask
    before_ref[0] = slot.astype(jnp.int32)
    slot_ref[0] = jnp.where(mask > 0.5, slot, float(NOT_ROUTED)).astype(jnp.int32)

    tok = lax.broadcasted_iota(jnp.int32, (1, seq), 1).astype(F32)
    tok_hi = jnp.floor(tok * (1.0 / 64))
    tok_lo = tok - 64.0 * tok_hi
    n_hi = ROUTE_SLOT_HI
    n_lo = cap // n_hi
    hi_iota = lax.broadcasted_iota(jnp.int32, (n_hi, seq), 0).astype(F32)
    lo_iota = lax.broadcasted_iota(jnp.int32, (n_lo, seq), 0).astype(F32)
    for e in range(n_e):
        se = slot[e:e + 1]
        s_hi = jnp.floor(se * (1.0 / n_lo))
        s_lo = se - n_lo * s_hi
        in_hi = jnp.where((s_hi == hi_iota) & (mask[e:e + 1] > 0.5), 1.0, 0.0)
        in_lo = (s_lo == lo_iota).astype(BF16)
        a = aff[e:e + 1]
        a_hi = a.astype(BF16).astype(F32)
        a_mid = (a - a_hi).astype(BF16).astype(F32)
        a_lo = a - a_hi - a_mid
        vals = (tok_hi, tok_lo, a_hi, a_mid, a_lo)
        lhs = jnp.concatenate([in_hi * v for v in vals], axis=0).astype(BF16)
        got = _dot_nt(lhs, in_lo)
        part = lambda r: got[r * n_hi:(r + 1) * n_hi]
        gidx_ref[0, e] = (part(0) * 64.0 + part(1)).astype(jnp.int32) + b * seq
        gate_ref[0, e] = part(2) + part(3) + part(4)


def _routing(logits_t, batch, seq):
    n_e = logits_t.shape[0]
    cap = CAPACITY_FACTOR * seq // n_e
    n_hi, n_lo = ROUTE_SLOT_HI, cap // ROUTE_SLOT_HI
    bec = pl.BlockSpec((1, n_e, n_hi, n_lo), lambda b: (b, 0, 0, 0))
    bel = pl.BlockSpec((1, n_e, seq), lambda b: (b, 0, 0))
    gidx, gates, slot, before = pl.pallas_call(
        functools.partial(_route_body, cap=cap),
        grid=(batch,),
        in_specs=[pl.BlockSpec((n_e, seq), lambda b: (0, b))],
        out_specs=[bec, bec, bel, bel],
        out_shape=[
            jax.ShapeDtypeStruct((batch, n_e, n_hi, n_lo), jnp.int32),
            jax.ShapeDtypeStruct((batch, n_e, n_hi, n_lo), F32),
            jax.ShapeDtypeStruct((batch, n_e, seq), jnp.int32),
            jax.ShapeDtypeStruct((batch, n_e, seq), jnp.int32),
        ],
        compiler_params=_cparams(1),
        name="routing",
    )(logits_t)
    flat = lambda a: a.reshape(batch, n_e, cap)
    return flat(gidx), flat(gates), slot, before


def _hbm_row(ref, r):
    return ref.at[lax.shift_right_logical(r, 3), pl.ds(r & (SUBLANES - 1), 1)]


MOE_ROW_SPLITS = 2


def _moe_body(gidx_ref, gnext_ref, gate_ref, u2_hbm, wg_ref, wu_ref, wd_ref, y_ref,
              stage_ref, xe_ref, acc_ref, gsem, *, n_f):
    e = pl.program_id(0)
    f = pl.program_id(1)
    n_e = pl.num_programs(0)
    m, d = y_ref.shape
    sub = SUBLANES
    n_tiles = m // sub
    assert n_f >= 2
    per_step = -(-n_tiles // (n_f - 1))
    last_count = n_tiles - per_step * (n_f - 2)

    def gather_copy(idx_ref, i, k):
        return pltpu.make_async_copy(_hbm_row(u2_hbm, idx_ref[0, i * sub + k]), stage_ref.at[i, pl.ds(k, 1)], gsem)

    def for_rows(fn, tiles_per_trip=4):
        def body(t, c):
            for kk in range(tiles_per_trip * sub):
                fn(t * tiles_per_trip + kk // sub, kk % sub)
            return c
        lax.fori_loop(0, n_tiles // tiles_per_trip, body, 0)

    wait_gather = lambda: for_rows(lambda i, k: gather_copy(gidx_ref, i, k).wait())

    def ffn_step(first, gather_tiles):
        i0 = (f - 1) * per_step
        for kk in range(gather_tiles * sub):
            gather_copy(gnext_ref, i0 + kk // sub, kk % sub).start()
        wg, wu, wd = wg_ref[...].astype(BF16), wu_ref[...].astype(BF16), wd_ref[...].astype(BF16)
        part = m // MOE_ROW_SPLITS
        for r in range(MOE_ROW_SPLITS):
            rows = pl.ds(r * part, part)
            if first:
                x = stage_ref[pl.ds(r * (part // sub), part // sub)].reshape(part, d).astype(BF16)
                xe_ref[rows, :] = x
            else:
                x = xe_ref[rows, :]
            hid = _silu(_dot(x, wg)) * _dot(x, wu)
            out = _dot(hid.astype(BF16), wd)
            acc_ref[rows, :] = out if first else acc_ref[rows, :] + out

    @pl.when(f == 0)
    def _():
        @pl.when(e == 0)
        def _():
            for_rows(lambda i, k: gather_copy(gidx_ref, i, k).start())

        wait_gather()
        ffn_step(True, 0)

    if n_f > 2:
        pl.when((f > 0) & (f < n_f - 1))(lambda: ffn_step(False, per_step))

    @pl.when(f == n_f - 1)
    def _():
        ffn_step(False, last_count)

        @pl.when(f > 0)
        def _():
            g = jnp.transpose(jnp.broadcast_to(gate_ref[...], (LANES, m)))
            for k in range(d // LANES):
                cols = slice(k * LANES, (k + 1) * LANES)
                y_ref[:, cols] = (acc_ref[:, cols] * g).astype(y_ref.dtype)

        @pl.when(e == n_e - 1)
        def _():
            wait_gather()


def _moe_ffn(gidx, gates, u2, w_gate, w_up, w_down, tf=MOE_F_TILE):
    n_e, _, m = gidx.shape
    rows, d = u2.shape
    ff = w_gate.shape[2]
    n_f = ff // tf
    sub = SUBLANES
    smem = lambda fn: pl.BlockSpec((None, 1, m), fn, memory_space=pltpu.SMEM)
    return pl.pallas_call(
        functools.partial(_moe_body, n_f=n_f),
        grid=(n_e, n_f),
        in_specs=[
            smem(lambda e, f: (e, 0, 0)),
            smem(lambda e, f: (jnp.minimum(e + 1, n_e - 1), 0, 0)),
            pl.BlockSpec((None, 1, m), lambda e, f: (e, 0, 0)),
            pl.BlockSpec(memory_space=pl.ANY),
            pl.BlockSpec((None, d, tf), lambda e, f: (e, 0, f)),
            pl.BlockSpec((None, d, tf), lambda e, f: (e, 0, f)),
            pl.BlockSpec((None, tf, d), lambda e, f: (e, f, 0)),
        ],
        out_specs=pl.BlockSpec((m, d), lambda e, f: (e, 0)),
        out_shape=jax.ShapeDtypeStruct((n_e * m, d), BF16),
        scratch_shapes=[
            pltpu.VMEM((m // sub, sub, d), F32),
            pltpu.VMEM((m, d), BF16),
            pltpu.VMEM((m, d), F32),
            pltpu.SemaphoreType.DMA,
        ],
        compiler_params=_cparams(2),
        name="moe_ffn",
    )(gidx, gidx, gates, u2.reshape(rows // sub, sub, d), w_gate, w_up, w_down)


COMBINE_TOKENS = 512
COMBINE_PIECE = 96


def _combine_body(start_ref, npiece_ref, h1_ref, y_hbm, slot_ref, gate_ref, nw_ref, o_ref,
                  acc_ref, buf_ref, xbuf_ref, sem, xsem, *, m, cap, per_b):
    i = pl.program_id(0)
    n_tiles = pl.num_programs(0)
    n_e = slot_ref.shape[1]
    t, p = COMBINE_TOKENS, COMBINE_PIECE
    total = n_e * m
    cur = i % 2
    b = i // per_b

    def piece_copy(tile, e, dst):
        st = pl.multiple_of(start_ref[tile * n_e + e], BF16_ROWS)
        return pltpu.make_async_copy(y_hbm.at[pl.ds(st, p)], buf_ref.at[dst, pl.ds(e * p, p)], sem.at[dst])

    @pl.when(i == 0)
    def _():
        for e in range(n_e):
            piece_copy(0, e, 0).start()

    @pl.when(i + 1 < n_tiles)
    def _():
        for e in range(n_e):
            piece_copy(i + 1, e, 1 - cur).start()

    for e in range(n_e):
        piece_copy(i, e, cur).wait()

    slots = slot_ref[0]
    riota = lax.broadcasted_iota(jnp.int32, (p, t), 0)

    def local_row(e, st):
        return slots[e:e + 1, :] + (e * m + b * cap - st)

    sel = jnp.concatenate(
        [(local_row(e, start_ref[i * n_e + e]) == riota).astype(BF16) for e in range(n_e)], axis=0)
    acc_ref[...] = _dot_tn(sel, buf_ref[cur])

    for e in range(n_e):
        st0 = start_ref[i * n_e + e]

        def extra(k, carry, e=e, st0=st0):
            want = st0 + k * p
            st = pl.multiple_of(jnp.minimum(want, total - p), BF16_ROWS)
            cp = pltpu.make_async_copy(y_hbm.at[pl.ds(st, p)], xbuf_ref, xsem)
            cp.start()
            cp.wait()
            blk = ((local_row(e, st) == riota) & (riota >= want - st)).astype(BF16)
            acc_ref[...] += _dot_tn(blk, xbuf_ref[...])
            return carry

        lax.fori_loop(1, npiece_ref[i * n_e + e], extra, 0)

    h2 = h1_ref[...] + gate_ref[0] * acc_ref[...]
    ms = jnp.mean(h2 * h2, axis=-1, keepdims=True)
    o_ref[...] = h2 * lax.rsqrt(ms + EPS) * nw_ref[...]


def _combine_pieces(before, cap):
    batch, n_e, seq = before.shape
    t, p = COMBINE_TOKENS, COMBINE_PIECE
    m = batch * cap
    total = n_e * m
    bounds = jnp.concatenate([before[:, :, ::t], jnp.full((batch, n_e, 1), cap, jnp.int32)], axis=2)
    base = (jnp.arange(n_e, dtype=jnp.int32) * m)[None, :, None] + (jnp.arange(batch, dtype=jnp.int32) * cap)[:, None, None]
    first = base + bounds[:, :, :-1]
    end = base + bounds[:, :, 1:]
    start = jnp.minimum(first // BF16_ROWS * BF16_ROWS, total - p)
    npiece = jnp.maximum((end - start + p - 1) // p, 1)
    by_tile = lambda a: a.transpose(0, 2, 1).reshape(-1)
    return by_tile(start), by_tile(npiece)


def _combine(h1, y, slot, before, mod3, final_w, seq, cap):
    rows, d = h1.shape
    batch, n_e, _ = slot.shape
    t, p = COMBINE_TOKENS, COMBINE_PIECE
    per_b = seq // t
    start, npiece = _combine_pieces(before, cap)
    grid_spec = pltpu.PrefetchScalarGridSpec(
        num_scalar_prefetch=2,
        grid=(rows // t,),
        in_specs=[
            pl.BlockSpec((t, d), lambda i, st, npc: (i, 0)),
            pl.BlockSpec(memory_space=pl.ANY),
            pl.BlockSpec((1, n_e, t), lambda i, st, npc: (i // per_b, 0, i % per_b)),
            pl.BlockSpec((1, 1, d), lambda i, st, npc: (i // per_b, 0, MOD_LATE_GATE_FFN)),
            pl.BlockSpec((1, d), lambda i, st, npc: (0, 0)),
        ],
        out_specs=pl.BlockSpec((t, d), lambda i, st, npc: (i, 0)),
        scratch_shapes=[
            pltpu.VMEM((t, d), F32),
            pltpu.VMEM((2, n_e * p, d), y.dtype),
            pltpu.VMEM((p, d), y.dtype),
            pltpu.SemaphoreType.DMA((2,)),
            pltpu.SemaphoreType.DMA,
        ],
    )
    return pl.pallas_call(
        functools.partial(_combine_body, m=batch * cap, cap=cap, per_b=per_b),
        grid_spec=grid_spec,
        out_shape=jax.ShapeDtypeStruct((rows, d), F32),
        compiler_params=_cparams(1),
        name="combine",
    )(start, npiece, h1, y, slot, mod3, final_w.reshape(1, d))


def kernel(x, c, ctx, c_ctx, w_mod, b_mod, norm_mix_w, norm_ffn_w, w_in, na_rpb, ret_decay_fwd,
           ret_decay_bwd, ret_gn_w, w_out, w_router, w_gate, w_up, w_down, final_norm_w):
    batch, seq, d = x.shape
    ctx_len = ctx.shape[1]
    assert w_mod.shape[0] == 1, "one trunk layer"
    n_e = w_router.shape[2]
    cap = CAPACITY_FACTOR * seq // n_e

    mod_rows = SUBLANES
    cc = jnp.concatenate([c, c_ctx[None], jnp.zeros((mod_rows - batch - 1, d), c.dtype)], axis=0)
    mod3 = _modulation(cc, w_mod[0], b_mod[0], MOD_EARLY * d)

    x2d = x.reshape(batch * seq, d)
    tm = INPROJ_ROW_TILE
    cproj, w_in_bf16 = _ctx_projection(ctx.reshape(batch * ctx_len, d), norm_mix_w[0], mod3, batch,
                                       w_in[0], KV_COLS)
    proj = _in_projection(x2d, norm_mix_w[0], mod3, lambda i: i // (seq // tm), w_in_bf16,
                          w_in.shape[2], tm, w_in.shape[2] // INPROJ_COL_TILES)

    na = _neighbourhood_attention(proj, cproj, na_rpb[0], batch, seq, ctx_len)
    ret, mod_late3, w_out_bf16 = _retention(proj, cproj, ret_decay_fwd[0], ret_decay_bwd[0], ret_gn_w[0],
                                            batch, seq, ctx_len, cc, w_mod[0], b_mod[0], MOD_EARLY * d, w_out[0])

    h1, u2, logits_t = _out_projection(na, ret, w_out_bf16, x2d, mod_late3, norm_ffn_w[0], w_router[0].T, seq)

    gidx, gates, slot, before = _routing(logits_t, batch, seq)
    per_expert = lambda a: a.transpose(1, 0, 2).reshape(n_e, 1, batch * cap)
    y = _moe_ffn(per_expert(gidx), per_expert(gates), u2, w_gate[0], w_up[0], w_down[0])
    out = _combine(h1, y, slot, before, mod_late3, final_norm_w, seq, cap)
    return out.reshape(batch, seq, d)
```

```python
import functools

import jax
import jax.numpy as jnp
from jax import lax
from jax.experimental import pallas as pl
from jax.experimental.pallas import tpu as pltpu

GRID_W = 64
HEAD_DIM = 128
NA_HEADS = 8
RET_HEADS = 8
NA_WIDTH = NA_HEADS * HEAD_DIM
RET_WIDTH = RET_HEADS * HEAD_DIM
WIN_ROWS = 8
WIN_COLS = 16
RET_BLOCK = 256
ROPE_BASE = 10000.0
N_EXPERTS = 16
CAPACITY_FACTOR = 2
N_MOD = 6
MOD_EARLY = 2
MOD_LATE_GATE_MIX, MOD_LATE_SHIFT_FFN, MOD_LATE_SCALE_FFN, MOD_LATE_GATE_FFN = 0, 1, 2, 3
EPS = 1e-6
NEG_INF = -1e30
LOG2_E = 1.4426950408889634
KV_COLS = 2 * NA_WIDTH + 2 * RET_WIDTH

F32 = jnp.float32
BF16 = jnp.bfloat16
LANES = 128
SUBLANES = 8
BF16_ROWS = 16
ROUTE_SLOT_HI = 16
MIB = 1024 * 1024
VMEM_LIMIT_V7X = 56 * MIB

MOD_COL_TILE = 1024
CTX_COL_TILE = 1024
INPROJ_ROW_TILE = 1024
INPROJ_COL_TILES = 4
OUTPROJ_ROW_TILE = 512
MOE_F_TILE = 256


def _cparams(n_axes):
    return pltpu.CompilerParams(
        dimension_semantics=("arbitrary",) * n_axes, vmem_limit_bytes=VMEM_LIMIT_V7X)


def _silu(x):
    return x * jax.nn.sigmoid(x)


def _dot(a, b):
    return jnp.dot(a, b, preferred_element_type=F32)


def _dot_nt(a, b):
    return lax.dot_general(a, b, (((1,), (1,)), ((), ())), preferred_element_type=F32)


def _dot_tn(a, b):
    return lax.dot_general(a, b, (((0,), (0,)), ((), ())), preferred_element_type=F32)


def _mod_body(c_ref, w_ref, b_ref, o_ref):
    a = _silu(c_ref[...]).astype(BF16)
    o_ref[:, 0, :] = _dot(a, w_ref[...].astype(BF16)) + b_ref[...]


def _modulation(cc, w_mod, b_mod, n_cols, tn=MOD_COL_TILE):
    rows, d = cc.shape
    n = n_cols
    return pl.pallas_call(
        _mod_body,
        grid=(n // tn,),
        in_specs=[
            pl.BlockSpec((rows, d), lambda j: (0, 0)),
            pl.BlockSpec((d, tn), lambda j: (0, j)),
            pl.BlockSpec((1, tn), lambda j: (0, j)),
        ],
        out_specs=pl.BlockSpec((rows, 1, tn), lambda j: (0, 0, j)),
        out_shape=jax.ShapeDtypeStruct((rows, 1, n), F32),
        compiler_params=_cparams(1),
        name="modulation",
    )(cc, w_mod, b_mod.reshape(1, -1))


def _rms_mod(x, nw, shift, scale):
    ms = jnp.mean(x * x, axis=-1, keepdims=True)
    y = x * lax.rsqrt(ms + EPS) * nw
    return y * (1.0 + scale) + shift


INPROJ_NORM_SPLITS = 4


def _inproj_body(x_ref, nw_ref, sh_ref, sc_ref, w_ref, o_ref, u_ref):
    j = pl.program_id(1)
    tm = x_ref.shape[0]

    @pl.when(j == 0)
    def _():
        part = tm // INPROJ_NORM_SPLITS
        for r in range(INPROJ_NORM_SPLITS):
            rows = pl.ds(r * part, part)
            u = _rms_mod(x_ref[rows, :], nw_ref[...], sh_ref[0], sc_ref[0]).astype(BF16)
            u_ref[rows, :] = u
            o_ref[rows, :] = _dot(u, w_ref[...]).astype(o_ref.dtype)

    @pl.when(j > 0)
    def _():
        o_ref[...] = _dot(u_ref[...], w_ref[...]).astype(o_ref.dtype)


def _ctx_proj_body(x_ref, nw_ref, sh_ref, sc_ref, w_ref, o_ref, wb_ref, u_ref, *, kv_tiles):
    j = pl.program_id(0)

    @pl.when(j == 0)
    def _():
        u_ref[...] = _rms_mod(x_ref[...], nw_ref[...], sh_ref[0], sc_ref[0]).astype(BF16)

    wb_ref[...] = w_ref[...].astype(BF16)

    @pl.when(j < kv_tiles)
    def _():
        o_ref[...] = _dot(u_ref[...], wb_ref[...]).astype(o_ref.dtype)


def _ctx_projection(x2d, norm_w, mod3, mod_row, w_in, kv_cols, tn=CTX_COL_TILE):
    rows, d = x2d.shape
    n = w_in.shape[1]
    kv_tiles = kv_cols // tn
    return pl.pallas_call(
        functools.partial(_ctx_proj_body, kv_tiles=kv_tiles),
        grid=(n // tn,),
        in_specs=[
            pl.BlockSpec((rows, d), lambda j: (0, 0)),
            pl.BlockSpec((1, d), lambda j: (0, 0)),
            pl.BlockSpec((1, 1, d), lambda j: (mod_row, 0, 0)),
            pl.BlockSpec((1, 1, d), lambda j: (mod_row, 0, 1)),
            pl.BlockSpec((d, tn), lambda j: (0, j)),
        ],
        out_specs=[
            pl.BlockSpec((rows, tn), lambda j: (0, jnp.minimum(j, kv_tiles - 1))),
            pl.BlockSpec((d, tn), lambda j: (0, j)),
        ],
        out_shape=[jax.ShapeDtypeStruct((rows, kv_cols), BF16), jax.ShapeDtypeStruct((d, n), BF16)],
        scratch_shapes=[pltpu.VMEM((rows, d), BF16)],
        compiler_params=_cparams(1),
        name="ctx_projection",
    )(x2d, norm_w.reshape(1, d), mod3, mod3, w_in)


def _in_projection(x2d, norm_w, mod3, mod_row_fn, w_in, n_cols, tm, tn):
    rows, d = x2d.shape
    return pl.pallas_call(
        _inproj_body,
        grid=(rows // tm, n_cols // tn),
        in_specs=[
            pl.BlockSpec((tm, d), lambda i, j: (i, 0)),
            pl.BlockSpec((1, d), lambda i, j: (0, 0)),
            pl.BlockSpec((1, 1, d), lambda i, j: (mod_row_fn(i), 0, 0)),
            pl.BlockSpec((1, 1, d), lambda i, j: (mod_row_fn(i), 0, 1)),
            pl.BlockSpec((d, tn), lambda i, j: (0, j)),
        ],
        out_specs=pl.BlockSpec((tm, tn), lambda i, j: (i, j)),
        out_shape=jax.ShapeDtypeStruct((rows, n_cols), BF16),
        scratch_shapes=[pltpu.VMEM((tm, d), BF16)],
        compiler_params=_cparams(2),
        name="in_projection",
    )(x2d, norm_w.reshape(1, d), mod3, mod3, w_in)


NA_QROWS = 4
NA_KROWS = NA_QROWS + WIN_ROWS

def _na_row_offset(tile_kind, i, w, rows):
    half = WIN_ROWS // 2
    if tile_kind == 0:
        r, key = i, w
    elif tile_kind == 1:
        r, key = NA_QROWS + i, NA_QROWS - half + w
    else:
        r, key = rows - NA_QROWS + i, rows - NA_KROWS + w
    start = min(max(r - half, 0), rows - WIN_ROWS)
    if not (start <= key < start + WIN_ROWS):
        return None
    return key - r + (WIN_ROWS - 1)


def _na_build_bias(rpb_ref, bias_ref, h, rows):
    w = GRID_W
    cq = lax.broadcasted_iota(jnp.int32, (w, 2 * w), 0)
    ck = lax.broadcasted_iota(jnp.int32, (w, 2 * w), 1) % w
    col_start = jnp.clip(cq - WIN_COLS // 2, 0, w - WIN_COLS)
    col_ok = (ck >= col_start) & (ck < col_start + WIN_COLS)
    col_off = jnp.clip(ck - cq, -(WIN_COLS - 1), WIN_COLS - 1) + (WIN_COLS - 1)
    neg = jnp.full((w, 2 * w), NEG_INF, F32)
    n_ro, n_co = 2 * WIN_ROWS - 1, 2 * WIN_COLS - 1
    tabs = []
    for ro in range(n_ro):
        t = jnp.zeros((w, 2 * w), F32)
        for j in range(n_co):
            t = jnp.where(col_off == j, rpb_ref[h, ro * n_co + j] * (HEAD_DIM ** 0.5), t)
        tabs.append(jnp.where(col_ok, t, neg))
    left = lax.broadcasted_iota(jnp.int32, (w, 2 * w), 1) < w
    for kind in range(3):
        for i in range(NA_QROWS):
            for wp in range(NA_KROWS // 2):
                ra = _na_row_offset(kind, i, 2 * wp, rows)
                rb = _na_row_offset(kind, i, 2 * wp + 1, rows)
                ta = neg if ra is None else tabs[ra]
                tb = neg if rb is None else tabs[rb]
                blk = ta if ra == rb else jnp.where(left, ta, tb)
                bias_ref[kind, i * w:(i + 1) * w, wp * 2 * w:(wp + 1) * 2 * w] = blk


def _na_body(rpb_ref, q_ref, k_ref, v_ref, ck_ref, cv_ref, o_ref, bias_ref, sa_ref, sb_ref, *, rows):
    h = pl.program_id(0)
    w = GRID_W
    tq, tk = NA_QROWS * w, NA_KROWS * w
    n_tiles = rows // NA_QROWS
    scale = HEAD_DIM ** -0.5

    @pl.when(pl.program_id(1) == 0)
    def _():
        _na_build_bias(rpb_ref, bias_ref, h, rows)

    def offsets(t):
        krow0 = jnp.clip(t * NA_QROWS - WIN_ROWS // 2, 0, rows - NA_KROWS)
        return pl.multiple_of(t * tq, tq), pl.multiple_of(krow0 * w, 4 * w)

    def scores(t, s_ref):
        kind = jnp.where(t == 0, 0, jnp.where(t == n_tiles - 1, 2, 1))
        q0, k0 = offsets(t)
        q = q_ref[pl.ds(q0, tq), :]
        s_ref[:, :tk] = _dot_nt(q, k_ref[pl.ds(k0, tk), :]) + bias_ref[kind]
        s_ref[:, tk:] = _dot_nt(q, ck_ref[...])

    def attend(t, s_ref):
        q0, k0 = offsets(t)
        s = s_ref[...]
        p = jnp.exp2((s - jnp.max(s, axis=-1, keepdims=True)) * (scale * LOG2_E))
        l = jnp.sum(p, axis=-1, keepdims=True)
        pb = p.astype(BF16)
        o = _dot(pb[:, :tk], v_ref[pl.ds(k0, tk), :]) + _dot(pb[:, tk:], cv_ref[...])
        o_ref[pl.ds(q0, tq), :] = (o / l).astype(o_ref.dtype)

    bufs = (sa_ref, sb_ref)
    scores(0, bufs[0])
    for t in range(n_tiles):
        if t + 1 < n_tiles:
            scores(t + 1, bufs[(t + 1) % 2])
        attend(t, bufs[t % 2])


def _neighbourhood_attention(proj, cproj, rpb, batch, seq, ctx_len):
    rows = seq // GRID_W
    assert rows % (2 * NA_QROWS) == 0 and rows >= NA_KROWS + 2 * NA_QROWS
    hb = NA_WIDTH // HEAD_DIM
    n_rpb = (2 * WIN_ROWS - 1) * (2 * WIN_COLS - 1)
    tq, tk = NA_QROWS * GRID_W, NA_KROWS * GRID_W
    grid_spec = pltpu.PrefetchScalarGridSpec(
        num_scalar_prefetch=0,
        grid=(NA_HEADS, batch),
        in_specs=[
            pl.BlockSpec(memory_space=pltpu.SMEM),
            pl.BlockSpec((seq, HEAD_DIM), lambda h, b: (b, 4 * hb + h)),
            pl.BlockSpec((seq, HEAD_DIM), lambda h, b: (b, h)),
            pl.BlockSpec((seq, HEAD_DIM), lambda h, b: (b, hb + h)),
            pl.BlockSpec((ctx_len, HEAD_DIM), lambda h, b: (b, h)),
            pl.BlockSpec((ctx_len, HEAD_DIM), lambda h, b: (b, hb + h)),
        ],
        out_specs=pl.BlockSpec((seq, HEAD_DIM), lambda h, b: (b, h)),
        scratch_shapes=[
            pltpu.VMEM((3, tq, tk), F32),
            pltpu.VMEM((tq, tk + ctx_len), F32),
            pltpu.VMEM((tq, tk + ctx_len), F32),
        ],
    )
    return pl.pallas_call(
        functools.partial(_na_body, rows=rows),
        grid_spec=grid_spec,
        out_shape=jax.ShapeDtypeStruct((batch * seq, NA_WIDTH), BF16),
        compiler_params=_cparams(2),
        name="neighbourhood_attention",
    )(rpb.reshape(NA_HEADS, n_rpb), proj, proj, proj, cproj, cproj)


def _log_sigmoid(x):
    return -(jnp.maximum(-x, 0.0) + jnp.log1p(jnp.exp(-jnp.abs(x))))


def _rope_partner_matrix():
    quarter = HEAD_DIM // 4
    src = lax.broadcasted_iota(jnp.int32, (HEAD_DIM, HEAD_DIM), 0)
    dst = lax.broadcasted_iota(jnp.int32, (HEAD_DIM, HEAD_DIM), 1)
    want = jnp.where(dst % (2 * quarter) < quarter, dst + quarter, dst - quarter)
    return (src == want).astype(BF16)


def _rope(x_bf16, partner_matrix, cos, sin_signed):
    return x_bf16.astype(F32) * cos + _dot(x_bf16, partner_matrix) * sin_signed


def _ret_body(df_ref, db_ref, gn_ref, cos_ref, sin_ref, q_ref, k_ref, v_ref, g_ref, ck_ref, cv_ref,
              cc_ref, wm_ref, bm_ref, wo_ref,
              o_ref, modl_ref, wob_ref, qr_ref, kr_ref, sf_ref, sb_ref):
    modl_ref[:, 0, :] = _dot(_silu(cc_ref[...]).astype(BF16), wm_ref[...].astype(BF16)) + bm_ref[...]
    wob_ref[...] = wo_ref[...].astype(BF16)

    c, d = RET_BLOCK, HEAD_DIM
    seq = q_ref.shape[0]
    n_chunks = seq // c
    ctx_len = ck_ref.shape[0]
    scale = HEAD_DIM ** -0.5
    head = pl.program_id(0)
    lgf_row = _log_sigmoid(jnp.full((1, HEAD_DIM), df_ref[head], F32))
    lgb_row = _log_sigmoid(jnp.full((1, HEAD_DIM), db_ref[head], F32))
    lgf = jnp.broadcast_to(lgf_row, (c, d))
    lgb = jnp.broadcast_to(lgb_row, (c, d))
    pos = lax.broadcasted_iota(jnp.int32, (c, d), 0).astype(F32)
    kdf = jnp.exp(lgf * (c - 1.0 - pos))
    kdb = jnp.exp(lgb * pos)
    qdf = jnp.exp(lgf * (pos + 1.0))
    qdb = jnp.exp(lgb * (c - pos))
    cdf = jnp.exp(lgf_row * float(c))
    cdb = jnp.exp(lgb_row * float(c))
    diff = (lax.broadcasted_iota(jnp.int32, (c, c), 0) - lax.broadcasted_iota(jnp.int32, (c, c), 1)).astype(F32)
    lgf_cc = jnp.broadcast_to(lgf_row[:, :1], (c, c))
    lgb_cc = jnp.broadcast_to(lgb_row[:, :1], (c, c))
    dmat = (jnp.where(diff >= 0, jnp.exp(lgf_cc * jnp.maximum(diff, 0.0)), 0.0)
            + jnp.where(diff <= 0, jnp.exp(lgb_cc * jnp.maximum(-diff, 0.0)), 0.0))

    cpos = lax.broadcasted_iota(jnp.int32, (ctx_len, d), 0).astype(F32)
    ckf = ck_ref[...].astype(F32) * scale
    cv = cv_ref[...]
    wf = jnp.exp(jnp.broadcast_to(lgf_row, (ctx_len, d)) * (ctx_len - 1.0 - cpos))
    wb = jnp.exp(jnp.broadcast_to(lgb_row, (ctx_len, d)) * cpos)
    s_f = _dot_tn((ckf * wf).astype(BF16), cv)
    s_b = _dot_tn((ckf * wb).astype(BF16), cv)

    rope_rows = 512
    pmat = _rope_partner_matrix()

    def rope_blk(i, carry):
        r0 = pl.multiple_of(i * rope_rows, rope_rows)
        cs = cos_ref[pl.ds(r0, rope_rows), :]
        sn = sin_ref[pl.ds(r0, rope_rows), :]
        qr_ref[pl.ds(r0, rope_rows), :] = _rope(q_ref[pl.ds(r0, rope_rows), :], pmat, cs, sn)
        kr_ref[pl.ds(r0, rope_rows), :] = _rope(k_ref[pl.ds(r0, rope_rows), :], pmat, cs, sn) * scale
        return carry

    lax.fori_loop(0, seq // rope_rows, rope_blk, 0, unroll=4)

    def scan_blk(n, carry):
        s, t = carry
        r0 = pl.multiple_of(n * c, c)
        kvf = _dot_tn((kr_ref[pl.ds(r0, c), :] * kdf).astype(BF16), v_ref[pl.ds(r0, c), :])
        sf_ref[n] = s.astype(BF16)
        m = n_chunks - 1 - n
        m0 = pl.multiple_of(m * c, c)
        kvb = _dot_tn((kr_ref[pl.ds(m0, c), :] * kdb).astype(BF16), v_ref[pl.ds(m0, c), :])
        sb_ref[m] = t.astype(BF16)
        return s * cdf + kvf, t * cdb + kvb

    lax.fori_loop(0, n_chunks, scan_blk, (s_f, s_b), unroll=16)

    gn = gn_ref[...]

    def out_blk(n, carry):
        r0 = pl.multiple_of(n * c, c)
        qc = qr_ref[pl.ds(r0, c), :]
        kc = kr_ref[pl.ds(r0, c), :]
        a = _dot_nt(qc.astype(BF16), kc.astype(BF16))
        o = (_dot((a * dmat).astype(BF16), v_ref[pl.ds(r0, c), :])
             + _dot((qc * qdf).astype(BF16), sf_ref[n])
             + _dot((qc * qdb).astype(BF16), sb_ref[n]))
        mu = jnp.mean(o, axis=-1, keepdims=True)
        var = jnp.mean(jnp.square(o - mu), axis=-1, keepdims=True)
        y = (o - mu) * lax.rsqrt(var + EPS) * gn
        o_ref[pl.ds(r0, c), :] = (y * _silu(g_ref[pl.ds(r0, c), :].astype(F32))).astype(o_ref.dtype)
        return carry

    lax.fori_loop(0, n_chunks, out_blk, 0, unroll=16)


def _rope_tables(seq):
    axis_dim = HEAD_DIM // 2
    inv_freq = ROPE_BASE ** (-jnp.arange(0, axis_dim, 2, dtype=F32) / axis_dim)
    rows = seq // GRID_W
    ang_r = jnp.arange(rows, dtype=F32)[:, None] * inv_freq
    ang_c = jnp.arange(GRID_W, dtype=F32)[:, None] * inv_freq
    by_row = lambda a: jnp.repeat(a, GRID_W, axis=0)
    by_col = lambda a: jnp.tile(a, (rows, 1))
    cr, sr, cc, sc = by_row(jnp.cos(ang_r)), by_row(jnp.sin(ang_r)), by_col(jnp.cos(ang_c)), by_col(jnp.sin(ang_c))
    return jnp.concatenate([cr, cr, cc, cc], axis=-1), jnp.concatenate([-sr, sr, -sc, sc], axis=-1)


def _retention(proj, cproj, decay_f, decay_b, gn_w, batch, seq, ctx_len, cc, w_mod, b_mod, mod_from, w_out):
    hb = RET_WIDTH // HEAD_DIM
    cos, sin = _rope_tables(seq)
    n_chunks = seq // RET_BLOCK
    blk = lambda rows, fn: pl.BlockSpec((rows, HEAD_DIM), fn)
    n_steps = RET_HEADS * batch
    step = lambda h, b: h * batch + b
    mod_rows, d = cc.shape
    late = w_mod.shape[1] - mod_from
    mcols, orows = late // n_steps, w_out.shape[0] // n_steps
    side_in = [
        pl.BlockSpec((mod_rows, d), lambda h, b: (0, 0)),
        pl.BlockSpec((d, mcols), lambda h, b: (0, mod_from // mcols + step(h, b))),
        pl.BlockSpec((1, mcols), lambda h, b: (0, mod_from // mcols + step(h, b))),
        pl.BlockSpec((orows, w_out.shape[1]), lambda h, b: (step(h, b), 0)),
    ]
    side_out = [
        pl.BlockSpec((mod_rows, 1, mcols), lambda h, b: (0, 0, step(h, b))),
        pl.BlockSpec((orows, w_out.shape[1]), lambda h, b: (step(h, b), 0)),
    ]
    side_shape = [jax.ShapeDtypeStruct((mod_rows, 1, late), F32), jax.ShapeDtypeStruct(w_out.shape, BF16)]
    return pl.pallas_call(
        _ret_body,
        grid=(RET_HEADS, batch),
        in_specs=[
            pl.BlockSpec(memory_space=pltpu.SMEM),
            pl.BlockSpec(memory_space=pltpu.SMEM),
            blk(1, lambda h, b: (0, h)),
            blk(seq, lambda h, b: (0, 0)),
            blk(seq, lambda h, b: (0, 0)),
            blk(seq, lambda h, b: (b, 5 * hb + h)),
            blk(seq, lambda h, b: (b, 2 * hb + h)),
            blk(seq, lambda h, b: (b, 3 * hb + h)),
            blk(seq, lambda h, b: (b, 6 * hb + h)),
            blk(ctx_len, lambda h, b: (b, 2 * hb + h)),
            blk(ctx_len, lambda h, b: (b, 3 * hb + h)),
        ] + side_in,
        out_specs=[blk(seq, lambda h, b: (b, h))] + side_out,
        out_shape=[jax.ShapeDtypeStruct((batch * seq, RET_WIDTH), BF16)] + side_shape,
        scratch_shapes=[
            pltpu.VMEM((seq, HEAD_DIM), F32),
            pltpu.VMEM((seq, HEAD_DIM), F32),
            pltpu.VMEM((n_chunks, HEAD_DIM, HEAD_DIM), BF16),
            pltpu.VMEM((n_chunks, HEAD_DIM, HEAD_DIM), BF16),
        ],
        compiler_params=_cparams(2),
        name="retention",
    )(decay_f.astype(F32), decay_b.astype(F32), gn_w.reshape(1, RET_WIDTH), cos, sin, proj, proj, proj, proj, cproj, cproj,
      cc, w_mod, b_mod.reshape(1, -1), w_out)


def _split_bf16(x):
    hi = x.astype(BF16)
    lo = (x - hi.astype(F32)).astype(BF16)
    return hi, lo


OUTPROJ_ROW_SPLITS = 2


def _outproj_body(na_ref, ret_ref, w_ref, x_ref, gate_ref, nw_ref, sh_ref, sc_ref, wr_ref,
                  h1_ref, u2_ref, lg_ref):
    half = na_ref.shape[1]
    n_e = wr_ref.shape[0]
    tm = x_ref.shape[0]
    wh, wl = _split_bf16(wr_ref[...])
    w_router = jnp.concatenate([wh, wl], axis=0)
    for r in range(OUTPROJ_ROW_SPLITS):
        rows = pl.ds(r * (tm // OUTPROJ_ROW_SPLITS), tm // OUTPROJ_ROW_SPLITS)
        mix = _dot(na_ref[rows, :], w_ref[:half, :]) + _dot(ret_ref[rows, :], w_ref[half:, :])
        h1 = x_ref[rows, :] + gate_ref[0] * mix
        h1_ref[rows, :] = h1
        u2 = _rms_mod(h1, nw_ref[...], sh_ref[0], sc_ref[0])
        u2_ref[rows, :] = u2
        both = _dot_nt(w_router, u2.astype(BF16))
        lg_ref[:, rows] = both[:n_e] + both[n_e:]


def _out_projection(na, ret, w_out_bf16, x2d, mod3, norm_w, w_router_t, seq, tm=OUTPROJ_ROW_TILE):
    rows, d = x2d.shape
    half = na.shape[1]
    per_b = seq // tm
    return pl.pallas_call(
        _outproj_body,
        grid=(rows // tm,),
        in_specs=[
            pl.BlockSpec((tm, half), lambda i: (i, 0)),
            pl.BlockSpec((tm, half), lambda i: (i, 0)),
            pl.BlockSpec((2 * half, d), lambda i: (0, 0)),
            pl.BlockSpec((tm, d), lambda i: (i, 0)),
            pl.BlockSpec((1, 1, d), lambda i: (i // per_b, 0, MOD_LATE_GATE_MIX)),
            pl.BlockSpec((1, d), lambda i: (0, 0)),
            pl.BlockSpec((1, 1, d), lambda i: (i // per_b, 0, MOD_LATE_SHIFT_FFN)),
            pl.BlockSpec((1, 1, d), lambda i: (i // per_b, 0, MOD_LATE_SCALE_FFN)),
            pl.BlockSpec((N_EXPERTS, d), lambda i: (0, 0)),
        ],
        out_specs=[
            pl.BlockSpec((tm, d), lambda i: (i, 0)),
            pl.BlockSpec((tm, d), lambda i: (i, 0)),
            pl.BlockSpec((N_EXPERTS, tm), lambda i: (0, i)),
        ],
        out_shape=[
            jax.ShapeDtypeStruct((rows, d), F32),
            jax.ShapeDtypeStruct((rows, d), F32),
            jax.ShapeDtypeStruct((N_EXPERTS, rows), F32),
        ],
        compiler_params=_cparams(1),
        name="out_projection",
    )(na, ret, w_out_bf16, x2d, mod3, norm_w.reshape(1, d), mod3, mod3, w_router_t)


def _prefix_incl_lanes(x, tri):
    r, l = x.shape
    nb = l // LANES
    xs = jnp.concatenate([x[:, t * LANES:(t + 1) * LANES] for t in range(nb)], axis=0).astype(BF16)
    p = _dot(xs, tri)
    outs, run = [], jnp.zeros((r, 1), F32)
    for t in range(nb):
        blk = p[t * r:(t + 1) * r] + run
        outs.append(blk)
        run = blk[:, LANES - 1:LANES]
    return jnp.concatenate(outs, axis=1)


NOT_ROUTED = -(1 << 20)


def _route_body(lg_ref, gidx_ref, gate_ref, slot_ref, before_ref, *, cap):
    b = pl.program_id(0)
    n_e, seq = lg_ref.shape
    kf = float(cap)
    lg = lg_ref[...]
    ex = jnp.exp(lg - jnp.max(lg, axis=0, keepdims=True))
    aff = ex / jnp.sum(ex, axis=0, keepdims=True)

    def cond(c):
        return (c[0] < 4096) & (c[5] > 0.5)

    def step(c):
        it, lo, hi, thr, done, _ = c
        mid = 0.5 * (lo + hi)
        above = jnp.sum((aff > mid).astype(F32), axis=1, keepdims=True)
        hit = above == kf
        stuck = (mid <= lo) | (mid >= hi)
        active = done < 0.5
        thr = jnp.where(active & hit, mid, jnp.where(active & stuck, hi, thr))
        go = active & ~(hit | stuck)
        ge = above >= kf
        lo = jnp.where(go & ge, mid, lo)
        hi = jnp.where(go & ~ge, mid, hi)
        done = jnp.where(active & (hit | stuck), 1.0, done)
        return it + 1, lo, hi, thr, done, jnp.sum(1.0 - done)

    col = lambda v: jnp.full((n_e, 1), v, F32)
    init = (jnp.int32(0), col(-1.0), col(2.0), col(0.0), col(0.0), jnp.float32(n_e))
    thr = lax.while_loop(cond, step, init)[3]

    ii = lax.broadcasted_iota(jnp.int32, (LANES, LANES), 0)
    jj = lax.broadcasted_iota(jnp.int32, (LANES, LANES), 1)
    tri = (ii <= jj).astype(BF16)
    gt = aff > thr
    eq = (aff == thr).astype(F32)
    need = kf - jnp.sum(gt.astype(F32), axis=1, keepdims=True)
    eq_before = _prefix_incl_lanes(eq, tri) - eq
    mask = jnp.where(gt | ((eq > 0.5) & (eq_before < need)), 1.0, 0.0)

    slot = _prefix_incl_lanes(mask, tri) - mask
    before_ref[0] = slot.astype(jnp.int32)
    slot_ref[0] = jnp.where(mask > 0.5, slot, float(NOT_ROUTED)).astype(jnp.int32)

    tok = lax.broadcasted_iota(jnp.int32, (1, seq), 1).astype(F32)
    tok_hi = jnp.floor(tok * (1.0 / 64))
    tok_lo = tok - 64.0 * tok_hi
    n_hi = ROUTE_SLOT_HI
    n_lo = cap // n_hi
    hi_iota = lax.broadcasted_iota(jnp.int32, (n_hi, seq), 0).astype(F32)
    lo_iota = lax.broadcasted_iota(jnp.int32, (n_lo, seq), 0).astype(F32)
    for e in range(n_e):
        se = slot[e:e + 1]
        s_hi = jnp.floor(se * (1.0 / n_lo))
        s_lo = se - n_lo * s_hi
        in_hi = jnp.where((s_hi == hi_iota) & (mask[e:e + 1] > 0.5), 1.0, 0.0)
        in_lo = (s_lo == lo_iota).astype(BF16)
        a = aff[e:e + 1]
        a_hi = a.astype(BF16).astype(F32)
        a_mid = (a - a_hi).astype(BF16).astype(F32)
        a_lo = a - a_hi - a_mid
        vals = (tok_hi, tok_lo, a_hi, a_mid, a_lo)
        lhs = jnp.concatenate([in_hi * v for v in vals], axis=0).astype(BF16)
        got = _dot_nt(lhs, in_lo)
        part = lambda r: got[r * n_hi:(r + 1) * n_hi]
        gidx_ref[0, e] = (part(0) * 64.0 + part(1)).astype(jnp.int32) + b * seq
        gate_ref[0, e] = part(2) + part(3) + part(4)


def _routing(logits_t, batch, seq):
    n_e = logits_t.shape[0]
    cap = CAPACITY_FACTOR * seq // n_e
    n_hi, n_lo = ROUTE_SLOT_HI, cap // ROUTE_SLOT_HI
    bec = pl.BlockSpec((1, n_e, n_hi, n_lo), lambda b: (b, 0, 0, 0))
    bel = pl.BlockSpec((1, n_e, seq), lambda b: (b, 0, 0))
    gidx, gates, slot, before = pl.pallas_call(
        functools.partial(_route_body, cap=cap),
        grid=(batch,),
        in_specs=[pl.BlockSpec((n_e, seq), lambda b: (0, b))],
        out_specs=[bec, bec, bel, bel],
        out_shape=[
            jax.ShapeDtypeStruct((batch, n_e, n_hi, n_lo), jnp.int32),
            jax.ShapeDtypeStruct((batch, n_e, n_hi, n_lo), F32),
            jax.ShapeDtypeStruct((batch, n_e, seq), jnp.int32),
            jax.ShapeDtypeStruct((batch, n_e, seq), jnp.int32),
        ],
        compiler_params=_cparams(1),
        name="routing",
    )(logits_t)
    flat = lambda a: a.reshape(batch, n_e, cap)
    return flat(gidx), flat(gates), slot, before


def _hbm_row(ref, r):
    return ref.at[lax.shift_right_logical(r, 3), pl.ds(r & (SUBLANES - 1), 1)]


MOE_ROW_SPLITS = 2
MOE_GATE_ROWS = 4 * BF16_ROWS


def _moe_body(gidx_ref, gnext_ref, gate_ref, u2_hbm, wg_ref, wu_ref, wd_ref, y_ref,
              stage_ref, xe_ref, acc_ref, gcol_ref, gsem, *, n_f):
    e = pl.program_id(0)
    f = pl.program_id(1)
    n_e = pl.num_programs(0)
    m, d = y_ref.shape
    sub = SUBLANES
    n_tiles = m // sub
    assert n_f >= 2
    per_step = -(-n_tiles // (n_f - 1))
    last_count = n_tiles - per_step * (n_f - 2)

    def gather_copy(idx_ref, i, k):
        return pltpu.make_async_copy(_hbm_row(u2_hbm, idx_ref[0, i * sub + k]), stage_ref.at[i, pl.ds(k, 1)], gsem)

    def for_rows(fn, tiles_per_trip=4):
        def body(t, c):
            for kk in range(tiles_per_trip * sub):
                fn(t * tiles_per_trip + kk // sub, kk % sub)
            return c
        lax.fori_loop(0, n_tiles // tiles_per_trip, body, 0)

    wait_gather = lambda: for_rows(lambda i, k: gather_copy(gidx_ref, i, k).wait())

    def ffn_step(first, gather_tiles):
        i0 = (f - 1) * per_step
        for kk in range(gather_tiles * sub):
            gather_copy(gnext_ref, i0 + kk // sub, kk % sub).start()
        wg, wu, wd = wg_ref[...].astype(BF16), wu_ref[...].astype(BF16), wd_ref[...].astype(BF16)
        if first:
            gcol_ref[...] = jnp.transpose(jnp.broadcast_to(gate_ref[...], (LANES, m)))
        part = m // MOE_ROW_SPLITS
        for r in range(MOE_ROW_SPLITS):
            rows = pl.ds(r * part, part)
            if first:
                x = stage_ref[pl.ds(r * (part // sub), part // sub)].reshape(part, d).astype(BF16)
                xe_ref[rows, :] = x
            else:
                x = xe_ref[rows, :]
            hid = _silu(_dot(x, wg)) * _dot(x, wu)
            out = _dot(hid.astype(BF16), wd)
            acc_ref[rows, :] = out if first else acc_ref[rows, :] + out

    @pl.when(f == 0)
    def _():
        @pl.when(e == 0)
        def _():
            for_rows(lambda i, k: gather_copy(gidx_ref, i, k).start())

        wait_gather()
        ffn_step(True, 0)

    if n_f > 2:
        pl.when((f > 0) & (f < n_f - 1))(lambda: ffn_step(False, per_step))

    @pl.when(f == n_f - 1)
    def _():
        ffn_step(False, last_count)

        @pl.when(f > 0)
        def _():
            for r0 in range(0, m, MOE_GATE_ROWS):
                rows = slice(r0, r0 + MOE_GATE_ROWS)
                g = jnp.tile(gcol_ref[rows, :], (1, d // LANES))
                y_ref[rows, :] = (acc_ref[rows, :] * g).astype(y_ref.dtype)

        @pl.when(e == n_e - 1)
        def _():
            wait_gather()


def _moe_ffn(gidx, gates, u2, w_gate, w_up, w_down, tf=MOE_F_TILE):
    n_e, _, m = gidx.shape
    rows, d = u2.shape
    ff = w_gate.shape[2]
    n_f = ff // tf
    sub = SUBLANES
    smem = lambda fn: pl.BlockSpec((None, 1, m), fn, memory_space=pltpu.SMEM)
    return pl.pallas_call(
        functools.partial(_moe_body, n_f=n_f),
        grid=(n_e, n_f),
        in_specs=[
            smem(lambda e, f: (e, 0, 0)),
            smem(lambda e, f: (jnp.minimum(e + 1, n_e - 1), 0, 0)),
            pl.BlockSpec((None, 1, m), lambda e, f: (e, 0, 0)),
            pl.BlockSpec(memory_space=pl.ANY),
            pl.BlockSpec((None, d, tf), lambda e, f: (e, 0, f)),
            pl.BlockSpec((None, d, tf), lambda e, f: (e, 0, f)),
            pl.BlockSpec((None, tf, d), lambda e, f: (e, f, 0)),
        ],
        out_specs=pl.BlockSpec((m, d), lambda e, f: (e, 0)),
        out_shape=jax.ShapeDtypeStruct((n_e * m, d), BF16),
        scratch_shapes=[
            pltpu.VMEM((m // sub, sub, d), F32),
            pltpu.VMEM((m, d), BF16),
            pltpu.VMEM((m, d), F32),
            pltpu.VMEM((m, LANES), F32),
            pltpu.SemaphoreType.DMA,
        ],
        compiler_params=_cparams(2),
        name="moe_ffn",
    )(gidx, gidx, gates, u2.reshape(rows // sub, sub, d), w_gate, w_up, w_down)


COMBINE_TOKENS = 512
COMBINE_PIECE = 96


def _combine_body(start_ref, npiece_ref, h1_ref, y_hbm, slot_ref, gate_ref, nw_ref, o_ref,
                  acc_ref, buf_ref, xbuf_ref, sem, xsem, *, m, cap, per_b):
    i = pl.program_id(0)
    n_tiles = pl.num_programs(0)
    n_e = slot_ref.shape[1]
    t, p = COMBINE_TOKENS, COMBINE_PIECE
    total = n_e * m
    cur = i % 2
    b = i // per_b

    def piece_copy(tile, e, dst):
        st = pl.multiple_of(start_ref[tile * n_e + e], BF16_ROWS)
        return pltpu.make_async_copy(y_hbm.at[pl.ds(st, p)], buf_ref.at[dst, pl.ds(e * p, p)], sem.at[dst])

    @pl.when(i == 0)
    def _():
        for e in range(n_e):
            piece_copy(0, e, 0).start()

    @pl.when(i + 1 < n_tiles)
    def _():
        for e in range(n_e):
            piece_copy(i + 1, e, 1 - cur).start()

    for e in range(n_e):
        piece_copy(i, e, cur).wait()

    slots = slot_ref[0]
    riota = lax.broadcasted_iota(jnp.int32, (p, t), 0)

    def local_row(e, st):
        return slots[e:e + 1, :] + (e * m + b * cap - st)

    sel = jnp.concatenate(
        [(local_row(e, start_ref[i * n_e + e]) == riota).astype(BF16) for e in range(n_e)], axis=0)
    acc_ref[...] = _dot_tn(sel, buf_ref[cur])

    for e in range(n_e):
        st0 = start_ref[i * n_e + e]

        def extra(k, carry, e=e, st0=st0):
            want = st0 + k * p
            st = pl.multiple_of(jnp.minimum(want, total - p), BF16_ROWS)
            cp = pltpu.make_async_copy(y_hbm.at[pl.ds(st, p)], xbuf_ref, xsem)
            cp.start()
            cp.wait()
            blk = ((local_row(e, st) == riota) & (riota >= want - st)).astype(BF16)
            acc_ref[...] += _dot_tn(blk, xbuf_ref[...])
            return carry

        lax.fori_loop(1, npiece_ref[i * n_e + e], extra, 0)

    h2 = h1_ref[...] + gate_ref[0] * acc_ref[...]
    ms = jnp.mean(h2 * h2, axis=-1, keepdims=True)
    o_ref[...] = h2 * lax.rsqrt(ms + EPS) * nw_ref[...]


def _combine_pieces(before, cap):
    batch, n_e, seq = before.shape
    t, p = COMBINE_TOKENS, COMBINE_PIECE
    m = batch * cap
    total = n_e * m
    bounds = jnp.concatenate([before[:, :, ::t], jnp.full((batch, n_e, 1), cap, jnp.int32)], axis=2)
    base = (jnp.arange(n_e, dtype=jnp.int32) * m)[None, :, None] + (jnp.arange(batch, dtype=jnp.int32) * cap)[:, None, None]
    first = base + bounds[:, :, :-1]
    end = base + bounds[:, :, 1:]
    start = jnp.minimum(first // BF16_ROWS * BF16_ROWS, total - p)
    npiece = jnp.maximum((end - start + p - 1) // p, 1)
    by_tile = lambda a: a.transpose(0, 2, 1).reshape(-1)
    return by_tile(start), by_tile(npiece)


def _combine(h1, y, slot, before, mod3, final_w, seq, cap):
    rows, d = h1.shape
    batch, n_e, _ = slot.shape
    t, p = COMBINE_TOKENS, COMBINE_PIECE
    per_b = seq // t
    start, npiece = _combine_pieces(before, cap)
    grid_spec = pltpu.PrefetchScalarGridSpec(
        num_scalar_prefetch=2,
        grid=(rows // t,),
        in_specs=[
            pl.BlockSpec((t, d), lambda i, st, npc: (i, 0)),
            pl.BlockSpec(memory_space=pl.ANY),
            pl.BlockSpec((1, n_e, t), lambda i, st, npc: (i // per_b, 0, i % per_b)),
            pl.BlockSpec((1, 1, d), lambda i, st, npc: (i // per_b, 0, MOD_LATE_GATE_FFN)),
            pl.BlockSpec((1, d), lambda i, st, npc: (0, 0)),
        ],
        out_specs=pl.BlockSpec((t, d), lambda i, st, npc: (i, 0)),
        scratch_shapes=[
            pltpu.VMEM((t, d), F32),
            pltpu.VMEM((2, n_e * p, d), y.dtype),
            pltpu.VMEM((p, d), y.dtype),
            pltpu.SemaphoreType.DMA((2,)),
            pltpu.SemaphoreType.DMA,
        ],
    )
    return pl.pallas_call(
        functools.partial(_combine_body, m=batch * cap, cap=cap, per_b=per_b),
        grid_spec=grid_spec,
        out_shape=jax.ShapeDtypeStruct((rows, d), F32),
        compiler_params=_cparams(1),
        name="combine",
    )(start, npiece, h1, y, slot, mod3, final_w.reshape(1, d))


def kernel(x, c, ctx, c_ctx, w_mod, b_mod, norm_mix_w, norm_ffn_w, w_in, na_rpb, ret_decay_fwd,
           ret_decay_bwd, ret_gn_w, w_out, w_router, w_gate, w_up, w_down, final_norm_w):
    batch, seq, d = x.shape
    ctx_len = ctx.shape[1]
    assert w_mod.shape[0] == 1, "one trunk layer"
    n_e = w_router.shape[2]
    cap = CAPACITY_FACTOR * seq // n_e

    mod_rows = SUBLANES
    cc = jnp.concatenate([c, c_ctx[None], jnp.zeros((mod_rows - batch - 1, d), c.dtype)], axis=0)
    mod3 = _modulation(cc, w_mod[0], b_mod[0], MOD_EARLY * d)

    x2d = x.reshape(batch * seq, d)
    tm = INPROJ_ROW_TILE
    cproj, w_in_bf16 = _ctx_projection(ctx.reshape(batch * ctx_len, d), norm_mix_w[0], mod3, batch,
                                       w_in[0], KV_COLS)
    proj = _in_projection(x2d, norm_mix_w[0], mod3, lambda i: i // (seq // tm), w_in_bf16,
                          w_in.shape[2], tm, w_in.shape[2] // INPROJ_COL_TILES)

    na = _neighbourhood_attention(proj, cproj, na_rpb[0], batch, seq, ctx_len)
    ret, mod_late3, w_out_bf16 = _retention(proj, cproj, ret_decay_fwd[0], ret_decay_bwd[0], ret_gn_w[0],
                                            batch, seq, ctx_len, cc, w_mod[0], b_mod[0], MOD_EARLY * d, w_out[0])

    h1, u2, logits_t = _out_projection(na, ret, w_out_bf16, x2d, mod_late3, norm_ffn_w[0], w_router[0].T, seq)

    gidx, gates, slot, before = _routing(logits_t, batch, seq)
    per_expert = lambda a: a.transpose(1, 0, 2).reshape(n_e, 1, batch * cap)
    y = _moe_ffn(per_expert(gidx), per_expert(gates), u2, w_gate[0], w_up[0], w_down[0])
    out = _combine(h1, y, slot, before, mod_late3, final_norm_w, seq, cap)
    return out.reshape(batch, seq, d)
```

```python
import functools

import jax
import jax.numpy as jnp
from jax import lax
from jax.experimental import pallas as pl
from jax.experimental.pallas import tpu as pltpu

GRID_W = 64
HEAD_DIM = 128
NA_HEADS = 8
RET_HEADS = 8
NA_WIDTH = NA_HEADS * HEAD_DIM
RET_WIDTH = RET_HEADS * HEAD_DIM
WIN_ROWS = 8
WIN_COLS = 16
RET_BLOCK = 256
ROPE_BASE = 10000.0
N_EXPERTS = 16
CAPACITY_FACTOR = 2
N_MOD = 6
MOD_EARLY = 2
MOD_LATE_GATE_MIX, MOD_LATE_SHIFT_FFN, MOD_LATE_SCALE_FFN, MOD_LATE_GATE_FFN = 0, 1, 2, 3
EPS = 1e-6
NEG_INF = -1e30
LOG2_E = 1.4426950408889634
KV_COLS = 2 * NA_WIDTH + 2 * RET_WIDTH

F32 = jnp.float32
BF16 = jnp.bfloat16
LANES = 128
SUBLANES = 8
BF16_ROWS = 16
ROUTE_SLOT_HI = 16
MIB = 1024 * 1024
VMEM_LIMIT_V7X = 56 * MIB

MOD_COL_TILE = 1024
CTX_COL_TILE = 1024
INPROJ_ROW_TILE = 1024
INPROJ_COL_TILES = 4
OUTPROJ_ROW_TILE = 512
MOE_F_TILE = 256


def _cparams(n_axes):
    return pltpu.CompilerParams(
        dimension_semantics=("arbitrary",) * n_axes, vmem_limit_bytes=VMEM_LIMIT_V7X)


def _silu(x):
    return x * jax.nn.sigmoid(x)


def _dot(a, b):
    return jnp.dot(a, b, preferred_element_type=F32)


def _dot_nt(a, b):
    return lax.dot_general(a, b, (((1,), (1,)), ((), ())), preferred_element_type=F32)


def _dot_tn(a, b):
    return lax.dot_general(a, b, (((0,), (0,)), ((), ())), preferred_element_type=F32)


def _mod_body(c_ref, w_ref, b_ref, o_ref):
    a = _silu(c_ref[...]).astype(BF16)
    o_ref[:, 0, :] = _dot(a, w_ref[...].astype(BF16)) + b_ref[...]


def _modulation(cc, w_mod, b_mod, n_cols, tn=MOD_COL_TILE):
    rows, d = cc.shape
    n = n_cols
    return pl.pallas_call(
        _mod_body,
        grid=(n // tn,),
        in_specs=[
            pl.BlockSpec((rows, d), lambda j: (0, 0)),
            pl.BlockSpec((d, tn), lambda j: (0, j)),
            pl.BlockSpec((1, tn), lambda j: (0, j)),
        ],
        out_specs=pl.BlockSpec((rows, 1, tn), lambda j: (0, 0, j)),
        out_shape=jax.ShapeDtypeStruct((rows, 1, n), F32),
        compiler_params=_cparams(1),
        name="modulation",
    )(cc, w_mod, b_mod.reshape(1, -1))


def _rms_mod(x, nw, shift, scale):
    ms = jnp.mean(x * x, axis=-1, keepdims=True)
    y = x * lax.rsqrt(ms + EPS) * nw
    return y * (1.0 + scale) + shift


INPROJ_NORM_SPLITS = 4


def _inproj_body(x_ref, nw_ref, sh_ref, sc_ref, w_ref, o_ref, u_ref):
    j = pl.program_id(1)
    tm = x_ref.shape[0]

    @pl.when(j == 0)
    def _():
        part = tm // INPROJ_NORM_SPLITS
        for r in range(INPROJ_NORM_SPLITS):
            rows = pl.ds(r * part, part)
            u = _rms_mod(x_ref[rows, :], nw_ref[...], sh_ref[0], sc_ref[0]).astype(BF16)
            u_ref[rows, :] = u
            o_ref[rows, :] = _dot(u, w_ref[...]).astype(o_ref.dtype)

    @pl.when(j > 0)
    def _():
        o_ref[...] = _dot(u_ref[...], w_ref[...]).astype(o_ref.dtype)


def _ctx_proj_body(x_ref, nw_ref, sh_ref, sc_ref, w_ref, o_ref, wb_ref, u_ref, *, kv_tiles):
    j = pl.program_id(0)

    @pl.when(j == 0)
    def _():
        u_ref[...] = _rms_mod(x_ref[...], nw_ref[...], sh_ref[0], sc_ref[0]).astype(BF16)

    wb_ref[...] = w_ref[...].astype(BF16)

    @pl.when(j < kv_tiles)
    def _():
        o_ref[...] = _dot(u_ref[...], wb_ref[...]).astype(o_ref.dtype)


def _ctx_projection(x2d, norm_w, mod3, mod_row, w_in, kv_cols, tn=CTX_COL_TILE):
    rows, d = x2d.shape
    n = w_in.shape[1]
    kv_tiles = kv_cols // tn
    return pl.pallas_call(
        functools.partial(_ctx_proj_body, kv_tiles=kv_tiles),
        grid=(n // tn,),
        in_specs=[
            pl.BlockSpec((rows, d), lambda j: (0, 0)),
            pl.BlockSpec((1, d), lambda j: (0, 0)),
            pl.BlockSpec((1, 1, d), lambda j: (mod_row, 0, 0)),
            pl.BlockSpec((1, 1, d), lambda j: (mod_row, 0, 1)),
            pl.BlockSpec((d, tn), lambda j: (0, j)),
        ],
        out_specs=[
            pl.BlockSpec((rows, tn), lambda j: (0, jnp.minimum(j, kv_tiles - 1))),
            pl.BlockSpec((d, tn), lambda j: (0, j)),
        ],
        out_shape=[jax.ShapeDtypeStruct((rows, kv_cols), BF16), jax.ShapeDtypeStruct((d, n), BF16)],
        scratch_shapes=[pltpu.VMEM((rows, d), BF16)],
        compiler_params=_cparams(1),
        name="ctx_projection",
    )(x2d, norm_w.reshape(1, d), mod3, mod3, w_in)


def _in_projection(x2d, norm_w, mod3, mod_row_fn, w_in, n_cols, tm, tn):
    rows, d = x2d.shape
    return pl.pallas_call(
        _inproj_body,
        grid=(rows // tm, n_cols // tn),
        in_specs=[
            pl.BlockSpec((tm, d), lambda i, j: (i, 0)),
            pl.BlockSpec((1, d), lambda i, j: (0, 0)),
            pl.BlockSpec((1, 1, d), lambda i, j: (mod_row_fn(i), 0, 0)),
            pl.BlockSpec((1, 1, d), lambda i, j: (mod_row_fn(i), 0, 1)),
            pl.BlockSpec((d, tn), lambda i, j: (0, j)),
        ],
        out_specs=pl.BlockSpec((tm, tn), lambda i, j: (i, j)),
        out_shape=jax.ShapeDtypeStruct((rows, n_cols), BF16),
        scratch_shapes=[pltpu.VMEM((tm, d), BF16)],
        compiler_params=_cparams(2),
        name="in_projection",
    )(x2d, norm_w.reshape(1, d), mod3, mod3, w_in)


NA_QROWS = 4
NA_KROWS = NA_QROWS + WIN_ROWS
NA_ROW_SPLITS = 2

def _na_row_offset(tile_kind, i, w, rows):
    half = WIN_ROWS // 2
    if tile_kind == 0:
        r, key = i, w
    elif tile_kind == 1:
        r, key = NA_QROWS + i, NA_QROWS - half + w
    else:
        r, key = rows - NA_QROWS + i, rows - NA_KROWS + w
    start = min(max(r - half, 0), rows - WIN_ROWS)
    if not (start <= key < start + WIN_ROWS):
        return None
    return key - r + (WIN_ROWS - 1)


def _na_build_bias(rpb_ref, bias_ref, h, rows):
    w = GRID_W
    cq = lax.broadcasted_iota(jnp.int32, (w, 2 * w), 0)
    ck = lax.broadcasted_iota(jnp.int32, (w, 2 * w), 1) % w
    col_start = jnp.clip(cq - WIN_COLS // 2, 0, w - WIN_COLS)
    col_ok = (ck >= col_start) & (ck < col_start + WIN_COLS)
    col_off = jnp.clip(ck - cq, -(WIN_COLS - 1), WIN_COLS - 1) + (WIN_COLS - 1)
    neg = jnp.full((w, 2 * w), NEG_INF, F32)
    n_ro, n_co = 2 * WIN_ROWS - 1, 2 * WIN_COLS - 1
    tabs = []
    for ro in range(n_ro):
        t = jnp.zeros((w, 2 * w), F32)
        for j in range(n_co):
            t = jnp.where(col_off == j, rpb_ref[h, ro * n_co + j] * (HEAD_DIM ** 0.5), t)
        tabs.append(jnp.where(col_ok, t, neg))
    left = lax.broadcasted_iota(jnp.int32, (w, 2 * w), 1) < w
    for kind in range(3):
        for i in range(NA_QROWS):
            for wp in range(NA_KROWS // 2):
                ra = _na_row_offset(kind, i, 2 * wp, rows)
                rb = _na_row_offset(kind, i, 2 * wp + 1, rows)
                ta = neg if ra is None else tabs[ra]
                tb = neg if rb is None else tabs[rb]
                blk = ta if ra == rb else jnp.where(left, ta, tb)
                bias_ref[kind, i * w:(i + 1) * w, wp * 2 * w:(wp + 1) * 2 * w] = blk


def _na_body(rpb_ref, q_ref, k_ref, v_ref, ck_ref, cv_ref, o_ref, bias_ref, va_ref, cva_ref, *, rows):
    h = pl.program_id(0)
    w = GRID_W
    tq, tk = NA_QROWS * w, NA_KROWS * w
    n_tiles = rows // NA_QROWS
    scale = HEAD_DIM ** -0.5

    @pl.when(pl.program_id(1) == 0)
    def _():
        _na_build_bias(rpb_ref, bias_ref, h, rows)

    def with_ones(v):
        ones = (lax.broadcasted_iota(jnp.int32, v.shape, 1) == 0).astype(v.dtype)
        return jnp.concatenate([v, ones], axis=1)

    va_ref[...] = with_ones(v_ref[...])
    cva_ref[...] = with_ones(cv_ref[...])

    def attend(t, part):
        kind = 0 if t == 0 else (2 if t == n_tiles - 1 else 1)
        k0 = min(max(t * NA_QROWS - WIN_ROWS // 2, 0), rows - NA_KROWS) * w
        nq = tq // NA_ROW_SPLITS
        q0 = t * tq + part * nq
        q = q_ref[q0:q0 + nq, :]
        s_win = _dot_nt(q, k_ref[k0:k0 + tk, :]) + bias_ref[kind, part * nq:(part + 1) * nq, :]
        s_ctx = _dot_nt(q, ck_ref[...])
        top = jnp.maximum(jnp.max(s_win, axis=-1, keepdims=True), jnp.max(s_ctx, axis=-1, keepdims=True))
        p_win = jnp.exp2((s_win - top) * (scale * LOG2_E)).astype(BF16)
        p_ctx = jnp.exp2((s_ctx - top) * (scale * LOG2_E)).astype(BF16)
        o = _dot(p_win, va_ref[k0:k0 + tk, :]) + _dot(p_ctx, cva_ref[...])
        o_ref[q0:q0 + nq, :] = (o[:, :HEAD_DIM] / o[:, HEAD_DIM:HEAD_DIM + 1]).astype(o_ref.dtype)

    for t in range(n_tiles):
        for part in range(NA_ROW_SPLITS):
            attend(t, part)


def _neighbourhood_attention(proj, cproj, rpb, batch, seq, ctx_len):
    rows = seq // GRID_W
    assert rows % NA_QROWS == 0 and rows >= NA_KROWS + 2 * NA_QROWS
    hb = NA_WIDTH // HEAD_DIM
    n_rpb = (2 * WIN_ROWS - 1) * (2 * WIN_COLS - 1)
    tq, tk = NA_QROWS * GRID_W, NA_KROWS * GRID_W
    grid_spec = pltpu.PrefetchScalarGridSpec(
        num_scalar_prefetch=0,
        grid=(NA_HEADS, batch),
        in_specs=[
            pl.BlockSpec(memory_space=pltpu.SMEM),
            pl.BlockSpec((seq, HEAD_DIM), lambda h, b: (b, 4 * hb + h)),
            pl.BlockSpec((seq, HEAD_DIM), lambda h, b: (b, h)),
            pl.BlockSpec((seq, HEAD_DIM), lambda h, b: (b, hb + h)),
            pl.BlockSpec((ctx_len, HEAD_DIM), lambda h, b: (b, h)),
            pl.BlockSpec((ctx_len, HEAD_DIM), lambda h, b: (b, hb + h)),
        ],
        out_specs=pl.BlockSpec((seq, HEAD_DIM), lambda h, b: (b, h)),
        scratch_shapes=[
            pltpu.VMEM((3, tq, tk), F32),
            pltpu.VMEM((seq, 2 * HEAD_DIM), BF16),
            pltpu.VMEM((ctx_len, 2 * HEAD_DIM), BF16),
        ],
    )
    return pl.pallas_call(
        functools.partial(_na_body, rows=rows),
        grid_spec=grid_spec,
        out_shape=jax.ShapeDtypeStruct((batch * seq, NA_WIDTH), BF16),
        compiler_params=_cparams(2),
        name="neighbourhood_attention",
    )(rpb.reshape(NA_HEADS, n_rpb), proj, proj, proj, cproj, cproj)


def _log_sigmoid(x):
    return -(jnp.maximum(-x, 0.0) + jnp.log1p(jnp.exp(-jnp.abs(x))))


def _rope_partner_matrix():
    quarter = HEAD_DIM // 4
    src = lax.broadcasted_iota(jnp.int32, (HEAD_DIM, HEAD_DIM), 0)
    dst = lax.broadcasted_iota(jnp.int32, (HEAD_DIM, HEAD_DIM), 1)
    want = jnp.where(dst % (2 * quarter) < quarter, dst + quarter, dst - quarter)
    return (src == want).astype(BF16)


def _rope(x_bf16, partner_matrix, cos, sin_signed):
    return x_bf16.astype(F32) * cos + _dot(x_bf16, partner_matrix) * sin_signed


def _ret_body(df_ref, db_ref, gn_ref, cos_ref, sin_ref, q_ref, k_ref, v_ref, g_ref, ck_ref, cv_ref,
              cc_ref, wm_ref, bm_ref, wo_ref,
              o_ref, modl_ref, wob_ref, qr_ref, kr_ref, sf_ref, sb_ref):
    modl_ref[:, 0, :] = _dot(_silu(cc_ref[...]).astype(BF16), wm_ref[...].astype(BF16)) + bm_ref[...]
    wob_ref[...] = wo_ref[...].astype(BF16)

    c, d = RET_BLOCK, HEAD_DIM
    seq = q_ref.shape[0]
    n_chunks = seq // c
    ctx_len = ck_ref.shape[0]
    scale = HEAD_DIM ** -0.5
    head = pl.program_id(0)
    lgf_row = _log_sigmoid(jnp.full((1, HEAD_DIM), df_ref[head], F32))
    lgb_row = _log_sigmoid(jnp.full((1, HEAD_DIM), db_ref[head], F32))
    lgf = jnp.broadcast_to(lgf_row, (c, d))
    lgb = jnp.broadcast_to(lgb_row, (c, d))
    pos = lax.broadcasted_iota(jnp.int32, (c, d), 0).astype(F32)
    kdf = jnp.exp(lgf * (c - 1.0 - pos))
    kdb = jnp.exp(lgb * pos)
    qdf = jnp.exp(lgf * (pos + 1.0))
    qdb = jnp.exp(lgb * (c - pos))
    cdf = jnp.exp(lgf_row * float(c))
    cdb = jnp.exp(lgb_row * float(c))
    diff = (lax.broadcasted_iota(jnp.int32, (c, c), 0) - lax.broadcasted_iota(jnp.int32, (c, c), 1)).astype(F32)
    lgf_cc = jnp.broadcast_to(lgf_row[:, :1], (c, c))
    lgb_cc = jnp.broadcast_to(lgb_row[:, :1], (c, c))
    dmat = (jnp.where(diff >= 0, jnp.exp(lgf_cc * jnp.maximum(diff, 0.0)), 0.0)
            + jnp.where(diff <= 0, jnp.exp(lgb_cc * jnp.maximum(-diff, 0.0)), 0.0))

    cpos = lax.broadcasted_iota(jnp.int32, (ctx_len, d), 0).astype(F32)
    ckf = ck_ref[...].astype(F32) * scale
    cv = cv_ref[...]
    wf = jnp.exp(jnp.broadcast_to(lgf_row, (ctx_len, d)) * (ctx_len - 1.0 - cpos))
    wb = jnp.exp(jnp.broadcast_to(lgb_row, (ctx_len, d)) * cpos)
    s_f = _dot_tn((ckf * wf).astype(BF16), cv)
    s_b = _dot_tn((ckf * wb).astype(BF16), cv)

    rope_rows = 512
    pmat = _rope_partner_matrix()

    def rope_blk(i, carry):
        r0 = pl.multiple_of(i * rope_rows, rope_rows)
        cs = cos_ref[pl.ds(r0, rope_rows), :]
        sn = sin_ref[pl.ds(r0, rope_rows), :]
        qr_ref[pl.ds(r0, rope_rows), :] = _rope(q_ref[pl.ds(r0, rope_rows), :], pmat, cs, sn)
        kr_ref[pl.ds(r0, rope_rows), :] = _rope(k_ref[pl.ds(r0, rope_rows), :], pmat, cs, sn) * scale
        return carry

    lax.fori_loop(0, seq // rope_rows, rope_blk, 0, unroll=4)

    def scan_blk(n, carry):
        s, t = carry
        r0 = pl.multiple_of(n * c, c)
        kvf = _dot_tn((kr_ref[pl.ds(r0, c), :] * kdf).astype(BF16), v_ref[pl.ds(r0, c), :])
        sf_ref[n] = s.astype(BF16)
        m = n_chunks - 1 - n
        m0 = pl.multiple_of(m * c, c)
        kvb = _dot_tn((kr_ref[pl.ds(m0, c), :] * kdb).astype(BF16), v_ref[pl.ds(m0, c), :])
        sb_ref[m] = t.astype(BF16)
        return s * cdf + kvf, t * cdb + kvb

    lax.fori_loop(0, n_chunks, scan_blk, (s_f, s_b), unroll=16)

    gn = gn_ref[...]

    def out_blk(n, carry):
        r0 = pl.multiple_of(n * c, c)
        qc = qr_ref[pl.ds(r0, c), :]
        kc = kr_ref[pl.ds(r0, c), :]
        a = _dot_nt(qc.astype(BF16), kc.astype(BF16))
        o = (_dot((a * dmat).astype(BF16), v_ref[pl.ds(r0, c), :])
             + _dot((qc * qdf).astype(BF16), sf_ref[n])
             + _dot((qc * qdb).astype(BF16), sb_ref[n]))
        mu = jnp.mean(o, axis=-1, keepdims=True)
        var = jnp.mean(jnp.square(o - mu), axis=-1, keepdims=True)
        y = (o - mu) * lax.rsqrt(var + EPS) * gn
        o_ref[pl.ds(r0, c), :] = (y * _silu(g_ref[pl.ds(r0, c), :].astype(F32))).astype(o_ref.dtype)
        return carry

    lax.fori_loop(0, n_chunks, out_blk, 0, unroll=16)


def _rope_tables(seq):
    axis_dim = HEAD_DIM // 2
    inv_freq = ROPE_BASE ** (-jnp.arange(0, axis_dim, 2, dtype=F32) / axis_dim)
    rows = seq // GRID_W
    ang_r = jnp.arange(rows, dtype=F32)[:, None] * inv_freq
    ang_c = jnp.arange(GRID_W, dtype=F32)[:, None] * inv_freq
    by_row = lambda a: jnp.repeat(a, GRID_W, axis=0)
    by_col = lambda a: jnp.tile(a, (rows, 1))
    cr, sr, cc, sc = by_row(jnp.cos(ang_r)), by_row(jnp.sin(ang_r)), by_col(jnp.cos(ang_c)), by_col(jnp.sin(ang_c))
    return jnp.concatenate([cr, cr, cc, cc], axis=-1), jnp.concatenate([-sr, sr, -sc, sc], axis=-1)


def _retention(proj, cproj, decay_f, decay_b, gn_w, batch, seq, ctx_len, cc, w_mod, b_mod, mod_from, w_out):
    hb = RET_WIDTH // HEAD_DIM
    cos, sin = _rope_tables(seq)
    n_chunks = seq // RET_BLOCK
    blk = lambda rows, fn: pl.BlockSpec((rows, HEAD_DIM), fn)
    n_steps = RET_HEADS * batch
    step = lambda h, b: h * batch + b
    mod_rows, d = cc.shape
    late = w_mod.shape[1] - mod_from
    mcols, orows = late // n_steps, w_out.shape[0] // n_steps
    side_in = [
        pl.BlockSpec((mod_rows, d), lambda h, b: (0, 0)),
        pl.BlockSpec((d, mcols), lambda h, b: (0, mod_from // mcols + step(h, b))),
        pl.BlockSpec((1, mcols), lambda h, b: (0, mod_from // mcols + step(h, b))),
        pl.BlockSpec((orows, w_out.shape[1]), lambda h, b: (step(h, b), 0)),
    ]
    side_out = [
        pl.BlockSpec((mod_rows, 1, mcols), lambda h, b: (0, 0, step(h, b))),
        pl.BlockSpec((orows, w_out.shape[1]), lambda h, b: (step(h, b), 0)),
    ]
    side_shape = [jax.ShapeDtypeStruct((mod_rows, 1, late), F32), jax.ShapeDtypeStruct(w_out.shape, BF16)]
    return pl.pallas_call(
        _ret_body,
        grid=(RET_HEADS, batch),
        in_specs=[
            pl.BlockSpec(memory_space=pltpu.SMEM),
            pl.BlockSpec(memory_space=pltpu.SMEM),
            blk(1, lambda h, b: (0, h)),
            blk(seq, lambda h, b: (0, 0)),
            blk(seq, lambda h, b: (0, 0)),
            blk(seq, lambda h, b: (b, 5 * hb + h)),
            blk(seq, lambda h, b: (b, 2 * hb + h)),
            blk(seq, lambda h, b: (b, 3 * hb + h)),
            blk(seq, lambda h, b: (b, 6 * hb + h)),
            blk(ctx_len, lambda h, b: (b, 2 * hb + h)),
            blk(ctx_len, lambda h, b: (b, 3 * hb + h)),
        ] + side_in,
        out_specs=[blk(seq, lambda h, b: (b, h))] + side_out,
        out_shape=[jax.ShapeDtypeStruct((batch * seq, RET_WIDTH), BF16)] + side_shape,
        scratch_shapes=[
            pltpu.VMEM((seq, HEAD_DIM), F32),
            pltpu.VMEM((seq, HEAD_DIM), F32),
            pltpu.VMEM((n_chunks, HEAD_DIM, HEAD_DIM), BF16),
            pltpu.VMEM((n_chunks, HEAD_DIM, HEAD_DIM), BF16),
        ],
        compiler_params=_cparams(2),
        name="retention",
    )(decay_f.astype(F32), decay_b.astype(F32), gn_w.reshape(1, RET_WIDTH), cos, sin, proj, proj, proj, proj, cproj, cproj,
      cc, w_mod, b_mod.reshape(1, -1), w_out)


def _split_bf16(x):
    hi = x.astype(BF16)
    lo = (x - hi.astype(F32)).astype(BF16)
    return hi, lo


OUTPROJ_ROW_SPLITS = 2


def _outproj_body(na_ref, ret_ref, w_ref, x_ref, gate_ref, nw_ref, sh_ref, sc_ref, wr_ref,
                  h1_ref, u2_ref, lg_ref):
    half = na_ref.shape[1]
    n_e = wr_ref.shape[0]
    tm = x_ref.shape[0]
    wh, wl = _split_bf16(wr_ref[...])
    w_router = jnp.concatenate([wh, wl], axis=0)
    for r in range(OUTPROJ_ROW_SPLITS):
        rows = pl.ds(r * (tm // OUTPROJ_ROW_SPLITS), tm // OUTPROJ_ROW_SPLITS)
        mix = _dot(na_ref[rows, :], w_ref[:half, :]) + _dot(ret_ref[rows, :], w_ref[half:, :])
        h1 = x_ref[rows, :] + gate_ref[0] * mix
        h1_ref[rows, :] = h1
        u2 = _rms_mod(h1, nw_ref[...], sh_ref[0], sc_ref[0])
        u2_ref[rows, :] = u2
        both = _dot_nt(w_router, u2.astype(BF16))
        lg_ref[:, rows] = both[:n_e] + both[n_e:]


def _out_projection(na, ret, w_out_bf16, x2d, mod3, norm_w, w_router_t, seq, tm=OUTPROJ_ROW_TILE):
    rows, d = x2d.shape
    half = na.shape[1]
    per_b = seq // tm
    return pl.pallas_call(
        _outproj_body,
        grid=(rows // tm,),
        in_specs=[
            pl.BlockSpec((tm, half), lambda i: (i, 0)),
            pl.BlockSpec((tm, half), lambda i: (i, 0)),
            pl.BlockSpec((2 * half, d), lambda i: (0, 0)),
            pl.BlockSpec((tm, d), lambda i: (i, 0)),
            pl.BlockSpec((1, 1, d), lambda i: (i // per_b, 0, MOD_LATE_GATE_MIX)),
            pl.BlockSpec((1, d), lambda i: (0, 0)),
            pl.BlockSpec((1, 1, d), lambda i: (i // per_b, 0, MOD_LATE_SHIFT_FFN)),
            pl.BlockSpec((1, 1, d), lambda i: (i // per_b, 0, MOD_LATE_SCALE_FFN)),
            pl.BlockSpec((N_EXPERTS, d), lambda i: (0, 0)),
        ],
        out_specs=[
            pl.BlockSpec((tm, d), lambda i: (i, 0)),
            pl.BlockSpec((tm, d), lambda i: (i, 0)),
            pl.BlockSpec((N_EXPERTS, tm), lambda i: (0, i)),
        ],
        out_shape=[
            jax.ShapeDtypeStruct((rows, d), F32),
            jax.ShapeDtypeStruct((rows, d), F32),
            jax.ShapeDtypeStruct((N_EXPERTS, rows), F32),
        ],
        compiler_params=_cparams(1),
        name="out_projection",
    )(na, ret, w_out_bf16, x2d, mod3, norm_w.reshape(1, d), mod3, mod3, w_router_t)


def _prefix_incl_lanes(x, tri):
    r, l = x.shape
    nb = l // LANES
    xs = jnp.concatenate([x[:, t * LANES:(t + 1) * LANES] for t in range(nb)], axis=0).astype(BF16)
    p = _dot(xs, tri)
    outs, run = [], jnp.zeros((r, 1), F32)
    for t in range(nb):
        blk = p[t * r:(t + 1) * r] + run
        outs.append(blk)
        run = blk[:, LANES - 1:LANES]
    return jnp.concatenate(outs, axis=1)


NOT_ROUTED = -(1 << 20)


def _route_body(lg_ref, gidx_ref, gate_ref, slot_ref, before_ref, *, cap):
    b = pl.program_id(0)
    n_e, seq = lg_ref.shape
    kf = float(cap)
    lg = lg_ref[...]
    ex = jnp.exp(lg - jnp.max(lg, axis=0, keepdims=True))
    aff = ex / jnp.sum(ex, axis=0, keepdims=True)

    def cond(c):
        return (c[0] < 4096) & (c[5] > 0.5)

    def step(c):
        it, lo, hi, thr, done, _ = c
        mid = 0.5 * (lo + hi)
        above = jnp.sum((aff > mid).astype(F32), axis=1, keepdims=True)
        hit = above == kf
        stuck = (mid <= lo) | (mid >= hi)
        active = done < 0.5
        thr = jnp.where(active & hit, mid, jnp.where(active & stuck, hi, thr))
        go = active & ~(hit | stuck)
        ge = above >= kf
        lo = jnp.where(go & ge, mid, lo)
        hi = jnp.where(go & ~ge, mid, hi)
        done = jnp.where(active & (hit | stuck), 1.0, done)
        return it + 1, lo, hi, thr, done, jnp.sum(1.0 - done)

    col = lambda v: jnp.full((n_e, 1), v, F32)
    init = (jnp.int32(0), col(-1.0), col(2.0), col(0.0), col(0.0), jnp.float32(n_e))
    thr = lax.while_loop(cond, step, init)[3]

    ii = lax.broadcasted_iota(jnp.int32, (LANES, LANES), 0)
    jj = lax.broadcasted_iota(jnp.int32, (LANES, LANES), 1)
    tri = (ii <= jj).astype(BF16)
    gt = aff > thr
    eq = (aff == thr).astype(F32)
    need = kf - jnp.sum(gt.astype(F32), axis=1, keepdims=True)
    eq_before = _prefix_incl_lanes(eq, tri) - eq
    mask = jnp.where(gt | ((eq > 0.5) & (eq_before < need)), 1.0, 0.0)

    slot = _prefix_incl_lanes(mask, tri) - mask
    before_ref[0] = slot.astype(jnp.int32)
    slot_ref[0] = jnp.where(mask > 0.5, slot, float(NOT_ROUTED)).astype(jnp.int32)

    tok = lax.broadcasted_iota(jnp.int32, (1, seq), 1).astype(F32)
    tok_hi = jnp.floor(tok * (1.0 / 64))
    tok_lo = tok - 64.0 * tok_hi
    n_hi = ROUTE_SLOT_HI
    n_lo = cap // n_hi
    hi_iota = lax.broadcasted_iota(jnp.int32, (n_hi, seq), 0).astype(F32)
    lo_iota = lax.broadcasted_iota(jnp.int32, (n_lo, seq), 0).astype(F32)
    for e in range(n_e):
        se = slot[e:e + 1]
        s_hi = jnp.floor(se * (1.0 / n_lo))
        s_lo = se - n_lo * s_hi
        in_hi = jnp.where((s_hi == hi_iota) & (mask[e:e + 1] > 0.5), 1.0, 0.0)
        in_lo = (s_lo == lo_iota).astype(BF16)
        a = aff[e:e + 1]
        a_hi = a.astype(BF16).astype(F32)
        a_mid = (a - a_hi).astype(BF16).astype(F32)
        a_lo = a - a_hi - a_mid
        vals = (tok_hi, tok_lo, a_hi, a_mid, a_lo)
        lhs = jnp.concatenate([in_hi * v for v in vals], axis=0).astype(BF16)
        got = _dot_nt(lhs, in_lo)
        part = lambda r: got[r * n_hi:(r + 1) * n_hi]
        gidx_ref[0, e] = (part(0) * 64.0 + part(1)).astype(jnp.int32) + b * seq
        gate_ref[0, e] = part(2) + part(3) + part(4)


def _routing(logits_t, batch, seq):
    n_e = logits_t.shape[0]
    cap = CAPACITY_FACTOR * seq // n_e
    n_hi, n_lo = ROUTE_SLOT_HI, cap // ROUTE_SLOT_HI
    bec = pl.BlockSpec((1, n_e, n_hi, n_lo), lambda b: (b, 0, 0, 0))
    bel = pl.BlockSpec((1, n_e, seq), lambda b: (b, 0, 0))
    gidx, gates, slot, before = pl.pallas_call(
        functools.partial(_route_body, cap=cap),
        grid=(batch,),
        in_specs=[pl.BlockSpec((n_e, seq), lambda b: (0, b))],
        out_specs=[bec, bec, bel, bel],
        out_shape=[
            jax.ShapeDtypeStruct((batch, n_e, n_hi, n_lo), jnp.int32),
            jax.ShapeDtypeStruct((batch, n_e, n_hi, n_lo), F32),
            jax.ShapeDtypeStruct((batch, n_e, seq), jnp.int32),
            jax.ShapeDtypeStruct((batch, n_e, seq), jnp.int32),
        ],
        compiler_params=_cparams(1),
        name="routing",
    )(logits_t)
    flat = lambda a: a.reshape(batch, n_e, cap)
    return flat(gidx), flat(gates), slot, before


def _hbm_row(ref, r):
    return ref.at[lax.shift_right_logical(r, 3), pl.ds(r & (SUBLANES - 1), 1)]


MOE_ROW_SPLITS = 2
MOE_GATE_ROWS = 4 * BF16_ROWS


def _moe_body(gidx_ref, gnext_ref, gate_ref, u2_hbm, wg_ref, wu_ref, wd_ref, y_ref,
              stage_ref, xe_ref, acc_ref, gcol_ref, gsem, *, n_f):
    e = pl.program_id(0)
    f = pl.program_id(1)
    n_e = pl.num_programs(0)
    m, d = y_ref.shape
    sub = SUBLANES
    n_tiles = m // sub
    assert n_f >= 2
    per_step = -(-n_tiles // (n_f - 1))
    last_count = n_tiles - per_step * (n_f - 2)

    def gather_copy(idx_ref, i, k):
        return pltpu.make_async_copy(_hbm_row(u2_hbm, idx_ref[0, i * sub + k]), stage_ref.at[i, pl.ds(k, 1)], gsem)

    def for_rows(fn, tiles_per_trip=4):
        def body(t, c):
            for kk in range(tiles_per_trip * sub):
                fn(t * tiles_per_trip + kk // sub, kk % sub)
            return c
        lax.fori_loop(0, n_tiles // tiles_per_trip, body, 0)

    wait_gather = lambda: for_rows(lambda i, k: gather_copy(gidx_ref, i, k).wait())

    def ffn_step(first, gather_tiles):
        i0 = (f - 1) * per_step
        for kk in range(gather_tiles * sub):
            gather_copy(gnext_ref, i0 + kk // sub, kk % sub).start()
        wg, wu, wd = wg_ref[...].astype(BF16), wu_ref[...].astype(BF16), wd_ref[...].astype(BF16)
        if first:
            gcol_ref[...] = jnp.transpose(jnp.broadcast_to(gate_ref[...], (LANES, m)))
        part = m // MOE_ROW_SPLITS
        for r in range(MOE_ROW_SPLITS):
            rows = pl.ds(r * part, part)
            if first:
                x = stage_ref[pl.ds(r * (part // sub), part // sub)].reshape(part, d).astype(BF16)
                xe_ref[rows, :] = x
            else:
                x = xe_ref[rows, :]
            hid = _silu(_dot(x, wg)) * _dot(x, wu)
            out = _dot(hid.astype(BF16), wd)
            acc_ref[rows, :] = out if first else acc_ref[rows, :] + out

    @pl.when(f == 0)
    def _():
        @pl.when(e == 0)
        def _():
            for_rows(lambda i, k: gather_copy(gidx_ref, i, k).start())

        wait_gather()
        ffn_step(True, 0)

    if n_f > 2:
        pl.when((f > 0) & (f < n_f - 1))(lambda: ffn_step(False, per_step))

    @pl.when(f == n_f - 1)
    def _():
        ffn_step(False, last_count)

        @pl.when(f > 0)
        def _():
            for r0 in range(0, m, MOE_GATE_ROWS):
                rows = slice(r0, r0 + MOE_GATE_ROWS)
                g = jnp.tile(gcol_ref[rows, :], (1, d // LANES))
                y_ref[rows, :] = (acc_ref[rows, :] * g).astype(y_ref.dtype)

        @pl.when(e == n_e - 1)
        def _():
            wait_gather()


def _moe_ffn(gidx, gates, u2, w_gate, w_up, w_down, tf=MOE_F_TILE):
    n_e, _, m = gidx.shape
    rows, d = u2.shape
    ff = w_gate.shape[2]
    n_f = ff // tf
    sub = SUBLANES
    smem = lambda fn: pl.BlockSpec((None, 1, m), fn, memory_space=pltpu.SMEM)
    return pl.pallas_call(
        functools.partial(_moe_body, n_f=n_f),
        grid=(n_e, n_f),
        in_specs=[
            smem(lambda e, f: (e, 0, 0)),
            smem(lambda e, f: (jnp.minimum(e + 1, n_e - 1), 0, 0)),
            pl.BlockSpec((None, 1, m), lambda e, f: (e, 0, 0)),
            pl.BlockSpec(memory_space=pl.ANY),
            pl.BlockSpec((None, d, tf), lambda e, f: (e, 0, f)),
            pl.BlockSpec((None, d, tf), lambda e, f: (e, 0, f)),
            pl.BlockSpec((None, tf, d), lambda e, f: (e, f, 0)),
        ],
        out_specs=pl.BlockSpec((m, d), lambda e, f: (e, 0)),
        out_shape=jax.ShapeDtypeStruct((n_e * m, d), BF16),
        scratch_shapes=[
            pltpu.VMEM((m // sub, sub, d), F32),
            pltpu.VMEM((m, d), BF16),
            pltpu.VMEM((m, d), F32),
            pltpu.VMEM((m, LANES), F32),
            pltpu.SemaphoreType.DMA,
        ],
        compiler_params=_cparams(2),
        name="moe_ffn",
    )(gidx, gidx, gates, u2.reshape(rows // sub, sub, d), w_gate, w_up, w_down)


COMBINE_TOKENS = 512
COMBINE_PIECE = 96


def _combine_body(start_ref, npiece_ref, h1_ref, y_hbm, slot_ref, gate_ref, nw_ref, o_ref,
                  acc_ref, buf_ref, xbuf_ref, sem, xsem, *, m, cap, per_b):
    i = pl.program_id(0)
    n_tiles = pl.num_programs(0)
    n_e = slot_ref.shape[1]
    t, p = COMBINE_TOKENS, COMBINE_PIECE
    total = n_e * m
    cur = i % 2
    b = i // per_b

    def piece_copy(tile, e, dst):
        st = pl.multiple_of(start_ref[tile * n_e + e], BF16_ROWS)
        return pltpu.make_async_copy(y_hbm.at[pl.ds(st, p)], buf_ref.at[dst, pl.ds(e * p, p)], sem.at[dst])

    @pl.when(i == 0)
    def _():
        for e in range(n_e):
            piece_copy(0, e, 0).start()

    @pl.when(i + 1 < n_tiles)
    def _():
        for e in range(n_e):
            piece_copy(i + 1, e, 1 - cur).start()

    for e in range(n_e):
        piece_copy(i, e, cur).wait()

    slots = slot_ref[0]
    riota = lax.broadcasted_iota(jnp.int32, (p, t), 0)

    def local_row(e, st):
        return slots[e:e + 1, :] + (e * m + b * cap - st)

    sel = jnp.concatenate(
        [(local_row(e, start_ref[i * n_e + e]) == riota).astype(BF16) for e in range(n_e)], axis=0)
    acc_ref[...] = _dot_tn(sel, buf_ref[cur])

    for e in range(n_e):
        st0 = start_ref[i * n_e + e]

        def extra(k, carry, e=e, st0=st0):
            want = st0 + k * p
            st = pl.multiple_of(jnp.minimum(want, total - p), BF16_ROWS)
            cp = pltpu.make_async_copy(y_hbm.at[pl.ds(st, p)], xbuf_ref, xsem)
            cp.start()
            cp.wait()
            blk = ((local_row(e, st) == riota) & (riota >= want - st)).astype(BF16)
            acc_ref[...] += _dot_tn(blk, xbuf_ref[...])
            return carry

        lax.fori_loop(1, npiece_ref[i * n_e + e], extra, 0)

    h2 = h1_ref[...] + gate_ref[0] * acc_ref[...]
    ms = jnp.mean(h2 * h2, axis=-1, keepdims=True)
    o_ref[...] = h2 * lax.rsqrt(ms + EPS) * nw_ref[...]


def _combine_pieces(before, cap):
    batch, n_e, seq = before.shape
    t, p = COMBINE_TOKENS, COMBINE_PIECE
    m = batch * cap
    total = n_e * m
    bounds = jnp.concatenate([before[:, :, ::t], jnp.full((batch, n_e, 1), cap, jnp.int32)], axis=2)
    base = (jnp.arange(n_e, dtype=jnp.int32) * m)[None, :, None] + (jnp.arange(batch, dtype=jnp.int32) * cap)[:, None, None]
    first = base + bounds[:, :, :-1]
    end = base + bounds[:, :, 1:]
    start = jnp.minimum(first // BF16_ROWS * BF16_ROWS, total - p)
    npiece = jnp.maximum((end - start + p - 1) // p, 1)
    by_tile = lambda a: a.transpose(0, 2, 1).reshape(-1)
    return by_tile(start), by_tile(npiece)


def _combine(h1, y, slot, before, mod3, final_w, seq, cap):
    rows, d = h1.shape
    batch, n_e, _ = slot.shape
    t, p = COMBINE_TOKENS, COMBINE_PIECE
    per_b = seq // t
    start, npiece = _combine_pieces(before, cap)
    grid_spec = pltpu.PrefetchScalarGridSpec(
        num_scalar_prefetch=2,
        grid=(rows // t,),
        in_specs=[
            pl.BlockSpec((t, d), lambda i, st, npc: (i, 0)),
            pl.BlockSpec(memory_space=pl.ANY),
            pl.BlockSpec((1, n_e, t), lambda i, st, npc: (i // per_b, 0, i % per_b)),
            pl.BlockSpec((1, 1, d), lambda i, st, npc: (i // per_b, 0, MOD_LATE_GATE_FFN)),
            pl.BlockSpec((1, d), lambda i, st, npc: (0, 0)),
        ],
        out_specs=pl.BlockSpec((t, d), lambda i, st, npc: (i, 0)),
        scratch_shapes=[
            pltpu.VMEM((t, d), F32),
            pltpu.VMEM((2, n_e * p, d), y.dtype),
            pltpu.VMEM((p, d), y.dtype),
            pltpu.SemaphoreType.DMA((2,)),
            pltpu.SemaphoreType.DMA,
        ],
    )
    return pl.pallas_call(
        functools.partial(_combine_body, m=batch * cap, cap=cap, per_b=per_b),
        grid_spec=grid_spec,
        out_shape=jax.ShapeDtypeStruct((rows, d), F32),
        compiler_params=_cparams(1),
        name="combine",
    )(start, npiece, h1, y, slot, mod3, final_w.reshape(1, d))


def kernel(x, c, ctx, c_ctx, w_mod, b_mod, norm_mix_w, norm_ffn_w, w_in, na_rpb, ret_decay_fwd,
           ret_decay_bwd, ret_gn_w, w_out, w_router, w_gate, w_up, w_down, final_norm_w):
    batch, seq, d = x.shape
    ctx_len = ctx.shape[1]
    assert w_mod.shape[0] == 1, "one trunk layer"
    n_e = w_router.shape[2]
    cap = CAPACITY_FACTOR * seq // n_e

    mod_rows = SUBLANES
    cc = jnp.concatenate([c, c_ctx[None], jnp.zeros((mod_rows - batch - 1, d), c.dtype)], axis=0)
    mod3 = _modulation(cc, w_mod[0], b_mod[0], MOD_EARLY * d)

    x2d = x.reshape(batch * seq, d)
    tm = INPROJ_ROW_TILE
    cproj, w_in_bf16 = _ctx_projection(ctx.reshape(batch * ctx_len, d), norm_mix_w[0], mod3, batch,
                                       w_in[0], KV_COLS)
    proj = _in_projection(x2d, norm_mix_w[0], mod3, lambda i: i // (seq // tm), w_in_bf16,
                          w_in.shape[2], tm, w_in.shape[2] // INPROJ_COL_TILES)

    na = _neighbourhood_attention(proj, cproj, na_rpb[0], batch, seq, ctx_len)
    ret, mod_late3, w_out_bf16 = _retention(proj, cproj, ret_decay_fwd[0], ret_decay_bwd[0], ret_gn_w[0],
                                            batch, seq, ctx_len, cc, w_mod[0], b_mod[0], MOD_EARLY * d, w_out[0])

    h1, u2, logits_t = _out_projection(na, ret, w_out_bf16, x2d, mod_late3, norm_ffn_w[0], w_router[0].T, seq)

    gidx, gates, slot, before = _routing(logits_t, batch, seq)
    per_expert = lambda a: a.transpose(1, 0, 2).reshape(n_e, 1, batch * cap)
    y = _moe_ffn(per_expert(gidx), per_expert(gates), u2, w_gate[0], w_up[0], w_down[0])
    out = _combine(h1, y, slot, before, mod_late3, final_norm_w, seq, cap)
    return out.reshape(batch, seq, d)
```

```python
import functools

import jax
import jax.numpy as jnp
from jax import lax
from jax.experimental import pallas as pl
from jax.experimental.pallas import tpu as pltpu

GRID_W = 64
HEAD_DIM = 128
NA_HEADS = 8
RET_HEADS = 8
NA_WIDTH = NA_HEADS * HEAD_DIM
RET_WIDTH = RET_HEADS * HEAD_DIM
WIN_ROWS = 8
WIN_COLS = 16
RET_BLOCK = 256
ROPE_BASE = 10000.0
N_EXPERTS = 16
CAPACITY_FACTOR = 2
N_MOD = 6
MOD_EARLY = 2
MOD_LATE_GATE_MIX, MOD_LATE_SHIFT_FFN, MOD_LATE_SCALE_FFN, MOD_LATE_GATE_FFN = 0, 1, 2, 3
EPS = 1e-6
NEG_INF = -1e30
LOG2_E = 1.4426950408889634
KV_COLS = 2 * NA_WIDTH + 2 * RET_WIDTH

F32 = jnp.float32
BF16 = jnp.bfloat16
LANES = 128
SUBLANES = 8
BF16_ROWS = 16
ROUTE_SLOT_HI = 16
MIB = 1024 * 1024
VMEM_LIMIT_V7X = 56 * MIB

MOD_COL_TILE = 1024
CTX_COL_TILE = 1024
INPROJ_ROW_TILE = 1024
INPROJ_COL_TILES = 4
OUTPROJ_ROW_TILE = 512
MOE_F_TILE = 256


def _cparams(n_axes):
    return pltpu.CompilerParams(
        dimension_semantics=("arbitrary",) * n_axes, vmem_limit_bytes=VMEM_LIMIT_V7X)


def _silu(x):
    return x * jax.nn.sigmoid(x)


def _dot(a, b):
    return jnp.dot(a, b, preferred_element_type=F32)


def _dot_nt(a, b):
    return lax.dot_general(a, b, (((1,), (1,)), ((), ())), preferred_element_type=F32)


def _dot_tn(a, b):
    return lax.dot_general(a, b, (((0,), (0,)), ((), ())), preferred_element_type=F32)


def _mod_body(c_ref, w_ref, b_ref, o_ref):
    a = _silu(c_ref[...]).astype(BF16)
    o_ref[:, 0, :] = _dot(a, w_ref[...].astype(BF16)) + b_ref[...]


def _modulation(cc, w_mod, b_mod, n_cols, tn=MOD_COL_TILE):
    rows, d = cc.shape
    n = n_cols
    return pl.pallas_call(
        _mod_body,
        grid=(n // tn,),
        in_specs=[
            pl.BlockSpec((rows, d), lambda j: (0, 0)),
            pl.BlockSpec((d, tn), lambda j: (0, j)),
            pl.BlockSpec((1, tn), lambda j: (0, j)),
        ],
        out_specs=pl.BlockSpec((rows, 1, tn), lambda j: (0, 0, j)),
        out_shape=jax.ShapeDtypeStruct((rows, 1, n), F32),
        compiler_params=_cparams(1),
        name="modulation",
    )(cc, w_mod, b_mod.reshape(1, -1))


def _rms_mod(x, nw, shift, scale):
    ms = jnp.mean(x * x, axis=-1, keepdims=True)
    y = x * lax.rsqrt(ms + EPS) * nw
    return y * (1.0 + scale) + shift


INPROJ_NORM_SPLITS = 4


def _inproj_body(x_ref, nw_ref, sh_ref, sc_ref, w_ref, o_ref, u_ref):
    j = pl.program_id(1)
    tm = x_ref.shape[0]

    @pl.when(j == 0)
    def _():
        part = tm // INPROJ_NORM_SPLITS
        for r in range(INPROJ_NORM_SPLITS):
            rows = pl.ds(r * part, part)
            u = _rms_mod(x_ref[rows, :], nw_ref[...], sh_ref[0], sc_ref[0]).astype(BF16)
            u_ref[rows, :] = u
            o_ref[rows, :] = _dot(u, w_ref[...]).astype(o_ref.dtype)

    @pl.when(j > 0)
    def _():
        o_ref[...] = _dot(u_ref[...], w_ref[...]).astype(o_ref.dtype)


def _ctx_proj_body(x_ref, nw_ref, sh_ref, sc_ref, w_ref, o_ref, wb_ref, u_ref, *, kv_tiles):
    j = pl.program_id(0)

    @pl.when(j == 0)
    def _():
        u_ref[...] = _rms_mod(x_ref[...], nw_ref[...], sh_ref[0], sc_ref[0]).astype(BF16)

    wb_ref[...] = w_ref[...].astype(BF16)

    @pl.when(j < kv_tiles)
    def _():
        o_ref[...] = _dot(u_ref[...], wb_ref[...]).astype(o_ref.dtype)


def _ctx_projection(x2d, norm_w, mod3, mod_row, w_in, kv_cols, tn=CTX_COL_TILE):
    rows, d = x2d.shape
    n = w_in.shape[1]
    kv_tiles = kv_cols // tn
    return pl.pallas_call(
        functools.partial(_ctx_proj_body, kv_tiles=kv_tiles),
        grid=(n // tn,),
        in_specs=[
            pl.BlockSpec((rows, d), lambda j: (0, 0)),
            pl.BlockSpec((1, d), lambda j: (0, 0)),
            pl.BlockSpec((1, 1, d), lambda j: (mod_row, 0, 0)),
            pl.BlockSpec((1, 1, d), lambda j: (mod_row, 0, 1)),
            pl.BlockSpec((d, tn), lambda j: (0, j)),
        ],
        out_specs=[
            pl.BlockSpec((rows, tn), lambda j: (0, jnp.minimum(j, kv_tiles - 1))),
            pl.BlockSpec((d, tn), lambda j: (0, j)),
        ],
        out_shape=[jax.ShapeDtypeStruct((rows, kv_cols), BF16), jax.ShapeDtypeStruct((d, n), BF16)],
        scratch_shapes=[pltpu.VMEM((rows, d), BF16)],
        compiler_params=_cparams(1),
        name="ctx_projection",
    )(x2d, norm_w.reshape(1, d), mod3, mod3, w_in)


def _in_projection(x2d, norm_w, mod3, mod_row_fn, w_in, n_cols, tm, tn):
    rows, d = x2d.shape
    return pl.pallas_call(
        _inproj_body,
        grid=(rows // tm, n_cols // tn),
        in_specs=[
            pl.BlockSpec((tm, d), lambda i, j: (i, 0)),
            pl.BlockSpec((1, d), lambda i, j: (0, 0)),
            pl.BlockSpec((1, 1, d), lambda i, j: (mod_row_fn(i), 0, 0)),
            pl.BlockSpec((1, 1, d), lambda i, j: (mod_row_fn(i), 0, 1)),
            pl.BlockSpec((d, tn), lambda i, j: (0, j)),
        ],
        out_specs=pl.BlockSpec((tm, tn), lambda i, j: (i, j)),
        out_shape=jax.ShapeDtypeStruct((rows, n_cols), BF16),
        scratch_shapes=[pltpu.VMEM((tm, d), BF16)],
        compiler_params=_cparams(2),
        name="in_projection",
    )(x2d, norm_w.reshape(1, d), mod3, mod3, w_in)


NA_QROWS = 4
NA_KROWS = NA_QROWS + WIN_ROWS
NA_ROW_SPLITS = 2

def _na_row_offset(tile_kind, i, w, rows):
    half = WIN_ROWS // 2
    if tile_kind == 0:
        r, key = i, w
    elif tile_kind == 1:
        r, key = NA_QROWS + i, NA_QROWS - half + w
    else:
        r, key = rows - NA_QROWS + i, rows - NA_KROWS + w
    start = min(max(r - half, 0), rows - WIN_ROWS)
    if not (start <= key < start + WIN_ROWS):
        return None
    return key - r + (WIN_ROWS - 1)


def _na_build_bias(rpb_ref, bias_ref, h, rows):
    w = GRID_W
    cq = lax.broadcasted_iota(jnp.int32, (w, 2 * w), 0)
    ck = lax.broadcasted_iota(jnp.int32, (w, 2 * w), 1) % w
    col_start = jnp.clip(cq - WIN_COLS // 2, 0, w - WIN_COLS)
    col_ok = (ck >= col_start) & (ck < col_start + WIN_COLS)
    neg = jnp.full((w, 2 * w), NEG_INF, F32)
    n_ro, n_co = 2 * WIN_ROWS - 1, 2 * WIN_COLS - 1
    rel = (lax.broadcasted_iota(jnp.int32, (SUBLANES, 2 * w), 1) + w // 2) % w - w // 2
    tabs = []
    for ro in range(n_ro):
        base = jnp.zeros((SUBLANES, 2 * w), F32)
        for j in range(n_co):
            base = jnp.where(rel == j - (WIN_COLS - 1), rpb_ref[h, ro * n_co + j] * (HEAD_DIM ** 0.5), base)
        t = jnp.concatenate([pltpu.roll(base, SUBLANES * g, axis=1, stride=1, stride_axis=0)
                             for g in range(w // SUBLANES)], axis=0)
        tabs.append(jnp.where(col_ok, t, neg))
    left = lax.broadcasted_iota(jnp.int32, (w, 2 * w), 1) < w
    for kind in range(3):
        for i in range(NA_QROWS):
            for wp in range(NA_KROWS // 2):
                ra = _na_row_offset(kind, i, 2 * wp, rows)
                rb = _na_row_offset(kind, i, 2 * wp + 1, rows)
                ta = neg if ra is None else tabs[ra]
                tb = neg if rb is None else tabs[rb]
                blk = ta if ra == rb else jnp.where(left, ta, tb)
                bias_ref[kind, i * w:(i + 1) * w, wp * 2 * w:(wp + 1) * 2 * w] = blk


def _na_body(rpb_ref, q_ref, k_ref, v_ref, ck_ref, cv_ref, o_ref, bias_ref, va_ref, cva_ref, *, rows):
    h = pl.program_id(0)
    w = GRID_W
    tq, tk = NA_QROWS * w, NA_KROWS * w
    n_tiles = rows // NA_QROWS
    scale = HEAD_DIM ** -0.5

    @pl.when(pl.program_id(1) == 0)
    def _():
        _na_build_bias(rpb_ref, bias_ref, h, rows)

    def with_ones(v):
        ones = (lax.broadcasted_iota(jnp.int32, v.shape, 1) == 0).astype(v.dtype)
        return jnp.concatenate([v, ones], axis=1)

    va_ref[...] = with_ones(v_ref[...])
    cva_ref[...] = with_ones(cv_ref[...])

    def attend(t, part):
        kind = 0 if t == 0 else (2 if t == n_tiles - 1 else 1)
        k0 = min(max(t * NA_QROWS - WIN_ROWS // 2, 0), rows - NA_KROWS) * w
        nq = tq // NA_ROW_SPLITS
        q0 = t * tq + part * nq
        q = q_ref[q0:q0 + nq, :]
        s_win = _dot_nt(q, k_ref[k0:k0 + tk, :]) + bias_ref[kind, part * nq:(part + 1) * nq, :]
        s_ctx = _dot_nt(q, ck_ref[...])
        top = jnp.maximum(jnp.max(s_win, axis=-1, keepdims=True), jnp.max(s_ctx, axis=-1, keepdims=True))
        p_win = jnp.exp2((s_win - top) * (scale * LOG2_E)).astype(BF16)
        p_ctx = jnp.exp2((s_ctx - top) * (scale * LOG2_E)).astype(BF16)
        o = _dot(p_win, va_ref[k0:k0 + tk, :]) + _dot(p_ctx, cva_ref[...])
        o_ref[q0:q0 + nq, :] = (o[:, :HEAD_DIM] / o[:, HEAD_DIM:HEAD_DIM + 1]).astype(o_ref.dtype)

    for t in range(n_tiles):
        for part in range(NA_ROW_SPLITS):
            attend(t, part)


def _neighbourhood_attention(proj, cproj, rpb, batch, seq, ctx_len):
    rows = seq // GRID_W
    assert rows % NA_QROWS == 0 and rows >= NA_KROWS + 2 * NA_QROWS
    hb = NA_WIDTH // HEAD_DIM
    n_rpb = (2 * WIN_ROWS - 1) * (2 * WIN_COLS - 1)
    tq, tk = NA_QROWS * GRID_W, NA_KROWS * GRID_W
    grid_spec = pltpu.PrefetchScalarGridSpec(
        num_scalar_prefetch=0,
        grid=(NA_HEADS, batch),
        in_specs=[
            pl.BlockSpec(memory_space=pltpu.SMEM),
            pl.BlockSpec((seq, HEAD_DIM), lambda h, b: (b, 4 * hb + h)),
            pl.BlockSpec((seq, HEAD_DIM), lambda h, b: (b, h)),
            pl.BlockSpec((seq, HEAD_DIM), lambda h, b: (b, hb + h)),
            pl.BlockSpec((ctx_len, HEAD_DIM), lambda h, b: (b, h)),
            pl.BlockSpec((ctx_len, HEAD_DIM), lambda h, b: (b, hb + h)),
        ],
        out_specs=pl.BlockSpec((seq, HEAD_DIM), lambda h, b: (b, h)),
        scratch_shapes=[
            pltpu.VMEM((3, tq, tk), F32),
            pltpu.VMEM((seq, 2 * HEAD_DIM), BF16),
            pltpu.VMEM((ctx_len, 2 * HEAD_DIM), BF16),
        ],
    )
    return pl.pallas_call(
        functools.partial(_na_body, rows=rows),
        grid_spec=grid_spec,
        out_shape=jax.ShapeDtypeStruct((batch * seq, NA_WIDTH), BF16),
        compiler_params=_cparams(2),
        name="neighbourhood_attention",
    )(rpb.reshape(NA_HEADS, n_rpb), proj, proj, proj, cproj, cproj)


def _log_sigmoid(x):
    return -(jnp.maximum(-x, 0.0) + jnp.log1p(jnp.exp(-jnp.abs(x))))


def _rope_partner_matrix():
    quarter = HEAD_DIM // 4
    src = lax.broadcasted_iota(jnp.int32, (HEAD_DIM, HEAD_DIM), 0)
    dst = lax.broadcasted_iota(jnp.int32, (HEAD_DIM, HEAD_DIM), 1)
    want = jnp.where(dst % (2 * quarter) < quarter, dst + quarter, dst - quarter)
    return (src == want).astype(BF16)


def _rope(x_bf16, partner_matrix, cos, sin_signed):
    return x_bf16.astype(F32) * cos + _dot(x_bf16, partner_matrix) * sin_signed


def _ret_body(df_ref, db_ref, gn_ref, cos_ref, sin_ref, q_ref, k_ref, v_ref, g_ref, ck_ref, cv_ref,
              cc_ref, wm_ref, bm_ref, wo_ref,
              o_ref, modl_ref, wob_ref, qr_ref, kr_ref, sf_ref, sb_ref):
    modl_ref[:, 0, :] = _dot(_silu(cc_ref[...]).astype(BF16), wm_ref[...].astype(BF16)) + bm_ref[...]
    wob_ref[...] = wo_ref[...].astype(BF16)

    c, d = RET_BLOCK, HEAD_DIM
    seq = q_ref.shape[0]
    n_chunks = seq // c
    ctx_len = ck_ref.shape[0]
    scale = HEAD_DIM ** -0.5
    head = pl.program_id(0)
    lgf_row = _log_sigmoid(jnp.full((1, HEAD_DIM), df_ref[head], F32))
    lgb_row = _log_sigmoid(jnp.full((1, HEAD_DIM), db_ref[head], F32))
    lgf = jnp.broadcast_to(lgf_row, (c, d))
    lgb = jnp.broadcast_to(lgb_row, (c, d))
    pos = lax.broadcasted_iota(jnp.int32, (c, d), 0).astype(F32)
    kdf = jnp.exp(lgf * (c - 1.0 - pos))
    kdb = jnp.exp(lgb * pos)
    qdf = jnp.exp(lgf * (pos + 1.0))
    qdb = jnp.exp(lgb * (c - pos))
    cdf = jnp.exp(lgf_row * float(c))
    cdb = jnp.exp(lgb_row * float(c))
    diff = (lax.broadcasted_iota(jnp.int32, (c, c), 0) - lax.broadcasted_iota(jnp.int32, (c, c), 1)).astype(F32)
    lgf_cc = jnp.broadcast_to(lgf_row[:, :1], (c, c))
    lgb_cc = jnp.broadcast_to(lgb_row[:, :1], (c, c))
    dmat = (jnp.where(diff >= 0, jnp.exp(lgf_cc * jnp.maximum(diff, 0.0)), 0.0)
            + jnp.where(diff <= 0, jnp.exp(lgb_cc * jnp.maximum(-diff, 0.0)), 0.0))

    cpos = lax.broadcasted_iota(jnp.int32, (ctx_len, d), 0).astype(F32)
    ckf = ck_ref[...].astype(F32) * scale
    cv = cv_ref[...]
    wf = jnp.exp(jnp.broadcast_to(lgf_row, (ctx_len, d)) * (ctx_len - 1.0 - cpos))
    wb = jnp.exp(jnp.broadcast_to(lgb_row, (ctx_len, d)) * cpos)
    s_f = _dot_tn((ckf * wf).astype(BF16), cv)
    s_b = _dot_tn((ckf * wb).astype(BF16), cv)

    rope_rows = 512
    pmat = _rope_partner_matrix()

    def rope_blk(i, carry):
        r0 = pl.multiple_of(i * rope_rows, rope_rows)
        cs = cos_ref[pl.ds(r0, rope_rows), :]
        sn = sin_ref[pl.ds(r0, rope_rows), :]
        qr_ref[pl.ds(r0, rope_rows), :] = _rope(q_ref[pl.ds(r0, rope_rows), :], pmat, cs, sn)
        kr_ref[pl.ds(r0, rope_rows), :] = _rope(k_ref[pl.ds(r0, rope_rows), :], pmat, cs, sn) * scale
        return carry

    lax.fori_loop(0, seq // rope_rows, rope_blk, 0, unroll=4)

    def scan_blk(n, carry):
        s, t = carry
        r0 = pl.multiple_of(n * c, c)
        kvf = _dot_tn((kr_ref[pl.ds(r0, c), :] * kdf).astype(BF16), v_ref[pl.ds(r0, c), :])
        sf_ref[n] = s.astype(BF16)
        m = n_chunks - 1 - n
        m0 = pl.multiple_of(m * c, c)
        kvb = _dot_tn((kr_ref[pl.ds(m0, c), :] * kdb).astype(BF16), v_ref[pl.ds(m0, c), :])
        sb_ref[m] = t.astype(BF16)
        return s * cdf + kvf, t * cdb + kvb

    lax.fori_loop(0, n_chunks, scan_blk, (s_f, s_b), unroll=16)

    gn = gn_ref[...]

    def out_blk(n, carry):
        r0 = pl.multiple_of(n * c, c)
        qc = qr_ref[pl.ds(r0, c), :]
        kc = kr_ref[pl.ds(r0, c), :]
        a = _dot_nt(qc.astype(BF16), kc.astype(BF16))
        o = (_dot((a * dmat).astype(BF16), v_ref[pl.ds(r0, c), :])
             + _dot((qc * qdf).astype(BF16), sf_ref[n])
             + _dot((qc * qdb).astype(BF16), sb_ref[n]))
        mu = jnp.mean(o, axis=-1, keepdims=True)
        var = jnp.mean(jnp.square(o - mu), axis=-1, keepdims=True)
        y = (o - mu) * lax.rsqrt(var + EPS) * gn
        o_ref[pl.ds(r0, c), :] = (y * _silu(g_ref[pl.ds(r0, c), :].astype(F32))).astype(o_ref.dtype)
        return carry

    lax.fori_loop(0, n_chunks, out_blk, 0, unroll=16)


def _rope_tables(seq):
    axis_dim = HEAD_DIM // 2
    inv_freq = ROPE_BASE ** (-jnp.arange(0, axis_dim, 2, dtype=F32) / axis_dim)
    rows = seq // GRID_W
    ang_r = jnp.arange(rows, dtype=F32)[:, None] * inv_freq
    ang_c = jnp.arange(GRID_W, dtype=F32)[:, None] * inv_freq
    by_row = lambda a: jnp.repeat(a, GRID_W, axis=0)
    by_col = lambda a: jnp.tile(a, (rows, 1))
    cr, sr, cc, sc = by_row(jnp.cos(ang_r)), by_row(jnp.sin(ang_r)), by_col(jnp.cos(ang_c)), by_col(jnp.sin(ang_c))
    return jnp.concatenate([cr, cr, cc, cc], axis=-1), jnp.concatenate([-sr, sr, -sc, sc], axis=-1)


def _retention(proj, cproj, decay_f, decay_b, gn_w, batch, seq, ctx_len, cc, w_mod, b_mod, mod_from, w_out):
    hb = RET_WIDTH // HEAD_DIM
    cos, sin = _rope_tables(seq)
    n_chunks = seq // RET_BLOCK
    blk = lambda rows, fn: pl.BlockSpec((rows, HEAD_DIM), fn)
    n_steps = RET_HEADS * batch
    step = lambda h, b: h * batch + b
    mod_rows, d = cc.shape
    late = w_mod.shape[1] - mod_from
    mcols, orows = late // n_steps, w_out.shape[0] // n_steps
    side_in = [
        pl.BlockSpec((mod_rows, d), lambda h, b: (0, 0)),
        pl.BlockSpec((d, mcols), lambda h, b: (0, mod_from // mcols + step(h, b))),
        pl.BlockSpec((1, mcols), lambda h, b: (0, mod_from // mcols + step(h, b))),
        pl.BlockSpec((orows, w_out.shape[1]), lambda h, b: (step(h, b), 0)),
    ]
    side_out = [
        pl.BlockSpec((mod_rows, 1, mcols), lambda h, b: (0, 0, step(h, b))),
        pl.BlockSpec((orows, w_out.shape[1]), lambda h, b: (step(h, b), 0)),
    ]
    side_shape = [jax.ShapeDtypeStruct((mod_rows, 1, late), F32), jax.ShapeDtypeStruct(w_out.shape, BF16)]
    return pl.pallas_call(
        _ret_body,
        grid=(RET_HEADS, batch),
        in_specs=[
            pl.BlockSpec(memory_space=pltpu.SMEM),
            pl.BlockSpec(memory_space=pltpu.SMEM),
            blk(1, lambda h, b: (0, h)),
            blk(seq, lambda h, b: (0, 0)),
            blk(seq, lambda h, b: (0, 0)),
            blk(seq, lambda h, b: (b, 5 * hb + h)),
            blk(seq, lambda h, b: (b, 2 * hb + h)),
            blk(seq, lambda h, b: (b, 3 * hb + h)),
            blk(seq, lambda h, b: (b, 6 * hb + h)),
            blk(ctx_len, lambda h, b: (b, 2 * hb + h)),
            blk(ctx_len, lambda h, b: (b, 3 * hb + h)),
        ] + side_in,
        out_specs=[blk(seq, lambda h, b: (b, h))] + side_out,
        out_shape=[jax.ShapeDtypeStruct((batch * seq, RET_WIDTH), BF16)] + side_shape,
        scratch_shapes=[
            pltpu.VMEM((seq, HEAD_DIM), F32),
            pltpu.VMEM((seq, HEAD_DIM), F32),
            pltpu.VMEM((n_chunks, HEAD_DIM, HEAD_DIM), BF16),
            pltpu.VMEM((n_chunks, HEAD_DIM, HEAD_DIM), BF16),
        ],
        compiler_params=_cparams(2),
        name="retention",
    )(decay_f.astype(F32), decay_b.astype(F32), gn_w.reshape(1, RET_WIDTH), cos, sin, proj, proj, proj, proj, cproj, cproj,
      cc, w_mod, b_mod.reshape(1, -1), w_out)


def _split_bf16(x):
    hi = x.astype(BF16)
    lo = (x - hi.astype(F32)).astype(BF16)
    return hi, lo


OUTPROJ_ROW_SPLITS = 2


def _outproj_body(na_ref, ret_ref, w_ref, x_ref, gate_ref, nw_ref, sh_ref, sc_ref, wr_ref,
                  h1_ref, u2_ref, lg_ref):
    half = na_ref.shape[1]
    n_e = wr_ref.shape[0]
    tm = x_ref.shape[0]
    wh, wl = _split_bf16(wr_ref[...])
    w_router = jnp.concatenate([wh, wl], axis=0)
    for r in range(OUTPROJ_ROW_SPLITS):
        rows = pl.ds(r * (tm // OUTPROJ_ROW_SPLITS), tm // OUTPROJ_ROW_SPLITS)
        mix = _dot(na_ref[rows, :], w_ref[:half, :]) + _dot(ret_ref[rows, :], w_ref[half:, :])
        h1 = x_ref[rows, :] + gate_ref[0] * mix
        h1_ref[rows, :] = h1
        u2 = _rms_mod(h1, nw_ref[...], sh_ref[0], sc_ref[0])
        u2_ref[rows, :] = u2
        both = _dot_nt(w_router, u2.astype(BF16))
        lg_ref[:, rows] = both[:n_e] + both[n_e:]


def _out_projection(na, ret, w_out_bf16, x2d, mod3, norm_w, w_router_t, seq, tm=OUTPROJ_ROW_TILE):
    rows, d = x2d.shape
    half = na.shape[1]
    per_b = seq // tm
    return pl.pallas_call(
        _outproj_body,
        grid=(rows // tm,),
        in_specs=[
            pl.BlockSpec((tm, half), lambda i: (i, 0)),
            pl.BlockSpec((tm, half), lambda i: (i, 0)),
            pl.BlockSpec((2 * half, d), lambda i: (0, 0)),
            pl.BlockSpec((tm, d), lambda i: (i, 0)),
            pl.BlockSpec((1, 1, d), lambda i: (i // per_b, 0, MOD_LATE_GATE_MIX)),
            pl.BlockSpec((1, d), lambda i: (0, 0)),
            pl.BlockSpec((1, 1, d), lambda i: (i // per_b, 0, MOD_LATE_SHIFT_FFN)),
            pl.BlockSpec((1, 1, d), lambda i: (i // per_b, 0, MOD_LATE_SCALE_FFN)),
            pl.BlockSpec((N_EXPERTS, d), lambda i: (0, 0)),
        ],
        out_specs=[
            pl.BlockSpec((tm, d), lambda i: (i, 0)),
            pl.BlockSpec((tm, d), lambda i: (i, 0)),
            pl.BlockSpec((N_EXPERTS, tm), lambda i: (0, i)),
        ],
        out_shape=[
            jax.ShapeDtypeStruct((rows, d), F32),
            jax.ShapeDtypeStruct((rows, d), F32),
            jax.ShapeDtypeStruct((N_EXPERTS, rows), F32),
        ],
        compiler_params=_cparams(1),
        name="out_projection",
    )(na, ret, w_out_bf16, x2d, mod3, norm_w.reshape(1, d), mod3, mod3, w_router_t)


def _prefix_incl_lanes(x, tri):
    r, l = x.shape
    nb = l // LANES
    xs = jnp.concatenate([x[:, t * LANES:(t + 1) * LANES] for t in range(nb)], axis=0).astype(BF16)
    p = _dot(xs, tri)
    outs, run = [], jnp.zeros((r, 1), F32)
    for t in range(nb):
        blk = p[t * r:(t + 1) * r] + run
        outs.append(blk)
        run = blk[:, LANES - 1:LANES]
    return jnp.concatenate(outs, axis=1)


NOT_ROUTED = -(1 << 20)


def _route_body(lg_ref, gidx_ref, gate_ref, slot_ref, before_ref, *, cap):
    b = pl.program_id(0)
    n_e, seq = lg_ref.shape
    kf = float(cap)
    lg = lg_ref[...]
    ex = jnp.exp(lg - jnp.max(lg, axis=0, keepdims=True))
    aff = ex / jnp.sum(ex, axis=0, keepdims=True)

    def cond(c):
        return (c[0] < 4096) & (c[5] > 0.5)

    def step(c):
        it, lo, hi, thr, done, _ = c
        mid = 0.5 * (lo + hi)
        above = jnp.sum((aff > mid).astype(F32), axis=1, keepdims=True)
        hit = above == kf
        stuck = (mid <= lo) | (mid >= hi)
        active = done < 0.5
        thr = jnp.where(active & hit, mid, jnp.where(active & stuck, hi, thr))
        go = active & ~(hit | stuck)
        ge = above >= kf
        lo = jnp.where(go & ge, mid, lo)
        hi = jnp.where(go & ~ge, mid, hi)
        done = jnp.where(active & (hit | stuck), 1.0, done)
        return it + 1, lo, hi, thr, done, jnp.sum(1.0 - done)

    col = lambda v: jnp.full((n_e, 1), v, F32)
    init = (jnp.int32(0), col(-1.0), col(2.0), col(0.0), col(0.0), jnp.float32(n_e))
    thr = lax.while_loop(cond, step, init)[3]

    ii = lax.broadcasted_iota(jnp.int32, (LANES, LANES), 0)
    jj = lax.broadcasted_iota(jnp.int32, (LANES, LANES), 1)
    tri = (ii <= jj).astype(BF16)
    gt = aff > thr
    eq = (aff == thr).astype(F32)
    need = kf - jnp.sum(gt.astype(F32), axis=1, keepdims=True)
    eq_before = _prefix_incl_lanes(eq, tri) - eq
    mask = jnp.where(gt | ((eq > 0.5) & (eq_before < need)), 1.0, 0.0)

    slot = _prefix_incl_lanes(mask, tri) - mask
    before_ref[0] = slot.astype(jnp.int32)
    slot_ref[0] = jnp.where(mask > 0.5, slot, float(NOT_ROUTED)).astype(jnp.int32)

    tok = lax.broadcasted_iota(jnp.int32, (1, seq), 1).astype(F32)
    tok_hi = jnp.floor(tok * (1.0 / 64))
    tok_lo = tok - 64.0 * tok_hi
    n_hi = ROUTE_SLOT_HI
    n_lo = cap // n_hi
    hi_iota = lax.broadcasted_iota(jnp.int32, (n_hi, seq), 0).astype(F32)
    lo_iota = lax.broadcasted_iota(jnp.int32, (n_lo, seq), 0).astype(F32)
    for e in range(n_e):
        se = slot[e:e + 1]
        s_hi = jnp.floor(se * (1.0 / n_lo))
        s_lo = se - n_lo * s_hi
        in_hi = jnp.where((s_hi == hi_iota) & (mask[e:e + 1] > 0.5), 1.0, 0.0)
        in_lo = (s_lo == lo_iota).astype(BF16)
        a = aff[e:e + 1]
        a_hi = a.astype(BF16).astype(F32)
        a_mid = (a - a_hi).astype(BF16).astype(F32)
        a_lo = a - a_hi - a_mid
        vals = (tok_hi, tok_lo, a_hi, a_mid, a_lo)
        lhs = jnp.concatenate([in_hi * v for v in vals], axis=0).astype(BF16)
        got = _dot_nt(lhs, in_lo)
        part = lambda r: got[r * n_hi:(r + 1) * n_hi]
        gidx_ref[0, e] = (part(0) * 64.0 + part(1)).astype(jnp.int32) + b * seq
        gate_ref[0, e] = part(2) + part(3) + part(4)


def _routing(logits_t, batch, seq):
    n_e = logits_t.shape[0]
    cap = CAPACITY_FACTOR * seq // n_e
    n_hi, n_lo = ROUTE_SLOT_HI, cap // ROUTE_SLOT_HI
    bec = pl.BlockSpec((1, n_e, n_hi, n_lo), lambda b: (b, 0, 0, 0))
    bel = pl.BlockSpec((1, n_e, seq), lambda b: (b, 0, 0))
    gidx, gates, slot, before = pl.pallas_call(
        functools.partial(_route_body, cap=cap),
        grid=(batch,),
        in_specs=[pl.BlockSpec((n_e, seq), lambda b: (0, b))],
        out_specs=[bec, bec, bel, bel],
        out_shape=[
            jax.ShapeDtypeStruct((batch, n_e, n_hi, n_lo), jnp.int32),
            jax.ShapeDtypeStruct((batch, n_e, n_hi, n_lo), F32),
            jax.ShapeDtypeStruct((batch, n_e, seq), jnp.int32),
            jax.ShapeDtypeStruct((batch, n_e, seq), jnp.int32),
        ],
        compiler_params=_cparams(1),
        name="routing",
    )(logits_t)
    flat = lambda a: a.reshape(batch, n_e, cap)
    return flat(gidx), flat(gates), slot, before


def _hbm_row(ref, r):
    return ref.at[lax.shift_right_logical(r, 3), pl.ds(r & (SUBLANES - 1), 1)]


MOE_ROW_SPLITS = 2
MOE_GATE_ROWS = 4 * BF16_ROWS


def _moe_body(gidx_ref, gnext_ref, gate_ref, u2_hbm, wg_ref, wu_ref, wd_ref, y_ref,
              stage_ref, xe_ref, acc_ref, gcol_ref, gsem, *, n_f):
    e = pl.program_id(0)
    f = pl.program_id(1)
    n_e = pl.num_programs(0)
    m, d = y_ref.shape
    sub = SUBLANES
    n_tiles = m // sub
    assert n_f >= 2
    per_step = -(-n_tiles // (n_f - 1))
    last_count = n_tiles - per_step * (n_f - 2)

    def gather_copy(idx_ref, i, k):
        return pltpu.make_async_copy(_hbm_row(u2_hbm, idx_ref[0, i * sub + k]), stage_ref.at[i, pl.ds(k, 1)], gsem)

    def for_rows(fn, tiles_per_trip=4):
        def body(t, c):
            for kk in range(tiles_per_trip * sub):
                fn(t * tiles_per_trip + kk // sub, kk % sub)
            return c
        lax.fori_loop(0, n_tiles // tiles_per_trip, body, 0)

    wait_gather = lambda: for_rows(lambda i, k: gather_copy(gidx_ref, i, k).wait())

    def ffn_step(first, gather_tiles):
        i0 = (f - 1) * per_step
        for kk in range(gather_tiles * sub):
            gather_copy(gnext_ref, i0 + kk // sub, kk % sub).start()
        wg, wu, wd = wg_ref[...].astype(BF16), wu_ref[...].astype(BF16), wd_ref[...].astype(BF16)
        if first:
            gcol_ref[...] = jnp.transpose(jnp.broadcast_to(gate_ref[...], (LANES, m)))
        part = m // MOE_ROW_SPLITS
        for r in range(MOE_ROW_SPLITS):
            rows = pl.ds(r * part, part)
            if first:
                x = stage_ref[pl.ds(r * (part // sub), part // sub)].reshape(part, d).astype(BF16)
                xe_ref[rows, :] = x
            else:
                x = xe_ref[rows, :]
            hid = _silu(_dot(x, wg)) * _dot(x, wu)
            out = _dot(hid.astype(BF16), wd)
            acc_ref[rows, :] = out if first else acc_ref[rows, :] + out

    @pl.when(f == 0)
    def _():
        @pl.when(e == 0)
        def _():
            for_rows(lambda i, k: gather_copy(gidx_ref, i, k).start())

        wait_gather()
        ffn_step(True, 0)

    if n_f > 2:
        pl.when((f > 0) & (f < n_f - 1))(lambda: ffn_step(False, per_step))

    @pl.when(f == n_f - 1)
    def _():
        ffn_step(False, last_count)

        @pl.when(f > 0)
        def _():
            for r0 in range(0, m, MOE_GATE_ROWS):
                rows = slice(r0, r0 + MOE_GATE_ROWS)
                g = jnp.tile(gcol_ref[rows, :], (1, d // LANES))
                y_ref[rows, :] = (acc_ref[rows, :] * g).astype(y_ref.dtype)

        @pl.when(e == n_e - 1)
        def _():
            wait_gather()


def _moe_ffn(gidx, gates, u2, w_gate, w_up, w_down, tf=MOE_F_TILE):
    n_e, _, m = gidx.shape
    rows, d = u2.shape
    ff = w_gate.shape[2]
    n_f = ff // tf
    sub = SUBLANES
    smem = lambda fn: pl.BlockSpec((None, 1, m), fn, memory_space=pltpu.SMEM)
    return pl.pallas_call(
        functools.partial(_moe_body, n_f=n_f),
        grid=(n_e, n_f),
        in_specs=[
            smem(lambda e, f: (e, 0, 0)),
            smem(lambda e, f: (jnp.minimum(e + 1, n_e - 1), 0, 0)),
            pl.BlockSpec((None, 1, m), lambda e, f: (e, 0, 0)),
            pl.BlockSpec(memory_space=pl.ANY),
            pl.BlockSpec((None, d, tf), lambda e, f: (e, 0, f)),
            pl.BlockSpec((None, d, tf), lambda e, f: (e, 0, f)),
            pl.BlockSpec((None, tf, d), lambda e, f: (e, f, 0)),
        ],
        out_specs=pl.BlockSpec((m, d), lambda e, f: (e, 0)),
        out_shape=jax.ShapeDtypeStruct((n_e * m, d), BF16),
        scratch_shapes=[
            pltpu.VMEM((m // sub, sub, d), F32),
            pltpu.VMEM((m, d), BF16),
            pltpu.VMEM((m, d), F32),
            pltpu.VMEM((m, LANES), F32),
            pltpu.SemaphoreType.DMA,
        ],
        compiler_params=_cparams(2),
        name="moe_ffn",
    )(gidx, gidx, gates, u2.reshape(rows // sub, sub, d), w_gate, w_up, w_down)


COMBINE_TOKENS = 512
COMBINE_PIECE = 96


def _combine_body(start_ref, npiece_ref, h1_ref, y_hbm, slot_ref, gate_ref, nw_ref, o_ref,
                  acc_ref, buf_ref, xbuf_ref, sem, xsem, *, m, cap, per_b):
    i = pl.program_id(0)
    n_tiles = pl.num_programs(0)
    n_e = slot_ref.shape[1]
    t, p = COMBINE_TOKENS, COMBINE_PIECE
    total = n_e * m
    cur = i % 2
    b = i // per_b

    def piece_copy(tile, e, dst):
        st = pl.multiple_of(start_ref[tile * n_e + e], BF16_ROWS)
        return pltpu.make_async_copy(y_hbm.at[pl.ds(st, p)], buf_ref.at[dst, pl.ds(e * p, p)], sem.at[dst])

    @pl.when(i == 0)
    def _():
        for e in range(n_e):
            piece_copy(0, e, 0).start()

    @pl.when(i + 1 < n_tiles)
    def _():
        for e in range(n_e):
            piece_copy(i + 1, e, 1 - cur).start()

    for e in range(n_e):
        piece_copy(i, e, cur).wait()

    slots = slot_ref[0]
    riota = lax.broadcasted_iota(jnp.int32, (p, t), 0)

    def local_row(e, st):
        return slots[e:e + 1, :] + (e * m + b * cap - st)

    sel = jnp.concatenate(
        [(local_row(e, start_ref[i * n_e + e]) == riota).astype(BF16) for e in range(n_e)], axis=0)
    acc_ref[...] = _dot_tn(sel, buf_ref[cur])

    for e in range(n_e):
        st0 = start_ref[i * n_e + e]

        def extra(k, carry, e=e, st0=st0):
            want = st0 + k * p
            st = pl.multiple_of(jnp.minimum(want, total - p), BF16_ROWS)
            cp = pltpu.make_async_copy(y_hbm.at[pl.ds(st, p)], xbuf_ref, xsem)
            cp.start()
            cp.wait()
            blk = ((local_row(e, st) == riota) & (riota >= want - st)).astype(BF16)
            acc_ref[...] += _dot_tn(blk, xbuf_ref[...])
            return carry

        lax.fori_loop(1, npiece_ref[i * n_e + e], extra, 0)

    h2 = h1_ref[...] + gate_ref[0] * acc_ref[...]
    ms = jnp.mean(h2 * h2, axis=-1, keepdims=True)
    o_ref[...] = h2 * lax.rsqrt(ms + EPS) * nw_ref[...]


def _combine_pieces(before, cap):
    batch, n_e, seq = before.shape
    t, p = COMBINE_TOKENS, COMBINE_PIECE
    m = batch * cap
    total = n_e * m
    bounds = jnp.concatenate([before[:, :, ::t], jnp.full((batch, n_e, 1), cap, jnp.int32)], axis=2)
    base = (jnp.arange(n_e, dtype=jnp.int32) * m)[None, :, None] + (jnp.arange(batch, dtype=jnp.int32) * cap)[:, None, None]
    first = base + bounds[:, :, :-1]
    end = base + bounds[:, :, 1:]
    start = jnp.minimum(first // BF16_ROWS * BF16_ROWS, total - p)
    npiece = jnp.maximum((end - start + p - 1) // p, 1)
    by_tile = lambda a: a.transpose(0, 2, 1).reshape(-1)
    return by_tile(start), by_tile(npiece)


def _combine(h1, y, slot, before, mod3, final_w, seq, cap):
    rows, d = h1.shape
    batch, n_e, _ = slot.shape
    t, p = COMBINE_TOKENS, COMBINE_PIECE
    per_b = seq // t
    start, npiece = _combine_pieces(before, cap)
    grid_spec = pltpu.PrefetchScalarGridSpec(
        num_scalar_prefetch=2,
        grid=(rows // t,),
        in_specs=[
            pl.BlockSpec((t, d), lambda i, st, npc: (i, 0)),
            pl.BlockSpec(memory_space=pl.ANY),
            pl.BlockSpec((1, n_e, t), lambda i, st, npc: (i // per_b, 0, i % per_b)),
            pl.BlockSpec((1, 1, d), lambda i, st, npc: (i // per_b, 0, MOD_LATE_GATE_FFN)),
            pl.BlockSpec((1, d), lambda i, st, npc: (0, 0)),
        ],
        out_specs=pl.BlockSpec((t, d), lambda i, st, npc: (i, 0)),
        scratch_shapes=[
            pltpu.VMEM((t, d), F32),
            pltpu.VMEM((2, n_e * p, d), y.dtype),
            pltpu.VMEM((p, d), y.dtype),
            pltpu.SemaphoreType.DMA((2,)),
            pltpu.SemaphoreType.DMA,
        ],
    )
    return pl.pallas_call(
        functools.partial(_combine_body, m=batch * cap, cap=cap, per_b=per_b),
        grid_spec=grid_spec,
        out_shape=jax.ShapeDtypeStruct((rows, d), F32),
        compiler_params=_cparams(1),
        name="combine",
    )(start, npiece, h1, y, slot, mod3, final_w.reshape(1, d))


def kernel(x, c, ctx, c_ctx, w_mod, b_mod, norm_mix_w, norm_ffn_w, w_in, na_rpb, ret_decay_fwd,
           ret_decay_bwd, ret_gn_w, w_out, w_router, w_gate, w_up, w_down, final_norm_w):
    batch, seq, d = x.shape
    ctx_len = ctx.shape[1]
    assert w_mod.shape[0] == 1, "one trunk layer"
    n_e = w_router.shape[2]
    cap = CAPACITY_FACTOR * seq // n_e

    mod_rows = SUBLANES
    cc = jnp.concatenate([c, c_ctx[None], jnp.zeros((mod_rows - batch - 1, d), c.dtype)], axis=0)
    mod3 = _modulation(cc, w_mod[0], b_mod[0], MOD_EARLY * d)

    x2d = x.reshape(batch * seq, d)
    tm = INPROJ_ROW_TILE
    cproj, w_in_bf16 = _ctx_projection(ctx.reshape(batch * ctx_len, d), norm_mix_w[0], mod3, batch,
                                       w_in[0], KV_COLS)
    proj = _in_projection(x2d, norm_mix_w[0], mod3, lambda i: i // (seq // tm), w_in_bf16,
                          w_in.shape[2], tm, w_in.shape[2] // INPROJ_COL_TILES)

    na = _neighbourhood_attention(proj, cproj, na_rpb[0], batch, seq, ctx_len)
    ret, mod_late3, w_out_bf16 = _retention(proj, cproj, ret_decay_fwd[0], ret_decay_bwd[0], ret_gn_w[0],
                                            batch, seq, ctx_len, cc, w_mod[0], b_mod[0], MOD_EARLY * d, w_out[0])

    h1, u2, logits_t = _out_projection(na, ret, w_out_bf16, x2d, mod_late3, norm_ffn_w[0], w_router[0].T, seq)

    gidx, gates, slot, before = _routing(logits_t, batch, seq)
    per_expert = lambda a: a.transpose(1, 0, 2).reshape(n_e, 1, batch * cap)
    y = _moe_ffn(per_expert(gidx), per_expert(gates), u2, w_gate[0], w_up[0], w_down[0])
    out = _combine(h1, y, slot, before, mod_late3, final_norm_w, seq, cap)
    return out.reshape(batch, seq, d)
```

```python
import functools

import jax
import jax.numpy as jnp
from jax import lax
from jax.experimental import pallas as pl
from jax.experimental.pallas import tpu as pltpu

GRID_W = 64
HEAD_DIM = 128
NA_HEADS = 8
RET_HEADS = 8
NA_WIDTH = NA_HEADS * HEAD_DIM
RET_WIDTH = RET_HEADS * HEAD_DIM
WIN_ROWS = 8
WIN_COLS = 16
RET_BLOCK = 256
ROPE_BASE = 10000.0
N_EXPERTS = 16
CAPACITY_FACTOR = 2
N_MOD = 6
MOD_EARLY = 2
MOD_LATE_GATE_MIX, MOD_LATE_SHIFT_FFN, MOD_LATE_SCALE_FFN, MOD_LATE_GATE_FFN = 0, 1, 2, 3
EPS = 1e-6
NEG_INF = -1e30
LOG2_E = 1.4426950408889634
KV_COLS = 2 * NA_WIDTH + 2 * RET_WIDTH

F32 = jnp.float32
BF16 = jnp.bfloat16
LANES = 128
SUBLANES = 8
BF16_ROWS = 16
ROUTE_SLOT_HI = 16
MIB = 1024 * 1024
VMEM_LIMIT_V7X = 56 * MIB

MOD_COL_TILE = 1024
CTX_COL_TILE = 1024
INPROJ_ROW_TILE = 1024
INPROJ_COL_TILES = 4
OUTPROJ_ROW_TILE = 512
MOE_F_TILE = 256


def _cparams(n_axes):
    return pltpu.CompilerParams(
        dimension_semantics=("arbitrary",) * n_axes, vmem_limit_bytes=VMEM_LIMIT_V7X)


def _silu(x):
    return x * jax.nn.sigmoid(x)


def _dot(a, b):
    return jnp.dot(a, b, preferred_element_type=F32)


def _dot_nt(a, b):
    return lax.dot_general(a, b, (((1,), (1,)), ((), ())), preferred_element_type=F32)


def _dot_tn(a, b):
    return lax.dot_general(a, b, (((0,), (0,)), ((), ())), preferred_element_type=F32)


def _mod_body(c_ref, w_ref, b_ref, o_ref):
    a = _silu(c_ref[...]).astype(BF16)
    o_ref[:, 0, :] = _dot(a, w_ref[...].astype(BF16)) + b_ref[...]


def _modulation(cc, w_mod, b_mod, n_cols, tn=MOD_COL_TILE):
    rows, d = cc.shape
    n = n_cols
    return pl.pallas_call(
        _mod_body,
        grid=(n // tn,),
        in_specs=[
            pl.BlockSpec((rows, d), lambda j: (0, 0)),
            pl.BlockSpec((d, tn), lambda j: (0, j)),
            pl.BlockSpec((1, tn), lambda j: (0, j)),
        ],
        out_specs=pl.BlockSpec((rows, 1, tn), lambda j: (0, 0, j)),
        out_shape=jax.ShapeDtypeStruct((rows, 1, n), F32),
        compiler_params=_cparams(1),
        name="modulation",
    )(cc, w_mod, b_mod.reshape(1, -1))


def _rms_mod(x, nw, shift, scale):
    ms = jnp.mean(x * x, axis=-1, keepdims=True)
    y = x * lax.rsqrt(ms + EPS) * nw
    return y * (1.0 + scale) + shift


INPROJ_NORM_SPLITS = 4


def _inproj_body(x_ref, nw_ref, sh_ref, sc_ref, w_ref, o_ref, u_ref):
    j = pl.program_id(1)
    tm = x_ref.shape[0]

    @pl.when(j == 0)
    def _():
        part = tm // INPROJ_NORM_SPLITS
        for r in range(INPROJ_NORM_SPLITS):
            rows = pl.ds(r * part, part)
            u = _rms_mod(x_ref[rows, :], nw_ref[...], sh_ref[0], sc_ref[0]).astype(BF16)
            u_ref[rows, :] = u
            o_ref[rows, :] = _dot(u, w_ref[...]).astype(o_ref.dtype)

    @pl.when(j > 0)
    def _():
        o_ref[...] = _dot(u_ref[...], w_ref[...]).astype(o_ref.dtype)


def _ctx_proj_body(x_ref, nw_ref, sh_ref, sc_ref, w_ref, o_ref, wb_ref, u_ref, *, kv_tiles):
    j = pl.program_id(0)

    @pl.when(j == 0)
    def _():
        u_ref[...] = _rms_mod(x_ref[...], nw_ref[...], sh_ref[0], sc_ref[0]).astype(BF16)

    wb_ref[...] = w_ref[...].astype(BF16)

    @pl.when(j < kv_tiles)
    def _():
        o_ref[...] = _dot(u_ref[...], wb_ref[...]).astype(o_ref.dtype)


def _ctx_projection(x2d, norm_w, mod3, mod_row, w_in, kv_cols, tn=CTX_COL_TILE):
    rows, d = x2d.shape
    n = w_in.shape[1]
    kv_tiles = kv_cols // tn
    return pl.pallas_call(
        functools.partial(_ctx_proj_body, kv_tiles=kv_tiles),
        grid=(n // tn,),
        in_specs=[
            pl.BlockSpec((rows, d), lambda j: (0, 0)),
            pl.BlockSpec((1, d), lambda j: (0, 0)),
            pl.BlockSpec((1, 1, d), lambda j: (mod_row, 0, 0)),
            pl.BlockSpec((1, 1, d), lambda j: (mod_row, 0, 1)),
            pl.BlockSpec((d, tn), lambda j: (0, j)),
        ],
        out_specs=[
            pl.BlockSpec((rows, tn), lambda j: (0, jnp.minimum(j, kv_tiles - 1))),
            pl.BlockSpec((d, tn), lambda j: (0, j)),
        ],
        out_shape=[jax.ShapeDtypeStruct((rows, kv_cols), BF16), jax.ShapeDtypeStruct((d, n), BF16)],
        scratch_shapes=[pltpu.VMEM((rows, d), BF16)],
        compiler_params=_cparams(1),
        name="ctx_projection",
    )(x2d, norm_w.reshape(1, d), mod3, mod3, w_in)


def _in_projection(x2d, norm_w, mod3, mod_row_fn, w_in, n_cols, tm, tn):
    rows, d = x2d.shape
    return pl.pallas_call(
        _inproj_body,
        grid=(rows // tm, n_cols // tn),
        in_specs=[
            pl.BlockSpec((tm, d), lambda i, j: (i, 0)),
            pl.BlockSpec((1, d), lambda i, j: (0, 0)),
            pl.BlockSpec((1, 1, d), lambda i, j: (mod_row_fn(i), 0, 0)),
            pl.BlockSpec((1, 1, d), lambda i, j: (mod_row_fn(i), 0, 1)),
            pl.BlockSpec((d, tn), lambda i, j: (0, j)),
        ],
        out_specs=pl.BlockSpec((tm, tn), lambda i, j: (i, j)),
        out_shape=jax.ShapeDtypeStruct((rows, n_cols), BF16),
        scratch_shapes=[pltpu.VMEM((tm, d), BF16)],
        compiler_params=_cparams(2),
        name="in_projection",
    )(x2d, norm_w.reshape(1, d), mod3, mod3, w_in)


NA_QROWS = 4
NA_KROWS = NA_QROWS + WIN_ROWS
NA_ROW_SPLITS = 2

def _na_row_offset(tile_kind, i, w, rows):
    half = WIN_ROWS // 2
    if tile_kind == 0:
        r, key = i, w
    elif tile_kind == 1:
        r, key = NA_QROWS + i, NA_QROWS - half + w
    else:
        r, key = rows - NA_QROWS + i, rows - NA_KROWS + w
    start = min(max(r - half, 0), rows - WIN_ROWS)
    if not (start <= key < start + WIN_ROWS):
        return None
    return key - r + (WIN_ROWS - 1)


def _na_build_bias(rpb_ref, bias_ref, h, rows):
    w = GRID_W
    cq = lax.broadcasted_iota(jnp.int32, (w, 2 * w), 0)
    ck = lax.broadcasted_iota(jnp.int32, (w, 2 * w), 1) % w
    col_start = jnp.clip(cq - WIN_COLS // 2, 0, w - WIN_COLS)
    col_ok = (ck >= col_start) & (ck < col_start + WIN_COLS)
    neg = jnp.full((w, 2 * w), NEG_INF, F32)
    n_ro, n_co = 2 * WIN_ROWS - 1, 2 * WIN_COLS - 1
    rel = (lax.broadcasted_iota(jnp.int32, (SUBLANES, 2 * w), 1) + w // 2) % w - w // 2
    tabs = []
    for ro in range(n_ro):
        base = jnp.zeros((SUBLANES, 2 * w), F32)
        for j in range(n_co):
            base = jnp.where(rel == j - (WIN_COLS - 1), rpb_ref[h, ro * n_co + j] * (HEAD_DIM ** 0.5), base)
        t = jnp.concatenate([pltpu.roll(base, SUBLANES * g, axis=1, stride=1, stride_axis=0)
                             for g in range(w // SUBLANES)], axis=0)
        tabs.append(jnp.where(col_ok, t, neg))
    left = lax.broadcasted_iota(jnp.int32, (w, 2 * w), 1) < w
    for kind in range(3):
        for i in range(NA_QROWS):
            for wp in range(NA_KROWS // 2):
                ra = _na_row_offset(kind, i, 2 * wp, rows)
                rb = _na_row_offset(kind, i, 2 * wp + 1, rows)
                ta = neg if ra is None else tabs[ra]
                tb = neg if rb is None else tabs[rb]
                blk = ta if ra == rb else jnp.where(left, ta, tb)
                bias_ref[kind, i * w:(i + 1) * w, wp * 2 * w:(wp + 1) * 2 * w] = blk


def _na_body(rpb_ref, q_ref, k_ref, v_ref, ck_ref, cv_ref, o_ref, bias_ref, va_ref, cva_ref, *, rows):
    h = pl.program_id(0)
    w = GRID_W
    tq, tk = NA_QROWS * w, NA_KROWS * w
    n_tiles = rows // NA_QROWS
    scale = HEAD_DIM ** -0.5

    @pl.when(pl.program_id(1) == 0)
    def _():
        _na_build_bias(rpb_ref, bias_ref, h, rows)

    def with_ones(v):
        ones = (lax.broadcasted_iota(jnp.int32, v.shape, 1) == 0).astype(v.dtype)
        return jnp.concatenate([v, ones], axis=1)

    va_ref[...] = with_ones(v_ref[...])
    cva_ref[...] = with_ones(cv_ref[...])

    def attend(t, part):
        kind = 0 if t == 0 else (2 if t == n_tiles - 1 else 1)
        k0 = min(max(t * NA_QROWS - WIN_ROWS // 2, 0), rows - NA_KROWS) * w
        nq = tq // NA_ROW_SPLITS
        q0 = t * tq + part * nq
        q = q_ref[q0:q0 + nq, :]
        s_win = _dot_nt(q, k_ref[k0:k0 + tk, :]) + bias_ref[kind, part * nq:(part + 1) * nq, :]
        s_ctx = _dot_nt(q, ck_ref[...])
        top = jnp.maximum(jnp.max(s_win, axis=-1, keepdims=True), jnp.max(s_ctx, axis=-1, keepdims=True))
        p_win = jnp.exp2((s_win - top) * (scale * LOG2_E)).astype(BF16)
        p_ctx = jnp.exp2((s_ctx - top) * (scale * LOG2_E)).astype(BF16)
        o = _dot(p_win, va_ref[k0:k0 + tk, :]) + _dot(p_ctx, cva_ref[...])
        o_ref[q0:q0 + nq, :] = (o[:, :HEAD_DIM] / o[:, HEAD_DIM:HEAD_DIM + 1]).astype(o_ref.dtype)

    for t in range(n_tiles):
        for part in range(NA_ROW_SPLITS):
            attend(t, part)


def _neighbourhood_attention(proj, cproj, rpb, batch, seq, ctx_len):
    rows = seq // GRID_W
    assert rows % NA_QROWS == 0 and rows >= NA_KROWS + 2 * NA_QROWS
    hb = NA_WIDTH // HEAD_DIM
    n_rpb = (2 * WIN_ROWS - 1) * (2 * WIN_COLS - 1)
    tq, tk = NA_QROWS * GRID_W, NA_KROWS * GRID_W
    grid_spec = pltpu.PrefetchScalarGridSpec(
        num_scalar_prefetch=0,
        grid=(NA_HEADS, batch),
        in_specs=[
            pl.BlockSpec(memory_space=pltpu.SMEM),
            pl.BlockSpec((seq, HEAD_DIM), lambda h, b: (b, 4 * hb + h)),
            pl.BlockSpec((seq, HEAD_DIM), lambda h, b: (b, h)),
            pl.BlockSpec((seq, HEAD_DIM), lambda h, b: (b, hb + h)),
            pl.BlockSpec((ctx_len, HEAD_DIM), lambda h, b: (b, h)),
            pl.BlockSpec((ctx_len, HEAD_DIM), lambda h, b: (b, hb + h)),
        ],
        out_specs=pl.BlockSpec((seq, HEAD_DIM), lambda h, b: (b, h)),
        scratch_shapes=[
            pltpu.VMEM((3, tq, tk), F32),
            pltpu.VMEM((seq, 2 * HEAD_DIM), BF16),
            pltpu.VMEM((ctx_len, 2 * HEAD_DIM), BF16),
        ],
    )
    return pl.pallas_call(
        functools.partial(_na_body, rows=rows),
        grid_spec=grid_spec,
        out_shape=jax.ShapeDtypeStruct((batch * seq, NA_WIDTH), BF16),
        compiler_params=_cparams(2),
        name="neighbourhood_attention",
    )(rpb.reshape(NA_HEADS, n_rpb), proj, proj, proj, cproj, cproj)


def _log_sigmoid(x):
    return -(jnp.maximum(-x, 0.0) + jnp.log1p(jnp.exp(-jnp.abs(x))))


def _rope_partner_matrix():
    quarter = HEAD_DIM // 4
    src = lax.broadcasted_iota(jnp.int32, (HEAD_DIM, HEAD_DIM), 0)
    dst = lax.broadcasted_iota(jnp.int32, (HEAD_DIM, HEAD_DIM), 1)
    want = jnp.where(dst % (2 * quarter) < quarter, dst + quarter, dst - quarter)
    return (src == want).astype(BF16)


def _rope(x_bf16, partner_matrix, cos, sin_signed):
    return x_bf16.astype(F32) * cos + _dot(x_bf16, partner_matrix) * sin_signed


def _ret_body(df_ref, db_ref, gn_ref, cos_ref, sin_ref, q_ref, k_ref, v_ref, g_ref, ck_ref, cv_ref,
              cc_ref, wm_ref, bm_ref, wo_ref,
              o_ref, modl_ref, wob_ref, qr_ref, kr_ref, sf_ref, sb_ref):
    modl_ref[:, 0, :] = _dot(_silu(cc_ref[...]).astype(BF16), wm_ref[...].astype(BF16)) + bm_ref[...]
    wob_ref[...] = wo_ref[...].astype(BF16)

    c, d = RET_BLOCK, HEAD_DIM
    seq = q_ref.shape[0]
    n_chunks = seq // c
    ctx_len = ck_ref.shape[0]
    scale = HEAD_DIM ** -0.5
    head = pl.program_id(0)
    lgf_row = _log_sigmoid(jnp.full((1, HEAD_DIM), df_ref[head], F32))
    lgb_row = _log_sigmoid(jnp.full((1, HEAD_DIM), db_ref[head], F32))
    lgf = jnp.broadcast_to(lgf_row, (c, d))
    lgb = jnp.broadcast_to(lgb_row, (c, d))
    pos = lax.broadcasted_iota(jnp.int32, (c, d), 0).astype(F32)
    kdf = jnp.exp(lgf * (c - 1.0 - pos))
    kdb = jnp.exp(lgb * pos)
    qdf = jnp.exp(lgf * (pos + 1.0))
    qdb = jnp.exp(lgb * (c - pos))
    cdf = jnp.exp(lgf_row * float(c))
    cdb = jnp.exp(lgb_row * float(c))
    diff = (lax.broadcasted_iota(jnp.int32, (c, c), 0) - lax.broadcasted_iota(jnp.int32, (c, c), 1)).astype(F32)
    lgf_cc = jnp.broadcast_to(lgf_row[:, :1], (c, c))
    lgb_cc = jnp.broadcast_to(lgb_row[:, :1], (c, c))
    dmat = (jnp.where(diff >= 0, jnp.exp(lgf_cc * jnp.maximum(diff, 0.0)), 0.0)
            + jnp.where(diff <= 0, jnp.exp(lgb_cc * jnp.maximum(-diff, 0.0)), 0.0))

    cpos = lax.broadcasted_iota(jnp.int32, (ctx_len, d), 0).astype(F32)
    ckf = ck_ref[...].astype(F32) * scale
    cv = cv_ref[...]
    wf = jnp.exp(jnp.broadcast_to(lgf_row, (ctx_len, d)) * (ctx_len - 1.0 - cpos))
    wb = jnp.exp(jnp.broadcast_to(lgb_row, (ctx_len, d)) * cpos)
    s_f = _dot_tn((ckf * wf).astype(BF16), cv)
    s_b = _dot_tn((ckf * wb).astype(BF16), cv)

    rope_rows = 512
    pmat = _rope_partner_matrix()

    def rope_blk(i, carry):
        r0 = pl.multiple_of(i * rope_rows, rope_rows)
        cs = cos_ref[pl.ds(r0, rope_rows), :]
        sn = sin_ref[pl.ds(r0, rope_rows), :]
        qr_ref[pl.ds(r0, rope_rows), :] = _rope(q_ref[pl.ds(r0, rope_rows), :], pmat, cs, sn)
        kr_ref[pl.ds(r0, rope_rows), :] = _rope(k_ref[pl.ds(r0, rope_rows), :], pmat, cs, sn) * scale
        return carry

    lax.fori_loop(0, seq // rope_rows, rope_blk, 0, unroll=4)

    def scan_blk(n, carry):
        s, t = carry
        r0 = pl.multiple_of(n * c, c)
        kvf = _dot_tn((kr_ref[pl.ds(r0, c), :] * kdf).astype(BF16), v_ref[pl.ds(r0, c), :])
        sf_ref[n] = s.astype(BF16)
        m = n_chunks - 1 - n
        m0 = pl.multiple_of(m * c, c)
        kvb = _dot_tn((kr_ref[pl.ds(m0, c), :] * kdb).astype(BF16), v_ref[pl.ds(m0, c), :])
        sb_ref[m] = t.astype(BF16)
        return s * cdf + kvf, t * cdb + kvb

    lax.fori_loop(0, n_chunks, scan_blk, (s_f, s_b), unroll=16)

    gn = gn_ref[...]

    def out_blk(n, carry):
        r0 = pl.multiple_of(n * c, c)
        qc = qr_ref[pl.ds(r0, c), :]
        kc = kr_ref[pl.ds(r0, c), :]
        a = _dot_nt(qc.astype(BF16), kc.astype(BF16))
        o = (_dot((a * dmat).astype(BF16), v_ref[pl.ds(r0, c), :])
             + _dot((qc * qdf).astype(BF16), sf_ref[n])
             + _dot((qc * qdb).astype(BF16), sb_ref[n]))
        mu = jnp.mean(o, axis=-1, keepdims=True)
        var = jnp.mean(jnp.square(o - mu), axis=-1, keepdims=True)
        y = (o - mu) * lax.rsqrt(var + EPS) * gn
        o_ref[pl.ds(r0, c), :] = (y * _silu(g_ref[pl.ds(r0, c), :].astype(F32))).astype(o_ref.dtype)
        return carry

    lax.fori_loop(0, n_chunks, out_blk, 0, unroll=16)


def _rope_tables(seq):
    axis_dim = HEAD_DIM // 2
    inv_freq = ROPE_BASE ** (-jnp.arange(0, axis_dim, 2, dtype=F32) / axis_dim)
    rows = seq // GRID_W
    ang_r = jnp.arange(rows, dtype=F32)[:, None] * inv_freq
    ang_c = jnp.arange(GRID_W, dtype=F32)[:, None] * inv_freq
    by_row = lambda a: jnp.repeat(a, GRID_W, axis=0)
    by_col = lambda a: jnp.tile(a, (rows, 1))
    cr, sr, cc, sc = by_row(jnp.cos(ang_r)), by_row(jnp.sin(ang_r)), by_col(jnp.cos(ang_c)), by_col(jnp.sin(ang_c))
    return jnp.concatenate([cr, cr, cc, cc], axis=-1), jnp.concatenate([-sr, sr, -sc, sc], axis=-1)


def _retention(proj, cproj, decay_f, decay_b, gn_w, batch, seq, ctx_len, cc, w_mod, b_mod, mod_from, w_out):
    hb = RET_WIDTH // HEAD_DIM
    cos, sin = _rope_tables(seq)
    n_chunks = seq // RET_BLOCK
    blk = lambda rows, fn: pl.BlockSpec((rows, HEAD_DIM), fn)
    n_steps = RET_HEADS * batch
    step = lambda h, b: h * batch + b
    mod_rows, d = cc.shape
    late = w_mod.shape[1] - mod_from
    mcols, orows = late // n_steps, w_out.shape[0] // n_steps
    side_in = [
        pl.BlockSpec((mod_rows, d), lambda h, b: (0, 0)),
        pl.BlockSpec((d, mcols), lambda h, b: (0, mod_from // mcols + step(h, b))),
        pl.BlockSpec((1, mcols), lambda h, b: (0, mod_from // mcols + step(h, b))),
        pl.BlockSpec((orows, w_out.shape[1]), lambda h, b: (step(h, b), 0)),
    ]
    side_out = [
        pl.BlockSpec((mod_rows, 1, mcols), lambda h, b: (0, 0, step(h, b))),
        pl.BlockSpec((orows, w_out.shape[1]), lambda h, b: (step(h, b), 0)),
    ]
    side_shape = [jax.ShapeDtypeStruct((mod_rows, 1, late), F32), jax.ShapeDtypeStruct(w_out.shape, BF16)]
    return pl.pallas_call(
        _ret_body,
        grid=(RET_HEADS, batch),
        in_specs=[
            pl.BlockSpec(memory_space=pltpu.SMEM),
            pl.BlockSpec(memory_space=pltpu.SMEM),
            blk(1, lambda h, b: (0, h)),
            blk(seq, lambda h, b: (0, 0)),
            blk(seq, lambda h, b: (0, 0)),
            blk(seq, lambda h, b: (b, 5 * hb + h)),
            blk(seq, lambda h, b: (b, 2 * hb + h)),
            blk(seq, lambda h, b: (b, 3 * hb + h)),
            blk(seq, lambda h, b: (b, 6 * hb + h)),
            blk(ctx_len, lambda h, b: (b, 2 * hb + h)),
            blk(ctx_len, lambda h, b: (b, 3 * hb + h)),
        ] + side_in,
        out_specs=[blk(seq, lambda h, b: (b, h))] + side_out,
        out_shape=[jax.ShapeDtypeStruct((batch * seq, RET_WIDTH), BF16)] + side_shape,
        scratch_shapes=[
            pltpu.VMEM((seq, HEAD_DIM), F32),
            pltpu.VMEM((seq, HEAD_DIM), F32),
            pltpu.VMEM((n_chunks, HEAD_DIM, HEAD_DIM), BF16),
            pltpu.VMEM((n_chunks, HEAD_DIM, HEAD_DIM), BF16),
        ],
        compiler_params=_cparams(2),
        name="retention",
    )(decay_f.astype(F32), decay_b.astype(F32), gn_w.reshape(1, RET_WIDTH), cos, sin, proj, proj, proj, proj, cproj, cproj,
      cc, w_mod, b_mod.reshape(1, -1), w_out)


def _split_bf16(x):
    hi = x.astype(BF16)
    lo = (x - hi.astype(F32)).astype(BF16)
    return hi, lo


OUTPROJ_ROW_SPLITS = 2


def _outproj_body(na_ref, ret_ref, w_ref, x_ref, gate_ref, nw_ref, sh_ref, sc_ref, wr_ref,
                  h1_ref, u2_ref, lg_ref):
    half = na_ref.shape[1]
    n_e = wr_ref.shape[0]
    tm = x_ref.shape[0]
    wh, wl = _split_bf16(wr_ref[...])
    w_router = jnp.concatenate([wh, wl], axis=0)
    for r in range(OUTPROJ_ROW_SPLITS):
        rows = pl.ds(r * (tm // OUTPROJ_ROW_SPLITS), tm // OUTPROJ_ROW_SPLITS)
        mix = _dot(na_ref[rows, :], w_ref[:half, :]) + _dot(ret_ref[rows, :], w_ref[half:, :])
        h1 = x_ref[rows, :] + gate_ref[0] * mix
        h1_ref[rows, :] = h1
        u2 = _rms_mod(h1, nw_ref[...], sh_ref[0], sc_ref[0])
        u2_ref[rows, :] = u2
        both = _dot_nt(w_router, u2.astype(BF16))
        lg_ref[:, rows] = both[:n_e] + both[n_e:]


def _out_projection(na, ret, w_out_bf16, x2d, mod3, norm_w, w_router_t, seq, tm=OUTPROJ_ROW_TILE):
    rows, d = x2d.shape
    half = na.shape[1]
    per_b = seq // tm
    return pl.pallas_call(
        _outproj_body,
        grid=(rows // tm,),
        in_specs=[
            pl.BlockSpec((tm, half), lambda i: (i, 0)),
            pl.BlockSpec((tm, half), lambda i: (i, 0)),
            pl.BlockSpec((2 * half, d), lambda i: (0, 0)),
            pl.BlockSpec((tm, d), lambda i: (i, 0)),
            pl.BlockSpec((1, 1, d), lambda i: (i // per_b, 0, MOD_LATE_GATE_MIX)),
            pl.BlockSpec((1, d), lambda i: (0, 0)),
            pl.BlockSpec((1, 1, d), lambda i: (i // per_b, 0, MOD_LATE_SHIFT_FFN)),
            pl.BlockSpec((1, 1, d), lambda i: (i // per_b, 0, MOD_LATE_SCALE_FFN)),
            pl.BlockSpec((N_EXPERTS, d), lambda i: (0, 0)),
        ],
        out_specs=[
            pl.BlockSpec((tm, d), lambda i: (i, 0)),
            pl.BlockSpec((tm, d), lambda i: (i, 0)),
            pl.BlockSpec((N_EXPERTS, tm), lambda i: (0, i)),
        ],
        out_shape=[
            jax.ShapeDtypeStruct((rows, d), F32),
            jax.ShapeDtypeStruct((rows, d), F32),
            jax.ShapeDtypeStruct((N_EXPERTS, rows), F32),
        ],
        compiler_params=_cparams(1),
        name="out_projection",
    )(na, ret, w_out_bf16, x2d, mod3, norm_w.reshape(1, d), mod3, mod3, w_router_t)


def _prefix_incl_lanes(x, tri):
    r, l = x.shape
    nb = l // LANES
    xs = jnp.concatenate([x[:, t * LANES:(t + 1) * LANES] for t in range(nb)], axis=0).astype(BF16)
    p = _dot(xs, tri)
    outs, run = [], jnp.zeros((r, 1), F32)
    for t in range(nb):
        blk = p[t * r:(t + 1) * r] + run
        outs.append(blk)
        run = blk[:, LANES - 1:LANES]
    return jnp.concatenate(outs, axis=1)


NOT_ROUTED = -(1 << 20)


def _route_body(lg_ref, gidx_ref, gate_ref, slot_ref, before_ref, *, cap):
    b = pl.program_id(0)
    n_e, seq = lg_ref.shape
    kf = float(cap)
    lg = lg_ref[...]
    ex = jnp.exp(lg - jnp.max(lg, axis=0, keepdims=True))
    aff = ex / jnp.sum(ex, axis=0, keepdims=True)

    def cond(c):
        return (c[0] < 4096) & (c[5] > 0.5)

    def step(c):
        it, lo, hi, thr, done, _ = c
        mid = 0.5 * (lo + hi)
        above = jnp.sum((aff > mid).astype(F32), axis=1, keepdims=True)
        hit = above == kf
        stuck = (mid <= lo) | (mid >= hi)
        active = done < 0.5
        thr = jnp.where(active & hit, mid, jnp.where(active & stuck, hi, thr))
        go = active & ~(hit | stuck)
        ge = above >= kf
        lo = jnp.where(go & ge, mid, lo)
        hi = jnp.where(go & ~ge, mid, hi)
        done = jnp.where(active & (hit | stuck), 1.0, done)
        return it + 1, lo, hi, thr, done, jnp.sum(1.0 - done)

    col = lambda v: jnp.full((n_e, 1), v, F32)
    init = (jnp.int32(0), col(-1.0), col(2.0), col(0.0), col(0.0), jnp.float32(n_e))
    thr = lax.while_loop(cond, step, init)[3]

    ii = lax.broadcasted_iota(jnp.int32, (LANES, LANES), 0)
    jj = lax.broadcasted_iota(jnp.int32, (LANES, LANES), 1)
    tri = (ii <= jj).astype(BF16)
    gt = aff > thr
    eq = (aff == thr).astype(F32)
    need = kf - jnp.sum(gt.astype(F32), axis=1, keepdims=True)
    eq_before = _prefix_incl_lanes(eq, tri) - eq
    mask = jnp.where(gt | ((eq > 0.5) & (eq_before < need)), 1.0, 0.0)

    slot = _prefix_incl_lanes(mask, tri) - mask
    lane = lax.broadcasted_iota(jnp.int32, (n_e, LANES), 1)
    first = jnp.zeros((n_e, LANES), F32)
    for k in range(seq // COMBINE_TOKENS):
        first = jnp.where(lane == k, slot[:, k * COMBINE_TOKENS:k * COMBINE_TOKENS + 1], first)
    before_ref[0] = first.astype(jnp.int32)
    slot_ref[0] = jnp.where(mask > 0.5, slot, float(NOT_ROUTED)).astype(jnp.int32)

    tok = lax.broadcasted_iota(jnp.int32, (1, seq), 1).astype(F32)
    tok_hi = jnp.floor(tok * (1.0 / 64))
    tok_lo = tok - 64.0 * tok_hi
    n_hi = ROUTE_SLOT_HI
    n_lo = cap // n_hi
    hi_iota = lax.broadcasted_iota(jnp.int32, (n_hi, seq), 0).astype(F32)
    lo_iota = lax.broadcasted_iota(jnp.int32, (n_lo, seq), 0).astype(F32)
    for e in range(n_e):
        se = slot[e:e + 1]
        s_hi = jnp.floor(se * (1.0 / n_lo))
        s_lo = se - n_lo * s_hi
        in_hi = jnp.where((s_hi == hi_iota) & (mask[e:e + 1] > 0.5), 1.0, 0.0)
        in_lo = (s_lo == lo_iota).astype(BF16)
        a = aff[e:e + 1]
        a_hi = a.astype(BF16).astype(F32)
        a_mid = (a - a_hi).astype(BF16).astype(F32)
        a_lo = a - a_hi - a_mid
        vals = (tok_hi, tok_lo, a_hi, a_mid, a_lo)
        lhs = jnp.concatenate([in_hi * v for v in vals], axis=0).astype(BF16)
        got = _dot_nt(lhs, in_lo)
        part = lambda r: got[r * n_hi:(r + 1) * n_hi]
        gidx_ref[0, e] = (part(0) * 64.0 + part(1)).astype(jnp.int32) + b * seq
        gate_ref[0, e] = part(2) + part(3) + part(4)


def _routing(logits_t, batch, seq):
    n_e = logits_t.shape[0]
    cap = CAPACITY_FACTOR * seq // n_e
    n_hi, n_lo = ROUTE_SLOT_HI, cap // ROUTE_SLOT_HI
    assert seq % COMBINE_TOKENS == 0 and seq // COMBINE_TOKENS <= LANES
    bec = pl.BlockSpec((1, n_e, n_hi, n_lo), lambda b: (b, 0, 0, 0))
    bel = pl.BlockSpec((1, n_e, seq), lambda b: (b, 0, 0))
    gidx, gates, slot, before = pl.pallas_call(
        functools.partial(_route_body, cap=cap),
        grid=(batch,),
        in_specs=[pl.BlockSpec((n_e, seq), lambda b: (0, b))],
        out_specs=[bec, bec, bel, pl.BlockSpec((1, n_e, LANES), lambda b: (b, 0, 0))],
        out_shape=[
            jax.ShapeDtypeStruct((batch, n_e, n_hi, n_lo), jnp.int32),
            jax.ShapeDtypeStruct((batch, n_e, n_hi, n_lo), F32),
            jax.ShapeDtypeStruct((batch, n_e, seq), jnp.int32),
            jax.ShapeDtypeStruct((batch, n_e, LANES), jnp.int32),
        ],
        compiler_params=_cparams(1),
        name="routing",
    )(logits_t)
    flat = lambda a: a.reshape(batch, n_e, cap)
    return flat(gidx), flat(gates), slot, before


def _hbm_row(ref, r):
    return ref.at[lax.shift_right_logical(r, 3), pl.ds(r & (SUBLANES - 1), 1)]


MOE_ROW_SPLITS = 2
MOE_GATE_ROWS = 4 * BF16_ROWS


def _moe_body(gidx_ref, gnext_ref, gate_ref, u2_hbm, wg_ref, wu_ref, wd_ref, y_ref,
              stage_ref, xe_ref, acc_ref, gcol_ref, gsem, *, n_f):
    e = pl.program_id(0)
    f = pl.program_id(1)
    n_e = pl.num_programs(0)
    m, d = y_ref.shape
    sub = SUBLANES
    n_tiles = m // sub
    assert n_f >= 2
    per_step = -(-n_tiles // (n_f - 1))
    last_count = n_tiles - per_step * (n_f - 2)

    def gather_copy(idx_ref, i, k):
        return pltpu.make_async_copy(_hbm_row(u2_hbm, idx_ref[0, i * sub + k]), stage_ref.at[i, pl.ds(k, 1)], gsem)

    def for_rows(fn, tiles_per_trip=4):
        def body(t, c):
            for kk in range(tiles_per_trip * sub):
                fn(t * tiles_per_trip + kk // sub, kk % sub)
            return c
        lax.fori_loop(0, n_tiles // tiles_per_trip, body, 0)

    wait_gather = lambda: for_rows(lambda i, k: gather_copy(gidx_ref, i, k).wait())

    def ffn_step(first, gather_tiles):
        i0 = (f - 1) * per_step
        for kk in range(gather_tiles * sub):
            gather_copy(gnext_ref, i0 + kk // sub, kk % sub).start()
        wg, wu, wd = wg_ref[...].astype(BF16), wu_ref[...].astype(BF16), wd_ref[...].astype(BF16)
        if first:
            gcol_ref[...] = jnp.transpose(jnp.broadcast_to(gate_ref[...], (LANES, m)))
        part = m // MOE_ROW_SPLITS
        for r in range(MOE_ROW_SPLITS):
            rows = pl.ds(r * part, part)
            if first:
                x = stage_ref[pl.ds(r * (part // sub), part // sub)].reshape(part, d).astype(BF16)
                xe_ref[rows, :] = x
            else:
                x = xe_ref[rows, :]
            hid = _silu(_dot(x, wg)) * _dot(x, wu)
            out = _dot(hid.astype(BF16), wd)
            acc_ref[rows, :] = out if first else acc_ref[rows, :] + out

    @pl.when(f == 0)
    def _():
        @pl.when(e == 0)
        def _():
            for_rows(lambda i, k: gather_copy(gidx_ref, i, k).start())

        wait_gather()
        ffn_step(True, 0)

    if n_f > 2:
        pl.when((f > 0) & (f < n_f - 1))(lambda: ffn_step(False, per_step))

    @pl.when(f == n_f - 1)
    def _():
        ffn_step(False, last_count)

        @pl.when(f > 0)
        def _():
            for r0 in range(0, m, MOE_GATE_ROWS):
                rows = slice(r0, r0 + MOE_GATE_ROWS)
                g = jnp.tile(gcol_ref[rows, :], (1, d // LANES))
                y_ref[rows, :] = (acc_ref[rows, :] * g).astype(y_ref.dtype)

        @pl.when(e == n_e - 1)
        def _():
            wait_gather()


def _moe_ffn(gidx, gates, u2, w_gate, w_up, w_down, tf=MOE_F_TILE):
    n_e, _, m = gidx.shape
    rows, d = u2.shape
    ff = w_gate.shape[2]
    n_f = ff // tf
    sub = SUBLANES
    smem = lambda fn: pl.BlockSpec((None, 1, m), fn, memory_space=pltpu.SMEM)
    return pl.pallas_call(
        functools.partial(_moe_body, n_f=n_f),
        grid=(n_e, n_f),
        in_specs=[
            smem(lambda e, f: (e, 0, 0)),
            smem(lambda e, f: (jnp.minimum(e + 1, n_e - 1), 0, 0)),
            pl.BlockSpec((None, 1, m), lambda e, f: (e, 0, 0)),
            pl.BlockSpec(memory_space=pl.ANY),
            pl.BlockSpec((None, d, tf), lambda e, f: (e, 0, f)),
            pl.BlockSpec((None, d, tf), lambda e, f: (e, 0, f)),
            pl.BlockSpec((None, tf, d), lambda e, f: (e, f, 0)),
        ],
        out_specs=pl.BlockSpec((m, d), lambda e, f: (e, 0)),
        out_shape=jax.ShapeDtypeStruct((n_e * m, d), BF16),
        scratch_shapes=[
            pltpu.VMEM((m // sub, sub, d), F32),
            pltpu.VMEM((m, d), BF16),
            pltpu.VMEM((m, d), F32),
            pltpu.VMEM((m, LANES), F32),
            pltpu.SemaphoreType.DMA,
        ],
        compiler_params=_cparams(2),
        name="moe_ffn",
    )(gidx, gidx, gates, u2.reshape(rows // sub, sub, d), w_gate, w_up, w_down)


COMBINE_TOKENS = 512
COMBINE_PIECE = 96


def _combine_body(start_ref, npiece_ref, h1_ref, y_hbm, slot_ref, gate_ref, nw_ref, o_ref,
                  acc_ref, buf_ref, xbuf_ref, sem, xsem, *, m, cap, per_b):
    i = pl.program_id(0)
    n_tiles = pl.num_programs(0)
    n_e = slot_ref.shape[1]
    t, p = COMBINE_TOKENS, COMBINE_PIECE
    total = n_e * m
    cur = i % 2
    b = i // per_b

    def piece_copy(tile, e, dst):
        st = pl.multiple_of(start_ref[tile * n_e + e], BF16_ROWS)
        return pltpu.make_async_copy(y_hbm.at[pl.ds(st, p)], buf_ref.at[dst, pl.ds(e * p, p)], sem.at[dst])

    @pl.when(i == 0)
    def _():
        for e in range(n_e):
            piece_copy(0, e, 0).start()

    @pl.when(i + 1 < n_tiles)
    def _():
        for e in range(n_e):
            piece_copy(i + 1, e, 1 - cur).start()

    for e in range(n_e):
        piece_copy(i, e, cur).wait()

    slots = slot_ref[0]
    riota = lax.broadcasted_iota(jnp.int32, (p, t), 0)

    def local_row(e, st):
        return slots[e:e + 1, :] + (e * m + b * cap - st)

    sel = jnp.concatenate(
        [(local_row(e, start_ref[i * n_e + e]) == riota).astype(BF16) for e in range(n_e)], axis=0)
    acc_ref[...] = _dot_tn(sel, buf_ref[cur])

    for e in range(n_e):
        st0 = start_ref[i * n_e + e]

        def extra(k, carry, e=e, st0=st0):
            want = st0 + k * p
            st = pl.multiple_of(jnp.minimum(want, total - p), BF16_ROWS)
            cp = pltpu.make_async_copy(y_hbm.at[pl.ds(st, p)], xbuf_ref, xsem)
            cp.start()
            cp.wait()
            blk = ((local_row(e, st) == riota) & (riota >= want - st)).astype(BF16)
            acc_ref[...] += _dot_tn(blk, xbuf_ref[...])
            return carry

        lax.fori_loop(1, npiece_ref[i * n_e + e], extra, 0)

    h2 = h1_ref[...] + gate_ref[0] * acc_ref[...]
    ms = jnp.mean(h2 * h2, axis=-1, keepdims=True)
    o_ref[...] = h2 * lax.rsqrt(ms + EPS) * nw_ref[...]


def _combine_pieces(before, seq, cap):
    batch, n_e, _ = before.shape
    t, p = COMBINE_TOKENS, COMBINE_PIECE
    m = batch * cap
    total = n_e * m
    bounds = jnp.concatenate([before[:, :, :seq // t], jnp.full((batch, n_e, 1), cap, jnp.int32)], axis=2)
    base = (jnp.arange(n_e, dtype=jnp.int32) * m)[None, :, None] + (jnp.arange(batch, dtype=jnp.int32) * cap)[:, None, None]
    first = base + bounds[:, :, :-1]
    end = base + bounds[:, :, 1:]
    start = jnp.minimum(first // BF16_ROWS * BF16_ROWS, total - p)
    npiece = jnp.maximum((end - start + p - 1) // p, 1)
    by_tile = lambda a: a.transpose(0, 2, 1).reshape(-1)
    return by_tile(start), by_tile(npiece)


def _combine(h1, y, slot, before, mod3, final_w, seq, cap):
    rows, d = h1.shape
    batch, n_e, _ = slot.shape
    t, p = COMBINE_TOKENS, COMBINE_PIECE
    per_b = seq // t
    start, npiece = _combine_pieces(before, seq, cap)
    grid_spec = pltpu.PrefetchScalarGridSpec(
        num_scalar_prefetch=2,
        grid=(rows // t,),
        in_specs=[
            pl.BlockSpec((t, d), lambda i, st, npc: (i, 0)),
            pl.BlockSpec(memory_space=pl.ANY),
            pl.BlockSpec((1, n_e, t), lambda i, st, npc: (i // per_b, 0, i % per_b)),
            pl.BlockSpec((1, 1, d), lambda i, st, npc: (i // per_b, 0, MOD_LATE_GATE_FFN)),
            pl.BlockSpec((1, d), lambda i, st, npc: (0, 0)),
        ],
        out_specs=pl.BlockSpec((t, d), lambda i, st, npc: (i, 0)),
        scratch_shapes=[
            pltpu.VMEM((t, d), F32),
            pltpu.VMEM((2, n_e * p, d), y.dtype),
            pltpu.VMEM((p, d), y.dtype),
            pltpu.SemaphoreType.DMA((2,)),
            pltpu.SemaphoreType.DMA,
        ],
    )
    return pl.pallas_call(
        functools.partial(_combine_body, m=batch * cap, cap=cap, per_b=per_b),
        grid_spec=grid_spec,
        out_shape=jax.ShapeDtypeStruct((rows, d), F32),
        compiler_params=_cparams(1),
        name="combine",
    )(start, npiece, h1, y, slot, mod3, final_w.reshape(1, d))


def kernel(x, c, ctx, c_ctx, w_mod, b_mod, norm_mix_w, norm_ffn_w, w_in, na_rpb, ret_decay_fwd,
           ret_decay_bwd, ret_gn_w, w_out, w_router, w_gate, w_up, w_down, final_norm_w):
    batch, seq, d = x.shape
    ctx_len = ctx.shape[1]
    assert w_mod.shape[0] == 1, "one trunk layer"
    n_e = w_router.shape[2]
    cap = CAPACITY_FACTOR * seq // n_e

    mod_rows = SUBLANES
    cc = jnp.concatenate([c, c_ctx[None], jnp.zeros((mod_rows - batch - 1, d), c.dtype)], axis=0)
    mod3 = _modulation(cc, w_mod[0], b_mod[0], MOD_EARLY * d)

    x2d = x.reshape(batch * seq, d)
    tm = INPROJ_ROW_TILE
    cproj, w_in_bf16 = _ctx_projection(ctx.reshape(batch * ctx_len, d), norm_mix_w[0], mod3, batch,
                                       w_in[0], KV_COLS)
    proj = _in_projection(x2d, norm_mix_w[0], mod3, lambda i: i // (seq // tm), w_in_bf16,
                          w_in.shape[2], tm, w_in.shape[2] // INPROJ_COL_TILES)

    na = _neighbourhood_attention(proj, cproj, na_rpb[0], batch, seq, ctx_len)
    ret, mod_late3, w_out_bf16 = _retention(proj, cproj, ret_decay_fwd[0], ret_decay_bwd[0], ret_gn_w[0],
                                            batch, seq, ctx_len, cc, w_mod[0], b_mod[0], MOD_EARLY * d, w_out[0])

    h1, u2, logits_t = _out_projection(na, ret, w_out_bf16, x2d, mod_late3, norm_ffn_w[0], w_router[0].T, seq)

    gidx, gates, slot, before = _routing(logits_t, batch, seq)
    per_expert = lambda a: a.transpose(1, 0, 2).reshape(n_e, 1, batch * cap)
    y = _moe_ffn(per_expert(gidx), per_expert(gates), u2, w_gate[0], w_up[0], w_down[0])
    out = _combine(h1, y, slot, before, mod_late3, final_norm_w, seq, cap)
    return out.reshape(batch, seq, d)
```

```python
import functools

import jax
import jax.numpy as jnp
from jax import lax
from jax.experimental import pallas as pl
from jax.experimental.pallas import tpu as pltpu

GRID_W = 64
HEAD_DIM = 128
NA_HEADS = 8
RET_HEADS = 8
NA_WIDTH = NA_HEADS * HEAD_DIM
RET_WIDTH = RET_HEADS * HEAD_DIM
WIN_ROWS = 8
WIN_COLS = 16
RET_BLOCK = 256
ROPE_BASE = 10000.0
N_EXPERTS = 16
CAPACITY_FACTOR = 2
N_MOD = 6
MOD_EARLY = 2
MOD_LATE_GATE_MIX, MOD_LATE_SHIFT_FFN, MOD_LATE_SCALE_FFN, MOD_LATE_GATE_FFN = 0, 1, 2, 3
EPS = 1e-6
NEG_INF = -1e30
LOG2_E = 1.4426950408889634
KV_COLS = 2 * NA_WIDTH + 2 * RET_WIDTH

F32 = jnp.float32
BF16 = jnp.bfloat16
LANES = 128
SUBLANES = 8
BF16_ROWS = 16
ROUTE_SLOT_HI = 16
MIB = 1024 * 1024
VMEM_LIMIT_V7X = 56 * MIB

MOD_COL_TILE = 1024
CTX_COL_TILE = 1024
INPROJ_ROW_TILE = 1024
INPROJ_COL_TILES = 4
OUTPROJ_ROW_TILE = 512
MOE_F_TILE = 256


def _cparams(n_axes):
    return pltpu.CompilerParams(
        dimension_semantics=("arbitrary",) * n_axes, vmem_limit_bytes=VMEM_LIMIT_V7X)


def _silu(x):
    return x * jax.nn.sigmoid(x)


def _dot(a, b):
    return jnp.dot(a, b, preferred_element_type=F32)


def _dot_nt(a, b):
    return lax.dot_general(a, b, (((1,), (1,)), ((), ())), preferred_element_type=F32)


def _dot_tn(a, b):
    return lax.dot_general(a, b, (((0,), (0,)), ((), ())), preferred_element_type=F32)


def _mod_body(c_ref, w_ref, b_ref, o_ref):
    a = _silu(c_ref[...]).astype(BF16)
    o_ref[:, 0, :] = _dot(a, w_ref[...].astype(BF16)) + b_ref[...]


def _modulation(cc, w_mod, b_mod, n_cols, tn=MOD_COL_TILE):
    rows, d = cc.shape
    n = n_cols
    return pl.pallas_call(
        _mod_body,
        grid=(n // tn,),
        in_specs=[
            pl.BlockSpec((rows, d), lambda j: (0, 0)),
            pl.BlockSpec((d, tn), lambda j: (0, j)),
            pl.BlockSpec((1, tn), lambda j: (0, j)),
        ],
        out_specs=pl.BlockSpec((rows, 1, tn), lambda j: (0, 0, j)),
        out_shape=jax.ShapeDtypeStruct((rows, 1, n), F32),
        compiler_params=_cparams(1),
        name="modulation",
    )(cc, w_mod, b_mod.reshape(1, -1))


def _rms_mod(x, nw, shift, scale):
    ms = jnp.mean(x * x, axis=-1, keepdims=True)
    y = x * lax.rsqrt(ms + EPS) * nw
    return y * (1.0 + scale) + shift


INPROJ_NORM_SPLITS = 4


def _inproj_body(x_ref, nw_ref, sh_ref, sc_ref, w_ref, o_ref, u_ref):
    j = pl.program_id(1)
    tm = x_ref.shape[0]

    @pl.when(j == 0)
    def _():
        part = tm // INPROJ_NORM_SPLITS
        for r in range(INPROJ_NORM_SPLITS):
            rows = pl.ds(r * part, part)
            u = _rms_mod(x_ref[rows, :], nw_ref[...], sh_ref[0], sc_ref[0]).astype(BF16)
            u_ref[rows, :] = u
            o_ref[rows, :] = _dot(u, w_ref[...]).astype(o_ref.dtype)

    @pl.when(j > 0)
    def _():
        o_ref[...] = _dot(u_ref[...], w_ref[...]).astype(o_ref.dtype)


def _ctx_proj_body(x_ref, nw_ref, sh_ref, sc_ref, w_ref, o_ref, wb_ref, u_ref, *, kv_tiles):
    j = pl.program_id(0)

    @pl.when(j == 0)
    def _():
        u_ref[...] = _rms_mod(x_ref[...], nw_ref[...], sh_ref[0], sc_ref[0]).astype(BF16)

    wb_ref[...] = w_ref[...].astype(BF16)

    @pl.when(j < kv_tiles)
    def _():
        o_ref[...] = _dot(u_ref[...], wb_ref[...]).astype(o_ref.dtype)


def _ctx_projection(x2d, norm_w, mod3, mod_row, w_in, kv_cols, tn=CTX_COL_TILE):
    rows, d = x2d.shape
    n = w_in.shape[1]
    kv_tiles = kv_cols // tn
    return pl.pallas_call(
        functools.partial(_ctx_proj_body, kv_tiles=kv_tiles),
        grid=(n // tn,),
        in_specs=[
            pl.BlockSpec((rows, d), lambda j: (0, 0)),
            pl.BlockSpec((1, d), lambda j: (0, 0)),
            pl.BlockSpec((1, 1, d), lambda j: (mod_row, 0, 0)),
            pl.BlockSpec((1, 1, d), lambda j: (mod_row, 0, 1)),
            pl.BlockSpec((d, tn), lambda j: (0, j)),
        ],
        out_specs=[
            pl.BlockSpec((rows, tn), lambda j: (0, jnp.minimum(j, kv_tiles - 1))),
            pl.BlockSpec((d, tn), lambda j: (0, j)),
        ],
        out_shape=[jax.ShapeDtypeStruct((rows, kv_cols), BF16), jax.ShapeDtypeStruct((d, n), BF16)],
        scratch_shapes=[pltpu.VMEM((rows, d), BF16)],
        compiler_params=_cparams(1),
        name="ctx_projection",
    )(x2d, norm_w.reshape(1, d), mod3, mod3, w_in)


def _in_projection(x2d, norm_w, mod3, mod_row_fn, w_in, n_cols, tm, tn):
    rows, d = x2d.shape
    return pl.pallas_call(
        _inproj_body,
        grid=(rows // tm, n_cols // tn),
        in_specs=[
            pl.BlockSpec((tm, d), lambda i, j: (i, 0)),
            pl.BlockSpec((1, d), lambda i, j: (0, 0)),
            pl.BlockSpec((1, 1, d), lambda i, j: (mod_row_fn(i), 0, 0)),
            pl.BlockSpec((1, 1, d), lambda i, j: (mod_row_fn(i), 0, 1)),
            pl.BlockSpec((d, tn), lambda i, j: (0, j)),
        ],
        out_specs=pl.BlockSpec((tm, tn), lambda i, j: (i, j)),
        out_shape=jax.ShapeDtypeStruct((rows, n_cols), BF16),
        scratch_shapes=[pltpu.VMEM((tm, d), BF16)],
        compiler_params=_cparams(2),
        name="in_projection",
    )(x2d, norm_w.reshape(1, d), mod3, mod3, w_in)


NA_QROWS = 4
NA_KROWS = NA_QROWS + WIN_ROWS
NA_ROW_SPLITS = 2

def _na_row_offset(tile_kind, i, w, rows):
    half = WIN_ROWS // 2
    if tile_kind == 0:
        r, key = i, w
    elif tile_kind == 1:
        r, key = NA_QROWS + i, NA_QROWS - half + w
    else:
        r, key = rows - NA_QROWS + i, rows - NA_KROWS + w
    start = min(max(r - half, 0), rows - WIN_ROWS)
    if not (start <= key < start + WIN_ROWS):
        return None
    return key - r + (WIN_ROWS - 1)


def _na_build_bias(rpb_ref, bias_ref, h, rows):
    w = GRID_W
    cq = lax.broadcasted_iota(jnp.int32, (w, 2 * w), 0)
    ck = lax.broadcasted_iota(jnp.int32, (w, 2 * w), 1) % w
    col_start = jnp.clip(cq - WIN_COLS // 2, 0, w - WIN_COLS)
    col_ok = (ck >= col_start) & (ck < col_start + WIN_COLS)
    neg = jnp.full((w, 2 * w), NEG_INF, F32)
    n_ro, n_co = 2 * WIN_ROWS - 1, 2 * WIN_COLS - 1
    rel = (lax.broadcasted_iota(jnp.int32, (SUBLANES, 2 * w), 1) + w // 2) % w - w // 2
    tabs = []
    for ro in range(n_ro):
        base = jnp.zeros((SUBLANES, 2 * w), F32)
        for j in range(n_co):
            base = jnp.where(rel == j - (WIN_COLS - 1), rpb_ref[h, ro * n_co + j], base)
        base = base * (HEAD_DIM ** 0.5)
        t = jnp.concatenate([pltpu.roll(base, SUBLANES * g, axis=1, stride=1, stride_axis=0)
                             for g in range(w // SUBLANES)], axis=0)
        tabs.append(jnp.where(col_ok, t, neg))
    left = lax.broadcasted_iota(jnp.int32, (w, 2 * w), 1) < w
    for kind in range(3):
        for i in range(NA_QROWS):
            for wp in range(NA_KROWS // 2):
                ra = _na_row_offset(kind, i, 2 * wp, rows)
                rb = _na_row_offset(kind, i, 2 * wp + 1, rows)
                ta = neg if ra is None else tabs[ra]
                tb = neg if rb is None else tabs[rb]
                blk = ta if ra == rb else jnp.where(left, ta, tb)
                bias_ref[kind, i * w:(i + 1) * w, wp * 2 * w:(wp + 1) * 2 * w] = blk


def _na_body(rpb_ref, q_ref, k_ref, v_ref, ck_ref, cv_ref, o_ref, bias_ref, va_ref, cva_ref, *, rows):
    h = pl.program_id(0)
    w = GRID_W
    tq, tk = NA_QROWS * w, NA_KROWS * w
    n_tiles = rows // NA_QROWS
    scale = HEAD_DIM ** -0.5

    @pl.when(pl.program_id(1) == 0)
    def _():
        _na_build_bias(rpb_ref, bias_ref, h, rows)

    def with_ones(v):
        ones = (lax.broadcasted_iota(jnp.int32, v.shape, 1) == 0).astype(v.dtype)
        return jnp.concatenate([v, ones], axis=1)

    va_ref[...] = with_ones(v_ref[...])
    cva_ref[...] = with_ones(cv_ref[...])

    def attend(t, part):
        kind = 0 if t == 0 else (2 if t == n_tiles - 1 else 1)
        k0 = min(max(t * NA_QROWS - WIN_ROWS // 2, 0), rows - NA_KROWS) * w
        nq = tq // NA_ROW_SPLITS
        q0 = t * tq + part * nq
        q = q_ref[q0:q0 + nq, :]
        s_win = _dot_nt(q, k_ref[k0:k0 + tk, :]) + bias_ref[kind, part * nq:(part + 1) * nq, :]
        s_ctx = _dot_nt(q, ck_ref[...])
        top = jnp.maximum(jnp.max(s_win, axis=-1, keepdims=True), jnp.max(s_ctx, axis=-1, keepdims=True))
        p_win = jnp.exp2((s_win - top) * (scale * LOG2_E)).astype(BF16)
        p_ctx = jnp.exp2((s_ctx - top) * (scale * LOG2_E)).astype(BF16)
        o = _dot(p_win, va_ref[k0:k0 + tk, :]) + _dot(p_ctx, cva_ref[...])
        o_ref[q0:q0 + nq, :] = (o[:, :HEAD_DIM] / o[:, HEAD_DIM:HEAD_DIM + 1]).astype(o_ref.dtype)

    for t in range(n_tiles):
        for part in range(NA_ROW_SPLITS):
            attend(t, part)


def _neighbourhood_attention(proj, cproj, rpb, batch, seq, ctx_len):
    rows = seq // GRID_W
    assert rows % NA_QROWS == 0 and rows >= NA_KROWS + 2 * NA_QROWS
    hb = NA_WIDTH // HEAD_DIM
    n_rpb = (2 * WIN_ROWS - 1) * (2 * WIN_COLS - 1)
    tq, tk = NA_QROWS * GRID_W, NA_KROWS * GRID_W
    grid_spec = pltpu.PrefetchScalarGridSpec(
        num_scalar_prefetch=0,
        grid=(NA_HEADS, batch),
        in_specs=[
            pl.BlockSpec(memory_space=pltpu.SMEM),
            pl.BlockSpec((seq, HEAD_DIM), lambda h, b: (b, 4 * hb + h)),
            pl.BlockSpec((seq, HEAD_DIM), lambda h, b: (b, h)),
            pl.BlockSpec((seq, HEAD_DIM), lambda h, b: (b, hb + h)),
            pl.BlockSpec((ctx_len, HEAD_DIM), lambda h, b: (b, h)),
            pl.BlockSpec((ctx_len, HEAD_DIM), lambda h, b: (b, hb + h)),
        ],
        out_specs=pl.BlockSpec((seq, HEAD_DIM), lambda h, b: (b, h)),
        scratch_shapes=[
            pltpu.VMEM((3, tq, tk), F32),
            pltpu.VMEM((seq, 2 * HEAD_DIM), BF16),
            pltpu.VMEM((ctx_len, 2 * HEAD_DIM), BF16),
        ],
    )
    return pl.pallas_call(
        functools.partial(_na_body, rows=rows),
        grid_spec=grid_spec,
        out_shape=jax.ShapeDtypeStruct((batch * seq, NA_WIDTH), BF16),
        compiler_params=_cparams(2),
        name="neighbourhood_attention",
    )(rpb.reshape(NA_HEADS, n_rpb), proj, proj, proj, cproj, cproj)


def _log_sigmoid(x):
    return -(jnp.maximum(-x, 0.0) + jnp.log1p(jnp.exp(-jnp.abs(x))))


def _rope_partner_matrix():
    quarter = HEAD_DIM // 4
    src = lax.broadcasted_iota(jnp.int32, (HEAD_DIM, HEAD_DIM), 0)
    dst = lax.broadcasted_iota(jnp.int32, (HEAD_DIM, HEAD_DIM), 1)
    want = jnp.where(dst % (2 * quarter) < quarter, dst + quarter, dst - quarter)
    return (src == want).astype(BF16)


def _rope(x_bf16, partner_matrix, cos, sin_signed):
    return x_bf16.astype(F32) * cos + _dot(x_bf16, partner_matrix) * sin_signed


def _ret_body(df_ref, db_ref, gn_ref, cos_ref, sin_ref, q_ref, k_ref, v_ref, g_ref, ck_ref, cv_ref,
              cc_ref, wm_ref, bm_ref, wo_ref,
              o_ref, modl_ref, wob_ref, qr_ref, kr_ref, sf_ref, sb_ref):
    modl_ref[:, 0, :] = _dot(_silu(cc_ref[...]).astype(BF16), wm_ref[...].astype(BF16)) + bm_ref[...]
    wob_ref[...] = wo_ref[...].astype(BF16)

    c, d = RET_BLOCK, HEAD_DIM
    seq = q_ref.shape[0]
    n_chunks = seq // c
    ctx_len = ck_ref.shape[0]
    scale = HEAD_DIM ** -0.5
    head = pl.program_id(0)
    lgf_row = _log_sigmoid(jnp.full((1, HEAD_DIM), df_ref[head], F32))
    lgb_row = _log_sigmoid(jnp.full((1, HEAD_DIM), db_ref[head], F32))
    lgf = jnp.broadcast_to(lgf_row, (c, d))
    lgb = jnp.broadcast_to(lgb_row, (c, d))
    pos = lax.broadcasted_iota(jnp.int32, (c, d), 0).astype(F32)
    kdf = jnp.exp(lgf * (c - 1.0 - pos))
    kdb = jnp.exp(lgb * pos)
    qdf = jnp.exp(lgf * (pos + 1.0))
    qdb = jnp.exp(lgb * (c - pos))
    cdf = jnp.exp(lgf_row * float(c))
    cdb = jnp.exp(lgb_row * float(c))
    diff = (lax.broadcasted_iota(jnp.int32, (c, c), 0) - lax.broadcasted_iota(jnp.int32, (c, c), 1)).astype(F32)
    lgf_cc = jnp.broadcast_to(lgf_row[:, :1], (c, c))
    lgb_cc = jnp.broadcast_to(lgb_row[:, :1], (c, c))
    dmat = (jnp.where(diff >= 0, jnp.exp(lgf_cc * jnp.maximum(diff, 0.0)), 0.0)
            + jnp.where(diff <= 0, jnp.exp(lgb_cc * jnp.maximum(-diff, 0.0)), 0.0))

    cpos = lax.broadcasted_iota(jnp.int32, (ctx_len, d), 0).astype(F32)
    ckf = ck_ref[...].astype(F32) * scale
    cv = cv_ref[...]
    wf = jnp.exp(jnp.broadcast_to(lgf_row, (ctx_len, d)) * (ctx_len - 1.0 - cpos))
    wb = jnp.exp(jnp.broadcast_to(lgb_row, (ctx_len, d)) * cpos)
    s_f = _dot_tn((ckf * wf).astype(BF16), cv)
    s_b = _dot_tn((ckf * wb).astype(BF16), cv)

    rope_rows = 512
    pmat = _rope_partner_matrix()

    def rope_blk(i, carry):
        r0 = pl.multiple_of(i * rope_rows, rope_rows)
        cs = cos_ref[pl.ds(r0, rope_rows), :]
        sn = sin_ref[pl.ds(r0, rope_rows), :]
        qr_ref[pl.ds(r0, rope_rows), :] = _rope(q_ref[pl.ds(r0, rope_rows), :], pmat, cs, sn)
        kr_ref[pl.ds(r0, rope_rows), :] = _rope(k_ref[pl.ds(r0, rope_rows), :], pmat, cs, sn) * scale
        return carry

    lax.fori_loop(0, seq // rope_rows, rope_blk, 0, unroll=8)

    def scan_blk(n, carry):
        s, t = carry
        r0 = pl.multiple_of(n * c, c)
        kvf = _dot_tn((kr_ref[pl.ds(r0, c), :] * kdf).astype(BF16), v_ref[pl.ds(r0, c), :])
        sf_ref[n] = s.astype(BF16)
        m = n_chunks - 1 - n
        m0 = pl.multiple_of(m * c, c)
        kvb = _dot_tn((kr_ref[pl.ds(m0, c), :] * kdb).astype(BF16), v_ref[pl.ds(m0, c), :])
        sb_ref[m] = t.astype(BF16)
        return s * cdf + kvf, t * cdb + kvb

    lax.fori_loop(0, n_chunks, scan_blk, (s_f, s_b), unroll=16)

    gn = gn_ref[...]

    def out_blk(n, carry):
        r0 = pl.multiple_of(n * c, c)
        qc = qr_ref[pl.ds(r0, c), :]
        kc = kr_ref[pl.ds(r0, c), :]
        a = _dot_nt(qc.astype(BF16), kc.astype(BF16))
        o = (_dot((a * dmat).astype(BF16), v_ref[pl.ds(r0, c), :])
             + _dot((qc * qdf).astype(BF16), sf_ref[n])
             + _dot((qc * qdb).astype(BF16), sb_ref[n]))
        mu = jnp.mean(o, axis=-1, keepdims=True)
        var = jnp.mean(jnp.square(o - mu), axis=-1, keepdims=True)
        y = (o - mu) * lax.rsqrt(var + EPS) * gn
        o_ref[pl.ds(r0, c), :] = (y * _silu(g_ref[pl.ds(r0, c), :].astype(F32))).astype(o_ref.dtype)
        return carry

    lax.fori_loop(0, n_chunks, out_blk, 0, unroll=16)


def _rope_tables(seq):
    axis_dim = HEAD_DIM // 2
    inv_freq = ROPE_BASE ** (-jnp.arange(0, axis_dim, 2, dtype=F32) / axis_dim)
    rows = seq // GRID_W
    ang_r = jnp.arange(rows, dtype=F32)[:, None] * inv_freq
    ang_c = jnp.arange(GRID_W, dtype=F32)[:, None] * inv_freq
    by_row = lambda a: jnp.repeat(a, GRID_W, axis=0)
    by_col = lambda a: jnp.tile(a, (rows, 1))
    cr, sr, cc, sc = by_row(jnp.cos(ang_r)), by_row(jnp.sin(ang_r)), by_col(jnp.cos(ang_c)), by_col(jnp.sin(ang_c))
    return jnp.concatenate([cr, cr, cc, cc], axis=-1), jnp.concatenate([-sr, sr, -sc, sc], axis=-1)


def _retention(proj, cproj, decay_f, decay_b, gn_w, batch, seq, ctx_len, cc, w_mod, b_mod, mod_from, w_out):
    hb = RET_WIDTH // HEAD_DIM
    cos, sin = _rope_tables(seq)
    n_chunks = seq // RET_BLOCK
    blk = lambda rows, fn: pl.BlockSpec((rows, HEAD_DIM), fn)
    n_steps = RET_HEADS * batch
    step = lambda h, b: h * batch + b
    mod_rows, d = cc.shape
    late = w_mod.shape[1] - mod_from
    mcols, orows = late // n_steps, w_out.shape[0] // n_steps
    side_in = [
        pl.BlockSpec((mod_rows, d), lambda h, b: (0, 0)),
        pl.BlockSpec((d, mcols), lambda h, b: (0, mod_from // mcols + step(h, b))),
        pl.BlockSpec((1, mcols), lambda h, b: (0, mod_from // mcols + step(h, b))),
        pl.BlockSpec((orows, w_out.shape[1]), lambda h, b: (step(h, b), 0)),
    ]
    side_out = [
        pl.BlockSpec((mod_rows, 1, mcols), lambda h, b: (0, 0, step(h, b))),
        pl.BlockSpec((orows, w_out.shape[1]), lambda h, b: (step(h, b), 0)),
    ]
    side_shape = [jax.ShapeDtypeStruct((mod_rows, 1, late), F32), jax.ShapeDtypeStruct(w_out.shape, BF16)]
    return pl.pallas_call(
        _ret_body,
        grid=(RET_HEADS, batch),
        in_specs=[
            pl.BlockSpec(memory_space=pltpu.SMEM),
            pl.BlockSpec(memory_space=pltpu.SMEM),
            blk(1, lambda h, b: (0, h)),
            blk(seq, lambda h, b: (0, 0)),
            blk(seq, lambda h, b: (0, 0)),
            blk(seq, lambda h, b: (b, 5 * hb + h)),
            blk(seq, lambda h, b: (b, 2 * hb + h)),
            blk(seq, lambda h, b: (b, 3 * hb + h)),
            blk(seq, lambda h, b: (b, 6 * hb + h)),
            blk(ctx_len, lambda h, b: (b, 2 * hb + h)),
            blk(ctx_len, lambda h, b: (b, 3 * hb + h)),
        ] + side_in,
        out_specs=[blk(seq, lambda h, b: (b, h))] + side_out,
        out_shape=[jax.ShapeDtypeStruct((batch * seq, RET_WIDTH), BF16)] + side_shape,
        scratch_shapes=[
            pltpu.VMEM((seq, HEAD_DIM), F32),
            pltpu.VMEM((seq, HEAD_DIM), F32),
            pltpu.VMEM((n_chunks, HEAD_DIM, HEAD_DIM), BF16),
            pltpu.VMEM((n_chunks, HEAD_DIM, HEAD_DIM), BF16),
        ],
        compiler_params=_cparams(2),
        name="retention",
    )(decay_f.astype(F32), decay_b.astype(F32), gn_w.reshape(1, RET_WIDTH), cos, sin, proj, proj, proj, proj, cproj, cproj,
      cc, w_mod, b_mod.reshape(1, -1), w_out)


def _split_bf16(x):
    hi = x.astype(BF16)
    lo = (x - hi.astype(F32)).astype(BF16)
    return hi, lo


OUTPROJ_ROW_SPLITS = 2


def _outproj_body(na_ref, ret_ref, w_ref, x_ref, gate_ref, nw_ref, sh_ref, sc_ref, wr_ref,
                  h1_ref, u2_ref, lg_ref):
    half = na_ref.shape[1]
    n_e = wr_ref.shape[0]
    tm = x_ref.shape[0]
    wh, wl = _split_bf16(wr_ref[...])
    w_router = jnp.concatenate([wh, wl], axis=0)
    for r in range(OUTPROJ_ROW_SPLITS):
        rows = pl.ds(r * (tm // OUTPROJ_ROW_SPLITS), tm // OUTPROJ_ROW_SPLITS)
        mix = _dot(na_ref[rows, :], w_ref[:half, :]) + _dot(ret_ref[rows, :], w_ref[half:, :])
        h1 = x_ref[rows, :] + gate_ref[0] * mix
        h1_ref[rows, :] = h1
        u2 = _rms_mod(h1, nw_ref[...], sh_ref[0], sc_ref[0])
        u2_ref[rows, :] = u2
        both = _dot_nt(w_router, u2.astype(BF16))
        lg_ref[:, rows] = both[:n_e] + both[n_e:]


def _out_projection(na, ret, w_out_bf16, x2d, mod3, norm_w, w_router_t, seq, tm=OUTPROJ_ROW_TILE):
    rows, d = x2d.shape
    half = na.shape[1]
    per_b = seq // tm
    return pl.pallas_call(
        _outproj_body,
        grid=(rows // tm,),
        in_specs=[
            pl.BlockSpec((tm, half), lambda i: (i, 0)),
            pl.BlockSpec((tm, half), lambda i: (i, 0)),
            pl.BlockSpec((2 * half, d), lambda i: (0, 0)),
            pl.BlockSpec((tm, d), lambda i: (i, 0)),
            pl.BlockSpec((1, 1, d), lambda i: (i // per_b, 0, MOD_LATE_GATE_MIX)),
            pl.BlockSpec((1, d), lambda i: (0, 0)),
            pl.BlockSpec((1, 1, d), lambda i: (i // per_b, 0, MOD_LATE_SHIFT_FFN)),
            pl.BlockSpec((1, 1, d), lambda i: (i // per_b, 0, MOD_LATE_SCALE_FFN)),
            pl.BlockSpec((N_EXPERTS, d), lambda i: (0, 0)),
        ],
        out_specs=[
            pl.BlockSpec((tm, d), lambda i: (i, 0)),
            pl.BlockSpec((tm, d), lambda i: (i, 0)),
            pl.BlockSpec((N_EXPERTS, tm), lambda i: (0, i)),
        ],
        out_shape=[
            jax.ShapeDtypeStruct((rows, d), F32),
            jax.ShapeDtypeStruct((rows, d), F32),
            jax.ShapeDtypeStruct((N_EXPERTS, rows), F32),
        ],
        compiler_params=_cparams(1),
        name="out_projection",
    )(na, ret, w_out_bf16, x2d, mod3, norm_w.reshape(1, d), mod3, mod3, w_router_t)


def _prefix_incl_lanes(x, tri):
    r, l = x.shape
    nb = l // LANES
    xs = jnp.concatenate([x[:, t * LANES:(t + 1) * LANES] for t in range(nb)], axis=0).astype(BF16)
    p = _dot(xs, tri)
    outs, run = [], jnp.zeros((r, 1), F32)
    for t in range(nb):
        blk = p[t * r:(t + 1) * r] + run
        outs.append(blk)
        run = blk[:, LANES - 1:LANES]
    return jnp.concatenate(outs, axis=1)


NOT_ROUTED = -(1 << 20)


def _route_body(lg_ref, gidx_ref, gate_ref, slot_ref, before_ref, *, cap):
    b = pl.program_id(0)
    n_e, seq = lg_ref.shape
    kf = float(cap)
    lg = lg_ref[...]
    ex = jnp.exp(lg - jnp.max(lg, axis=0, keepdims=True))
    aff = ex / jnp.sum(ex, axis=0, keepdims=True)

    def cond(c):
        return (c[0] < 4096) & (c[5] > 0.5)

    def step(c):
        it, lo, hi, thr, done, _ = c
        mid = 0.5 * (lo + hi)
        above = jnp.sum((aff > mid).astype(F32), axis=1, keepdims=True)
        hit = above == kf
        stuck = (mid <= lo) | (mid >= hi)
        active = done < 0.5
        thr = jnp.where(active & hit, mid, jnp.where(active & stuck, hi, thr))
        go = active & ~(hit | stuck)
        ge = above >= kf
        lo = jnp.where(go & ge, mid, lo)
        hi = jnp.where(go & ~ge, mid, hi)
        done = jnp.where(active & (hit | stuck), 1.0, done)
        return it + 1, lo, hi, thr, done, jnp.sum(1.0 - done)

    col = lambda v: jnp.full((n_e, 1), v, F32)
    init = (jnp.int32(0), col(-1.0), col(2.0), col(0.0), col(0.0), jnp.float32(n_e))
    thr = lax.while_loop(cond, step, init)[3]

    ii = lax.broadcasted_iota(jnp.int32, (LANES, LANES), 0)
    jj = lax.broadcasted_iota(jnp.int32, (LANES, LANES), 1)
    tri = (ii <= jj).astype(BF16)
    gt = aff > thr
    eq = (aff == thr).astype(F32)
    need = kf - jnp.sum(gt.astype(F32), axis=1, keepdims=True)
    eq_before = _prefix_incl_lanes(eq, tri) - eq
    mask = jnp.where(gt | ((eq > 0.5) & (eq_before < need)), 1.0, 0.0)

    slot = _prefix_incl_lanes(mask, tri) - mask
    before_ref[0] = slot.astype(jnp.int32)
    slot_ref[0] = jnp.where(mask > 0.5, slot, float(NOT_ROUTED)).astype(jnp.int32)

    tok = lax.broadcasted_iota(jnp.int32, (1, seq), 1).astype(F32)
    tok_hi = jnp.floor(tok * (1.0 / 64))
    tok_lo = tok - 64.0 * tok_hi
    n_hi = ROUTE_SLOT_HI
    n_lo = cap // n_hi
    hi_iota = lax.broadcasted_iota(jnp.int32, (n_hi, seq), 0).astype(F32)
    lo_iota = lax.broadcasted_iota(jnp.int32, (n_lo, seq), 0).astype(F32)
    for e in range(n_e):
        se = slot[e:e + 1]
        s_hi = jnp.floor(se * (1.0 / n_lo))
        s_lo = se - n_lo * s_hi
        in_hi = jnp.where((s_hi == hi_iota) & (mask[e:e + 1] > 0.5), 1.0, 0.0)
        in_lo = (s_lo == lo_iota).astype(BF16)
        a = aff[e:e + 1]
        a_hi = a.astype(BF16).astype(F32)
        a_mid = (a - a_hi).astype(BF16).astype(F32)
        a_lo = a - a_hi - a_mid
        vals = (tok_hi, tok_lo, a_hi, a_mid, a_lo)
        lhs = jnp.concatenate([in_hi * v for v in vals], axis=0).astype(BF16)
        got = _dot_nt(lhs, in_lo)
        part = lambda r: got[r * n_hi:(r + 1) * n_hi]
        gidx_ref[0, e] = (part(0) * 64.0 + part(1)).astype(jnp.int32) + b * seq
        gate_ref[0, e] = part(2) + part(3) + part(4)


def _routing(logits_t, batch, seq):
    n_e = logits_t.shape[0]
    cap = CAPACITY_FACTOR * seq // n_e
    n_hi, n_lo = ROUTE_SLOT_HI, cap // ROUTE_SLOT_HI
    bec = pl.BlockSpec((1, n_e, n_hi, n_lo), lambda b: (b, 0, 0, 0))
    bel = pl.BlockSpec((1, n_e, seq), lambda b: (b, 0, 0))
    gidx, gates, slot, before = pl.pallas_call(
        functools.partial(_route_body, cap=cap),
        grid=(batch,),
        in_specs=[pl.BlockSpec((n_e, seq), lambda b: (0, b))],
        out_specs=[bec, bec, bel, bel],
        out_shape=[
            jax.ShapeDtypeStruct((batch, n_e, n_hi, n_lo), jnp.int32),
            jax.ShapeDtypeStruct((batch, n_e, n_hi, n_lo), F32),
            jax.ShapeDtypeStruct((batch, n_e, seq), jnp.int32),
            jax.ShapeDtypeStruct((batch, n_e, seq), jnp.int32),
        ],
        compiler_params=_cparams(1),
        name="routing",
    )(logits_t)
    flat = lambda a: a.reshape(batch, n_e, cap)
    return flat(gidx), flat(gates), slot, before


def _hbm_row(ref, r):
    return ref.at[lax.shift_right_logical(r, 3), pl.ds(r & (SUBLANES - 1), 1)]


MOE_ROW_SPLITS = 2
MOE_GATE_ROWS = 4 * BF16_ROWS


def _moe_body(gidx_ref, gnext_ref, gate_ref, u2_hbm, wg_ref, wu_ref, wd_ref, y_ref,
              stage_ref, xe_ref, acc_ref, gcol_ref, gsem, *, n_f):
    e = pl.program_id(0)
    f = pl.program_id(1)
    n_e = pl.num_programs(0)
    m, d = y_ref.shape
    sub = SUBLANES
    n_tiles = m // sub
    assert n_f >= 2
    per_step = -(-n_tiles // (n_f - 1))
    last_count = n_tiles - per_step * (n_f - 2)

    def gather_copy(idx_ref, i, k):
        return pltpu.make_async_copy(_hbm_row(u2_hbm, idx_ref[0, i * sub + k]), stage_ref.at[i, pl.ds(k, 1)], gsem)

    def for_rows(fn, tiles_per_trip=4):
        def body(t, c):
            for kk in range(tiles_per_trip * sub):
                fn(t * tiles_per_trip + kk // sub, kk % sub)
            return c
        lax.fori_loop(0, n_tiles // tiles_per_trip, body, 0)

    wait_gather = lambda: for_rows(lambda i, k: gather_copy(gidx_ref, i, k).wait())

    def ffn_step(first, gather_tiles):
        i0 = (f - 1) * per_step
        for kk in range(gather_tiles * sub):
            gather_copy(gnext_ref, i0 + kk // sub, kk % sub).start()
        wg, wu, wd = wg_ref[...].astype(BF16), wu_ref[...].astype(BF16), wd_ref[...].astype(BF16)
        if first:
            gcol_ref[...] = jnp.transpose(jnp.broadcast_to(gate_ref[...], (LANES, m)))
        part = m // MOE_ROW_SPLITS
        for r in range(MOE_ROW_SPLITS):
            rows = pl.ds(r * part, part)
            if first:
                x = stage_ref[pl.ds(r * (part // sub), part // sub)].reshape(part, d).astype(BF16)
                xe_ref[rows, :] = x
            else:
                x = xe_ref[rows, :]
            hid = _silu(_dot(x, wg)) * _dot(x, wu)
            out = _dot(hid.astype(BF16), wd)
            acc_ref[rows, :] = out if first else acc_ref[rows, :] + out

    @pl.when(f == 0)
    def _():
        @pl.when(e == 0)
        def _():
            for_rows(lambda i, k: gather_copy(gidx_ref, i, k).start())

        wait_gather()
        ffn_step(True, 0)

    if n_f > 2:
        pl.when((f > 0) & (f < n_f - 1))(lambda: ffn_step(False, per_step))

    @pl.when(f == n_f - 1)
    def _():
        ffn_step(False, last_count)

        @pl.when(f > 0)
        def _():
            for r0 in range(0, m, MOE_GATE_ROWS):
                rows = slice(r0, r0 + MOE_GATE_ROWS)
                g = jnp.tile(gcol_ref[rows, :], (1, d // LANES))
                y_ref[rows, :] = (acc_ref[rows, :] * g).astype(y_ref.dtype)

        @pl.when(e == n_e - 1)
        def _():
            wait_gather()


def _moe_ffn(gidx, gates, u2, w_gate, w_up, w_down, tf=MOE_F_TILE):
    n_e, _, m = gidx.shape
    rows, d = u2.shape
    ff = w_gate.shape[2]
    n_f = ff // tf
    sub = SUBLANES
    smem = lambda fn: pl.BlockSpec((None, 1, m), fn, memory_space=pltpu.SMEM)
    return pl.pallas_call(
        functools.partial(_moe_body, n_f=n_f),
        grid=(n_e, n_f),
        in_specs=[
            smem(lambda e, f: (e, 0, 0)),
            smem(lambda e, f: (jnp.minimum(e + 1, n_e - 1), 0, 0)),
            pl.BlockSpec((None, 1, m), lambda e, f: (e, 0, 0)),
            pl.BlockSpec(memory_space=pl.ANY),
            pl.BlockSpec((None, d, tf), lambda e, f: (e, 0, f)),
            pl.BlockSpec((None, d, tf), lambda e, f: (e, 0, f)),
            pl.BlockSpec((None, tf, d), lambda e, f: (e, f, 0)),
        ],
        out_specs=pl.BlockSpec((m, d), lambda e, f: (e, 0)),
        out_shape=jax.ShapeDtypeStruct((n_e * m, d), BF16),
        scratch_shapes=[
            pltpu.VMEM((m // sub, sub, d), F32),
            pltpu.VMEM((m, d), BF16),
            pltpu.VMEM((m, d), F32),
            pltpu.VMEM((m, LANES), F32),
            pltpu.SemaphoreType.DMA,
        ],
        compiler_params=_cparams(2),
        name="moe_ffn",
    )(gidx, gidx, gates, u2.reshape(rows // sub, sub, d), w_gate, w_up, w_down)


COMBINE_TOKENS = 512
COMBINE_PIECE = 96


def _combine_body(start_ref, npiece_ref, h1_ref, y_hbm, slot_ref, gate_ref, nw_ref, o_ref,
                  acc_ref, buf_ref, xbuf_ref, sem, xsem, *, m, cap, per_b):
    i = pl.program_id(0)
    n_tiles = pl.num_programs(0)
    n_e = slot_ref.shape[1]
    t, p = COMBINE_TOKENS, COMBINE_PIECE
    total = n_e * m
    cur = i % 2
    b = i // per_b

    def piece_copy(tile, e, dst):
        st = pl.multiple_of(start_ref[tile * n_e + e], BF16_ROWS)
        return pltpu.make_async_copy(y_hbm.at[pl.ds(st, p)], buf_ref.at[dst, pl.ds(e * p, p)], sem.at[dst])

    @pl.when(i == 0)
    def _():
        for e in range(n_e):
            piece_copy(0, e, 0).start()

    @pl.when(i + 1 < n_tiles)
    def _():
        for e in range(n_e):
            piece_copy(i + 1, e, 1 - cur).start()

    for e in range(n_e):
        piece_copy(i, e, cur).wait()

    slots = slot_ref[0]
    riota = lax.broadcasted_iota(jnp.int32, (p, t), 0)

    def local_row(e, st):
        return slots[e:e + 1, :] + (e * m + b * cap - st)

    sel = jnp.concatenate(
        [(local_row(e, start_ref[i * n_e + e]) == riota).astype(BF16) for e in range(n_e)], axis=0)
    acc_ref[...] = _dot_tn(sel, buf_ref[cur])

    for e in range(n_e):
        st0 = start_ref[i * n_e + e]

        def extra(k, carry, e=e, st0=st0):
            want = st0 + k * p
            st = pl.multiple_of(jnp.minimum(want, total - p), BF16_ROWS)
            cp = pltpu.make_async_copy(y_hbm.at[pl.ds(st, p)], xbuf_ref, xsem)
            cp.start()
            cp.wait()
            blk = ((local_row(e, st) == riota) & (riota >= want - st)).astype(BF16)
            acc_ref[...] += _dot_tn(blk, xbuf_ref[...])
            return carry

        lax.fori_loop(1, npiece_ref[i * n_e + e], extra, 0)

    h2 = h1_ref[...] + gate_ref[0] * acc_ref[...]
    ms = jnp.mean(h2 * h2, axis=-1, keepdims=True)
    o_ref[...] = h2 * lax.rsqrt(ms + EPS) * nw_ref[...]


def _combine_pieces(before, cap):
    batch, n_e, seq = before.shape
    t, p = COMBINE_TOKENS, COMBINE_PIECE
    m = batch * cap
    total = n_e * m
    bounds = jnp.concatenate([before[:, :, ::t], jnp.full((batch, n_e, 1), cap, jnp.int32)], axis=2)
    base = (jnp.arange(n_e, dtype=jnp.int32) * m)[None, :, None] + (jnp.arange(batch, dtype=jnp.int32) * cap)[:, None, None]
    first = base + bounds[:, :, :-1]
    end = base + bounds[:, :, 1:]
    start = jnp.minimum(first // BF16_ROWS * BF16_ROWS, total - p)
    npiece = jnp.maximum((end - start + p - 1) // p, 1)
    by_tile = lambda a: a.transpose(0, 2, 1).reshape(-1)
    return by_tile(start), by_tile(npiece)


def _combine(h1, y, slot, before, mod3, final_w, seq, cap):
    rows, d = h1.shape
    batch, n_e, _ = slot.shape
    t, p = COMBINE_TOKENS, COMBINE_PIECE
    per_b = seq // t
    start, npiece = _combine_pieces(before, cap)
    grid_spec = pltpu.PrefetchScalarGridSpec(
        num_scalar_prefetch=2,
        grid=(rows // t,),
        in_specs=[
            pl.BlockSpec((t, d), lambda i, st, npc: (i, 0)),
            pl.BlockSpec(memory_space=pl.ANY),
            pl.BlockSpec((1, n_e, t), lambda i, st, npc: (i // per_b, 0, i % per_b)),
            pl.BlockSpec((1, 1, d), lambda i, st, npc: (i // per_b, 0, MOD_LATE_GATE_FFN)),
            pl.BlockSpec((1, d), lambda i, st, npc: (0, 0)),
        ],
        out_specs=pl.BlockSpec((t, d), lambda i, st, npc: (i, 0)),
        scratch_shapes=[
            pltpu.VMEM((t, d), F32),
            pltpu.VMEM((2, n_e * p, d), y.dtype),
            pltpu.VMEM((p, d), y.dtype),
            pltpu.SemaphoreType.DMA((2,)),
            pltpu.SemaphoreType.DMA,
        ],
    )
    return pl.pallas_call(
        functools.partial(_combine_body, m=batch * cap, cap=cap, per_b=per_b),
        grid_spec=grid_spec,
        out_shape=jax.ShapeDtypeStruct((rows, d), F32),
        compiler_params=_cparams(1),
        name="combine",
    )(start, npiece, h1, y, slot, mod3, final_w.reshape(1, d))


def kernel(x, c, ctx, c_ctx, w_mod, b_mod, norm_mix_w, norm_ffn_w, w_in, na_rpb, ret_decay_fwd,
           ret_decay_bwd, ret_gn_w, w_out, w_router, w_gate, w_up, w_down, final_norm_w):
    batch, seq, d = x.shape
    ctx_len = ctx.shape[1]
    assert w_mod.shape[0] == 1, "one trunk layer"
    n_e = w_router.shape[2]
    cap = CAPACITY_FACTOR * seq // n_e

    mod_rows = SUBLANES
    cc = jnp.concatenate([c, c_ctx[None], jnp.zeros((mod_rows - batch - 1, d), c.dtype)], axis=0)
    mod3 = _modulation(cc, w_mod[0], b_mod[0], MOD_EARLY * d)

    x2d = x.reshape(batch * seq, d)
    tm = INPROJ_ROW_TILE
    cproj, w_in_bf16 = _ctx_projection(ctx.reshape(batch * ctx_len, d), norm_mix_w[0], mod3, batch,
                                       w_in[0], KV_COLS)
    proj = _in_projection(x2d, norm_mix_w[0], mod3, lambda i: i // (seq // tm), w_in_bf16,
                          w_in.shape[2], tm, w_in.shape[2] // INPROJ_COL_TILES)

    na = _neighbourhood_attention(proj, cproj, na_rpb[0], batch, seq, ctx_len)
    ret, mod_late3, w_out_bf16 = _retention(proj, cproj, ret_decay_fwd[0], ret_decay_bwd[0], ret_gn_w[0],
                                            batch, seq, ctx_len, cc, w_mod[0], b_mod[0], MOD_EARLY * d, w_out[0])

    h1, u2, logits_t = _out_projection(na, ret, w_out_bf16, x2d, mod_late3, norm_ffn_w[0], w_router[0].T, seq)

    gidx, gates, slot, before = _routing(logits_t, batch, seq)
    per_expert = lambda a: a.transpose(1, 0, 2).reshape(n_e, 1, batch * cap)
    y = _moe_ffn(per_expert(gidx), per_expert(gates), u2, w_gate[0], w_up[0], w_down[0])
    out = _combine(h1, y, slot, before, mod_late3, final_norm_w, seq, cap)
    return out.reshape(batch, seq, d)
```

```python
import functools

import jax
import jax.numpy as jnp
from jax import lax
from jax.experimental import pallas as pl
from jax.experimental.pallas import tpu as pltpu

GRID_W = 64
HEAD_DIM = 128
NA_HEADS = 8
RET_HEADS = 8
NA_WIDTH = NA_HEADS * HEAD_DIM
RET_WIDTH = RET_HEADS * HEAD_DIM
WIN_ROWS = 8
WIN_COLS = 16
RET_BLOCK = 256
ROPE_BASE = 10000.0
N_EXPERTS = 16
CAPACITY_FACTOR = 2
N_MOD = 6
MOD_EARLY = 2
MOD_LATE_GATE_MIX, MOD_LATE_SHIFT_FFN, MOD_LATE_SCALE_FFN, MOD_LATE_GATE_FFN = 0, 1, 2, 3
EPS = 1e-6
NEG_INF = -1e30
LOG2_E = 1.4426950408889634
KV_COLS = 2 * NA_WIDTH + 2 * RET_WIDTH

F32 = jnp.float32
BF16 = jnp.bfloat16
LANES = 128
SUBLANES = 8
BF16_ROWS = 16
ROUTE_SLOT_HI = 16
MIB = 1024 * 1024
VMEM_LIMIT_V7X = 56 * MIB

MOD_COL_TILE = 1024
CTX_COL_TILE = 1024
INPROJ_ROW_TILE = 1024
INPROJ_COL_TILES = 4
OUTPROJ_ROW_TILE = 512
MOE_F_TILE = 256


def _cparams(n_axes):
    return pltpu.CompilerParams(
        dimension_semantics=("arbitrary",) * n_axes, vmem_limit_bytes=VMEM_LIMIT_V7X)


def _silu(x):
    return x * jax.nn.sigmoid(x)


def _dot(a, b):
    return jnp.dot(a, b, preferred_element_type=F32)


def _dot_nt(a, b):
    return lax.dot_general(a, b, (((1,), (1,)), ((), ())), preferred_element_type=F32)


def _dot_tn(a, b):
    return lax.dot_general(a, b, (((0,), (0,)), ((), ())), preferred_element_type=F32)


def _mod_body(c_ref, w_ref, b_ref, o_ref):
    a = _silu(c_ref[...]).astype(BF16)
    o_ref[:, 0, :] = _dot(a, w_ref[...].astype(BF16)) + b_ref[...]


def _modulation(cc, w_mod, b_mod, n_cols, tn=MOD_COL_TILE):
    rows, d = cc.shape
    n = n_cols
    return pl.pallas_call(
        _mod_body,
        grid=(n // tn,),
        in_specs=[
            pl.BlockSpec((rows, d), lambda j: (0, 0)),
            pl.BlockSpec((d, tn), lambda j: (0, j)),
            pl.BlockSpec((1, tn), lambda j: (0, j)),
        ],
        out_specs=pl.BlockSpec((rows, 1, tn), lambda j: (0, 0, j)),
        out_shape=jax.ShapeDtypeStruct((rows, 1, n), F32),
        compiler_params=_cparams(1),
        name="modulation",
    )(cc, w_mod, b_mod.reshape(1, -1))


def _rms_mod(x, nw, shift, scale):
    ms = jnp.mean(x * x, axis=-1, keepdims=True)
    y = x * lax.rsqrt(ms + EPS) * nw
    return y * (1.0 + scale) + shift


INPROJ_NORM_SPLITS = 4


def _inproj_body(x_ref, nw_ref, sh_ref, sc_ref, w_ref, o_ref, u_ref):
    j = pl.program_id(1)
    tm = x_ref.shape[0]

    @pl.when(j == 0)
    def _():
        part = tm // INPROJ_NORM_SPLITS
        for r in range(INPROJ_NORM_SPLITS):
            rows = pl.ds(r * part, part)
            u = _rms_mod(x_ref[rows, :], nw_ref[...], sh_ref[0], sc_ref[0]).astype(BF16)
            u_ref[rows, :] = u
            o_ref[rows, :] = _dot(u, w_ref[...]).astype(o_ref.dtype)

    @pl.when(j > 0)
    def _():
        o_ref[...] = _dot(u_ref[...], w_ref[...]).astype(o_ref.dtype)


def _ctx_proj_body(x_ref, nw_ref, sh_ref, sc_ref, w_ref, o_ref, wb_ref, u_ref, *, kv_tiles):
    j = pl.program_id(0)

    @pl.when(j == 0)
    def _():
        u_ref[...] = _rms_mod(x_ref[...], nw_ref[...], sh_ref[0], sc_ref[0]).astype(BF16)

    wb_ref[...] = w_ref[...].astype(BF16)

    @pl.when(j < kv_tiles)
    def _():
        o_ref[...] = _dot(u_ref[...], wb_ref[...]).astype(o_ref.dtype)


def _ctx_projection(x2d, norm_w, mod3, mod_row, w_in, kv_cols, tn=CTX_COL_TILE):
    rows, d = x2d.shape
    n = w_in.shape[1]
    kv_tiles = kv_cols // tn
    return pl.pallas_call(
        functools.partial(_ctx_proj_body, kv_tiles=kv_tiles),
        grid=(n // tn,),
        in_specs=[
            pl.BlockSpec((rows, d), lambda j: (0, 0)),
            pl.BlockSpec((1, d), lambda j: (0, 0)),
            pl.BlockSpec((1, 1, d), lambda j: (mod_row, 0, 0)),
            pl.BlockSpec((1, 1, d), lambda j: (mod_row, 0, 1)),
            pl.BlockSpec((d, tn), lambda j: (0, j)),
        ],
        out_specs=[
            pl.BlockSpec((rows, tn), lambda j: (0, jnp.minimum(j, kv_tiles - 1))),
            pl.BlockSpec((d, tn), lambda j: (0, j)),
        ],
        out_shape=[jax.ShapeDtypeStruct((rows, kv_cols), BF16), jax.ShapeDtypeStruct((d, n), BF16)],
        scratch_shapes=[pltpu.VMEM((rows, d), BF16)],
        compiler_params=_cparams(1),
        name="ctx_projection",
    )(x2d, norm_w.reshape(1, d), mod3, mod3, w_in)


def _in_projection(x2d, norm_w, mod3, mod_row_fn, w_in, n_cols, tm, tn):
    rows, d = x2d.shape
    return pl.pallas_call(
        _inproj_body,
        grid=(rows // tm, n_cols // tn),
        in_specs=[
            pl.BlockSpec((tm, d), lambda i, j: (i, 0)),
            pl.BlockSpec((1, d), lambda i, j: (0, 0)),
            pl.BlockSpec((1, 1, d), lambda i, j: (mod_row_fn(i), 0, 0)),
            pl.BlockSpec((1, 1, d), lambda i, j: (mod_row_fn(i), 0, 1)),
            pl.BlockSpec((d, tn), lambda i, j: (0, j)),
        ],
        out_specs=pl.BlockSpec((tm, tn), lambda i, j: (i, j)),
        out_shape=jax.ShapeDtypeStruct((rows, n_cols), BF16),
        scratch_shapes=[pltpu.VMEM((tm, d), BF16)],
        compiler_params=_cparams(2),
        name="in_projection",
    )(x2d, norm_w.reshape(1, d), mod3, mod3, w_in)


NA_QROWS = 4
NA_KROWS = NA_QROWS + WIN_ROWS
NA_ROW_SPLITS = 2

def _na_row_offset(tile_kind, i, w, rows):
    half = WIN_ROWS // 2
    if tile_kind == 0:
        r, key = i, w
    elif tile_kind == 1:
        r, key = NA_QROWS + i, NA_QROWS - half + w
    else:
        r, key = rows - NA_QROWS + i, rows - NA_KROWS + w
    start = min(max(r - half, 0), rows - WIN_ROWS)
    if not (start <= key < start + WIN_ROWS):
        return None
    return key - r + (WIN_ROWS - 1)


def _na_build_bias(rpb_ref, bias_ref, h, rows):
    w = GRID_W
    cq = lax.broadcasted_iota(jnp.int32, (w, 2 * w), 0)
    ck = lax.broadcasted_iota(jnp.int32, (w, 2 * w), 1) % w
    col_start = jnp.clip(cq - WIN_COLS // 2, 0, w - WIN_COLS)
    col_ok = (ck >= col_start) & (ck < col_start + WIN_COLS)
    neg = jnp.full((w, 2 * w), NEG_INF, F32)
    n_ro, n_co = 2 * WIN_ROWS - 1, 2 * WIN_COLS - 1
    rel = (lax.broadcasted_iota(jnp.int32, (SUBLANES, 2 * w), 1) + w // 2) % w - w // 2
    tabs = []
    for ro in range(n_ro):
        base = jnp.zeros((SUBLANES, 2 * w), F32)
        for j in range(n_co):
            base = jnp.where(rel == j - (WIN_COLS - 1), rpb_ref[h, ro * n_co + j], base)
        base = base * (HEAD_DIM ** 0.5)
        t = jnp.concatenate([pltpu.roll(base, SUBLANES * g, axis=1, stride=1, stride_axis=0)
                             for g in range(w // SUBLANES)], axis=0)
        tabs.append(jnp.where(col_ok, t, neg))
    left = lax.broadcasted_iota(jnp.int32, (w, 2 * w), 1) < w
    for kind in range(3):
        for i in range(NA_QROWS):
            for wp in range(NA_KROWS // 2):
                ra = _na_row_offset(kind, i, 2 * wp, rows)
                rb = _na_row_offset(kind, i, 2 * wp + 1, rows)
                ta = neg if ra is None else tabs[ra]
                tb = neg if rb is None else tabs[rb]
                blk = ta if ra == rb else jnp.where(left, ta, tb)
                bias_ref[kind, i * w:(i + 1) * w, wp * 2 * w:(wp + 1) * 2 * w] = blk


def _na_body(rpb_ref, q_ref, k_ref, v_ref, ck_ref, cv_ref, o_ref, bias_ref, va_ref, cva_ref, *, rows):
    h = pl.program_id(0)
    w = GRID_W
    tq, tk = NA_QROWS * w, NA_KROWS * w
    n_tiles = rows // NA_QROWS
    scale = HEAD_DIM ** -0.5

    @pl.when(pl.program_id(1) == 0)
    def _():
        _na_build_bias(rpb_ref, bias_ref, h, rows)

    def with_ones(v):
        ones = (lax.broadcasted_iota(jnp.int32, v.shape, 1) == 0).astype(v.dtype)
        return jnp.concatenate([v, ones], axis=1)

    va_ref[...] = with_ones(v_ref[...])
    cva_ref[...] = with_ones(cv_ref[...])

    def attend(t, part):
        kind = 0 if t == 0 else (2 if t == n_tiles - 1 else 1)
        k0 = min(max(t * NA_QROWS - WIN_ROWS // 2, 0), rows - NA_KROWS) * w
        nq = tq // NA_ROW_SPLITS
        q0 = t * tq + part * nq
        q = q_ref[q0:q0 + nq, :]
        s_win = _dot_nt(q, k_ref[k0:k0 + tk, :]) + bias_ref[kind, part * nq:(part + 1) * nq, :]
        s_ctx = _dot_nt(q, ck_ref[...])
        top = jnp.maximum(jnp.max(s_win, axis=-1, keepdims=True), jnp.max(s_ctx, axis=-1, keepdims=True))
        p_win = jnp.exp2((s_win - top) * (scale * LOG2_E)).astype(BF16)
        p_ctx = jnp.exp2((s_ctx - top) * (scale * LOG2_E)).astype(BF16)
        o = _dot(p_win, va_ref[k0:k0 + tk, :]) + _dot(p_ctx, cva_ref[...])
        o_ref[q0:q0 + nq, :] = (o[:, :HEAD_DIM] / o[:, HEAD_DIM:HEAD_DIM + 1]).astype(o_ref.dtype)

    for t in range(n_tiles):
        for part in range(NA_ROW_SPLITS):
            attend(t, part)


def _neighbourhood_attention(proj, cproj, rpb, batch, seq, ctx_len):
    rows = seq // GRID_W
    assert rows % NA_QROWS == 0 and rows >= NA_KROWS + 2 * NA_QROWS
    hb = NA_WIDTH // HEAD_DIM
    n_rpb = (2 * WIN_ROWS - 1) * (2 * WIN_COLS - 1)
    tq, tk = NA_QROWS * GRID_W, NA_KROWS * GRID_W
    grid_spec = pltpu.PrefetchScalarGridSpec(
        num_scalar_prefetch=0,
        grid=(NA_HEADS, batch),
        in_specs=[
            pl.BlockSpec(memory_space=pltpu.SMEM),
            pl.BlockSpec((seq, HEAD_DIM), lambda h, b: (b, 4 * hb + h)),
            pl.BlockSpec((seq, HEAD_DIM), lambda h, b: (b, h)),
            pl.BlockSpec((seq, HEAD_DIM), lambda h, b: (b, hb + h)),
            pl.BlockSpec((ctx_len, HEAD_DIM), lambda h, b: (b, h)),
            pl.BlockSpec((ctx_len, HEAD_DIM), lambda h, b: (b, hb + h)),
        ],
        out_specs=pl.BlockSpec((seq, HEAD_DIM), lambda h, b: (b, h)),
        scratch_shapes=[
            pltpu.VMEM((3, tq, tk), F32),
            pltpu.VMEM((seq, 2 * HEAD_DIM), BF16),
            pltpu.VMEM((ctx_len, 2 * HEAD_DIM), BF16),
        ],
    )
    return pl.pallas_call(
        functools.partial(_na_body, rows=rows),
        grid_spec=grid_spec,
        out_shape=jax.ShapeDtypeStruct((batch * seq, NA_WIDTH), BF16),
        compiler_params=_cparams(2),
        name="neighbourhood_attention",
    )(rpb.reshape(NA_HEADS, n_rpb), proj, proj, proj, cproj, cproj)


def _log_sigmoid(x):
    return -(jnp.maximum(-x, 0.0) + jnp.log1p(jnp.exp(-jnp.abs(x))))


def _rope_partner_matrix():
    quarter = HEAD_DIM // 4
    src = lax.broadcasted_iota(jnp.int32, (HEAD_DIM, HEAD_DIM), 0)
    dst = lax.broadcasted_iota(jnp.int32, (HEAD_DIM, HEAD_DIM), 1)
    want = jnp.where(dst % (2 * quarter) < quarter, dst + quarter, dst - quarter)
    return (src == want).astype(BF16)


def _rope(x_bf16, partner_matrix, cos, sin_signed):
    return x_bf16.astype(F32) * cos + _dot(x_bf16, partner_matrix) * sin_signed


def _ret_body(df_ref, db_ref, gn_ref, cos_ref, sin_ref, q_ref, k_ref, v_ref, g_ref, ck_ref, cv_ref,
              cc_ref, wm_ref, bm_ref, wo_ref,
              o_ref, modl_ref, wob_ref, qr_ref, kr_ref, sf_ref, sb_ref):
    modl_ref[:, 0, :] = _dot(_silu(cc_ref[...]).astype(BF16), wm_ref[...].astype(BF16)) + bm_ref[...]
    wob_ref[...] = wo_ref[...].astype(BF16)

    c, d = RET_BLOCK, HEAD_DIM
    seq = q_ref.shape[0]
    n_chunks = seq // c
    ctx_len = ck_ref.shape[0]
    scale = HEAD_DIM ** -0.5
    head = pl.program_id(0)
    lgf_row = _log_sigmoid(jnp.full((1, HEAD_DIM), df_ref[head], F32))
    lgb_row = _log_sigmoid(jnp.full((1, HEAD_DIM), db_ref[head], F32))
    lgf = jnp.broadcast_to(lgf_row, (c, d))
    lgb = jnp.broadcast_to(lgb_row, (c, d))
    pos = lax.broadcasted_iota(jnp.int32, (c, d), 0).astype(F32)
    kdf = jnp.exp(lgf * (c - 1.0 - pos))
    kdb = jnp.exp(lgb * pos)
    qdf = jnp.exp(lgf * (pos + 1.0))
    qdb = jnp.exp(lgb * (c - pos))
    cdf = jnp.exp(lgf_row * float(c))
    cdb = jnp.exp(lgb_row * float(c))
    diff = (lax.broadcasted_iota(jnp.int32, (c, c), 0) - lax.broadcasted_iota(jnp.int32, (c, c), 1)).astype(F32)
    lgf_cc = jnp.broadcast_to(lgf_row[:, :1], (c, c))
    lgb_cc = jnp.broadcast_to(lgb_row[:, :1], (c, c))
    dmat = (jnp.where(diff >= 0, jnp.exp(lgf_cc * jnp.maximum(diff, 0.0)), 0.0)
            + jnp.where(diff <= 0, jnp.exp(lgb_cc * jnp.maximum(-diff, 0.0)), 0.0))

    cpos = lax.broadcasted_iota(jnp.int32, (ctx_len, d), 0).astype(F32)
    ckf = ck_ref[...].astype(F32) * scale
    cv = cv_ref[...]
    wf = jnp.exp(jnp.broadcast_to(lgf_row, (ctx_len, d)) * (ctx_len - 1.0 - cpos))
    wb = jnp.exp(jnp.broadcast_to(lgb_row, (ctx_len, d)) * cpos)
    s_f = _dot_tn((ckf * wf).astype(BF16), cv)
    s_b = _dot_tn((ckf * wb).astype(BF16), cv)

    rope_rows = 512
    pmat = _rope_partner_matrix()

    def rope_blk(i, carry):
        r0 = pl.multiple_of(i * rope_rows, rope_rows)
        cs = cos_ref[pl.ds(r0, rope_rows), :]
        sn = sin_ref[pl.ds(r0, rope_rows), :]
        qr_ref[pl.ds(r0, rope_rows), :] = _rope(q_ref[pl.ds(r0, rope_rows), :], pmat, cs, sn)
        kr_ref[pl.ds(r0, rope_rows), :] = _rope(k_ref[pl.ds(r0, rope_rows), :], pmat, cs, sn) * scale
        return carry

    lax.fori_loop(0, seq // rope_rows, rope_blk, 0, unroll=8)

    def scan_blk(n, carry):
        s, t = carry
        r0 = pl.multiple_of(n * c, c)
        kvf = _dot_tn((kr_ref[pl.ds(r0, c), :] * kdf).astype(BF16), v_ref[pl.ds(r0, c), :])
        sf_ref[n] = s.astype(BF16)
        m = n_chunks - 1 - n
        m0 = pl.multiple_of(m * c, c)
        kvb = _dot_tn((kr_ref[pl.ds(m0, c), :] * kdb).astype(BF16), v_ref[pl.ds(m0, c), :])
        sb_ref[m] = t.astype(BF16)
        return s * cdf + kvf, t * cdb + kvb

    lax.fori_loop(0, n_chunks, scan_blk, (s_f, s_b), unroll=16)

    gn = gn_ref[...]

    def out_blk(n, carry):
        r0 = pl.multiple_of(n * c, c)
        qc = qr_ref[pl.ds(r0, c), :]
        kc = kr_ref[pl.ds(r0, c), :]
        a = _dot_nt(qc.astype(BF16), kc.astype(BF16))
        o = (_dot((a * dmat).astype(BF16), v_ref[pl.ds(r0, c), :])
             + _dot((qc * qdf).astype(BF16), sf_ref[n])
             + _dot((qc * qdb).astype(BF16), sb_ref[n]))
        mu = jnp.mean(o, axis=-1, keepdims=True)
        var = jnp.mean(jnp.square(o - mu), axis=-1, keepdims=True)
        y = (o - mu) * lax.rsqrt(var + EPS) * gn
        o_ref[pl.ds(r0, c), :] = (y * _silu(g_ref[pl.ds(r0, c), :].astype(F32))).astype(o_ref.dtype)
        return carry

    lax.fori_loop(0, n_chunks, out_blk, 0, unroll=16)


def _rope_tables(seq):
    axis_dim = HEAD_DIM // 2
    inv_freq = ROPE_BASE ** (-jnp.arange(0, axis_dim, 2, dtype=F32) / axis_dim)
    rows = seq // GRID_W
    ang_r = jnp.arange(rows, dtype=F32)[:, None] * inv_freq
    ang_c = jnp.arange(GRID_W, dtype=F32)[:, None] * inv_freq
    by_row = lambda a: jnp.repeat(a, GRID_W, axis=0)
    by_col = lambda a: jnp.tile(a, (rows, 1))
    cr, sr, cc, sc = by_row(jnp.cos(ang_r)), by_row(jnp.sin(ang_r)), by_col(jnp.cos(ang_c)), by_col(jnp.sin(ang_c))
    return jnp.concatenate([cr, cr, cc, cc], axis=-1), jnp.concatenate([-sr, sr, -sc, sc], axis=-1)


def _retention(proj, cproj, decay_f, decay_b, gn_w, batch, seq, ctx_len, cc, w_mod, b_mod, mod_from, w_out):
    hb = RET_WIDTH // HEAD_DIM
    cos, sin = _rope_tables(seq)
    n_chunks = seq // RET_BLOCK
    blk = lambda rows, fn: pl.BlockSpec((rows, HEAD_DIM), fn)
    n_steps = RET_HEADS * batch
    step = lambda h, b: h * batch + b
    mod_rows, d = cc.shape
    late = w_mod.shape[1] - mod_from
    mcols, orows = late // n_steps, w_out.shape[0] // n_steps
    side_in = [
        pl.BlockSpec((mod_rows, d), lambda h, b: (0, 0)),
        pl.BlockSpec((d, mcols), lambda h, b: (0, mod_from // mcols + step(h, b))),
        pl.BlockSpec((1, mcols), lambda h, b: (0, mod_from // mcols + step(h, b))),
        pl.BlockSpec((orows, w_out.shape[1]), lambda h, b: (step(h, b), 0)),
    ]
    side_out = [
        pl.BlockSpec((mod_rows, 1, mcols), lambda h, b: (0, 0, step(h, b))),
        pl.BlockSpec((orows, w_out.shape[1]), lambda h, b: (step(h, b), 0)),
    ]
    side_shape = [jax.ShapeDtypeStruct((mod_rows, 1, late), F32), jax.ShapeDtypeStruct(w_out.shape, BF16)]
    return pl.pallas_call(
        _ret_body,
        grid=(RET_HEADS, batch),
        in_specs=[
            pl.BlockSpec(memory_space=pltpu.SMEM),
            pl.BlockSpec(memory_space=pltpu.SMEM),
            blk(1, lambda h, b: (0, h)),
            blk(seq, lambda h, b: (0, 0)),
            blk(seq, lambda h, b: (0, 0)),
            blk(seq, lambda h, b: (b, 5 * hb + h)),
            blk(seq, lambda h, b: (b, 2 * hb + h)),
            blk(seq, lambda h, b: (b, 3 * hb + h)),
            blk(seq, lambda h, b: (b, 6 * hb + h)),
            blk(ctx_len, lambda h, b: (b, 2 * hb + h)),
            blk(ctx_len, lambda h, b: (b, 3 * hb + h)),
        ] + side_in,
        out_specs=[blk(seq, lambda h, b: (b, h))] + side_out,
        out_shape=[jax.ShapeDtypeStruct((batch * seq, RET_WIDTH), BF16)] + side_shape,
        scratch_shapes=[
            pltpu.VMEM((seq, HEAD_DIM), F32),
            pltpu.VMEM((seq, HEAD_DIM), F32),
            pltpu.VMEM((n_chunks, HEAD_DIM, HEAD_DIM), BF16),
            pltpu.VMEM((n_chunks, HEAD_DIM, HEAD_DIM), BF16),
        ],
        compiler_params=_cparams(2),
        name="retention",
    )(decay_f.astype(F32), decay_b.astype(F32), gn_w.reshape(1, RET_WIDTH), cos, sin, proj, proj, proj, proj, cproj, cproj,
      cc, w_mod, b_mod.reshape(1, -1), w_out)


def _split_bf16(x):
    hi = x.astype(BF16)
    lo = (x - hi.astype(F32)).astype(BF16)
    return hi, lo


OUTPROJ_ROW_SPLITS = 2


def _outproj_body(na_ref, ret_ref, w_ref, x_ref, gate_ref, nw_ref, sh_ref, sc_ref, wr_ref,
                  h1_ref, u2_ref, lg_ref):
    half = na_ref.shape[1]
    n_e = wr_ref.shape[0]
    tm = x_ref.shape[0]
    wh, wl = _split_bf16(wr_ref[...])
    w_router = jnp.concatenate([wh, wl], axis=0)
    for r in range(OUTPROJ_ROW_SPLITS):
        rows = pl.ds(r * (tm // OUTPROJ_ROW_SPLITS), tm // OUTPROJ_ROW_SPLITS)
        mix = _dot(na_ref[rows, :], w_ref[:half, :]) + _dot(ret_ref[rows, :], w_ref[half:, :])
        h1 = x_ref[rows, :] + gate_ref[0] * mix
        h1_ref[rows, :] = h1
        u2 = _rms_mod(h1, nw_ref[...], sh_ref[0], sc_ref[0])
        u2_ref[rows, :] = u2
        both = _dot_nt(w_router, u2.astype(BF16))
        lg_ref[:, rows] = both[:n_e] + both[n_e:]


def _out_projection(na, ret, w_out_bf16, x2d, mod3, norm_w, w_router_t, seq, tm=OUTPROJ_ROW_TILE):
    rows, d = x2d.shape
    half = na.shape[1]
    per_b = seq // tm
    return pl.pallas_call(
        _outproj_body,
        grid=(rows // tm,),
        in_specs=[
            pl.BlockSpec((tm, half), lambda i: (i, 0)),
            pl.BlockSpec((tm, half), lambda i: (i, 0)),
            pl.BlockSpec((2 * half, d), lambda i: (0, 0)),
            pl.BlockSpec((tm, d), lambda i: (i, 0)),
            pl.BlockSpec((1, 1, d), lambda i: (i // per_b, 0, MOD_LATE_GATE_MIX)),
            pl.BlockSpec((1, d), lambda i: (0, 0)),
            pl.BlockSpec((1, 1, d), lambda i: (i // per_b, 0, MOD_LATE_SHIFT_FFN)),
            pl.BlockSpec((1, 1, d), lambda i: (i // per_b, 0, MOD_LATE_SCALE_FFN)),
            pl.BlockSpec((N_EXPERTS, d), lambda i: (0, 0)),
        ],
        out_specs=[
            pl.BlockSpec((tm, d), lambda i: (i, 0)),
            pl.BlockSpec((tm, d), lambda i: (i, 0)),
            pl.BlockSpec((N_EXPERTS, tm), lambda i: (0, i)),
        ],
        out_shape=[
            jax.ShapeDtypeStruct((rows, d), F32),
            jax.ShapeDtypeStruct((rows, d), F32),
            jax.ShapeDtypeStruct((N_EXPERTS, rows), F32),
        ],
        compiler_params=_cparams(1),
        name="out_projection",
    )(na, ret, w_out_bf16, x2d, mod3, norm_w.reshape(1, d), mod3, mod3, w_router_t)


def _prefix_incl_lanes(x, tri):
    r, l = x.shape
    nb = l // LANES
    xs = jnp.concatenate([x[:, t * LANES:(t + 1) * LANES] for t in range(nb)], axis=0).astype(BF16)
    p = _dot(xs, tri)
    outs, run = [], jnp.zeros((r, 1), F32)
    for t in range(nb):
        blk = p[t * r:(t + 1) * r] + run
        outs.append(blk)
        run = blk[:, LANES - 1:LANES]
    return jnp.concatenate(outs, axis=1)


NOT_ROUTED = -(1 << 20)


def _route_body(lg_ref, gidx_ref, gate_ref, slot_ref, before_ref, *, cap):
    b = pl.program_id(0)
    n_e, seq = lg_ref.shape
    kf = float(cap)
    lg = lg_ref[...]
    ex = jnp.exp(lg - jnp.max(lg, axis=0, keepdims=True))
    aff = ex / jnp.sum(ex, axis=0, keepdims=True)

    def cond(c):
        return (c[0] < 4096) & (c[5] > 0.5)

    def step(c):
        it, lo, hi, thr, done, _ = c
        mid = 0.5 * (lo + hi)
        above = jnp.sum((aff > mid).astype(F32), axis=1, keepdims=True)
        hit = above == kf
        stuck = (mid <= lo) | (mid >= hi)
        active = done < 0.5
        thr = jnp.where(active & hit, mid, jnp.where(active & stuck, hi, thr))
        go = active & ~(hit | stuck)
        ge = above >= kf
        lo = jnp.where(go & ge, mid, lo)
        hi = jnp.where(go & ~ge, mid, hi)
        done = jnp.where(active & (hit | stuck), 1.0, done)
        return it + 1, lo, hi, thr, done, jnp.sum(1.0 - done)

    col = lambda v: jnp.full((n_e, 1), v, F32)
    init = (jnp.int32(0), col(-1.0), col(2.0), col(0.0), col(0.0), jnp.float32(n_e))
    thr = lax.while_loop(cond, step, init)[3]

    ii = lax.broadcasted_iota(jnp.int32, (LANES, LANES), 0)
    jj = lax.broadcasted_iota(jnp.int32, (LANES, LANES), 1)
    tri = (ii <= jj).astype(BF16)
    gt = aff > thr
    eq = (aff == thr).astype(F32)
    need = kf - jnp.sum(gt.astype(F32), axis=1, keepdims=True)
    eq_before = _prefix_incl_lanes(eq, tri) - eq
    mask = jnp.where(gt | ((eq > 0.5) & (eq_before < need)), 1.0, 0.0)

    slot = _prefix_incl_lanes(mask, tri) - mask
    before_ref[0] = slot.astype(jnp.int32)
    slot_ref[0] = jnp.where(mask > 0.5, slot, float(NOT_ROUTED)).astype(jnp.int32)

    tok = lax.broadcasted_iota(jnp.int32, (1, seq), 1).astype(F32)
    tok_hi = jnp.floor(tok * (1.0 / 64))
    tok_lo = tok - 64.0 * tok_hi
    n_hi = ROUTE_SLOT_HI
    n_lo = cap // n_hi
    hi_iota = lax.broadcasted_iota(jnp.int32, (n_hi, seq), 0).astype(F32)
    lo_iota = lax.broadcasted_iota(jnp.int32, (n_lo, seq), 0).astype(F32)
    for e in range(n_e):
        se = slot[e:e + 1]
        s_hi = jnp.floor(se * (1.0 / n_lo))
        s_lo = se - n_lo * s_hi
        in_hi = jnp.where((s_hi == hi_iota) & (mask[e:e + 1] > 0.5), 1.0, 0.0)
        in_lo = (s_lo == lo_iota).astype(BF16)
        a = aff[e:e + 1]
        a_hi = a.astype(BF16).astype(F32)
        a_mid = (a - a_hi).astype(BF16).astype(F32)
        a_lo = a - a_hi - a_mid
        vals = (tok_hi, tok_lo, a_hi, a_mid, a_lo)
        lhs = jnp.concatenate([in_hi * v for v in vals], axis=0).astype(BF16)
        got = _dot_nt(lhs, in_lo)
        part = lambda r: got[r * n_hi:(r + 1) * n_hi]
        gidx_ref[0, e] = (part(0) * 64.0 + part(1)).astype(jnp.int32) + b * seq
        gate_ref[0, e] = part(2) + part(3) + part(4)


def _routing(logits_t, batch, seq):
    n_e = logits_t.shape[0]
    cap = CAPACITY_FACTOR * seq // n_e
    n_hi, n_lo = ROUTE_SLOT_HI, cap // ROUTE_SLOT_HI
    bec = pl.BlockSpec((1, n_e, n_hi, n_lo), lambda b: (b, 0, 0, 0))
    bel = pl.BlockSpec((1, n_e, seq), lambda b: (b, 0, 0))
    gidx, gates, slot, before = pl.pallas_call(
        functools.partial(_route_body, cap=cap),
        grid=(batch,),
        in_specs=[pl.BlockSpec((n_e, seq), lambda b: (0, b))],
        out_specs=[bec, bec, bel, bel],
        out_shape=[
            jax.ShapeDtypeStruct((batch, n_e, n_hi, n_lo), jnp.int32),
            jax.ShapeDtypeStruct((batch, n_e, n_hi, n_lo), F32),
            jax.ShapeDtypeStruct((batch, n_e, seq), jnp.int32),
            jax.ShapeDtypeStruct((batch, n_e, seq), jnp.int32),
        ],
        compiler_params=_cparams(1),
        name="routing",
    )(logits_t)
    flat = lambda a: a.reshape(batch, n_e, cap)
    return flat(gidx), flat(gates), slot, before


def _hbm_row(ref, r):
    return ref.at[lax.shift_right_logical(r, 3), pl.ds(r & (SUBLANES - 1), 1)]


MOE_ROW_SPLITS = 2


def _moe_body(gidx_ref, gnext_ref, gate_ref, u2_hbm, wg_ref, wu_ref, wd_ref, y_ref,
              stage_ref, xe_ref, acc_ref, gcol_ref, gsem, *, n_f):
    e = pl.program_id(0)
    f = pl.program_id(1)
    n_e = pl.num_programs(0)
    m, d = y_ref.shape
    sub = SUBLANES
    n_tiles = m // sub
    assert n_f >= 2
    per_step = -(-n_tiles // (n_f - 1))
    last_count = n_tiles - per_step * (n_f - 2)

    def gather_copy(idx_ref, i, k):
        return pltpu.make_async_copy(_hbm_row(u2_hbm, idx_ref[0, i * sub + k]), stage_ref.at[i, pl.ds(k, 1)], gsem)

    def for_rows(fn, tiles_per_trip=4):
        def body(t, c):
            for kk in range(tiles_per_trip * sub):
                fn(t * tiles_per_trip + kk // sub, kk % sub)
            return c
        lax.fori_loop(0, n_tiles // tiles_per_trip, body, 0)

    wait_gather = lambda: for_rows(lambda i, k: gather_copy(gidx_ref, i, k).wait())

    def ffn_step(first, gather_tiles, last=False):
        i0 = (f - 1) * per_step
        for kk in range(gather_tiles * sub):
            gather_copy(gnext_ref, i0 + kk // sub, kk % sub).start()
        wg, wu, wd = wg_ref[...].astype(BF16), wu_ref[...].astype(BF16), wd_ref[...].astype(BF16)
        if first:
            gcol_ref[...] = jnp.transpose(jnp.broadcast_to(gate_ref[...], (LANES, m)))
        part = m // MOE_ROW_SPLITS
        for r in range(MOE_ROW_SPLITS):
            rows = pl.ds(r * part, part)
            if first:
                x = stage_ref[pl.ds(r * (part // sub), part // sub)].reshape(part, d).astype(BF16)
                xe_ref[rows, :] = x
            else:
                x = xe_ref[rows, :]
            hid = _silu(_dot(x, wg)) * _dot(x, wu)
            out = _dot(hid.astype(BF16), wd)
            if last:
                g = jnp.tile(gcol_ref[rows, :], (1, d // LANES))
                y_ref[rows, :] = ((acc_ref[rows, :] + out) * g).astype(y_ref.dtype)
            else:
                acc_ref[rows, :] = out if first else acc_ref[rows, :] + out

    @pl.when(f == 0)
    def _():
        @pl.when(e == 0)
        def _():
            for_rows(lambda i, k: gather_copy(gidx_ref, i, k).start())

        wait_gather()
        ffn_step(True, 0)

    if n_f > 2:
        pl.when((f > 0) & (f < n_f - 1))(lambda: ffn_step(False, per_step))

    @pl.when(f == n_f - 1)
    def _():
        ffn_step(False, last_count, last=True)

        @pl.when(e == n_e - 1)
        def _():
            wait_gather()


def _moe_ffn(gidx, gates, u2, w_gate, w_up, w_down, tf=MOE_F_TILE):
    n_e, _, m = gidx.shape
    rows, d = u2.shape
    ff = w_gate.shape[2]
    n_f = ff // tf
    sub = SUBLANES
    smem = lambda fn: pl.BlockSpec((None, 1, m), fn, memory_space=pltpu.SMEM)
    return pl.pallas_call(
        functools.partial(_moe_body, n_f=n_f),
        grid=(n_e, n_f),
        in_specs=[
            smem(lambda e, f: (e, 0, 0)),
            smem(lambda e, f: (jnp.minimum(e + 1, n_e - 1), 0, 0)),
            pl.BlockSpec((None, 1, m), lambda e, f: (e, 0, 0)),
            pl.BlockSpec(memory_space=pl.ANY),
            pl.BlockSpec((None, d, tf), lambda e, f: (e, 0, f)),
            pl.BlockSpec((None, d, tf), lambda e, f: (e, 0, f)),
            pl.BlockSpec((None, tf, d), lambda e, f: (e, f, 0)),
        ],
        out_specs=pl.BlockSpec((m, d), lambda e, f: (e, 0)),
        out_shape=jax.ShapeDtypeStruct((n_e * m, d), BF16),
        scratch_shapes=[
            pltpu.VMEM((m // sub, sub, d), F32),
            pltpu.VMEM((m, d), BF16),
            pltpu.VMEM((m, d), F32),
            pltpu.VMEM((m, LANES), F32),
            pltpu.SemaphoreType.DMA,
        ],
        compiler_params=_cparams(2),
        name="moe_ffn",
    )(gidx, gidx, gates, u2.reshape(rows // sub, sub, d), w_gate, w_up, w_down)


COMBINE_TOKENS = 512
COMBINE_PIECE = 96


def _combine_body(start_ref, npiece_ref, h1_ref, y_hbm, slot_ref, gate_ref, nw_ref, o_ref,
                  acc_ref, buf_ref, xbuf_ref, sem, xsem, *, m, cap, per_b):
    i = pl.program_id(0)
    n_tiles = pl.num_programs(0)
    n_e = slot_ref.shape[1]
    t, p = COMBINE_TOKENS, COMBINE_PIECE
    total = n_e * m
    cur = i % 2
    b = i // per_b

    def piece_copy(tile, e, dst):
        st = pl.multiple_of(start_ref[tile * n_e + e], BF16_ROWS)
        return pltpu.make_async_copy(y_hbm.at[pl.ds(st, p)], buf_ref.at[dst, pl.ds(e * p, p)], sem.at[dst])

    @pl.when(i == 0)
    def _():
        for e in range(n_e):
            piece_copy(0, e, 0).start()

    @pl.when(i + 1 < n_tiles)
    def _():
        for e in range(n_e):
            piece_copy(i + 1, e, 1 - cur).start()

    for e in range(n_e):
        piece_copy(i, e, cur).wait()

    slots = slot_ref[0]
    riota = lax.broadcasted_iota(jnp.int32, (p, t), 0)

    def local_row(e, st):
        return slots[e:e + 1, :] + (e * m + b * cap - st)

    sel = jnp.concatenate(
        [(local_row(e, start_ref[i * n_e + e]) == riota).astype(BF16) for e in range(n_e)], axis=0)
    acc_ref[...] = _dot_tn(sel, buf_ref[cur])

    for e in range(n_e):
        st0 = start_ref[i * n_e + e]

        def extra(k, carry, e=e, st0=st0):
            want = st0 + k * p
            st = pl.multiple_of(jnp.minimum(want, total - p), BF16_ROWS)
            cp = pltpu.make_async_copy(y_hbm.at[pl.ds(st, p)], xbuf_ref, xsem)
            cp.start()
            cp.wait()
            blk = ((local_row(e, st) == riota) & (riota >= want - st)).astype(BF16)
            acc_ref[...] += _dot_tn(blk, xbuf_ref[...])
            return carry

        lax.fori_loop(1, npiece_ref[i * n_e + e], extra, 0)

    h2 = h1_ref[...] + gate_ref[0] * acc_ref[...]
    ms = jnp.mean(h2 * h2, axis=-1, keepdims=True)
    o_ref[...] = h2 * lax.rsqrt(ms + EPS) * nw_ref[...]


def _combine_pieces(before, cap):
    batch, n_e, seq = before.shape
    t, p = COMBINE_TOKENS, COMBINE_PIECE
    m = batch * cap
    total = n_e * m
    bounds = jnp.concatenate([before[:, :, ::t], jnp.full((batch, n_e, 1), cap, jnp.int32)], axis=2)
    base = (jnp.arange(n_e, dtype=jnp.int32) * m)[None, :, None] + (jnp.arange(batch, dtype=jnp.int32) * cap)[:, None, None]
    first = base + bounds[:, :, :-1]
    end = base + bounds[:, :, 1:]
    start = jnp.minimum(first // BF16_ROWS * BF16_ROWS, total - p)
    npiece = jnp.maximum((end - start + p - 1) // p, 1)
    by_tile = lambda a: a.transpose(0, 2, 1).reshape(-1)
    return by_tile(start), by_tile(npiece)


def _combine(h1, y, slot, before, mod3, final_w, seq, cap):
    rows, d = h1.shape
    batch, n_e, _ = slot.shape
    t, p = COMBINE_TOKENS, COMBINE_PIECE
    per_b = seq // t
    start, npiece = _combine_pieces(before, cap)
    grid_spec = pltpu.PrefetchScalarGridSpec(
        num_scalar_prefetch=2,
        grid=(rows // t,),
        in_specs=[
            pl.BlockSpec((t, d), lambda i, st, npc: (i, 0)),
            pl.BlockSpec(memory_space=pl.ANY),
            pl.BlockSpec((1, n_e, t), lambda i, st, npc: (i // per_b, 0, i % per_b)),
            pl.BlockSpec((1, 1, d), lambda i, st, npc: (i // per_b, 0, MOD_LATE_GATE_FFN)),
            pl.BlockSpec((1, d), lambda i, st, npc: (0, 0)),
        ],
        out_specs=pl.BlockSpec((t, d), lambda i, st, npc: (i, 0)),
        scratch_shapes=[
            pltpu.VMEM((t, d), F32),
            pltpu.VMEM((2, n_e * p, d), y.dtype),
            pltpu.VMEM((p, d), y.dtype),
            pltpu.SemaphoreType.DMA((2,)),
            pltpu.SemaphoreType.DMA,
        ],
    )
    return pl.pallas_call(
        functools.partial(_combine_body, m=batch * cap, cap=cap, per_b=per_b),
        grid_spec=grid_spec,
        out_shape=jax.ShapeDtypeStruct((rows, d), F32),
        compiler_params=_cparams(1),
        name="combine",
    )(start, npiece, h1, y, slot, mod3, final_w.reshape(1, d))


def kernel(x, c, ctx, c_ctx, w_mod, b_mod, norm_mix_w, norm_ffn_w, w_in, na_rpb, ret_decay_fwd,
           ret_decay_bwd, ret_gn_w, w_out, w_router, w_gate, w_up, w_down, final_norm_w):
    batch, seq, d = x.shape
    ctx_len = ctx.shape[1]
    assert w_mod.shape[0] == 1, "one trunk layer"
    n_e = w_router.shape[2]
    cap = CAPACITY_FACTOR * seq // n_e

    mod_rows = SUBLANES
    cc = jnp.concatenate([c, c_ctx[None], jnp.zeros((mod_rows - batch - 1, d), c.dtype)], axis=0)
    mod3 = _modulation(cc, w_mod[0], b_mod[0], MOD_EARLY * d)

    x2d = x.reshape(batch * seq, d)
    tm = INPROJ_ROW_TILE
    cproj, w_in_bf16 = _ctx_projection(ctx.reshape(batch * ctx_len, d), norm_mix_w[0], mod3, batch,
                                       w_in[0], KV_COLS)
    proj = _in_projection(x2d, norm_mix_w[0], mod3, lambda i: i // (seq // tm), w_in_bf16,
                          w_in.shape[2], tm, w_in.shape[2] // INPROJ_COL_TILES)

    na = _neighbourhood_attention(proj, cproj, na_rpb[0], batch, seq, ctx_len)
    ret, mod_late3, w_out_bf16 = _retention(proj, cproj, ret_decay_fwd[0], ret_decay_bwd[0], ret_gn_w[0],
                                            batch, seq, ctx_len, cc, w_mod[0], b_mod[0], MOD_EARLY * d, w_out[0])

    h1, u2, logits_t = _out_projection(na, ret, w_out_bf16, x2d, mod_late3, norm_ffn_w[0], w_router[0].T, seq)

    gidx, gates, slot, before = _routing(logits_t, batch, seq)
    per_expert = lambda a: a.transpose(1, 0, 2).reshape(n_e, 1, batch * cap)
    y = _moe_ffn(per_expert(gidx), per_expert(gates), u2, w_gate[0], w_up[0], w_down[0])
    out = _combine(h1, y, slot, before, mod_late3, final_norm_w, seq, cap)
    return out.reshape(batch, seq, d)
```
